```python
import math
import jax, jax.numpy as jnp
from jax import lax
import numpy as np

D_MODEL = 1024
BATCH = 8
SEQ = 4096
DEPTH = 4

HEAD_DIM = 64
N_META = 16
BLOCK = 128
NEG_INF = -1e30
SWA_WINDOW = 128
SWA_Q_HEADS = 8
SWA_KV_HEADS = 2
SWA_GROUP = SWA_Q_HEADS // SWA_KV_HEADS
FOX_HEADS = 8
LRU_WIDTH = D_MODEL // 2
LRU_BLOCKS = 8
LRU_BLOCK_DIM = LRU_WIDTH // LRU_BLOCKS
CONV_WIDTH = 4
LRU_C = 8.0
REL_BUCKETS = 32
REL_MAX_DIST = 128
D_FF = ((-(-8 * D_MODEL // 3)) + 255) // 256 * 256
N_BRANCH = 3
SPLIT_SIZES = (
    SWA_Q_HEADS * HEAD_DIM,
    SWA_KV_HEADS * HEAD_DIM,
    SWA_KV_HEADS * HEAD_DIM,
    FOX_HEADS * HEAD_DIM,
    FOX_HEADS * HEAD_DIM,
    FOX_HEADS * HEAD_DIM,
    FOX_HEADS,
    LRU_WIDTH,
    LRU_WIDTH,
    N_BRANCH * D_MODEL,
)
IN_COLS = sum(SPLIT_SIZES)

kernel_name = "hybrid_swa_fox_rglru_block"


def rms_norm(x, g, eps=1e-6):
    xf = x.astype(jnp.float32)
    y = xf * lax.rsqrt(jnp.mean(xf * xf, axis=-1, keepdims=True) + eps)
    return (y * g.astype(jnp.float32)).astype(x.dtype)


def pad_left(a, n):
    return jnp.pad(a, [(0, 0), (n, 0)] + [(0, 0)] * (a.ndim - 2))


def t5_bucket(dist):
    max_exact = REL_BUCKETS // 2
    d = jnp.maximum(dist, 0)
    scaled = jnp.log(jnp.maximum(d, 1).astype(jnp.float32) / max_exact) / math.log(REL_MAX_DIST / max_exact)
    large = jnp.minimum(max_exact + (scaled * (REL_BUCKETS - max_exact)).astype(jnp.int32), REL_BUCKETS - 1)
    return jnp.where(d < max_exact, d, large)


def swa_sink_attention(q, k, v, sinks, rel_table, n_pad):
    b, tp, _, dh = q.shape
    nb = tp // BLOCK
    qb = q.reshape(b, nb, BLOCK, SWA_KV_HEADS, SWA_GROUP, dh)

    def band(a):
        a = a.reshape(b, nb, BLOCK, SWA_KV_HEADS, dh)
        prev = jnp.pad(a, ((0, 0), (1, 0), (0, 0), (0, 0), (0, 0)))[:, :-1]
        return jnp.concatenate([prev, a], axis=2)

    k_band, v_band = band(k), band(v)
    s = jnp.einsum('bnqhgd,bnkhd->bnhgqk', qb, k_band).astype(jnp.float32) * (dh ** -0.5)
    q_idx = jnp.arange(BLOCK)[:, None]
    k_idx = jnp.arange(2 * BLOCK)[None, :]
    dist = q_idx + BLOCK - k_idx
    bias = rel_table.astype(jnp.float32)[t5_bucket(dist)]
    bias = bias.transpose(2, 0, 1).reshape(SWA_KV_HEADS, SWA_GROUP, BLOCK, 2 * BLOCK)
    key_abs = (jnp.arange(nb)[:, None] - 1) * BLOCK + k_idx
    mask = ((dist >= 0) & (dist < SWA_WINDOW))[None] & (key_abs >= n_pad)[:, None, :]
    s = jnp.where(mask[None, :, None, None], s + bias, NEG_INF)
    sink = sinks.astype(jnp.float32).reshape(SWA_KV_HEADS, SWA_GROUP)[None, None, :, :, None, None]
    m = jnp.maximum(jnp.max(s, axis=-1, keepdims=True), sink)
    p = jnp.exp(s - m)
    denom = jnp.sum(p, axis=-1, keepdims=True) + jnp.exp(sink - m)
    p = (p / denom).astype(v.dtype)
    o = jnp.einsum('bnhgqk,bnkhd->bnqhgd', p, v_band)
    return o.reshape(b, tp, SWA_Q_HEADS * dh)


def forgetting_attention(q, k, v, log_f, n_pad):
    b, tp, h, dh = q.shape
    nb = tp // BLOCK
    cum = jnp.cumsum(log_f, axis=1).transpose(0, 2, 1)
    k_pos = jnp.arange(tp)
    qb = q.reshape(b, nb, BLOCK, h, dh).transpose(1, 0, 2, 3, 4)
    cb = cum.reshape(b, h, nb, BLOCK).transpose(2, 0, 1, 3)

    def one_block(args):
        qi, ci, n = args
        s = jnp.einsum('bqhd,bkhd->bhqk', qi, k).astype(jnp.float32) * (dh ** -0.5)
        s = s + ci[..., :, None] - cum[:, :, None, :]
        q_pos = n * BLOCK + jnp.arange(BLOCK)
        mask = (k_pos[None, :] <= q_pos[:, None]) & (k_pos >= n_pad)[None, :]
        s = jnp.where(mask, s, NEG_INF)
        p = jax.nn.softmax(s, axis=-1).astype(v.dtype)
        return jnp.einsum('bhqk,bkhd->bqhd', p, v)

    o = lax.map(one_block, (qb, cb, jnp.arange(nb)))
    return o.transpose(1, 0, 2, 3, 4).reshape(b, tp, h * dh)


def causal_depthwise_conv(x, w, bias):
    t = x.shape[1]
    xp = jnp.pad(x, ((0, 0), (CONV_WIDTH - 1, 0), (0, 0)))
    out = xp[:, 0:t] * w[0]
    for i in range(1, CONV_WIDTH):
        out = out + xp[:, i:i + t] * w[i]
    return out + bias


def rg_lru(x, w_r, b_r, w_i, b_i, lam):
    b, t, c = x.shape
    xb = x.reshape(b, t, LRU_BLOCKS, LRU_BLOCK_DIM)
    r = jax.nn.sigmoid(jnp.einsum('bthi,hij->bthj', xb, w_r).reshape(b, t, c).astype(jnp.float32) + b_r)
    gi = jax.nn.sigmoid(jnp.einsum('bthi,hij->bthj', xb, w_i).reshape(b, t, c).astype(jnp.float32) + b_i)
    log_a = LRU_C * r * jax.nn.log_sigmoid(lam.astype(jnp.float32))
    a = jnp.exp(log_a)
    inp = jnp.sqrt(-jnp.expm1(2.0 * log_a)) * (gi * x.astype(jnp.float32))

    def combine(left, right):
        a1, b1 = left
        a2, b2 = right
        return a1 * a2, a2 * b1 + b2

    _, h = lax.associative_scan(combine, (a, inp), axis=1)
    return h.astype(x.dtype)


def _fwd_setup_inputs(seed: int = 0) -> dict:
    key = jax.random.key(seed)
    ks = jax.random.split(key, 24)
    f32 = jnp.float32
    nrm = lambda k, shape, scale: jax.random.normal(k, shape, f32) * scale
    u = jax.random.uniform(ks[10], (DEPTH, LRU_WIDTH), f32, 0.9, 0.999)
    a0 = u ** (1.0 / LRU_C)
    return {
        "x": nrm(ks[0], (BATCH, SEQ, D_MODEL), 1.0),
        "meta_tokens": nrm(ks[1], (N_META, D_MODEL), 1.0),
        "rel_bias_table": nrm(ks[2], (REL_BUCKETS, SWA_Q_HEADS), 0.5),
        "norm_mix": 1.0 + nrm(ks[3], (DEPTH, D_MODEL), 0.02),
        "w_in": nrm(ks[4], (DEPTH, D_MODEL, IN_COLS), D_MODEL ** -0.5),
        "swa_sinks": nrm(ks[5], (DEPTH, SWA_Q_HEADS), 0.5),
        "fox_forget_bias": 2.0 + 3.0 * jax.random.uniform(ks[6], (DEPTH, FOX_HEADS), f32),
        "conv_w": nrm(ks[7], (DEPTH, CONV_WIDTH, LRU_WIDTH), CONV_WIDTH ** -0.5),
        "conv_b": nrm(ks[8], (DEPTH, LRU_WIDTH), 0.02),
        "lru_w_r": nrm(ks[9], (DEPTH, LRU_BLOCKS, LRU_BLOCK_DIM, LRU_BLOCK_DIM), LRU_BLOCK_DIM ** -0.5),
        "lru_b_r": nrm(ks[11], (DEPTH, LRU_WIDTH), 0.02),
        "lru_w_i": nrm(ks[12], (DEPTH, LRU_BLOCKS, LRU_BLOCK_DIM, LRU_BLOCK_DIM), LRU_BLOCK_DIM ** -0.5),
        "lru_b_i": nrm(ks[13], (DEPTH, LRU_WIDTH), 0.02),
        "lru_lambda": jnp.log(a0) - jnp.log1p(-a0),
        "w_branch": nrm(ks[14], (DEPTH, N_BRANCH, LRU_WIDTH, D_MODEL), LRU_WIDTH ** -0.5),
        "w_out": nrm(ks[15], (DEPTH, D_MODEL, D_MODEL), D_MODEL ** -0.5),
        "norm_ffn": 1.0 + nrm(ks[16], (DEPTH, D_MODEL), 0.02),
        "w_ffn_in": nrm(ks[17], (DEPTH, D_MODEL, 2 * D_FF), D_MODEL ** -0.5),
        "w_ffn_out": nrm(ks[18], (DEPTH, D_FF, D_MODEL), D_FF ** -0.5),
        "norm_final": 1.0 + nrm(ks[19], (D_MODEL,), 0.02),
    }


def _fwd_reference(x, meta_tokens, rel_bias_table, norm_mix, w_in, swa_sinks, fox_forget_bias, conv_w, conv_b,
              lru_w_r, lru_b_r, lru_w_i, lru_b_i, lru_lambda, w_branch, w_out, norm_ffn, w_ffn_in,
              w_ffn_out, norm_final):
    b = x.shape[0]
    meta = jnp.broadcast_to(meta_tokens.astype(x.dtype)[None], (b, N_META, D_MODEL))
    h = jnp.concatenate([meta, x], axis=1)
    t = h.shape[1]
    n_pad = (-t) % BLOCK
    split_points = [int(p) for p in np.cumsum(SPLIT_SIZES)[:-1]]
    for l in range(DEPTH):
        u = rms_norm(h, norm_mix[l])
        proj = u @ w_in[l]
        qa, ka, va, qf, kf, vf, fl, xc, yc, gates = jnp.split(proj, split_points, axis=-1)
        o_a = swa_sink_attention(
            pad_left(qa.reshape(b, t, SWA_Q_HEADS, HEAD_DIM), n_pad),
            pad_left(ka.reshape(b, t, SWA_KV_HEADS, HEAD_DIM), n_pad),
            pad_left(va.reshape(b, t, SWA_KV_HEADS, HEAD_DIM), n_pad),
            swa_sinks[l], rel_bias_table, n_pad)[:, n_pad:]
        log_f = jax.nn.log_sigmoid(fl.astype(jnp.float32) + fox_forget_bias[l].astype(jnp.float32))
        o_f = forgetting_attention(
            pad_left(qf.reshape(b, t, FOX_HEADS, HEAD_DIM), n_pad),
            pad_left(kf.reshape(b, t, FOX_HEADS, HEAD_DIM), n_pad),
            pad_left(vf.reshape(b, t, FOX_HEADS, HEAD_DIM), n_pad),
            pad_left(log_f, n_pad), n_pad)[:, n_pad:]
        xc = causal_depthwise_conv(xc, conv_w[l], conv_b[l])
        o_c = rg_lru(xc, lru_w_r[l], lru_b_r[l], lru_w_i[l], lru_b_i[l], lru_lambda[l]) * jax.nn.gelu(yc)
        g = jax.nn.sigmoid(gates).reshape(b, t, N_BRANCH, D_MODEL)
        merged = (g[:, :, 0] * (o_a @ w_branch[l, 0])
                  + g[:, :, 1] * (o_f @ w_branch[l, 1])
                  + g[:, :, 2] * (o_c @ w_branch[l, 2]))
        h = h + merged @ w_out[l]
        u = rms_norm(h, norm_ffn[l])
        gate_ff, up_ff = jnp.split(u @ w_ffn_in[l], 2, axis=-1)
        h = h + (jax.nn.silu(gate_ff) * up_ff) @ w_ffn_out[l]
    return rms_norm(h, norm_final)[:, N_META:]


import jax as _jax
import jax.numpy as _jnp

TWIN_FORMAT = 'train_step'
FWD_PARAMS = ['x', 'meta_tokens', 'rel_bias_table', 'norm_mix', 'w_in', 'swa_sinks', 'fox_forget_bias', 'conv_w', 'conv_b', 'lru_w_r', 'lru_b_r', 'lru_w_i', 'lru_b_i', 'lru_lambda', 'w_branch', 'w_out', 'norm_ffn', 'w_ffn_in', 'w_ffn_out', 'norm_final']
TWIN_WEIGHTS = ['meta_tokens', 'rel_bias_table', 'norm_mix', 'w_in', 'swa_sinks', 'fox_forget_bias', 'conv_w', 'conv_b', 'lru_w_r', 'lru_b_r', 'lru_w_i', 'lru_b_i', 'lru_lambda', 'w_branch', 'w_out', 'norm_ffn', 'w_ffn_in', 'w_ffn_out', 'norm_final']
TWIN_DIFF_INPUT = 'x'
TWIN_INPUTS = ['x', 'meta_tokens', 'rel_bias_table', 'norm_mix', 'w_in', 'swa_sinks', 'fox_forget_bias', 'conv_w', 'conv_b', 'lru_w_r', 'lru_b_r', 'lru_w_i', 'lru_b_i', 'lru_lambda', 'w_branch', 'w_out', 'norm_ffn', 'w_ffn_in', 'w_ffn_out', 'norm_final', 'loss_target', 'm_meta_tokens', 'm_rel_bias_table', 'm_norm_mix', 'm_w_in', 'm_swa_sinks', 'm_fox_forget_bias', 'm_conv_w', 'm_conv_b', 'm_lru_w_r', 'm_lru_b_r', 'm_lru_w_i', 'm_lru_b_i', 'm_lru_lambda', 'm_w_branch', 'm_w_out', 'm_norm_ffn', 'm_w_ffn_in', 'm_w_ffn_out', 'm_norm_final', 'v_meta_tokens', 'v_rel_bias_table', 'v_norm_mix', 'v_w_in', 'v_swa_sinks', 'v_fox_forget_bias', 'v_conv_w', 'v_conv_b', 'v_lru_w_r', 'v_lru_b_r', 'v_lru_w_i', 'v_lru_b_i', 'v_lru_lambda', 'v_w_branch', 'v_w_out', 'v_norm_ffn', 'v_w_ffn_in', 'v_w_ffn_out', 'v_norm_final']
TWIN_OUTPUTS = ['loss', 'grad_x', 'grad_meta_tokens', 'grad_rel_bias_table', 'grad_norm_mix', 'grad_w_in', 'grad_swa_sinks', 'grad_fox_forget_bias', 'grad_conv_w', 'grad_conv_b', 'grad_lru_w_r', 'grad_lru_b_r', 'grad_lru_w_i', 'grad_lru_b_i', 'grad_lru_lambda', 'grad_w_branch', 'grad_w_out', 'grad_norm_ffn', 'grad_w_ffn_in', 'grad_w_ffn_out', 'grad_norm_final', 'delta_meta_tokens', 'delta_rel_bias_table', 'delta_norm_mix', 'delta_w_in', 'delta_swa_sinks', 'delta_fox_forget_bias', 'delta_conv_w', 'delta_conv_b', 'delta_lru_w_r', 'delta_lru_b_r', 'delta_lru_w_i', 'delta_lru_b_i', 'delta_lru_lambda', 'delta_w_branch', 'delta_w_out', 'delta_norm_ffn', 'delta_w_ffn_in', 'delta_w_ffn_out', 'delta_norm_final', 'new_m_meta_tokens', 'new_m_rel_bias_table', 'new_m_norm_mix', 'new_m_w_in', 'new_m_swa_sinks', 'new_m_fox_forget_bias', 'new_m_conv_w', 'new_m_conv_b', 'new_m_lru_w_r', 'new_m_lru_b_r', 'new_m_lru_w_i', 'new_m_lru_b_i', 'new_m_lru_lambda', 'new_m_w_branch', 'new_m_w_out', 'new_m_norm_ffn', 'new_m_w_ffn_in', 'new_m_w_ffn_out', 'new_m_norm_final', 'new_v_meta_tokens', 'new_v_rel_bias_table', 'new_v_norm_mix', 'new_v_w_in', 'new_v_swa_sinks', 'new_v_fox_forget_bias', 'new_v_conv_w', 'new_v_conv_b', 'new_v_lru_w_r', 'new_v_lru_b_r', 'new_v_lru_w_i', 'new_v_lru_b_i', 'new_v_lru_lambda', 'new_v_w_branch', 'new_v_w_out', 'new_v_norm_ffn', 'new_v_w_ffn_in', 'new_v_w_ffn_out', 'new_v_norm_final']
TWIN_LEAF_KINDS = {'loss': 'loss', 'grad_x': 'grad_x', 'grad_meta_tokens': 'grad_w', 'grad_rel_bias_table': 'grad_w', 'grad_norm_mix': 'grad_w', 'grad_w_in': 'grad_w', 'grad_swa_sinks': 'grad_w', 'grad_fox_forget_bias': 'grad_w', 'grad_conv_w': 'grad_w', 'grad_conv_b': 'grad_w', 'grad_lru_w_r': 'grad_w', 'grad_lru_b_r': 'grad_w', 'grad_lru_w_i': 'grad_w', 'grad_lru_b_i': 'grad_w', 'grad_lru_lambda': 'grad_w', 'grad_w_branch': 'grad_w', 'grad_w_out': 'grad_w', 'grad_norm_ffn': 'grad_w', 'grad_w_ffn_in': 'grad_w', 'grad_w_ffn_out': 'grad_w', 'grad_norm_final': 'grad_w', 'delta_meta_tokens': 'delta_w', 'delta_rel_bias_table': 'delta_w', 'delta_norm_mix': 'delta_w', 'delta_w_in': 'delta_w', 'delta_swa_sinks': 'delta_w', 'delta_fox_forget_bias': 'delta_w', 'delta_conv_w': 'delta_w', 'delta_conv_b': 'delta_w', 'delta_lru_w_r': 'delta_w', 'delta_lru_b_r': 'delta_w', 'delta_lru_w_i': 'delta_w', 'delta_lru_b_i': 'delta_w', 'delta_lru_lambda': 'delta_w', 'delta_w_branch': 'delta_w', 'delta_w_out': 'delta_w', 'delta_norm_ffn': 'delta_w', 'delta_w_ffn_in': 'delta_w', 'delta_w_ffn_out': 'delta_w', 'delta_norm_final': 'delta_w', 'new_m_meta_tokens': 'new_m', 'new_m_rel_bias_table': 'new_m', 'new_m_norm_mix': 'new_m', 'new_m_w_in': 'new_m', 'new_m_swa_sinks': 'new_m', 'new_m_fox_forget_bias': 'new_m', 'new_m_conv_w': 'new_m', 'new_m_conv_b': 'new_m', 'new_m_lru_w_r': 'new_m', 'new_m_lru_b_r': 'new_m', 'new_m_lru_w_i': 'new_m', 'new_m_lru_b_i': 'new_m', 'new_m_lru_lambda': 'new_m', 'new_m_w_branch': 'new_m', 'new_m_w_out': 'new_m', 'new_m_norm_ffn': 'new_m', 'new_m_w_ffn_in': 'new_m', 'new_m_w_ffn_out': 'new_m', 'new_m_norm_final': 'new_m', 'new_v_meta_tokens': 'new_v', 'new_v_rel_bias_table': 'new_v', 'new_v_norm_mix': 'new_v', 'new_v_w_in': 'new_v', 'new_v_swa_sinks': 'new_v', 'new_v_fox_forget_bias': 'new_v', 'new_v_conv_w': 'new_v', 'new_v_conv_b': 'new_v', 'new_v_lru_w_r': 'new_v', 'new_v_lru_b_r': 'new_v', 'new_v_lru_w_i': 'new_v', 'new_v_lru_b_i': 'new_v', 'new_v_lru_lambda': 'new_v', 'new_v_w_branch': 'new_v', 'new_v_w_out': 'new_v', 'new_v_norm_ffn': 'new_v', 'new_v_w_ffn_in': 'new_v', 'new_v_w_ffn_out': 'new_v', 'new_v_norm_final': 'new_v'}


def _forward(args):
    return _fwd_reference(*[args[k] for k in FWD_PARAMS])


def _output_shape():
    out = _jax.eval_shape(lambda: _forward(_fwd_setup_inputs(0)))
    return out.shape, out.dtype

N_MICROBATCH = 1
ADAM_LR = 0.001
ADAM_B1 = 0.9
ADAM_B2 = 0.999
ADAM_EPS = 1e-08
ADAM_WD = 0.01
ADAM_STEP = 10
PER_EXAMPLE_BATCH_AXIS = {'x': 0, 'loss_target': 0}
SHARED_INPUTS = []
_WEIGHT_DTYPES = {'meta_tokens': _jnp.float32, 'rel_bias_table': _jnp.float32, 'norm_mix': _jnp.float32, 'w_in': _jnp.float32, 'swa_sinks': _jnp.float32, 'fox_forget_bias': _jnp.float32, 'conv_w': _jnp.float32, 'conv_b': _jnp.float32, 'lru_w_r': _jnp.float32, 'lru_b_r': _jnp.float32, 'lru_w_i': _jnp.float32, 'lru_b_i': _jnp.float32, 'lru_lambda': _jnp.float32, 'w_branch': _jnp.float32, 'w_out': _jnp.float32, 'norm_ffn': _jnp.float32, 'w_ffn_in': _jnp.float32, 'w_ffn_out': _jnp.float32, 'norm_final': _jnp.float32}
MOMENT_SCALE = {'meta_tokens': 1.146533e-02, 'rel_bias_table': 7.913985e-02, 'norm_mix': 9.271599e-02, 'w_in': 3.770387e-02, 'swa_sinks': 9.540217e-03, 'fox_forget_bias': 1.860340e-01, 'conv_w': 9.439806e-02, 'conv_b': 9.120233e-01, 'lru_w_r': 3.247930e-02, 'lru_b_r': 2.362310e-02, 'lru_w_i': 5.989788e-02, 'lru_b_i': 3.392444e-02, 'lru_lambda': 4.410222e-02, 'w_branch': 4.132196e-02, 'w_out': 7.081289e-02, 'norm_ffn': 1.308358e-01, 'w_ffn_in': 5.550944e-02, 'w_ffn_out': 9.074215e-02, 'norm_final': 3.194353e+01}


def _to_microbatches(a, axis):
    t = _jnp.moveaxis(a, axis, 0)
    t = t.reshape((N_MICROBATCH, t.shape[0] // N_MICROBATCH) + t.shape[1:])
    return _jnp.moveaxis(t, 1, axis + 1)


def setup_inputs(seed: int = 0) -> dict:
    inp = _fwd_setup_inputs(seed)
    key = _jax.random.fold_in(_jax.random.key(seed), 7919)
    shape, _ = _output_shape()
    out = dict(inp)
    out["loss_target"] = _jax.random.normal(_jax.random.fold_in(key, 0), shape, _jnp.float32)
    for i, name in enumerate(TWIN_WEIGHTS):
        w = inp[name].astype(_jnp.float32)
        if MOMENT_SCALE is None:
            s = _jnp.sqrt(_jnp.mean(_jnp.square(w)) + 1e-30)
        else:
            s = MOMENT_SCALE[name]
        km, kv = _jax.random.split(_jax.random.fold_in(key, i + 1))
        out[name] = w
        out["m_" + name] = s * _jax.random.normal(km, w.shape, _jnp.float32)
        out["v_" + name] = (s * s) * _jax.random.uniform(kv, w.shape, _jnp.float32, 0.5, 1.5)
    if N_MICROBATCH > 1:
        for name, axis in PER_EXAMPLE_BATCH_AXIS.items():
            out[name] = _to_microbatches(out[name], axis)
    return {'x': out['x'], 'meta_tokens': out['meta_tokens'], 'rel_bias_table': out['rel_bias_table'], 'norm_mix': out['norm_mix'], 'w_in': out['w_in'], 'swa_sinks': out['swa_sinks'], 'fox_forget_bias': out['fox_forget_bias'], 'conv_w': out['conv_w'], 'conv_b': out['conv_b'], 'lru_w_r': out['lru_w_r'], 'lru_b_r': out['lru_b_r'], 'lru_w_i': out['lru_w_i'], 'lru_b_i': out['lru_b_i'], 'lru_lambda': out['lru_lambda'], 'w_branch': out['w_branch'], 'w_out': out['w_out'], 'norm_ffn': out['norm_ffn'], 'w_ffn_in': out['w_ffn_in'], 'w_ffn_out': out['w_ffn_out'], 'norm_final': out['norm_final'], 'loss_target': out['loss_target'], 'm_meta_tokens': out['m_meta_tokens'], 'm_rel_bias_table': out['m_rel_bias_table'], 'm_norm_mix': out['m_norm_mix'], 'm_w_in': out['m_w_in'], 'm_swa_sinks': out['m_swa_sinks'], 'm_fox_forget_bias': out['m_fox_forget_bias'], 'm_conv_w': out['m_conv_w'], 'm_conv_b': out['m_conv_b'], 'm_lru_w_r': out['m_lru_w_r'], 'm_lru_b_r': out['m_lru_b_r'], 'm_lru_w_i': out['m_lru_w_i'], 'm_lru_b_i': out['m_lru_b_i'], 'm_lru_lambda': out['m_lru_lambda'], 'm_w_branch': out['m_w_branch'], 'm_w_out': out['m_w_out'], 'm_norm_ffn': out['m_norm_ffn'], 'm_w_ffn_in': out['m_w_ffn_in'], 'm_w_ffn_out': out['m_w_ffn_out'], 'm_norm_final': out['m_norm_final'], 'v_meta_tokens': out['v_meta_tokens'], 'v_rel_bias_table': out['v_rel_bias_table'], 'v_norm_mix': out['v_norm_mix'], 'v_w_in': out['v_w_in'], 'v_swa_sinks': out['v_swa_sinks'], 'v_fox_forget_bias': out['v_fox_forget_bias'], 'v_conv_w': out['v_conv_w'], 'v_conv_b': out['v_conv_b'], 'v_lru_w_r': out['v_lru_w_r'], 'v_lru_b_r': out['v_lru_b_r'], 'v_lru_w_i': out['v_lru_w_i'], 'v_lru_b_i': out['v_lru_b_i'], 'v_lru_lambda': out['v_lru_lambda'], 'v_w_branch': out['v_w_branch'], 'v_w_out': out['v_w_out'], 'v_norm_ffn': out['v_norm_ffn'], 'v_w_ffn_in': out['v_w_ffn_in'], 'v_w_ffn_out': out['v_w_ffn_out'], 'v_norm_final': out['v_norm_final']}


def _loss(weights, diff, rest, loss_target):
    with _jax.named_scope("forward"):
        args = {**rest, TWIN_DIFF_INPUT: diff, **{k: w.astype(_WEIGHT_DTYPES[k]) for k, w in weights.items()}}
        y = _forward(args)
    with _jax.named_scope("loss_head"):
        err = _jnp.square(y.astype(_jnp.float32) - loss_target)
        return 0.5 * _jnp.sum(_jnp.mean(err, axis=-1)) if err.ndim else 0.5 * err


def _adamw(w, g, m, v):
    m = ADAM_B1 * m + (1.0 - ADAM_B1) * g
    v = ADAM_B2 * v + (1.0 - ADAM_B2) * _jnp.square(g)
    m_hat = m / (1.0 - ADAM_B1 ** ADAM_STEP)
    v_hat = v / (1.0 - ADAM_B2 ** ADAM_STEP)
    delta = -ADAM_LR * (m_hat / (_jnp.sqrt(v_hat) + ADAM_EPS) + ADAM_WD * w)
    return delta, m, v


def reference(x, meta_tokens, rel_bias_table, norm_mix, w_in, swa_sinks, fox_forget_bias, conv_w, conv_b, lru_w_r, lru_b_r, lru_w_i, lru_b_i, lru_lambda, w_branch, w_out, norm_ffn, w_ffn_in, w_ffn_out, norm_final, loss_target, m_meta_tokens, m_rel_bias_table, m_norm_mix, m_w_in, m_swa_sinks, m_fox_forget_bias, m_conv_w, m_conv_b, m_lru_w_r, m_lru_b_r, m_lru_w_i, m_lru_b_i, m_lru_lambda, m_w_branch, m_w_out, m_norm_ffn, m_w_ffn_in, m_w_ffn_out, m_norm_final, v_meta_tokens, v_rel_bias_table, v_norm_mix, v_w_in, v_swa_sinks, v_fox_forget_bias, v_conv_w, v_conv_b, v_lru_w_r, v_lru_b_r, v_lru_w_i, v_lru_b_i, v_lru_lambda, v_w_branch, v_w_out, v_norm_ffn, v_w_ffn_in, v_w_ffn_out, v_norm_final):
    given = dict(x=x, meta_tokens=meta_tokens, rel_bias_table=rel_bias_table, norm_mix=norm_mix, w_in=w_in, swa_sinks=swa_sinks, fox_forget_bias=fox_forget_bias, conv_w=conv_w, conv_b=conv_b, lru_w_r=lru_w_r, lru_b_r=lru_b_r, lru_w_i=lru_w_i, lru_b_i=lru_b_i, lru_lambda=lru_lambda, w_branch=w_branch, w_out=w_out, norm_ffn=norm_ffn, w_ffn_in=w_ffn_in, w_ffn_out=w_ffn_out, norm_final=norm_final, loss_target=loss_target, m_meta_tokens=m_meta_tokens, m_rel_bias_table=m_rel_bias_table, m_norm_mix=m_norm_mix, m_w_in=m_w_in, m_swa_sinks=m_swa_sinks, m_fox_forget_bias=m_fox_forget_bias, m_conv_w=m_conv_w, m_conv_b=m_conv_b, m_lru_w_r=m_lru_w_r, m_lru_b_r=m_lru_b_r, m_lru_w_i=m_lru_w_i, m_lru_b_i=m_lru_b_i, m_lru_lambda=m_lru_lambda, m_w_branch=m_w_branch, m_w_out=m_w_out, m_norm_ffn=m_norm_ffn, m_w_ffn_in=m_w_ffn_in, m_w_ffn_out=m_w_ffn_out, m_norm_final=m_norm_final, v_meta_tokens=v_meta_tokens, v_rel_bias_table=v_rel_bias_table, v_norm_mix=v_norm_mix, v_w_in=v_w_in, v_swa_sinks=v_swa_sinks, v_fox_forget_bias=v_fox_forget_bias, v_conv_w=v_conv_w, v_conv_b=v_conv_b, v_lru_w_r=v_lru_w_r, v_lru_b_r=v_lru_b_r, v_lru_w_i=v_lru_w_i, v_lru_b_i=v_lru_b_i, v_lru_lambda=v_lru_lambda, v_w_branch=v_w_branch, v_w_out=v_w_out, v_norm_ffn=v_norm_ffn, v_w_ffn_in=v_w_ffn_in, v_w_ffn_out=v_w_ffn_out, v_norm_final=v_norm_final)
    weights = {n: given[n] for n in TWIN_WEIGHTS}
    shared = {n: given[n] for n in SHARED_INPUTS}
    per_example = {n: given[n] for n in ['x']}
    grad_fn = _jax.value_and_grad(_loss, argnums=(0, 1))

    def one_microbatch(ex, loss_target):
        ex = dict(ex)
        diff = ex.pop(TWIN_DIFF_INPUT)
        return grad_fn(weights, diff, {**shared, **ex}, loss_target)

    if N_MICROBATCH == 1:
        loss, (grad_w, grad_x) = one_microbatch(per_example, given["loss_target"])
    else:
        def body(carry, xs):
            loss_sum, grad_sum = carry
            l_k, (gw_k, gx_k) = one_microbatch(xs[0], xs[1])
            with _jax.named_scope("update"):
                return (loss_sum + l_k, _jax.tree.map(_jnp.add, grad_sum, gw_k)), gx_k

        init = (_jnp.zeros((), _jnp.float32), _jax.tree.map(_jnp.zeros_like, weights))
        (loss, grad_w), grad_x = _jax.lax.scan(body, init, (per_example, given["loss_target"]))
    with _jax.named_scope("update"):
        delta_w, new_m, new_v = {}, {}, {}
        for n in TWIN_WEIGHTS:
            delta_w[n], new_m[n], new_v[n] = _adamw(weights[n], grad_w[n], given["m_" + n], given["v_" + n])
    return (loss, grad_x, *[grad_w[n] for n in TWIN_WEIGHTS], *[delta_w[n] for n in TWIN_WEIGHTS],
            *[new_m[n] for n in TWIN_WEIGHTS], *[new_v[n] for n in TWIN_WEIGHTS])
```

```python
import functools
import math

import numpy as np
import jax
import jax.numpy as jnp
from jax import lax
from jax.experimental import pallas as pl
from jax.experimental.pallas import tpu as pltpu

F32 = jnp.float32
BF16 = jnp.bfloat16

N_DEV = 8
D_MODEL = 1024
DEPTH = 4
HEAD_DIM = 64
N_META = 16
BLOCK = 128
NEG_INF = -1e30
SWA_Q_HEADS = 8
SWA_KV_HEADS = 2
SWA_GROUP = SWA_Q_HEADS // SWA_KV_HEADS
FOX_HEADS = 8
LRU_WIDTH = D_MODEL // 2
LRU_BLOCKS = 8
LRU_BLOCK_DIM = LRU_WIDTH // LRU_BLOCKS
CONV_WIDTH = 4
LRU_C = 8.0
REL_BUCKETS = 32
REL_MAX_DIST = 128
D_FF = 2816
N_BRANCH = 3
ATT_W = SWA_Q_HEADS * HEAD_DIM
KV_W = SWA_KV_HEADS * HEAD_DIM
SCALE = HEAD_DIM ** -0.5
EPS = 1e-6

_ORIG = (("qa", ATT_W), ("ka", KV_W), ("va", KV_W), ("qf", ATT_W), ("kf", ATT_W), ("vf", ATT_W),
         ("fl", FOX_HEADS), ("xc", LRU_WIDTH), ("yc", LRU_WIDTH), ("gates", N_BRANCH * D_MODEL))
IN_COLS = sum(w for _, w in _ORIG)
_WORK = (("gates", 3072), ("qa", 512), ("qf", 512), ("kf", 512), ("vf", 512), ("xc", 512), ("yc", 512),
         ("ka", 128), ("va", 128), ("fl", 128))
WORK_COLS = sum(w for _, w in _WORK)


def _offsets(table):
    off, out = 0, {}
    for n, w in table:
        out[n] = (off, w)
        off += w
    return out


_ORIG_OFF = _offsets(_ORIG)
_WORK_OFF = _offsets(_WORK)

ADAM_LR = 0.001
ADAM_B1 = 0.9
ADAM_B2 = 0.999
ADAM_EPS = 1e-08
ADAM_WD = 0.01
ADAM_STEP = 10

VMEM_LIMIT = 52 * 1024 * 1024
LANE = 128


def _tile(n, cap, mult=LANE):
    if n <= cap:
        return n
    best = None
    for d in range(mult, cap + 1, mult):
        if n % d == 0:
            best = d
    assert best is not None, (n, cap, mult)
    return best


def _params(sem):
    return pltpu.CompilerParams(dimension_semantics=sem, vmem_limit_bytes=VMEM_LIMIT)


def _sigmoid(x):
    return 1.0 / (1.0 + jnp.exp(-x))


def _log_sigmoid(x):
    return jnp.minimum(x, 0.0) - jnp.log(1.0 + jnp.exp(-jnp.abs(x)))


def _neg_expm1(x):
    series = -x * (1.0 + x * (0.5 + x * (1.0 / 6.0 + x * (1.0 / 24.0 + x * (1.0 / 120.0)))))
    return jnp.where(x > -0.1, series, 1.0 - jnp.exp(x))


_GELU_C = math.sqrt(2.0 / math.pi)


def _gelu(x):
    return 0.5 * x * (1.0 + jnp.tanh(_GELU_C * (x + 0.044715 * x * x * x)))


def _gelu_grad(x):
    t = jnp.tanh(_GELU_C * (x + 0.044715 * x * x * x))
    return 0.5 * (1.0 + t) + 0.5 * x * (1.0 - t * t) * _GELU_C * (1.0 + 3.0 * 0.044715 * x * x)


def _mm(a, b, mode, name, res=None, exact=False, tm_cap=1408, tn_cap=512, tk_cap=1408):
    if mode == "nn":
        (M, K), (K2, N) = a.shape, b.shape
    elif mode == "nt":
        (M, K), (N, K2) = a.shape, b.shape
    else:
        (K, M), (K2, N) = a.shape, b.shape
    assert K == K2, (a.shape, b.shape, mode)
    tm, tn, tk = _tile(M, tm_cap), _tile(N, tn_cap), _tile(K, tk_cap)
    nk = K // tk
    a_spec = {"nn": pl.BlockSpec((tm, tk), lambda i, j, k: (i, k)),
              "nt": pl.BlockSpec((tm, tk), lambda i, j, k: (i, k)),
              "tn": pl.BlockSpec((tk, tm), lambda i, j, k: (k, i))}[mode]
    b_spec = {"nn": pl.BlockSpec((tk, tn), lambda i, j, k: (k, j)),
              "nt": pl.BlockSpec((tn, tk), lambda i, j, k: (j, k)),
              "tn": pl.BlockSpec((tk, tn), lambda i, j, k: (k, j))}[mode]
    o_spec = pl.BlockSpec((tm, tn), lambda i, j, k: (i, j))
    has_res = res is not None

    def body(*refs):
        if has_res:
            a_ref, b_ref, r_ref, o_ref, acc_ref = refs
        else:
            a_ref, b_ref, o_ref, acc_ref = refs
        k = pl.program_id(2)

        @pl.when(k == 0)
        def _():
            acc_ref[...] = jnp.zeros_like(acc_ref)

        x, y = a_ref[...], b_ref[...]
        if exact:
            x, y, prec = x.astype(F32), y.astype(F32), lax.Precision.HIGHEST
        else:
            x, y, prec = x.astype(BF16), y.astype(BF16), None
        if mode == "tn":
            x = x.T
        dims = (((1,), (1,)), ((), ())) if mode == "nt" else (((1,), (0,)), ((), ()))
        acc_ref[...] += lax.dot_general(x, y, dims, precision=prec, preferred_element_type=F32)

        @pl.when(k == nk - 1)
        def _():
            if has_res:
                o_ref[...] = acc_ref[...] + r_ref[...]
            else:
                o_ref[...] = acc_ref[...]

    in_specs = [a_spec, b_spec] + ([o_spec] if has_res else [])
    args = (a, b) + ((res,) if has_res else ())
    return pl.pallas_call(
        body, name=name, grid=(M // tm, N // tn, nk), in_specs=in_specs, out_specs=o_spec,
        out_shape=jax.ShapeDtypeStruct((M, N), F32), scratch_shapes=[pltpu.VMEM((tm, tn), F32)],
        compiler_params=_params(("parallel", "parallel", "arbitrary")))(*args)


def _rowwise(fn, name, n_rows, tr, row_ins, const_ins, row_outs, red_outs=()):
    n_row_in, n_const, n_row_out, n_red = len(row_ins), len(const_ins), len(row_outs), len(red_outs)
    in_specs = []
    for arr, w, idx in row_ins:
        width = arr.shape[1] if w is None else w
        in_specs.append(pl.BlockSpec((tr, width), functools.partial(lambda i, j: (i, j), j=idx)))
    for c in const_ins:
        in_specs.append(pl.BlockSpec(c.shape, lambda i: (0, 0)))
    out_shape = [jax.ShapeDtypeStruct((n_rows, c), dt) for c, dt in row_outs]
    out_shape += [jax.ShapeDtypeStruct((1, c), F32) for c in red_outs]
    out_specs = [pl.BlockSpec((tr, c), lambda i: (i, 0)) for c, _ in row_outs]
    out_specs += [pl.BlockSpec((1, c), lambda i: (0, 0)) for c in red_outs]

    def body(*refs):
        i = pl.program_id(0)
        rows = [r[...] for r in refs[:n_row_in]]
        consts = [r[...] for r in refs[n_row_in:n_row_in + n_const]]
        outs = refs[n_row_in + n_const:]
        row_vals, red_vals = fn(i, rows, consts)
        for k in range(n_row_out):
            outs[k][...] = row_vals[k].astype(outs[k].dtype)
        if n_red:
            @pl.when(i == 0)
            def _():
                for k in range(n_red):
                    outs[n_row_out + k][...] = jnp.zeros_like(outs[n_row_out + k])

            for k in range(n_red):
                outs[n_row_out + k][...] += red_vals[k]

    res = pl.pallas_call(
        body, name=name, grid=(n_rows // tr,), in_specs=in_specs, out_specs=out_specs, out_shape=out_shape,
        compiler_params=_params(("arbitrary",)))(*[a for a, _, _ in row_ins], *const_ins)
    return res


def _row_ids(i, tr):
    return i * tr + lax.broadcasted_iota(jnp.int32, (tr, 1), 0)


def _colsum(x):
    return jnp.sum(x, axis=0, keepdims=True)


def _scan(a, b, name, b2=None):
    n_rows, c = b.shape
    tr = _tile(n_rows, 384)
    has_a = a is not None
    has_b2 = b2 is not None

    def body(*refs):
        o_ref, carry = refs[-2:]
        ins = list(refs[:-2])
        a_ref = ins.pop(0) if has_a else None
        b_ref = ins.pop(0)

        @pl.when(pl.program_id(0) == 0)
        def _():
            carry[...] = jnp.zeros_like(carry)

        rows = lax.broadcasted_iota(jnp.int32, (tr, c), 0)
        bv = b_ref[...] + ins[0][...] if has_b2 else b_ref[...]
        av = a_ref[...] if has_a else None
        s = 1
        while s < tr:
            keep = rows >= s
            b_sh = jnp.where(keep, pltpu.roll(bv, s, 0), 0.0)
            if has_a:
                a_sh = jnp.where(keep, pltpu.roll(av, s, 0), 1.0)
                bv = av * b_sh + bv
                av = av * a_sh
            else:
                bv = b_sh + bv
            s *= 2
        h = av * carry[0:1, :] + bv if has_a else carry[0:1, :] + bv
        o_ref[...] = h
        carry[...] = jnp.broadcast_to(h[tr - 1:tr, :], carry.shape)

    spec = pl.BlockSpec((tr, c), lambda i: (i, 0))
    args = ((a,) if has_a else ()) + (b,) + ((b2,) if has_b2 else ())
    return pl.pallas_call(
        body, name=name, grid=(n_rows // tr,), in_specs=[spec] * len(args), out_specs=spec,
        out_shape=jax.ShapeDtypeStruct((n_rows, c), F32), scratch_shapes=[pltpu.VMEM((8, c), F32)],
        compiler_params=_params(("arbitrary",)))(*args)


def _shift_down(x, s, fill=0.0):
    pad = jnp.full((s,) + x.shape[1:], fill, x.dtype)
    return jnp.concatenate([pad, x[:-s]], axis=0)


def _shift_up(x, s):
    pad = jnp.zeros((s,) + x.shape[1:], x.dtype)
    return jnp.concatenate([x[s:], pad], axis=0)


def _rev_scan(a, g, name, g2=None):
    gf = jnp.flip(g, axis=0)
    g2f = None if g2 is None else jnp.flip(g2, axis=0)
    af = None if a is None else _shift_down(jnp.flip(a, axis=0), 1, 1.0)
    return jnp.flip(_scan(af, gf, name, b2=g2f), axis=0)


def _swa_masks(blk, n_pad):
    qi = lax.broadcasted_iota(jnp.int32, (BLOCK, 2 * BLOCK), 0)
    ki = lax.broadcasted_iota(jnp.int32, (BLOCK, 2 * BLOCK), 1)
    dist = qi + BLOCK - ki
    key_abs = (blk - 1) * BLOCK + ki
    return (dist >= 0) & (dist < BLOCK) & (key_abs >= n_pad)


def _swa_fwd(q, k, v, bias, sinks, n_pad):
    H, Tp, dh = q.shape
    tq = _tile(Tp, 384)
    nsub = tq // BLOCK
    nq = Tp // tq

    def body(q_ref, kp_ref, kc_ref, vp_ref, vc_ref, b_ref, s_ref, o_ref, lse_ref):
        i = pl.program_id(1)
        kstrip = jnp.concatenate([kp_ref[0], kc_ref[0]], axis=0).astype(BF16)
        vstrip = jnp.concatenate([vp_ref[0], vc_ref[0]], axis=0).astype(BF16)
        sink = s_ref[0, :, 0:1]
        bias_h = b_ref[0]
        for j in range(nsub):
            qj = q_ref[0, j * BLOCK:(j + 1) * BLOCK, :].astype(BF16)
            kk = kstrip[j * BLOCK:(j + 2) * BLOCK]
            vv = vstrip[j * BLOCK:(j + 2) * BLOCK]
            s = lax.dot_general(qj, kk, (((1,), (1,)), ((), ())), preferred_element_type=F32) * SCALE
            mask = _swa_masks(i * nsub + j, n_pad)
            s = jnp.where(mask, s + bias_h, NEG_INF)
            m = jnp.maximum(jnp.max(s, axis=1, keepdims=True), sink)
            p = jnp.exp(s - m)
            denom = jnp.sum(p, axis=1, keepdims=True) + jnp.exp(sink - m)
            p = (p / denom).astype(BF16)
            o_ref[0, j * BLOCK:(j + 1) * BLOCK, :] = jnp.dot(p, vv, preferred_element_type=F32)
            lse_ref[0, j * BLOCK:(j + 1) * BLOCK, :] = m + jnp.log(denom)

    prev = lambda h, i: (h // SWA_GROUP, jnp.maximum(i * nsub - 1, 0), 0)
    cur = lambda h, i: (h // SWA_GROUP, i, 0)
    return pl.pallas_call(
        body, name="swa_fwd", grid=(H, nq),
        in_specs=[pl.BlockSpec((1, tq, dh), lambda h, i: (h, i, 0)),
                  pl.BlockSpec((1, BLOCK, dh), prev), pl.BlockSpec((1, tq, dh), cur),
                  pl.BlockSpec((1, BLOCK, dh), prev), pl.BlockSpec((1, tq, dh), cur),
                  pl.BlockSpec((1, BLOCK, 2 * BLOCK), lambda h, i: (h, 0, 0)),
                  pl.BlockSpec((1, 1, LANE), lambda h, i: (h, 0, 0))],
        out_specs=[pl.BlockSpec((1, tq, dh), lambda h, i: (h, i, 0)),
                   pl.BlockSpec((1, tq, 1), lambda h, i: (h, i, 0))],
        out_shape=[jax.ShapeDtypeStruct((H, Tp, dh), F32), jax.ShapeDtypeStruct((H, Tp, 1), F32)],
        compiler_params=_params(("parallel", "arbitrary")))(q, k, k, v, v, bias, sinks)


def _swa_bwd(q, k, v, bias, sinks, o, lse, do, n_pad):
    H, Tp, dh = q.shape
    tq = _tile(Tp, 384)
    nsub = tq // BLOCK
    nq = Tp // tq
    strip = BLOCK + tq

    def body(q_ref, kp_ref, kc_ref, vp_ref, vc_ref, b_ref, s_ref, o_ref, lse_ref, do_ref,
             dq_ref, dk_ref, dv_ref, db_ref, ds_ref):
        i = pl.program_id(1)

        @pl.when(i == 0)
        def _():
            db_ref[...] = jnp.zeros_like(db_ref)
            ds_ref[...] = jnp.zeros_like(ds_ref)

        dk_ref[...] = jnp.zeros_like(dk_ref)
        dv_ref[...] = jnp.zeros_like(dv_ref)
        kstrip = jnp.concatenate([kp_ref[0], kc_ref[0]], axis=0).astype(BF16)
        vstrip = jnp.concatenate([vp_ref[0], vc_ref[0]], axis=0).astype(BF16)
        sink = s_ref[0, :, 0:1]
        bias_h = b_ref[0]
        for j in range(nsub):
            rows = slice(j * BLOCK, (j + 1) * BLOCK)
            keys = slice(j * BLOCK, (j + 2) * BLOCK)
            qj = q_ref[0, rows, :].astype(BF16)
            doj = do_ref[0, rows, :]
            lsej = lse_ref[0, rows, :]
            delta = jnp.sum(doj * o_ref[0, rows, :], axis=1, keepdims=True)
            doj = doj.astype(BF16)
            kk, vv = kstrip[keys], vstrip[keys]
            s = lax.dot_general(qj, kk, (((1,), (1,)), ((), ())), preferred_element_type=F32) * SCALE
            mask = _swa_masks(i * nsub + j, n_pad)
            s = jnp.where(mask, s + bias_h, NEG_INF)
            p = jnp.exp(s - lsej)
            p_sink = jnp.exp(sink - lsej)
            dp = lax.dot_general(doj, vv, (((1,), (1,)), ((), ())), preferred_element_type=F32)
            ds = p * (dp - delta)
            db_ref[0] += ds
            ds_ref[0] += jnp.broadcast_to(-jnp.sum(p_sink * delta, axis=0, keepdims=True), (1, LANE))
            dsb = ds.astype(BF16)
            dq_ref[0, rows, :] = jnp.dot(dsb, kk, preferred_element_type=F32) * SCALE
            dk_ref[0, 0, keys, :] += jnp.dot(ds.T.astype(BF16), qj, preferred_element_type=F32) * SCALE
            dv_ref[0, 0, keys, :] += jnp.dot(p.T.astype(BF16), doj, preferred_element_type=F32)

    prev = lambda h, i: (h // SWA_GROUP, jnp.maximum(i * nsub - 1, 0), 0)
    cur = lambda h, i: (h // SWA_GROUP, i, 0)
    qspec = pl.BlockSpec((1, tq, dh), lambda h, i: (h, i, 0))
    cspec = pl.BlockSpec((1, tq, 1), lambda h, i: (h, i, 0))
    sspec = pl.BlockSpec((1, 1, strip, dh), lambda h, i: (h, i, 0, 0))
    return pl.pallas_call(
        body, name="swa_bwd", grid=(H, nq),
        in_specs=[qspec, pl.BlockSpec((1, BLOCK, dh), prev), pl.BlockSpec((1, tq, dh), cur),
                  pl.BlockSpec((1, BLOCK, dh), prev), pl.BlockSpec((1, tq, dh), cur),
                  pl.BlockSpec((1, BLOCK, 2 * BLOCK), lambda h, i: (h, 0, 0)),
                  pl.BlockSpec((1, 1, LANE), lambda h, i: (h, 0, 0)), qspec, cspec, qspec],
        out_specs=[qspec, sspec, sspec, pl.BlockSpec((1, BLOCK, 2 * BLOCK), lambda h, i: (h, 0, 0)),
                   pl.BlockSpec((1, 1, LANE), lambda h, i: (h, 0, 0))],
        out_shape=[jax.ShapeDtypeStruct((H, Tp, dh), F32), jax.ShapeDtypeStruct((H, nq, strip, dh), F32),
                   jax.ShapeDtypeStruct((H, nq, strip, dh), F32),
                   jax.ShapeDtypeStruct((H, BLOCK, 2 * BLOCK), F32), jax.ShapeDtypeStruct((H, 1, LANE), F32)],
        compiler_params=_params(("parallel", "arbitrary")))(q, k, k, v, v, bias, sinks, o, lse, do)


def _fox_scores(qb, kk, cq, ck, qi, kb, tq, n_pad):
    s = lax.dot_general(qb, kk, (((1,), (1,)), ((), ())), preferred_element_type=F32) * SCALE
    s = s + cq - ck
    qpos = qi * tq + lax.broadcasted_iota(jnp.int32, (tq, tq), 0)
    kpos = kb * tq + lax.broadcasted_iota(jnp.int32, (tq, tq), 1)
    return jnp.where((kpos <= qpos) & (kpos >= n_pad), s, NEG_INF)


def _fox_fwd(q, k, v, ccol, crow, n_pad, tq):
    H, Tp, dh = q.shape
    nq = Tp // tq
    k4, v4 = k.reshape(H, nq, tq, dh), v.reshape(H, nq, tq, dh)

    def body(q_ref, k_ref, v_ref, cc_ref, cr_ref, o_ref, lse_ref):
        qi = pl.program_id(1)
        qb = q_ref[0].astype(BF16)
        cq = cc_ref[0]

        def step(kb, carry):
            m, l, acc = carry
            s = _fox_scores(qb, k_ref[0, kb].astype(BF16), cq, cr_ref[0, kb], qi, kb, tq, n_pad)
            m_new = jnp.maximum(m, jnp.max(s, axis=1, keepdims=True))
            alpha = jnp.exp(m - m_new)
            p = jnp.exp(s - m_new)
            l = alpha * l + jnp.sum(p, axis=1, keepdims=True)
            acc = alpha * acc + jnp.dot(p.astype(BF16), v_ref[0, kb].astype(BF16), preferred_element_type=F32)
            return m_new, l, acc

        init = (jnp.full((tq, 1), NEG_INF, F32), jnp.zeros((tq, 1), F32), jnp.zeros((tq, dh), F32))
        m, l, acc = lax.fori_loop(0, qi + 1, step, init)
        o_ref[0] = acc / l
        lse_ref[0] = m + jnp.log(l)

    qspec = pl.BlockSpec((1, tq, dh), lambda h, i: (h, i, 0))
    cspec = pl.BlockSpec((1, tq, 1), lambda h, i: (h, i, 0))
    full4 = pl.BlockSpec((1, nq, tq, dh), lambda h, i: (h, 0, 0, 0))
    return pl.pallas_call(
        body, name="fox_fwd", grid=(H, nq),
        in_specs=[qspec, full4, full4, cspec, pl.BlockSpec((1, nq, 1, tq), lambda h, i: (h, 0, 0, 0))],
        out_specs=[qspec, cspec],
        out_shape=[jax.ShapeDtypeStruct((H, Tp, dh), F32), jax.ShapeDtypeStruct((H, Tp, 1), F32)],
        compiler_params=_params(("parallel", "arbitrary")))(q, k4, v4, ccol, crow)


def _fox_bwd_dq(q, k, v, ccol, crow, o, lse, do, n_pad, tq):
    H, Tp, dh = q.shape
    nq = Tp // tq
    k4, v4 = k.reshape(H, nq, tq, dh), v.reshape(H, nq, tq, dh)

    def body(q_ref, k_ref, v_ref, cc_ref, cr_ref, o_ref, lse_ref, do_ref, dq_ref, dl_ref, dc_ref):
        qi = pl.program_id(1)
        qb = q_ref[0].astype(BF16)
        cq = cc_ref[0]
        dof = do_ref[0]
        delta = jnp.sum(dof * o_ref[0], axis=1, keepdims=True)
        dob = dof.astype(BF16)
        lse = lse_ref[0]

        def step(kb, carry):
            dq, dc = carry
            kk = k_ref[0, kb].astype(BF16)
            s = _fox_scores(qb, kk, cq, cr_ref[0, kb], qi, kb, tq, n_pad)
            p = jnp.exp(s - lse)
            dp = lax.dot_general(dob, v_ref[0, kb].astype(BF16), (((1,), (1,)), ((), ())),
                                 preferred_element_type=F32)
            ds = p * (dp - delta)
            dq = dq + jnp.dot(ds.astype(BF16), kk, preferred_element_type=F32)
            return dq, dc + jnp.sum(ds, axis=1, keepdims=True)

        dq, dc = lax.fori_loop(0, qi + 1, step, (jnp.zeros((tq, dh), F32), jnp.zeros((tq, 1), F32)))
        dq_ref[0] = dq * SCALE
        dl_ref[0] = delta
        dc_ref[0] = dc

    qspec = pl.BlockSpec((1, tq, dh), lambda h, i: (h, i, 0))
    cspec = pl.BlockSpec((1, tq, 1), lambda h, i: (h, i, 0))
    full4 = pl.BlockSpec((1, nq, tq, dh), lambda h, i: (h, 0, 0, 0))
    return pl.pallas_call(
        body, name="fox_bwd_dq", grid=(H, nq),
        in_specs=[qspec, full4, full4, cspec, pl.BlockSpec((1, nq, 1, tq), lambda h, i: (h, 0, 0, 0)),
                  qspec, cspec, qspec],
        out_specs=[qspec, cspec, cspec],
        out_shape=[jax.ShapeDtypeStruct((H, Tp, dh), F32), jax.ShapeDtypeStruct((H, Tp, 1), F32),
                   jax.ShapeDtypeStruct((H, Tp, 1), F32)],
        compiler_params=_params(("parallel", "arbitrary")))(q, k4, v4, ccol, crow, o, lse, do)


def _fox_bwd_dkv(q, k, v, ccol, crow, lse, delta, do, n_pad, tq):
    H, Tp, dh = q.shape
    nq = Tp // tq
    q4, do4 = q.reshape(H, nq, tq, dh), do.reshape(H, nq, tq, dh)
    cc4, lse4, dl4 = (x.reshape(H, nq, tq, 1) for x in (ccol, lse, delta))

    def body(q_ref, do_ref, cc_ref, lse_ref, dl_ref, k_ref, v_ref, cr_ref, dk_ref, dv_ref, dc_ref):
        ki = pl.program_id(1)
        kk = k_ref[0].astype(BF16)
        vv = v_ref[0].astype(BF16)
        ck = cr_ref[0, 0]

        def step(qb_i, carry):
            dk, dv, dc = carry
            qb = q_ref[0, qb_i].astype(BF16)
            dob = do_ref[0, qb_i].astype(BF16)
            s = _fox_scores(qb, kk, cc_ref[0, qb_i], ck, qb_i, ki, tq, n_pad)
            p = jnp.exp(s - lse_ref[0, qb_i])
            dp = lax.dot_general(dob, vv, (((1,), (1,)), ((), ())), preferred_element_type=F32)
            ds = p * (dp - dl_ref[0, qb_i])
            dv = dv + jnp.dot(p.T.astype(BF16), dob, preferred_element_type=F32)
            dk = dk + jnp.dot(ds.T.astype(BF16), qb, preferred_element_type=F32)
            dc = dc - jnp.sum(ds, axis=0, keepdims=True)
            return dk, dv, dc

        init = (jnp.zeros((tq, dh), F32), jnp.zeros((tq, dh), F32), jnp.zeros((1, tq), F32))
        dk, dv, dc = lax.fori_loop(ki, nq, step, init)
        dk_ref[0] = dk * SCALE
        dv_ref[0] = dv
        dc_ref[0, 0] = dc

    kspec = pl.BlockSpec((1, tq, dh), lambda h, i: (h, i, 0))
    full4 = pl.BlockSpec((1, nq, tq, dh), lambda h, i: (h, 0, 0, 0))
    full1 = pl.BlockSpec((1, nq, tq, 1), lambda h, i: (h, 0, 0, 0))
    rspec = pl.BlockSpec((1, 1, 1, tq), lambda h, i: (h, i, 0, 0))
    return pl.pallas_call(
        body, name="fox_bwd_dkv", grid=(H, nq),
        in_specs=[full4, full4, full1, full1, full1, kspec, kspec, rspec],
        out_specs=[kspec, kspec, rspec],
        out_shape=[jax.ShapeDtypeStruct((H, Tp, dh), F32), jax.ShapeDtypeStruct((H, Tp, dh), F32),
                   jax.ShapeDtypeStruct((H, nq, 1, tq), F32)],
        compiler_params=_params(("parallel", "arbitrary")))(q4, do4, cc4, lse4, dl4, k, v, crow)


def _exchange(src, gather, name):
    blk = src.shape if gather else src.shape[1:]

    def body(src_ref, out_ref, send_sems, recv_sems, local_sem):
        x, y, c = lax.axis_index("x"), lax.axis_index("y"), lax.axis_index("c")
        me = 4 * x + 2 * y + c

        def peer(r):
            return (x ^ ((r >> 2) & 1), y ^ ((r >> 1) & 1), c ^ (r & 1))

        def copy(r):
            px, py, pc = peer(r)
            pid = 4 * px + 2 * py + pc
            return pltpu.make_async_remote_copy(
                src_ref=src_ref if gather else src_ref.at[pid], dst_ref=out_ref.at[me],
                send_sem=send_sems.at[r - 1], recv_sem=recv_sems.at[r - 1],
                device_id=(px, py, pc), device_id_type=pl.DeviceIdType.MESH)

        def landing(r):
            px, py, pc = peer(r)
            pid = 4 * px + 2 * py + pc
            return pltpu.make_async_remote_copy(
                src_ref=src_ref if gather else src_ref.at[pid], dst_ref=out_ref.at[pid],
                send_sem=send_sems.at[r - 1], recv_sem=recv_sems.at[r - 1],
                device_id=(px, py, pc), device_id_type=pl.DeviceIdType.MESH)

        mine = pltpu.make_async_copy(src_ref if gather else src_ref.at[me], out_ref.at[me], local_sem)
        mine.start()
        sends = [copy(r) for r in range(1, N_DEV)]
        for cp in sends:
            cp.start()
        for r in range(1, N_DEV):
            landing(r).wait_recv()
        for cp in sends:
            cp.wait_send()
        mine.wait()

    any_spec = pl.BlockSpec(memory_space=pl.ANY)
    return pl.pallas_call(
        body, name=name, in_specs=[any_spec], out_specs=any_spec,
        out_shape=jax.ShapeDtypeStruct((N_DEV,) + tuple(blk), src.dtype),
        scratch_shapes=[pltpu.SemaphoreType.DMA((N_DEV - 1,)), pltpu.SemaphoreType.DMA((N_DEV - 1,)),
                        pltpu.SemaphoreType.DMA])(src)


def _adam(gparts, w, m, v, name):
    P, R, C = gparts.shape
    tr = _tile(R, 256, 8)

    def body(g_ref, w_ref, m_ref, v_ref, go_ref, d_ref, mo_ref, vo_ref):
        g = g_ref[0]
        for p in range(1, P):
            g = g + g_ref[p]
        m2 = ADAM_B1 * m_ref[...] + (1.0 - ADAM_B1) * g
        v2 = ADAM_B2 * v_ref[...] + (1.0 - ADAM_B2) * (g * g)
        m_hat = m2 / (1.0 - ADAM_B1 ** ADAM_STEP)
        v_hat = v2 / (1.0 - ADAM_B2 ** ADAM_STEP)
        go_ref[...] = g
        d_ref[...] = -ADAM_LR * (m_hat / (jnp.sqrt(v_hat) + ADAM_EPS) + ADAM_WD * w_ref[...])
        mo_ref[...] = m2
        vo_ref[...] = v2

    spec = pl.BlockSpec((tr, C), lambda i: (i, 0))
    shp = jax.ShapeDtypeStruct((R, C), F32)
    return pl.pallas_call(
        body, name=name, grid=(R // tr,),
        in_specs=[pl.BlockSpec((P, tr, C), lambda i: (0, i, 0)), spec, spec, spec],
        out_specs=[spec] * 4, out_shape=[shp] * 4, compiler_params=_params(("parallel",)))(gparts, w, m, v)


def _sum_parts(gparts, name):
    P, R, C = gparts.shape
    tr = _tile(R, 256, 8)

    def body(g_ref, o_ref):
        g = g_ref[0]
        for p in range(1, P):
            g = g + g_ref[p]
        o_ref[...] = g

    return pl.pallas_call(
        body, name=name, grid=(R // tr,), in_specs=[pl.BlockSpec((P, tr, C), lambda i: (0, i, 0))],
        out_specs=pl.BlockSpec((tr, C), lambda i: (i, 0)), out_shape=jax.ShapeDtypeStruct((R, C), F32),
        compiler_params=_params(("parallel",)))(gparts)


def _pack(pieces, width, row_mult):
    flat = jnp.concatenate([p.reshape(-1) for p in pieces])
    n = flat.shape[0]
    rows = -(-n // width)
    rows = -(-rows // row_mult) * row_mult
    return jnp.pad(flat, (0, rows * width - n)).reshape(rows, width)


def _pack_lead(pieces, width, row_mult):
    lead = pieces[0].shape[0]
    flat = jnp.concatenate([p.reshape(lead, -1) for p in pieces], axis=1)
    n = flat.shape[1]
    rows = -(-n // width)
    rows = -(-rows // row_mult) * row_mult
    return jnp.pad(flat, ((0, 0), (0, rows * width - n))).reshape(lead, rows, width)


def _unpack(flat2d, shapes, lead=None):
    out, off = [], 0
    if lead is None:
        flat = flat2d.reshape(-1)
        for s in shapes:
            n = int(np.prod(s))
            out.append(flat[off:off + n].reshape(s))
            off += n
    else:
        flat = flat2d.reshape(lead, -1)
        for s in shapes:
            n = int(np.prod(s))
            out.append(flat[:, off:off + n].reshape((lead,) + tuple(s)))
            off += n
    return out


def _to_work_cols(w):
    parts = []
    for n, width in _WORK:
        off, ow = _ORIG_OFF[n]
        seg = w[..., off:off + ow]
        if ow < width:
            seg = jnp.pad(seg, [(0, 0)] * (w.ndim - 1) + [(0, width - ow)])
        parts.append(seg)
    return jnp.concatenate(parts, axis=-1)


def _to_orig_cols(w):
    parts = []
    for n, ow in _ORIG:
        off, _ = _WORK_OFF[n]
        parts.append(w[..., off:off + ow])
    return jnp.concatenate(parts, axis=-1)


def _heads(x, n):
    return x.reshape(x.shape[0], n, HEAD_DIM).transpose(1, 0, 2)


def _unheads(x):
    return x.transpose(1, 0, 2).reshape(x.shape[1], x.shape[0] * HEAD_DIM)


def _t5_bucket_np(dist):
    max_exact = REL_BUCKETS // 2
    d = np.maximum(dist, 0)
    scaled = (np.log(np.maximum(d, 1).astype(np.float32) / np.float32(max_exact))
              / np.float32(math.log(REL_MAX_DIST / max_exact))).astype(np.float32)
    large = np.minimum(max_exact + (scaled * np.float32(REL_BUCKETS - max_exact)).astype(np.int32), REL_BUCKETS - 1)
    return np.where(d < max_exact, d, large)


def _bucket_onehot():
    q_idx = np.arange(BLOCK)[:, None]
    k_idx = np.arange(2 * BLOCK)[None, :]
    bucket = _t5_bucket_np(q_idx + BLOCK - k_idx).reshape(-1)
    oh = np.zeros((LANE, BLOCK * 2 * BLOCK), np.float32)
    oh[bucket, np.arange(bucket.shape[0])] = 1.0
    return oh


def kernel(x, meta_tokens, rel_bias_table, norm_mix, w_in, swa_sinks, fox_forget_bias, conv_w, conv_b, lru_w_r, lru_b_r, lru_w_i, lru_b_i, lru_lambda, w_branch, w_out, norm_ffn, w_ffn_in, w_ffn_out, norm_final, loss_target, m_meta_tokens, m_rel_bias_table, m_norm_mix, m_w_in, m_swa_sinks, m_fox_forget_bias, m_conv_w, m_conv_b, m_lru_w_r, m_lru_b_r, m_lru_w_i, m_lru_b_i, m_lru_lambda, m_w_branch, m_w_out, m_norm_ffn, m_w_ffn_in, m_w_ffn_out, m_norm_final, v_meta_tokens, v_rel_bias_table, v_norm_mix, v_w_in, v_swa_sinks, v_fox_forget_bias, v_conv_w, v_conv_b, v_lru_w_r, v_lru_b_r, v_lru_w_i, v_lru_b_i, v_lru_lambda, v_w_branch, v_w_out, v_norm_ffn, v_w_ffn_in, v_w_ffn_out, v_norm_final):
    S = x.shape[1]
    T = N_META + S
    n_pad = (-T) % BLOCK
    Tp = T + n_pad
    first = n_pad + N_META
    TR = _tile(Tp, 384)
    TQ = _tile(Tp, 384)
    NQ = Tp // TQ
    me = 4 * lax.axis_index("x") + 2 * lax.axis_index("y") + lax.axis_index("c")

    big_names = ("w_in", "w_ffn_in", "w_ffn_out", "w_branch", "w_out")
    big_w = dict(w_in=w_in, w_ffn_in=w_ffn_in, w_ffn_out=w_ffn_out, w_branch=w_branch, w_out=w_out)
    big_m = dict(w_in=m_w_in, w_ffn_in=m_w_ffn_in, w_ffn_out=m_w_ffn_out, w_branch=m_w_branch, w_out=m_w_out)
    big_v = dict(w_in=v_w_in, w_ffn_in=v_w_ffn_in, w_ffn_out=v_w_ffn_out, w_branch=v_w_branch, w_out=v_w_out)
    big_shapes = [big_w[n].shape for n in big_names]
    wpack = _pack([big_w[n].astype(BF16) for n in big_names], D_MODEL, 256)
    wall = _exchange(wpack, True, "gather_weights")
    g_w_in, g_w_ffn_in, g_w_ffn_out, g_w_branch, g_w_out = _unpack(wall, big_shapes, lead=N_DEV)
    W_in = _to_work_cols(g_w_in.transpose(1, 2, 0, 3).reshape(DEPTH, D_MODEL, IN_COLS))
    W_ffn_in = g_w_ffn_in.transpose(1, 2, 0, 3).reshape(DEPTH, D_MODEL, 2 * D_FF)
    W_ffn_out = g_w_ffn_out.transpose(1, 0, 2, 3).reshape(DEPTH, D_FF, D_MODEL)
    W_branch = g_w_branch.transpose(1, 2, 3, 0, 4).reshape(DEPTH, N_BRANCH, LRU_WIDTH, D_MODEL)
    W_out = g_w_out.transpose(1, 0, 2, 3).reshape(DEPTH, D_MODEL, D_MODEL)

    small_sh = _exchange(_pack([meta_tokens, conv_w], D_MODEL, 8), True, "gather_small")
    g_meta, g_conv_w = _unpack(small_sh, [meta_tokens.shape, conv_w.shape], lead=N_DEV)
    meta_full = g_meta.transpose(1, 0, 2).reshape(N_META, D_MODEL)
    conv_w_full = g_conv_w.transpose(1, 2, 0, 3).reshape(DEPTH, CONV_WIDTH, LRU_WIDTH)

    onehot = jnp.asarray(_bucket_onehot())
    table_t = jnp.pad(rel_bias_table.T, ((0, 0), (0, LANE - REL_BUCKETS)))
    bias = _mm(table_t, onehot, "nn", "swa_bias", exact=True, tn_cap=4096).reshape(SWA_Q_HEADS, BLOCK, 2 * BLOCK)

    def dense_blocks(w):
        rows = []
        for b in range(LRU_BLOCKS):
            rows.append(jnp.pad(w[b], ((0, 0), (b * LRU_BLOCK_DIM, (LRU_BLOCKS - 1 - b) * LRU_BLOCK_DIM))))
        return jnp.concatenate(rows, axis=0)

    h = jnp.concatenate([jnp.zeros((n_pad, D_MODEL), F32), meta_full, x[0]], axis=0)
    saved = []

    for l in range(DEPTH):
        sv = {"h": h}
        g_mix = norm_mix[l][None, :]

        def norm_fwd(i, rows, consts):
            xx = rows[0]
            return [xx * lax.rsqrt(jnp.mean(xx * xx, axis=1, keepdims=True) + EPS) * consts[0]], []

        (u,) = _rowwise(norm_fwd, "norm_fwd", Tp, TR, [(h, None, 0)], [g_mix], [(D_MODEL, F32)])
        proj = _mm(u, W_in[l], "nn", "proj", tn_cap=384)
        sv["u"], sv["proj"] = u, proj

        def col(name):
            off, w = _WORK_OFF[name]
            return (proj, w, off // w)

        def cols(name):
            off, w = _WORK_OFF[name]
            return proj[:, off:off + w]

        qa, ka, va = _heads(cols("qa"), 8), _heads(cols("ka"), 2), _heads(cols("va"), 2)
        sinks_l = jnp.broadcast_to(swa_sinks[l][:, None, None], (SWA_Q_HEADS, 1, LANE))
        o_a_h, lse_a = _swa_fwd(qa, ka, va, bias, sinks_l, n_pad)
        o_a = _unheads(o_a_h)
        sv.update(qa=qa, ka=ka, va=va, sinks=sinks_l, o_a_h=o_a_h, lse_a=lse_a, o_a=o_a)

        fbias = jnp.pad(fox_forget_bias[l], (0, LANE - FOX_HEADS))[None, :]

        def logf_fwd(i, rows, consts):
            return [_log_sigmoid(rows[0] + consts[0])], []

        (logf,) = _rowwise(logf_fwd, "logf_fwd", Tp, TR, [col("fl")], [fbias], [(LANE, F32)])
        cum = _scan(None, logf, "cumsum")
        cum_h = cum[:, :FOX_HEADS].T
        ccol = cum_h[:, :, None]
        crow = cum_h.reshape(FOX_HEADS, NQ, 1, TQ)
        qf, kf, vf = _heads(cols("qf"), 8), _heads(cols("kf"), 8), _heads(cols("vf"), 8)
        o_f_h, lse_f = _fox_fwd(qf, kf, vf, ccol, crow, n_pad, TQ)
        o_f = _unheads(o_f_h)
        sv.update(fbias=fbias, qf=qf, kf=kf, vf=vf, ccol=ccol, crow=crow, o_f_h=o_f_h, lse_f=lse_f, o_f=o_f)

        xc = cols("xc")
        xs = [_shift_down(xc, CONV_WIDTH - 1 - i) for i in range(CONV_WIDTH - 1)] + [xc]
        cw = conv_w_full[l]
        cb = conv_b[l][None, :]

        def conv_fwd(i, rows, consts):
            w, b = consts
            acc = rows[0] * w[0:1, :]
            for t in range(1, CONV_WIDTH):
                acc = acc + rows[t] * w[t:t + 1, :]
            return [acc + b], []

        (xconv,) = _rowwise(conv_fwd, "conv_fwd", Tp, TR, [(a, None, 0) for a in xs], [cw, cb], [(LRU_WIDTH, F32)])
        w_ri = jnp.concatenate([dense_blocks(lru_w_r[l]), dense_blocks(lru_w_i[l])], axis=1)
        pre = _mm(xconv, w_ri, "nn", "lru_gates")
        lru_consts = [lru_b_r[l][None, :], lru_b_i[l][None, :], lru_lambda[l][None, :]]

        def lru_fwd(i, rows, consts):
            pr, xv = rows
            b_r, b_i, lam = consts
            r = _sigmoid(pr[:, :LRU_WIDTH] + b_r)
            gi = _sigmoid(pr[:, LRU_WIDTH:] + b_i)
            log_a = LRU_C * r * _log_sigmoid(lam)
            valid = (_row_ids(i, TR) >= n_pad).astype(F32)
            inp = jnp.sqrt(_neg_expm1(2.0 * log_a)) * (gi * xv) * valid
            return [jnp.exp(log_a), inp], []

        a_dec, inp = _rowwise(lru_fwd, "lru_fwd", Tp, TR, [(pre, None, 0), (xconv, None, 0)], lru_consts,
                              [(LRU_WIDTH, F32), (LRU_WIDTH, F32)])
        hs = _scan(a_dec, inp, "lru_scan")

        def oc_fwd(i, rows, consts):
            return [rows[0] * _gelu(rows[1])], []

        (o_c,) = _rowwise(oc_fwd, "oc_fwd", Tp, TR, [(hs, None, 0), col("yc")], [], [(LRU_WIDTH, F32)])
        sv.update(xs=xs, cw=cw, xconv=xconv, w_ri=w_ri, pre=pre, lru_consts=lru_consts, a_dec=a_dec, hs=hs, o_c=o_c)

        ba = _mm(o_a, W_branch[l, 0], "nn", "branch")
        bf = _mm(o_f, W_branch[l, 1], "nn", "branch")
        bc = _mm(o_c, W_branch[l, 2], "nn", "branch")

        def merge_fwd(i, rows, consts):
            g, b0, b1, b2 = rows
            valid = (_row_ids(i, BLOCK) >= n_pad).astype(F32)
            mg = (_sigmoid(g[:, :D_MODEL]) * b0 + _sigmoid(g[:, D_MODEL:2 * D_MODEL]) * b1
                  + _sigmoid(g[:, 2 * D_MODEL:]) * b2)
            return [mg * valid], []

        (merged,) = _rowwise(merge_fwd, "merge_fwd", Tp, BLOCK,
                             [col("gates"), (ba, None, 0), (bf, None, 0), (bc, None, 0)], [], [(D_MODEL, F32)])
        h1 = _mm(merged, W_out[l], "nn", "out_proj", res=h)
        sv.update(ba=ba, bf=bf, bc=bc, merged=merged, h1=h1)

        g_ffn = norm_ffn[l][None, :]
        (u2,) = _rowwise(norm_fwd, "norm_fwd", Tp, TR, [(h1, None, 0)], [g_ffn], [(D_MODEL, F32)])
        ff = _mm(u2, W_ffn_in[l], "nn", "ffn_in")

        def act_fwd(i, rows, consts):
            gate, up = rows
            return [gate * _sigmoid(gate) * up], []

        (act,) = _rowwise(act_fwd, "act_fwd", Tp, TR, [(ff, D_FF, 0), (ff, D_FF, 1)], [], [(D_FF, F32)])
        h = _mm(act, W_ffn_out[l], "nn", "ffn_out", res=h1)
        sv.update(u2=u2, ff=ff, act=act)
        saved.append(sv)

    tgt = jnp.concatenate([jnp.zeros((first, D_MODEL), F32), loss_target[0]], axis=0)
    g_fin = norm_final[None, :]

    def head(i, rows, consts):
        xx, tg = rows
        g = consts[0]
        valid = (_row_ids(i, TR) >= first).astype(F32)
        rstd = lax.rsqrt(jnp.mean(xx * xx, axis=1, keepdims=True) + EPS)
        xhat = xx * rstd
        err = (xhat * g - tg) * valid
        loss_rows = 0.5 * jnp.mean(err * err, axis=1, keepdims=True)
        dy = err * (1.0 / D_MODEL)
        dxhat = dy * g
        dx = rstd * (dxhat - xhat * jnp.mean(dxhat * xhat, axis=1, keepdims=True))
        return [dx], [jnp.broadcast_to(_colsum(loss_rows), (1, LANE)), _colsum(dy * xhat)]

    dh, loss_part, d_norm_final = _rowwise(head, "loss_head", Tp, TR, [(h, None, 0), (tgt, None, 0)], [g_fin],
                                           [(D_MODEL, F32)], [LANE, D_MODEL])
    loss = lax.psum(loss_part[0, 0], ("x", "y", "c"))

    def norm_bwd_call(xin, g, du, dres):
        def norm_bwd(i, rows, consts):
            xx, dd, rr = rows
            gg = consts[0]
            valid = (_row_ids(i, TR) >= n_pad).astype(F32)
            rstd = lax.rsqrt(jnp.mean(xx * xx, axis=1, keepdims=True) + EPS)
            xhat = xx * rstd
            dxhat = dd * gg
            dx = rstd * (dxhat - xhat * jnp.mean(dxhat * xhat, axis=1, keepdims=True))
            return [rr + dx * valid], [_colsum(dd * xhat)]

        return _rowwise(norm_bwd, "norm_bwd", Tp, TR, [(xin, None, 0), (du, None, 0), (dres, None, 0)], [g],
                        [(D_MODEL, F32)], [D_MODEL])

    grads = {k: [None] * DEPTH for k in ("norm_mix", "w_in", "swa_sinks", "fox_forget_bias", "conv_w", "conv_b",
                                         "lru_w_r", "lru_b_r", "lru_w_i", "lru_b_i", "lru_lambda", "w_branch",
                                         "w_out", "norm_ffn", "w_ffn_in", "w_ffn_out")}
    d_bias_total = None
    for l in reversed(range(DEPTH)):
        sv = saved[l]
        proj = sv["proj"]

        def col(name):
            off, w = _WORK_OFF[name]
            return (proj, w, off // w)

        dh2 = dh
        d_act = _mm(dh2, W_ffn_out[l], "nt", "d_act")
        grads["w_ffn_out"][l] = _mm(sv["act"], dh2, "tn", "dw_ffn_out")

        def act_bwd(i, rows, consts):
            gate, up, da = rows
            sg = _sigmoid(gate)
            d_gate = da * up * (sg * (1.0 + gate * (1.0 - sg)))
            d_up = da * (gate * sg)
            return [jnp.concatenate([d_gate, d_up], axis=1)], []

        (dff,) = _rowwise(act_bwd, "act_bwd", Tp, BLOCK, [(sv["ff"], D_FF, 0), (sv["ff"], D_FF, 1), (d_act, None, 0)],
                          [], [(2 * D_FF, F32)])
        grads["w_ffn_in"][l] = _mm(sv["u2"], dff, "tn", "dw_ffn_in")
        du2 = _mm(dff, W_ffn_in[l], "nt", "du2")
        dh1, dg = norm_bwd_call(sv["h1"], norm_ffn[l][None, :], du2, dh2)
        grads["norm_ffn"][l] = dg[0]

        dmerged = _mm(dh1, W_out[l], "nt", "d_merged")
        grads["w_out"][l] = _mm(sv["merged"], dh1, "tn", "dw_out")

        def merge_bwd(i, rows, consts):
            g, b0, b1, b2, dm = rows
            dm = dm * (_row_ids(i, BLOCK) >= n_pad).astype(F32)
            outs, dgs = [], []
            for k, bk in enumerate((b0, b1, b2)):
                sg = _sigmoid(g[:, k * D_MODEL:(k + 1) * D_MODEL])
                outs.append(dm * sg)
                dgs.append(dm * bk * sg * (1.0 - sg))
            return outs + [jnp.concatenate(dgs, axis=1)], []

        d_ba, d_bf, d_bc, d_gates = _rowwise(
            merge_bwd, "merge_bwd", Tp, BLOCK,
            [col("gates"), (sv["ba"], None, 0), (sv["bf"], None, 0), (sv["bc"], None, 0), (dmerged, None, 0)], [],
            [(D_MODEL, F32)] * 3 + [(3 * D_MODEL, F32)])
        grads["w_branch"][l] = jnp.stack([_mm(sv["o_a"], d_ba, "tn", "dw_branch"),
                                          _mm(sv["o_f"], d_bf, "tn", "dw_branch"),
                                          _mm(sv["o_c"], d_bc, "tn", "dw_branch")])
        do_a = _mm(d_ba, W_branch[l, 0], "nt", "d_branch")
        do_f = _mm(d_bf, W_branch[l, 1], "nt", "d_branch")
        do_c = _mm(d_bc, W_branch[l, 2], "nt", "d_branch")

        def oc_bwd(i, rows, consts):
            d, hv, yv = rows
            return [d * _gelu(yv), d * hv * _gelu_grad(yv)], []

        d_hs, d_yc = _rowwise(oc_bwd, "oc_bwd", Tp, TR, [(do_c, None, 0), (sv["hs"], None, 0), col("yc")], [],
                              [(LRU_WIDTH, F32)] * 2)
        d_state = _rev_scan(sv["a_dec"], d_hs, "lru_scan")
        hs_prev = _shift_down(sv["hs"], 1)

        def lru_bwd(i, rows, consts):
            dH, hp, pr, xv = rows
            b_r, b_i, lam = consts
            valid = (_row_ids(i, TR) >= n_pad).astype(F32)
            r = _sigmoid(pr[:, :LRU_WIDTH] + b_r)
            gi = _sigmoid(pr[:, LRU_WIDTH:] + b_i)
            lsl = _log_sigmoid(lam)
            log_a = LRU_C * r * lsl
            a = jnp.exp(log_a)
            one_m_e = _neg_expm1(2.0 * log_a)
            mult = jnp.sqrt(one_m_e)
            d_inp = dH * valid
            d_mult = d_inp * gi * xv
            d_gi = d_inp * mult * xv
            d_x = d_inp * mult * gi
            d_log_a = dH * hp * a - d_mult * (1.0 - one_m_e) / mult
            d_pre_r = d_log_a * (LRU_C * lsl) * r * (1.0 - r)
            d_pre_i = d_gi * gi * (1.0 - gi)
            d_lam = _colsum(d_log_a * (LRU_C * r)) * _sigmoid(-lam)
            return [jnp.concatenate([d_pre_r, d_pre_i], axis=1), d_x], [_colsum(d_pre_r), _colsum(d_pre_i), d_lam]

        d_pre, d_xdir, d_b_r, d_b_i, d_lam = _rowwise(
            lru_bwd, "lru_bwd", Tp, TR, [(d_state, None, 0), (hs_prev, None, 0), (sv["pre"], None, 0),
                                         (sv["xconv"], None, 0)], sv["lru_consts"],
            [(2 * LRU_WIDTH, F32), (LRU_WIDTH, F32)], [LRU_WIDTH] * 3)
        grads["lru_b_r"][l], grads["lru_b_i"][l], grads["lru_lambda"][l] = d_b_r[0], d_b_i[0], d_lam[0]
        d_w_ri = _mm(sv["xconv"], d_pre, "tn", "dw_lru_gates")

        def diag_blocks(wd):
            w4 = wd.reshape(LRU_BLOCKS, LRU_BLOCK_DIM, LRU_BLOCKS, LRU_BLOCK_DIM)
            return jnp.stack([w4[b, :, b, :] for b in range(LRU_BLOCKS)])

        grads["lru_w_r"][l] = diag_blocks(d_w_ri[:, :LRU_WIDTH])
        grads["lru_w_i"][l] = diag_blocks(d_w_ri[:, LRU_WIDTH:])
        d_conv = _mm(d_pre, sv["w_ri"], "nt", "d_xconv", res=d_xdir)
        dcs = [d_conv] + [_shift_up(d_conv, s) for s in range(1, CONV_WIDTH)]

        def conv_bwd(i, rows, consts):
            w = consts[0]
            d = rows[:CONV_WIDTH]
            xsh = rows[CONV_WIDTH:]
            dxc = d[0] * w[CONV_WIDTH - 1:CONV_WIDTH, :]
            for s in range(1, CONV_WIDTH):
                dxc = dxc + d[s] * w[CONV_WIDTH - 1 - s:CONV_WIDTH - s, :]
            return [dxc], [_colsum(xsh[t] * d[0]) for t in range(CONV_WIDTH)] + [_colsum(d[0])]

        res = _rowwise(conv_bwd, "conv_bwd", Tp, TR, [(a, None, 0) for a in dcs + sv["xs"]], [sv["cw"]],
                       [(LRU_WIDTH, F32)], [LRU_WIDTH] * (CONV_WIDTH + 1))
        d_xc = res[0]
        grads["conv_w"][l] = jnp.concatenate(res[1:1 + CONV_WIDTH], axis=0)
        grads["conv_b"][l] = res[1 + CONV_WIDTH][0]

        do_f_h = _heads(do_f, 8)
        dqf, delta_f, dccol = _fox_bwd_dq(sv["qf"], sv["kf"], sv["vf"], sv["ccol"], sv["crow"], sv["o_f_h"],
                                          sv["lse_f"], do_f_h, n_pad, TQ)
        dkf, dvf, dcrow = _fox_bwd_dkv(sv["qf"], sv["kf"], sv["vf"], sv["ccol"], sv["crow"], sv["lse_f"], delta_f,
                                       do_f_h, n_pad, TQ)
        lanes = lambda z: jnp.pad(z.reshape(FOX_HEADS, Tp).T, ((0, 0), (0, LANE - FOX_HEADS)))
        dlogf = _rev_scan(None, lanes(dcrow), "cumsum_bwd", g2=lanes(dccol))

        def logf_bwd(i, rows, consts):
            dl, fl = rows
            valid = (_row_ids(i, TR) >= n_pad).astype(F32)
            lane_ok = (lax.broadcasted_iota(jnp.int32, (1, LANE), 1) < FOX_HEADS).astype(F32)
            dfl = dl * _sigmoid(-(fl + consts[0])) * valid * lane_ok
            return [dfl], [_colsum(dfl)]

        d_fl, d_fb = _rowwise(logf_bwd, "logf_bwd", Tp, TR, [(dlogf, None, 0), col("fl")], [sv["fbias"]],
                              [(LANE, F32)], [LANE])
        grads["fox_forget_bias"][l] = d_fb[0, :FOX_HEADS]

        dqa, dk_strip, dv_strip, d_bias_l, d_sink = _swa_bwd(sv["qa"], sv["ka"], sv["va"], bias, sv["sinks"],
                                                             sv["o_a_h"], sv["lse_a"], _heads(do_a, 8), n_pad)
        grads["swa_sinks"][l] = d_sink[:, 0, 0]
        d_bias_total = d_bias_l if d_bias_total is None else d_bias_total + d_bias_l

        def kv_parts(strip):
            cur = strip[:, :, BLOCK:, :].reshape(SWA_Q_HEADS, Tp, HEAD_DIM)
            nxt = jnp.zeros((SWA_Q_HEADS, NQ, TQ, HEAD_DIM), F32)
            if NQ > 1:
                nxt = nxt.at[:, :NQ - 1, TQ - BLOCK:, :].set(strip[:, 1:, :BLOCK, :])
            nxt = nxt.reshape(SWA_Q_HEADS, Tp, HEAD_DIM)
            both = jnp.stack([cur, nxt]).reshape(2, SWA_KV_HEADS, SWA_GROUP, Tp, HEAD_DIM)
            return both.transpose(3, 0, 2, 1, 4).reshape(Tp, 2 * SWA_GROUP * KV_W)

        dk_parts, dv_parts = kv_parts(dk_strip), kv_parts(dv_strip)

        def assemble(i, rows, consts):
            dg, dqa_, dqf_, dkf_, dvf_, dxc_, dyc_, dkp, dvp, dfl_ = rows

            def fold(parts):
                acc = parts[:, :KV_W]
                for p in range(1, 2 * SWA_GROUP):
                    acc = acc + parts[:, p * KV_W:(p + 1) * KV_W]
                return acc

            return [jnp.concatenate([dg, dqa_, dqf_, dkf_, dvf_, dxc_, dyc_, fold(dkp), fold(dvp), dfl_], axis=1)], []

        (dproj,) = _rowwise(
            assemble, "assemble_dproj", Tp, BLOCK,
            [(d_gates, None, 0), (_unheads(dqa), None, 0), (_unheads(dqf), None, 0), (_unheads(dkf), None, 0),
             (_unheads(dvf), None, 0), (d_xc, None, 0), (d_yc, None, 0), (dk_parts, None, 0), (dv_parts, None, 0),
             (d_fl, None, 0)], [], [(WORK_COLS, F32)])
        grads["w_in"][l] = _mm(sv["u"], dproj, "tn", "dw_in", tn_cap=384)
        du = _mm(dproj, W_in[l], "nt", "du", tk_cap=384)
        dh, dg = norm_bwd_call(sv["h"], norm_mix[l][None, :], du, dh1)
        grads["norm_mix"][l] = dg[0]

    d_table = _mm(d_bias_total.reshape(SWA_Q_HEADS, BLOCK * 2 * BLOCK), onehot, "nt", "d_rel_table", exact=True,
                  tk_cap=4096)
    g_rel = d_table[:, :REL_BUCKETS].T
    g_meta_full = dh[n_pad:first]
    grad_x = dh[first:][None]

    stack = lambda k: jnp.stack(grads[k])
    G_w_in = _to_orig_cols(stack("w_in"))
    per_dev = [
        G_w_in.reshape(DEPTH, D_MODEL, N_DEV, IN_COLS // N_DEV).transpose(2, 0, 1, 3),
        stack("w_ffn_in").reshape(DEPTH, D_MODEL, N_DEV, 2 * D_FF // N_DEV).transpose(2, 0, 1, 3),
        stack("w_ffn_out").reshape(DEPTH, N_DEV, D_FF // N_DEV, D_MODEL).transpose(1, 0, 2, 3),
        stack("w_branch").reshape(DEPTH, N_BRANCH, LRU_WIDTH, N_DEV, D_MODEL // N_DEV).transpose(3, 0, 1, 2, 4),
        stack("w_out").reshape(DEPTH, N_DEV, D_MODEL // N_DEV, D_MODEL).transpose(1, 0, 2, 3),
    ]
    gsend = _pack_lead(per_dev, D_MODEL, 256)
    grecv = _exchange(gsend, False, "scatter_grads")
    wflat = _pack([big_w[n] for n in big_names], D_MODEL, 256)
    mflat = _pack([big_m[n] for n in big_names], D_MODEL, 256)
    vflat = _pack([big_v[n] for n in big_names], D_MODEL, 256)
    big_out = [_unpack(o, big_shapes) for o in _adam(grecv, wflat, mflat, vflat, "adam_big")]
    big_res = {n: [big_out[k][j] for k in range(4)] for j, n in enumerate(big_names)}

    rep_names = ("rel_bias_table", "norm_mix", "swa_sinks", "fox_forget_bias", "conv_b", "lru_w_r", "lru_b_r",
                 "lru_w_i", "lru_b_i", "lru_lambda", "norm_ffn", "norm_final")
    rep_w = dict(rel_bias_table=rel_bias_table, norm_mix=norm_mix, swa_sinks=swa_sinks,
                 fox_forget_bias=fox_forget_bias, conv_b=conv_b, lru_w_r=lru_w_r, lru_b_r=lru_b_r, lru_w_i=lru_w_i,
                 lru_b_i=lru_b_i, lru_lambda=lru_lambda, norm_ffn=norm_ffn, norm_final=norm_final)
    rep_m = dict(rel_bias_table=m_rel_bias_table, norm_mix=m_norm_mix, swa_sinks=m_swa_sinks,
                 fox_forget_bias=m_fox_forget_bias, conv_b=m_conv_b, lru_w_r=m_lru_w_r, lru_b_r=m_lru_b_r,
                 lru_w_i=m_lru_w_i, lru_b_i=m_lru_b_i, lru_lambda=m_lru_lambda, norm_ffn=m_norm_ffn,
                 norm_final=m_norm_final)
    rep_v = dict(rel_bias_table=v_rel_bias_table, norm_mix=v_norm_mix, swa_sinks=v_swa_sinks,
                 fox_forget_bias=v_fox_forget_bias, conv_b=v_conv_b, lru_w_r=v_lru_w_r, lru_b_r=v_lru_b_r,
                 lru_w_i=v_lru_w_i, lru_b_i=v_lru_b_i, lru_lambda=v_lru_lambda, norm_ffn=v_norm_ffn,
                 norm_final=v_norm_final)
    rep_g = {n: (g_rel if n == "rel_bias_table" else d_norm_final[0] if n == "norm_final" else stack(n))
             for n in rep_names}
    small_g = [rep_g[n] for n in rep_names] + [g_meta_full, stack("conv_w")]
    small_shapes = [rep_w[n].shape for n in rep_names] + [(N_META, D_MODEL), (DEPTH, CONV_WIDTH, LRU_WIDTH)]
    gs_all = _exchange(_pack(small_g, D_MODEL, 8), True, "gather_small_grads")
    gs_sum = _unpack(_sum_parts(gs_all, "sum_small_grads"), small_shapes)
    gsum = dict(zip(rep_names, gs_sum[:len(rep_names)]))
    g_meta_sh = lax.dynamic_slice_in_dim(gs_sum[-2], me * (D_MODEL // N_DEV), D_MODEL // N_DEV, axis=1)
    g_convw_sh = lax.dynamic_slice_in_dim(gs_sum[-1], me * (LRU_WIDTH // N_DEV), LRU_WIDTH // N_DEV, axis=2)
    sm_names = rep_names + ("meta_tokens", "conv_w")
    sm_w = [rep_w[n] for n in rep_names] + [meta_tokens, conv_w]
    sm_m = [rep_m[n] for n in rep_names] + [m_meta_tokens, m_conv_w]
    sm_v = [rep_v[n] for n in rep_names] + [v_meta_tokens, v_conv_w]
    sm_g = [gsum[n] for n in rep_names] + [g_meta_sh, g_convw_sh]
    sm_shapes = [w.shape for w in sm_w]
    sm_out = [_unpack(o, sm_shapes) for o in _adam(_pack(sm_g, D_MODEL, 8)[None], _pack(sm_w, D_MODEL, 8),
                                                   _pack(sm_m, D_MODEL, 8), _pack(sm_v, D_MODEL, 8), "adam_small")]
    sm_res = {n: [sm_out[k][j] for k in range(4)] for j, n in enumerate(sm_names)}

    order = ("meta_tokens", "rel_bias_table", "norm_mix", "w_in", "swa_sinks", "fox_forget_bias", "conv_w", "conv_b",
             "lru_w_r", "lru_b_r", "lru_w_i", "lru_b_i", "lru_lambda", "w_branch", "w_out", "norm_ffn", "w_ffn_in",
             "w_ffn_out", "norm_final")
    allres = {**big_res, **sm_res}
    outs = [loss, grad_x]
    for k in range(4):
        outs += [allres[n][k] for n in order]
    return tuple(outs)
```

```python
import functools
import math

import numpy as np
import jax
import jax.numpy as jnp
from jax import lax
from jax.experimental import pallas as pl
from jax.experimental.pallas import tpu as pltpu

F32 = jnp.float32
BF16 = jnp.bfloat16

N_DEV = 8
D_MODEL = 1024
DEPTH = 4
HEAD_DIM = 64
N_META = 16
BLOCK = 128
NEG_INF = -1e30
SWA_Q_HEADS = 8
SWA_KV_HEADS = 2
SWA_GROUP = SWA_Q_HEADS // SWA_KV_HEADS
FOX_HEADS = 8
LRU_WIDTH = D_MODEL // 2
LRU_BLOCKS = 8
LRU_BLOCK_DIM = LRU_WIDTH // LRU_BLOCKS
CONV_WIDTH = 4
LRU_C = 8.0
REL_BUCKETS = 32
REL_MAX_DIST = 128
D_FF = 2816
N_BRANCH = 3
ATT_W = SWA_Q_HEADS * HEAD_DIM
KV_W = SWA_KV_HEADS * HEAD_DIM
SCALE = HEAD_DIM ** -0.5
EPS = 1e-6

_ORIG = (("qa", ATT_W), ("ka", KV_W), ("va", KV_W), ("qf", ATT_W), ("kf", ATT_W), ("vf", ATT_W),
         ("fl", FOX_HEADS), ("xc", LRU_WIDTH), ("yc", LRU_WIDTH), ("gates", N_BRANCH * D_MODEL))
IN_COLS = sum(w for _, w in _ORIG)
_WORK = (("gates", 3072), ("qa", 512), ("qf", 512), ("kf", 512), ("vf", 512), ("xc", 512), ("yc", 512),
         ("ka", 128), ("va", 128), ("fl", 128))
WORK_COLS = sum(w for _, w in _WORK)


def _offsets(table):
    off, out = 0, {}
    for n, w in table:
        out[n] = (off, w)
        off += w
    return out


_ORIG_OFF = _offsets(_ORIG)
_WORK_OFF = _offsets(_WORK)

ADAM_LR = 0.001
ADAM_B1 = 0.9
ADAM_B2 = 0.999
ADAM_EPS = 1e-08
ADAM_WD = 0.01
ADAM_STEP = 10

VMEM_LIMIT = 52 * 1024 * 1024
LANE = 128


def _tile(n, cap, mult=LANE):
    if n <= cap:
        return n
    best = None
    for d in range(mult, cap + 1, mult):
        if n % d == 0:
            best = d
    assert best is not None, (n, cap, mult)
    return best


def _params(sem):
    return pltpu.CompilerParams(dimension_semantics=sem, vmem_limit_bytes=VMEM_LIMIT)


def _sigmoid(x):
    return 1.0 / (1.0 + jnp.exp(-x))


def _log_sigmoid(x):
    return jnp.minimum(x, 0.0) - jnp.log(1.0 + jnp.exp(-jnp.abs(x)))


def _neg_expm1(x):
    series = -x * (1.0 + x * (0.5 + x * (1.0 / 6.0 + x * (1.0 / 24.0 + x * (1.0 / 120.0)))))
    return jnp.where(x > -0.1, series, 1.0 - jnp.exp(x))


_GELU_C = math.sqrt(2.0 / math.pi)


def _gelu(x):
    return 0.5 * x * (1.0 + jnp.tanh(_GELU_C * (x + 0.044715 * x * x * x)))


def _gelu_grad(x):
    t = jnp.tanh(_GELU_C * (x + 0.044715 * x * x * x))
    return 0.5 * (1.0 + t) + 0.5 * x * (1.0 - t * t) * _GELU_C * (1.0 + 3.0 * 0.044715 * x * x)


def _mm(a, b, mode, name, res=None, exact=False, tm_cap=1408, tn_cap=512, tk_cap=1408):
    if mode == "nn":
        (M, K), (K2, N) = a.shape, b.shape
    elif mode == "nt":
        (M, K), (N, K2) = a.shape, b.shape
    else:
        (K, M), (K2, N) = a.shape, b.shape
    assert K == K2, (a.shape, b.shape, mode)
    tm, tn, tk = _tile(M, tm_cap), _tile(N, tn_cap), _tile(K, tk_cap)
    nk = K // tk
    a_spec = {"nn": pl.BlockSpec((tm, tk), lambda i, j, k: (i, k)),
              "nt": pl.BlockSpec((tm, tk), lambda i, j, k: (i, k)),
              "tn": pl.BlockSpec((tk, tm), lambda i, j, k: (k, i))}[mode]
    b_spec = {"nn": pl.BlockSpec((tk, tn), lambda i, j, k: (k, j)),
              "nt": pl.BlockSpec((tn, tk), lambda i, j, k: (j, k)),
              "tn": pl.BlockSpec((tk, tn), lambda i, j, k: (k, j))}[mode]
    o_spec = pl.BlockSpec((tm, tn), lambda i, j, k: (i, j))
    has_res = res is not None

    def body(*refs):
        if has_res:
            a_ref, b_ref, r_ref, o_ref, acc_ref = refs
        else:
            a_ref, b_ref, o_ref, acc_ref = refs
        k = pl.program_id(2)

        @pl.when(k == 0)
        def _():
            acc_ref[...] = jnp.zeros_like(acc_ref)

        x, y = a_ref[...], b_ref[...]
        if exact:
            x, y, prec = x.astype(F32), y.astype(F32), lax.Precision.HIGHEST
        else:
            x, y, prec = x.astype(BF16), y.astype(BF16), None
        if mode == "tn":
            x = x.T
        dims = (((1,), (1,)), ((), ())) if mode == "nt" else (((1,), (0,)), ((), ()))
        acc_ref[...] += lax.dot_general(x, y, dims, precision=prec, preferred_element_type=F32)

        @pl.when(k == nk - 1)
        def _():
            if has_res:
                o_ref[...] = acc_ref[...] + r_ref[...]
            else:
                o_ref[...] = acc_ref[...]

    in_specs = [a_spec, b_spec] + ([o_spec] if has_res else [])
    args = (a, b) + ((res,) if has_res else ())
    return pl.pallas_call(
        body, name=name, grid=(M // tm, N // tn, nk), in_specs=in_specs, out_specs=o_spec,
        out_shape=jax.ShapeDtypeStruct((M, N), F32), scratch_shapes=[pltpu.VMEM((tm, tn), F32)],
        compiler_params=_params(("parallel", "parallel", "arbitrary")))(*args)


def _rowwise(fn, name, n_rows, tr, row_ins, const_ins, row_outs, red_outs=()):
    n_row_in, n_const, n_row_out, n_red = len(row_ins), len(const_ins), len(row_outs), len(red_outs)
    in_specs = []
    for arr, w, idx in row_ins:
        width = arr.shape[1] if w is None else w
        in_specs.append(pl.BlockSpec((tr, width), functools.partial(lambda i, j: (i, j), j=idx)))
    for c in const_ins:
        in_specs.append(pl.BlockSpec(c.shape, lambda i: (0, 0)))
    out_shape = [jax.ShapeDtypeStruct((n_rows, c), dt) for c, dt in row_outs]
    out_shape += [jax.ShapeDtypeStruct((1, c), F32) for c in red_outs]
    out_specs = [pl.BlockSpec((tr, c), lambda i: (i, 0)) for c, _ in row_outs]
    out_specs += [pl.BlockSpec((1, c), lambda i: (0, 0)) for c in red_outs]

    def body(*refs):
        i = pl.program_id(0)
        rows = [r[...] for r in refs[:n_row_in]]
        consts = [r[...] for r in refs[n_row_in:n_row_in + n_const]]
        outs = refs[n_row_in + n_const:]
        row_vals, red_vals = fn(i, rows, consts)
        for k in range(n_row_out):
            outs[k][...] = row_vals[k].astype(outs[k].dtype)
        if n_red:
            @pl.when(i == 0)
            def _():
                for k in range(n_red):
                    outs[n_row_out + k][...] = jnp.zeros_like(outs[n_row_out + k])

            for k in range(n_red):
                outs[n_row_out + k][...] += red_vals[k]

    res = pl.pallas_call(
        body, name=name, grid=(n_rows // tr,), in_specs=in_specs, out_specs=out_specs, out_shape=out_shape,
        compiler_params=_params(("arbitrary",)))(*[a for a, _, _ in row_ins], *const_ins)
    return res


def _row_ids(i, tr):
    return i * tr + lax.broadcasted_iota(jnp.int32, (tr, 1), 0)


def _colsum(x):
    return jnp.sum(x, axis=0, keepdims=True)


def _scan(a, b, name, b2=None):
    n_rows, c = b.shape
    tr = _tile(n_rows, 384)
    has_a = a is not None
    has_b2 = b2 is not None

    def body(*refs):
        o_ref, carry = refs[-2:]
        ins = list(refs[:-2])
        a_ref = ins.pop(0) if has_a else None
        b_ref = ins.pop(0)

        @pl.when(pl.program_id(0) == 0)
        def _():
            carry[...] = jnp.zeros_like(carry)

        rows = lax.broadcasted_iota(jnp.int32, (tr, c), 0)
        bv = b_ref[...] + ins[0][...] if has_b2 else b_ref[...]
        av = a_ref[...] if has_a else None
        s = 1
        while s < tr:
            keep = rows >= s
            b_sh = jnp.where(keep, pltpu.roll(bv, s, 0), 0.0)
            if has_a:
                a_sh = jnp.where(keep, pltpu.roll(av, s, 0), 1.0)
                bv = av * b_sh + bv
                av = av * a_sh
            else:
                bv = b_sh + bv
            s *= 2
        h = av * carry[0:1, :] + bv if has_a else carry[0:1, :] + bv
        o_ref[...] = h
        carry[...] = jnp.broadcast_to(h[tr - 1:tr, :], carry.shape)

    spec = pl.BlockSpec((tr, c), lambda i: (i, 0))
    args = ((a,) if has_a else ()) + (b,) + ((b2,) if has_b2 else ())
    return pl.pallas_call(
        body, name=name, grid=(n_rows // tr,), in_specs=[spec] * len(args), out_specs=spec,
        out_shape=jax.ShapeDtypeStruct((n_rows, c), F32), scratch_shapes=[pltpu.VMEM((8, c), F32)],
        compiler_params=_params(("arbitrary",)))(*args)


def _shift_down(x, s, fill=0.0):
    pad = jnp.full((s,) + x.shape[1:], fill, x.dtype)
    return jnp.concatenate([pad, x[:-s]], axis=0)


def _shift_up(x, s):
    pad = jnp.zeros((s,) + x.shape[1:], x.dtype)
    return jnp.concatenate([x[s:], pad], axis=0)


def _rev_scan(a, g, name, g2=None):
    gf = jnp.flip(g, axis=0)
    g2f = None if g2 is None else jnp.flip(g2, axis=0)
    af = None if a is None else _shift_down(jnp.flip(a, axis=0), 1, 1.0)
    return jnp.flip(_scan(af, gf, name, b2=g2f), axis=0)


def _swa_masks(blk, n_pad):
    qi = lax.broadcasted_iota(jnp.int32, (BLOCK, 2 * BLOCK), 0)
    ki = lax.broadcasted_iota(jnp.int32, (BLOCK, 2 * BLOCK), 1)
    dist = qi + BLOCK - ki
    key_abs = (blk - 1) * BLOCK + ki
    return (dist >= 0) & (dist < BLOCK) & (key_abs >= n_pad)


def _swa_fwd(q, k, v, bias, sinks, n_pad):
    H, Tp, dh = q.shape
    tq = _tile(Tp, 384)
    nsub = tq // BLOCK
    nq = Tp // tq

    def body(q_ref, kp_ref, kc_ref, vp_ref, vc_ref, b_ref, s_ref, o_ref, lse_ref):
        i = pl.program_id(1)
        kstrip = jnp.concatenate([kp_ref[0], kc_ref[0]], axis=0).astype(BF16)
        vstrip = jnp.concatenate([vp_ref[0], vc_ref[0]], axis=0).astype(BF16)
        sink = s_ref[0, :, 0:1]
        bias_h = b_ref[0]
        for j in range(nsub):
            qj = q_ref[0, j * BLOCK:(j + 1) * BLOCK, :].astype(BF16)
            kk = kstrip[j * BLOCK:(j + 2) * BLOCK]
            vv = vstrip[j * BLOCK:(j + 2) * BLOCK]
            s = lax.dot_general(qj, kk, (((1,), (1,)), ((), ())), preferred_element_type=F32) * SCALE
            mask = _swa_masks(i * nsub + j, n_pad)
            s = jnp.where(mask, s + bias_h, NEG_INF)
            m = jnp.maximum(jnp.max(s, axis=1, keepdims=True), sink)
            p = jnp.exp(s - m)
            denom = jnp.sum(p, axis=1, keepdims=True) + jnp.exp(sink - m)
            p = (p / denom).astype(BF16)
            o_ref[0, j * BLOCK:(j + 1) * BLOCK, :] = jnp.dot(p, vv, preferred_element_type=F32)
            lse_ref[0, j * BLOCK:(j + 1) * BLOCK, :] = m + jnp.log(denom)

    prev = lambda h, i: (h // SWA_GROUP, jnp.maximum(i * nsub - 1, 0), 0)
    cur = lambda h, i: (h // SWA_GROUP, i, 0)
    return pl.pallas_call(
        body, name="swa_fwd", grid=(H, nq),
        in_specs=[pl.BlockSpec((1, tq, dh), lambda h, i: (h, i, 0)),
                  pl.BlockSpec((1, BLOCK, dh), prev), pl.BlockSpec((1, tq, dh), cur),
                  pl.BlockSpec((1, BLOCK, dh), prev), pl.BlockSpec((1, tq, dh), cur),
                  pl.BlockSpec((1, BLOCK, 2 * BLOCK), lambda h, i: (h, 0, 0)),
                  pl.BlockSpec((1, 1, LANE), lambda h, i: (h, 0, 0))],
        out_specs=[pl.BlockSpec((1, tq, dh), lambda h, i: (h, i, 0)),
                   pl.BlockSpec((1, tq, 1), lambda h, i: (h, i, 0))],
        out_shape=[jax.ShapeDtypeStruct((H, Tp, dh), F32), jax.ShapeDtypeStruct((H, Tp, 1), F32)],
        compiler_params=_params(("parallel", "arbitrary")))(q, k, k, v, v, bias, sinks)


def _swa_bwd(q, k, v, bias, sinks, o, lse, do, n_pad):
    H, Tp, dh = q.shape
    tq = _tile(Tp, 384)
    nsub = tq // BLOCK
    nq = Tp // tq
    strip = BLOCK + tq

    def body(q_ref, kp_ref, kc_ref, vp_ref, vc_ref, b_ref, s_ref, o_ref, lse_ref, do_ref,
             dq_ref, dk_ref, dv_ref, db_ref, ds_ref):
        i = pl.program_id(1)

        @pl.when(i == 0)
        def _():
            db_ref[...] = jnp.zeros_like(db_ref)
            ds_ref[...] = jnp.zeros_like(ds_ref)

        dk_ref[...] = jnp.zeros_like(dk_ref)
        dv_ref[...] = jnp.zeros_like(dv_ref)
        kstrip = jnp.concatenate([kp_ref[0], kc_ref[0]], axis=0).astype(BF16)
        vstrip = jnp.concatenate([vp_ref[0], vc_ref[0]], axis=0).astype(BF16)
        sink = s_ref[0, :, 0:1]
        bias_h = b_ref[0]
        for j in range(nsub):
            rows = slice(j * BLOCK, (j + 1) * BLOCK)
            keys = slice(j * BLOCK, (j + 2) * BLOCK)
            qj = q_ref[0, rows, :].astype(BF16)
            doj = do_ref[0, rows, :]
            lsej = lse_ref[0, rows, :]
            delta = jnp.sum(doj * o_ref[0, rows, :], axis=1, keepdims=True)
            doj = doj.astype(BF16)
            kk, vv = kstrip[keys], vstrip[keys]
            s = lax.dot_general(qj, kk, (((1,), (1,)), ((), ())), preferred_element_type=F32) * SCALE
            mask = _swa_masks(i * nsub + j, n_pad)
            s = jnp.where(mask, s + bias_h, NEG_INF)
            p = jnp.exp(s - lsej)
            p_sink = jnp.exp(sink - lsej)
            dp = lax.dot_general(doj, vv, (((1,), (1,)), ((), ())), preferred_element_type=F32)
            ds = p * (dp - delta)
            db_ref[0] += ds
            ds_ref[0] += jnp.broadcast_to(-jnp.sum(p_sink * delta, axis=0, keepdims=True), (1, LANE))
            dsb = ds.astype(BF16)
            dq_ref[0, rows, :] = jnp.dot(dsb, kk, preferred_element_type=F32) * SCALE
            dk_ref[0, 0, keys, :] += jnp.dot(ds.T.astype(BF16), qj, preferred_element_type=F32) * SCALE
            dv_ref[0, 0, keys, :] += jnp.dot(p.T.astype(BF16), doj, preferred_element_type=F32)

    prev = lambda h, i: (h // SWA_GROUP, jnp.maximum(i * nsub - 1, 0), 0)
    cur = lambda h, i: (h // SWA_GROUP, i, 0)
    qspec = pl.BlockSpec((1, tq, dh), lambda h, i: (h, i, 0))
    cspec = pl.BlockSpec((1, tq, 1), lambda h, i: (h, i, 0))
    sspec = pl.BlockSpec((1, 1, strip, dh), lambda h, i: (h, i, 0, 0))
    return pl.pallas_call(
        body, name="swa_bwd", grid=(H, nq),
        in_specs=[qspec, pl.BlockSpec((1, BLOCK, dh), prev), pl.BlockSpec((1, tq, dh), cur),
                  pl.BlockSpec((1, BLOCK, dh), prev), pl.BlockSpec((1, tq, dh), cur),
                  pl.BlockSpec((1, BLOCK, 2 * BLOCK), lambda h, i: (h, 0, 0)),
                  pl.BlockSpec((1, 1, LANE), lambda h, i: (h, 0, 0)), qspec, cspec, qspec],
        out_specs=[qspec, sspec, sspec, pl.BlockSpec((1, BLOCK, 2 * BLOCK), lambda h, i: (h, 0, 0)),
                   pl.BlockSpec((1, 1, LANE), lambda h, i: (h, 0, 0))],
        out_shape=[jax.ShapeDtypeStruct((H, Tp, dh), F32), jax.ShapeDtypeStruct((H, nq, strip, dh), F32),
                   jax.ShapeDtypeStruct((H, nq, strip, dh), F32),
                   jax.ShapeDtypeStruct((H, BLOCK, 2 * BLOCK), F32), jax.ShapeDtypeStruct((H, 1, LANE), F32)],
        compiler_params=_params(("parallel", "arbitrary")))(q, k, k, v, v, bias, sinks, o, lse, do)


def _fox_scores(qb, kk, cq, ck, diag, tq):
    s = lax.dot_general(qb, kk, (((1,), (1,)), ((), ())), preferred_element_type=F32)
    s = s + cq - ck
    if diag:
        r = lax.broadcasted_iota(jnp.int32, (tq, tq), 0)
        c = lax.broadcasted_iota(jnp.int32, (tq, tq), 1)
        s = jnp.where(c <= r, s, NEG_INF)
    return s


def _fox_fwd(q, k, v, ccol, crow, tq):
    H, Tp, dh = q.shape
    nq = Tp // tq
    k4, v4 = k.reshape(H, nq, tq, dh), v.reshape(H, nq, tq, dh)

    def body(q_ref, k_ref, v_ref, cc_ref, cr_ref, o_ref, lse_ref):
        qi = pl.program_id(1)
        qb = (q_ref[0] * SCALE).astype(BF16)
        cq = cc_ref[0]

        def step(kb, carry, diag):
            m, l, acc = carry
            s = _fox_scores(qb, k_ref[0, kb].astype(BF16), cq, cr_ref[0, kb], diag, tq)
            m_new = jnp.maximum(m, jnp.max(s, axis=1, keepdims=True))
            alpha = jnp.exp(m - m_new)
            p = jnp.exp(s - m_new)
            l = alpha * l + jnp.sum(p, axis=1, keepdims=True)
            acc = alpha * acc + jnp.dot(p.astype(BF16), v_ref[0, kb].astype(BF16), preferred_element_type=F32)
            return m_new, l, acc

        init = (jnp.full((tq, 1), NEG_INF, F32), jnp.zeros((tq, 1), F32), jnp.zeros((tq, dh), F32))
        carry = lax.fori_loop(0, qi, functools.partial(step, diag=False), init)
        m, l, acc = step(qi, carry, True)
        o_ref[0] = acc / l
        lse_ref[0] = m + jnp.log(l)

    qspec = pl.BlockSpec((1, tq, dh), lambda h, i: (h, i, 0))
    cspec = pl.BlockSpec((1, tq, 1), lambda h, i: (h, i, 0))
    full4 = pl.BlockSpec((1, nq, tq, dh), lambda h, i: (h, 0, 0, 0))
    return pl.pallas_call(
        body, name="fox_fwd", grid=(H, nq),
        in_specs=[qspec, full4, full4, cspec, pl.BlockSpec((1, nq, 1, tq), lambda h, i: (h, 0, 0, 0))],
        out_specs=[qspec, cspec],
        out_shape=[jax.ShapeDtypeStruct((H, Tp, dh), F32), jax.ShapeDtypeStruct((H, Tp, 1), F32)],
        compiler_params=_params(("parallel", "arbitrary")))(q, k4, v4, ccol, crow)


def _fox_bwd_dq(q, k, v, ccol, crow, o, lse, do, tq):
    H, Tp, dh = q.shape
    nq = Tp // tq
    k4, v4 = k.reshape(H, nq, tq, dh), v.reshape(H, nq, tq, dh)

    def body(q_ref, k_ref, v_ref, cc_ref, cr_ref, o_ref, lse_ref, do_ref, dq_ref, dl_ref, dc_ref):
        qi = pl.program_id(1)
        qb = (q_ref[0] * SCALE).astype(BF16)
        cq = cc_ref[0]
        dof = do_ref[0]
        delta = jnp.sum(dof * o_ref[0], axis=1, keepdims=True)
        dob = dof.astype(BF16)
        lse = lse_ref[0]

        def step(kb, carry, diag):
            dq, dc = carry
            kk = k_ref[0, kb].astype(BF16)
            s = _fox_scores(qb, kk, cq, cr_ref[0, kb], diag, tq)
            p = jnp.exp(s - lse)
            dp = lax.dot_general(dob, v_ref[0, kb].astype(BF16), (((1,), (1,)), ((), ())),
                                 preferred_element_type=F32)
            ds = p * (dp - delta)
            dq = dq + jnp.dot(ds.astype(BF16), kk, preferred_element_type=F32)
            return dq, dc + jnp.sum(ds, axis=1, keepdims=True)

        init = (jnp.zeros((tq, dh), F32), jnp.zeros((tq, 1), F32))
        dq, dc = step(qi, lax.fori_loop(0, qi, functools.partial(step, diag=False), init), True)
        dq_ref[0] = dq * SCALE
        dl_ref[0] = delta
        dc_ref[0] = dc

    qspec = pl.BlockSpec((1, tq, dh), lambda h, i: (h, i, 0))
    cspec = pl.BlockSpec((1, tq, 1), lambda h, i: (h, i, 0))
    full4 = pl.BlockSpec((1, nq, tq, dh), lambda h, i: (h, 0, 0, 0))
    return pl.pallas_call(
        body, name="fox_bwd_dq", grid=(H, nq),
        in_specs=[qspec, full4, full4, cspec, pl.BlockSpec((1, nq, 1, tq), lambda h, i: (h, 0, 0, 0)),
                  qspec, cspec, qspec],
        out_specs=[qspec, cspec, cspec],
        out_shape=[jax.ShapeDtypeStruct((H, Tp, dh), F32), jax.ShapeDtypeStruct((H, Tp, 1), F32),
                   jax.ShapeDtypeStruct((H, Tp, 1), F32)],
        compiler_params=_params(("parallel", "arbitrary")))(q, k4, v4, ccol, crow, o, lse, do)


def _fox_bwd_dkv(q, k, v, ccol, crow, lse, delta, do, tq):
    H, Tp, dh = q.shape
    nq = Tp // tq
    q4, do4 = q.reshape(H, nq, tq, dh), do.reshape(H, nq, tq, dh)
    cc4, lse4, dl4 = (x.reshape(H, nq, tq, 1) for x in (ccol, lse, delta))

    def body(q_ref, do_ref, cc_ref, lse_ref, dl_ref, k_ref, v_ref, cr_ref, dk_ref, dv_ref, dc_ref):
        ki = pl.program_id(1)
        kk = k_ref[0].astype(BF16)
        vv = v_ref[0].astype(BF16)
        ck = cr_ref[0, 0]

        def step(qb_i, carry, diag):
            dk, dv, dc = carry
            qb = (q_ref[0, qb_i] * SCALE).astype(BF16)
            dob = do_ref[0, qb_i].astype(BF16)
            s = _fox_scores(qb, kk, cc_ref[0, qb_i], ck, diag, tq)
            p = jnp.exp(s - lse_ref[0, qb_i])
            dp = lax.dot_general(dob, vv, (((1,), (1,)), ((), ())), preferred_element_type=F32)
            ds = p * (dp - dl_ref[0, qb_i])
            dv = dv + jnp.dot(p.T.astype(BF16), dob, preferred_element_type=F32)
            dk = dk + jnp.dot(ds.T.astype(BF16), qb, preferred_element_type=F32)
            dc = dc - jnp.sum(ds, axis=0, keepdims=True)
            return dk, dv, dc

        init = (jnp.zeros((tq, dh), F32), jnp.zeros((tq, dh), F32), jnp.zeros((1, tq), F32))
        dk, dv, dc = lax.fori_loop(ki + 1, nq, functools.partial(step, diag=False), step(ki, init, True))
        dk_ref[0] = dk
        dv_ref[0] = dv
        dc_ref[0, 0] = dc

    kspec = pl.BlockSpec((1, tq, dh), lambda h, i: (h, i, 0))
    full4 = pl.BlockSpec((1, nq, tq, dh), lambda h, i: (h, 0, 0, 0))
    full1 = pl.BlockSpec((1, nq, tq, 1), lambda h, i: (h, 0, 0, 0))
    rspec = pl.BlockSpec((1, 1, 1, tq), lambda h, i: (h, i, 0, 0))
    return pl.pallas_call(
        body, name="fox_bwd_dkv", grid=(H, nq),
        in_specs=[full4, full4, full1, full1, full1, kspec, kspec, rspec],
        out_specs=[kspec, kspec, rspec],
        out_shape=[jax.ShapeDtypeStruct((H, Tp, dh), F32), jax.ShapeDtypeStruct((H, Tp, dh), F32),
                   jax.ShapeDtypeStruct((H, nq, 1, tq), F32)],
        compiler_params=_params(("parallel", "arbitrary")))(q4, do4, cc4, lse4, dl4, k, v, crow)


def _exchange(src, gather, name):
    blk = src.shape if gather else src.shape[1:]

    def body(src_ref, out_ref, send_sems, recv_sems, local_sem):
        x, y, c = lax.axis_index("x"), lax.axis_index("y"), lax.axis_index("c")
        me = 4 * x + 2 * y + c

        def peer(r):
            return (x ^ ((r >> 2) & 1), y ^ ((r >> 1) & 1), c ^ (r & 1))

        def copy(r):
            px, py, pc = peer(r)
            pid = 4 * px + 2 * py + pc
            return pltpu.make_async_remote_copy(
                src_ref=src_ref if gather else src_ref.at[pid], dst_ref=out_ref.at[me],
                send_sem=send_sems.at[r - 1], recv_sem=recv_sems.at[r - 1],
                device_id=(px, py, pc), device_id_type=pl.DeviceIdType.MESH)

        def landing(r):
            px, py, pc = peer(r)
            pid = 4 * px + 2 * py + pc
            return pltpu.make_async_remote_copy(
                src_ref=src_ref if gather else src_ref.at[pid], dst_ref=out_ref.at[pid],
                send_sem=send_sems.at[r - 1], recv_sem=recv_sems.at[r - 1],
                device_id=(px, py, pc), device_id_type=pl.DeviceIdType.MESH)

        mine = pltpu.make_async_copy(src_ref if gather else src_ref.at[me], out_ref.at[me], local_sem)
        mine.start()
        sends = [copy(r) for r in range(1, N_DEV)]
        for cp in sends:
            cp.start()
        for r in range(1, N_DEV):
            landing(r).wait_recv()
        for cp in sends:
            cp.wait_send()
        mine.wait()

    any_spec = pl.BlockSpec(memory_space=pl.ANY)
    return pl.pallas_call(
        body, name=name, in_specs=[any_spec], out_specs=any_spec,
        out_shape=jax.ShapeDtypeStruct((N_DEV,) + tuple(blk), src.dtype),
        scratch_shapes=[pltpu.SemaphoreType.DMA((N_DEV - 1,)), pltpu.SemaphoreType.DMA((N_DEV - 1,)),
                        pltpu.SemaphoreType.DMA])(src)


def _adam(gparts, w, m, v, name):
    P, R, C = gparts.shape
    tr = _tile(R, 256, 8)

    def body(g_ref, w_ref, m_ref, v_ref, go_ref, d_ref, mo_ref, vo_ref):
        g = g_ref[0]
        for p in range(1, P):
            g = g + g_ref[p]
        m2 = ADAM_B1 * m_ref[...] + (1.0 - ADAM_B1) * g
        v2 = ADAM_B2 * v_ref[...] + (1.0 - ADAM_B2) * (g * g)
        m_hat = m2 / (1.0 - ADAM_B1 ** ADAM_STEP)
        v_hat = v2 / (1.0 - ADAM_B2 ** ADAM_STEP)
        go_ref[...] = g
        d_ref[...] = -ADAM_LR * (m_hat / (jnp.sqrt(v_hat) + ADAM_EPS) + ADAM_WD * w_ref[...])
        mo_ref[...] = m2
        vo_ref[...] = v2

    spec = pl.BlockSpec((tr, C), lambda i: (i, 0))
    shp = jax.ShapeDtypeStruct((R, C), F32)
    return pl.pallas_call(
        body, name=name, grid=(R // tr,),
        in_specs=[pl.BlockSpec((P, tr, C), lambda i: (0, i, 0)), spec, spec, spec],
        out_specs=[spec] * 4, out_shape=[shp] * 4, compiler_params=_params(("parallel",)))(gparts, w, m, v)


def _sum_parts(gparts, name):
    P, R, C = gparts.shape
    tr = _tile(R, 256, 8)

    def body(g_ref, o_ref):
        g = g_ref[0]
        for p in range(1, P):
            g = g + g_ref[p]
        o_ref[...] = g

    return pl.pallas_call(
        body, name=name, grid=(R // tr,), in_specs=[pl.BlockSpec((P, tr, C), lambda i: (0, i, 0))],
        out_specs=pl.BlockSpec((tr, C), lambda i: (i, 0)), out_shape=jax.ShapeDtypeStruct((R, C), F32),
        compiler_params=_params(("parallel",)))(gparts)


def _pack(pieces, width, row_mult):
    flat = jnp.concatenate([p.reshape(-1) for p in pieces])
    n = flat.shape[0]
    rows = -(-n // width)
    rows = -(-rows // row_mult) * row_mult
    return jnp.pad(flat, (0, rows * width - n)).reshape(rows, width)


def _unpack(flat2d, shapes, lead=None):
    out, off = [], 0
    if lead is None:
        flat = flat2d.reshape(-1)
        for s in shapes:
            n = int(np.prod(s))
            out.append(flat[off:off + n].reshape(s))
            off += n
    else:
        flat = flat2d.reshape(lead, -1)
        for s in shapes:
            n = int(np.prod(s))
            out.append(flat[:, off:off + n].reshape((lead,) + tuple(s)))
            off += n
    return out


def _col_pieces(off, width, shard_w, fetch):
    out, c, end = [], off, off + width
    while c < end:
        d = c // shard_w
        hi = min(end, (d + 1) * shard_w)
        out.append(fetch(d, c - d * shard_w, hi - d * shard_w))
        c = hi
    return out


def _work_cols_of_orig(lo, hi, fetch):
    out = []
    for name, ow in _ORIG:
        o = _ORIG_OFF[name][0]
        a, b = max(lo, o), min(hi, o + ow)
        if a < b:
            w0 = _WORK_OFF[name][0]
            out.append(fetch(w0 + a - o, w0 + b - o))
    return out


def _heads(x, n):
    return x.reshape(x.shape[0], n, HEAD_DIM).transpose(1, 0, 2)


def _unheads(x):
    return x.transpose(1, 0, 2).reshape(x.shape[1], x.shape[0] * HEAD_DIM)


def _t5_bucket_np(dist):
    max_exact = REL_BUCKETS // 2
    d = np.maximum(dist, 0)
    scaled = (np.log(np.maximum(d, 1).astype(np.float32) / np.float32(max_exact))
              / np.float32(math.log(REL_MAX_DIST / max_exact))).astype(np.float32)
    large = np.minimum(max_exact + (scaled * np.float32(REL_BUCKETS - max_exact)).astype(np.int32), REL_BUCKETS - 1)
    return np.where(d < max_exact, d, large)


def _bucket_onehot():
    q_idx = np.arange(BLOCK)[:, None]
    k_idx = np.arange(2 * BLOCK)[None, :]
    bucket = _t5_bucket_np(q_idx + BLOCK - k_idx).reshape(-1)
    oh = np.zeros((LANE, BLOCK * 2 * BLOCK), np.float32)
    oh[bucket, np.arange(bucket.shape[0])] = 1.0
    return oh


def kernel(x, meta_tokens, rel_bias_table, norm_mix, w_in, swa_sinks, fox_forget_bias, conv_w, conv_b, lru_w_r, lru_b_r, lru_w_i, lru_b_i, lru_lambda, w_branch, w_out, norm_ffn, w_ffn_in, w_ffn_out, norm_final, loss_target, m_meta_tokens, m_rel_bias_table, m_norm_mix, m_w_in, m_swa_sinks, m_fox_forget_bias, m_conv_w, m_conv_b, m_lru_w_r, m_lru_b_r, m_lru_w_i, m_lru_b_i, m_lru_lambda, m_w_branch, m_w_out, m_norm_ffn, m_w_ffn_in, m_w_ffn_out, m_norm_final, v_meta_tokens, v_rel_bias_table, v_norm_mix, v_w_in, v_swa_sinks, v_fox_forget_bias, v_conv_w, v_conv_b, v_lru_w_r, v_lru_b_r, v_lru_w_i, v_lru_b_i, v_lru_lambda, v_w_branch, v_w_out, v_norm_ffn, v_w_ffn_in, v_w_ffn_out, v_norm_final):
    S = x.shape[1]
    T = N_META + S
    n_pad = (-T) % BLOCK
    Tp = T + n_pad
    first = n_pad + N_META
    TR = _tile(Tp, 384)
    TQ = _tile(Tp, 384)
    NQ = Tp // TQ
    me = 4 * lax.axis_index("x") + 2 * lax.axis_index("y") + lax.axis_index("c")

    big_names = ("w_in", "w_ffn_in", "w_ffn_out", "w_branch", "w_out")
    big_w = dict(w_in=w_in, w_ffn_in=w_ffn_in, w_ffn_out=w_ffn_out, w_branch=w_branch, w_out=w_out)
    big_m = dict(w_in=m_w_in, w_ffn_in=m_w_ffn_in, w_ffn_out=m_w_ffn_out, w_branch=m_w_branch, w_out=m_w_out)
    big_v = dict(w_in=v_w_in, w_ffn_in=v_w_ffn_in, w_ffn_out=v_w_ffn_out, w_branch=v_w_branch, w_out=v_w_out)
    rows2d = lambda w: w.reshape(-1, w.shape[-1])
    gathered = {n: _exchange(rows2d(big_w[n]).astype(BF16), True, "gather_" + n) for n in big_names}
    shard = lambda n, d: gathered[n][d].reshape(big_w[n].shape)
    in_shard_w = IN_COLS // N_DEV
    W_in = jnp.concatenate(
        [p for name, width in _WORK
         for p in _col_pieces(*_ORIG_OFF[name], in_shard_w, lambda d, lo, hi: shard("w_in", d)[..., lo:hi])
         + ([jnp.zeros((DEPTH, D_MODEL, width - _ORIG_OFF[name][1]), BF16)] if width > _ORIG_OFF[name][1] else [])],
        axis=-1)
    W_ffn_in = jnp.concatenate([shard("w_ffn_in", d) for d in range(N_DEV)], axis=-1)
    W_ffn_out = jnp.concatenate([shard("w_ffn_out", d) for d in range(N_DEV)], axis=1)
    W_branch = jnp.concatenate([shard("w_branch", d) for d in range(N_DEV)], axis=-1)
    W_out = jnp.concatenate([shard("w_out", d) for d in range(N_DEV)], axis=1)

    small_sh = _exchange(_pack([meta_tokens, conv_w], D_MODEL, 8), True, "gather_small")
    g_meta, g_conv_w = _unpack(small_sh, [meta_tokens.shape, conv_w.shape], lead=N_DEV)
    meta_full = g_meta.transpose(1, 0, 2).reshape(N_META, D_MODEL)
    conv_w_full = g_conv_w.transpose(1, 2, 0, 3).reshape(DEPTH, CONV_WIDTH, LRU_WIDTH)

    onehot = jnp.asarray(_bucket_onehot())
    table_t = jnp.pad(rel_bias_table.T, ((0, 0), (0, LANE - REL_BUCKETS)))
    bias = _mm(table_t, onehot, "nn", "swa_bias", exact=True, tn_cap=4096).reshape(SWA_Q_HEADS, BLOCK, 2 * BLOCK)

    def dense_blocks(w):
        rows = []
        for b in range(LRU_BLOCKS):
            rows.append(jnp.pad(w[b], ((0, 0), (b * LRU_BLOCK_DIM, (LRU_BLOCKS - 1 - b) * LRU_BLOCK_DIM))))
        return jnp.concatenate(rows, axis=0)

    h = jnp.concatenate([jnp.zeros((n_pad, D_MODEL), F32), meta_full, x[0]], axis=0)
    saved = []

    for l in range(DEPTH):
        sv = {"h": h}
        g_mix = norm_mix[l][None, :]

        def norm_fwd(i, rows, consts):
            xx = rows[0]
            return [xx * lax.rsqrt(jnp.mean(xx * xx, axis=1, keepdims=True) + EPS) * consts[0]], []

        (u,) = _rowwise(norm_fwd, "norm_fwd", Tp, TR, [(h, None, 0)], [g_mix], [(D_MODEL, F32)])
        proj = _mm(u, W_in[l], "nn", "proj", tn_cap=384)
        sv["u"], sv["proj"] = u, proj

        def col(name):
            off, w = _WORK_OFF[name]
            return (proj, w, off // w)

        def cols(name):
            off, w = _WORK_OFF[name]
            return proj[:, off:off + w]

        qa, ka, va = _heads(cols("qa"), 8), _heads(cols("ka"), 2), _heads(cols("va"), 2)
        sinks_l = jnp.broadcast_to(swa_sinks[l][:, None, None], (SWA_Q_HEADS, 1, LANE))
        o_a_h, lse_a = _swa_fwd(qa, ka, va, bias, sinks_l, n_pad)
        o_a = _unheads(o_a_h)
        sv.update(qa=qa, ka=ka, va=va, sinks=sinks_l, o_a_h=o_a_h, lse_a=lse_a, o_a=o_a)

        fbias = jnp.pad(fox_forget_bias[l], (0, LANE - FOX_HEADS))[None, :]

        def logf_fwd(i, rows, consts):
            return [_log_sigmoid(rows[0] + consts[0])], []

        (logf,) = _rowwise(logf_fwd, "logf_fwd", Tp, TR, [col("fl")], [fbias], [(LANE, F32)])
        cum = _scan(None, logf, "cumsum")
        cum_h = cum[:, :FOX_HEADS].T
        ccol = cum_h[:, :, None]
        crow = jnp.where(jnp.arange(Tp)[None, :] < n_pad, -NEG_INF, cum_h).reshape(FOX_HEADS, NQ, 1, TQ)
        qf, kf, vf = _heads(cols("qf"), 8), _heads(cols("kf"), 8), _heads(cols("vf"), 8)
        o_f_h, lse_f = _fox_fwd(qf, kf, vf, ccol, crow, TQ)
        o_f = _unheads(o_f_h)
        sv.update(fbias=fbias, qf=qf, kf=kf, vf=vf, ccol=ccol, crow=crow, o_f_h=o_f_h, lse_f=lse_f, o_f=o_f)

        xc = cols("xc")
        xs = [_shift_down(xc, CONV_WIDTH - 1 - i) for i in range(CONV_WIDTH - 1)] + [xc]
        cw = conv_w_full[l]
        cb = conv_b[l][None, :]

        def conv_fwd(i, rows, consts):
            w, b = consts
            acc = rows[0] * w[0:1, :]
            for t in range(1, CONV_WIDTH):
                acc = acc + rows[t] * w[t:t + 1, :]
            return [acc + b], []

        (xconv,) = _rowwise(conv_fwd, "conv_fwd", Tp, TR, [(a, None, 0) for a in xs], [cw, cb], [(LRU_WIDTH, F32)])
        w_ri = jnp.concatenate([dense_blocks(lru_w_r[l]), dense_blocks(lru_w_i[l])], axis=1)
        pre = _mm(xconv, w_ri, "nn", "lru_gates")
        lru_consts = [lru_b_r[l][None, :], lru_b_i[l][None, :], lru_lambda[l][None, :]]

        def lru_fwd(i, rows, consts):
            pr, xv = rows
            b_r, b_i, lam = consts
            r = _sigmoid(pr[:, :LRU_WIDTH] + b_r)
            gi = _sigmoid(pr[:, LRU_WIDTH:] + b_i)
            log_a = LRU_C * r * _log_sigmoid(lam)
            valid = (_row_ids(i, TR) >= n_pad).astype(F32)
            inp = jnp.sqrt(_neg_expm1(2.0 * log_a)) * (gi * xv) * valid
            return [jnp.exp(log_a), inp], []

        a_dec, inp = _rowwise(lru_fwd, "lru_fwd", Tp, TR, [(pre, None, 0), (xconv, None, 0)], lru_consts,
                              [(LRU_WIDTH, F32), (LRU_WIDTH, F32)])
        hs = _scan(a_dec, inp, "lru_scan")

        def oc_fwd(i, rows, consts):
            return [rows[0] * _gelu(rows[1])], []

        (o_c,) = _rowwise(oc_fwd, "oc_fwd", Tp, TR, [(hs, None, 0), col("yc")], [], [(LRU_WIDTH, F32)])
        sv.update(xs=xs, cw=cw, xconv=xconv, w_ri=w_ri, pre=pre, lru_consts=lru_consts, a_dec=a_dec, hs=hs, o_c=o_c)

        ba = _mm(o_a, W_branch[l, 0], "nn", "branch")
        bf = _mm(o_f, W_branch[l, 1], "nn", "branch")
        bc = _mm(o_c, W_branch[l, 2], "nn", "branch")

        def merge_fwd(i, rows, consts):
            g, b0, b1, b2 = rows
            valid = (_row_ids(i, BLOCK) >= n_pad).astype(F32)
            mg = (_sigmoid(g[:, :D_MODEL]) * b0 + _sigmoid(g[:, D_MODEL:2 * D_MODEL]) * b1
                  + _sigmoid(g[:, 2 * D_MODEL:]) * b2)
            return [mg * valid], []

        (merged,) = _rowwise(merge_fwd, "merge_fwd", Tp, BLOCK,
                             [col("gates"), (ba, None, 0), (bf, None, 0), (bc, None, 0)], [], [(D_MODEL, F32)])
        h1 = _mm(merged, W_out[l], "nn", "out_proj", res=h)
        sv.update(ba=ba, bf=bf, bc=bc, merged=merged, h1=h1)

        g_ffn = norm_ffn[l][None, :]
        (u2,) = _rowwise(norm_fwd, "norm_fwd", Tp, TR, [(h1, None, 0)], [g_ffn], [(D_MODEL, F32)])
        ff = _mm(u2, W_ffn_in[l], "nn", "ffn_in")

        def act_fwd(i, rows, consts):
            gate, up = rows
            return [gate * _sigmoid(gate) * up], []

        (act,) = _rowwise(act_fwd, "act_fwd", Tp, TR, [(ff, D_FF, 0), (ff, D_FF, 1)], [], [(D_FF, F32)])
        h = _mm(act, W_ffn_out[l], "nn", "ffn_out", res=h1)
        sv.update(u2=u2, ff=ff, act=act)
        saved.append(sv)

    tgt = jnp.concatenate([jnp.zeros((first, D_MODEL), F32), loss_target[0]], axis=0)
    g_fin = norm_final[None, :]

    def head(i, rows, consts):
        xx, tg = rows
        g = consts[0]
        valid = (_row_ids(i, TR) >= first).astype(F32)
        rstd = lax.rsqrt(jnp.mean(xx * xx, axis=1, keepdims=True) + EPS)
        xhat = xx * rstd
        err = (xhat * g - tg) * valid
        loss_rows = 0.5 * jnp.mean(err * err, axis=1, keepdims=True)
        dy = err * (1.0 / D_MODEL)
        dxhat = dy * g
        dx = rstd * (dxhat - xhat * jnp.mean(dxhat * xhat, axis=1, keepdims=True))
        return [dx], [jnp.broadcast_to(_colsum(loss_rows), (1, LANE)), _colsum(dy * xhat)]

    dh, loss_part, d_norm_final = _rowwise(head, "loss_head", Tp, TR, [(h, None, 0), (tgt, None, 0)], [g_fin],
                                           [(D_MODEL, F32)], [LANE, D_MODEL])
    loss = lax.psum(loss_part[0, 0], ("x", "y", "c"))

    def norm_bwd_call(xin, g, du, dres):
        def norm_bwd(i, rows, consts):
            xx, dd, rr = rows
            gg = consts[0]
            valid = (_row_ids(i, TR) >= n_pad).astype(F32)
            rstd = lax.rsqrt(jnp.mean(xx * xx, axis=1, keepdims=True) + EPS)
            xhat = xx * rstd
            dxhat = dd * gg
            dx = rstd * (dxhat - xhat * jnp.mean(dxhat * xhat, axis=1, keepdims=True))
            return [rr + dx * valid], [_colsum(dd * xhat)]

        return _rowwise(norm_bwd, "norm_bwd", Tp, TR, [(xin, None, 0), (du, None, 0), (dres, None, 0)], [g],
                        [(D_MODEL, F32)], [D_MODEL])

    grads = {k: [None] * DEPTH for k in ("norm_mix", "w_in", "swa_sinks", "fox_forget_bias", "conv_w", "conv_b",
                                         "lru_w_r", "lru_b_r", "lru_w_i", "lru_b_i", "lru_lambda", "w_branch",
                                         "w_out", "norm_ffn", "w_ffn_in", "w_ffn_out")}
    d_bias_total = None
    for l in reversed(range(DEPTH)):
        sv = saved[l]
        proj = sv["proj"]

        def col(name):
            off, w = _WORK_OFF[name]
            return (proj, w, off // w)

        dh2 = dh
        d_act = _mm(dh2, W_ffn_out[l], "nt", "d_act")
        grads["w_ffn_out"][l] = _mm(sv["act"], dh2, "tn", "dw_ffn_out")

        def act_bwd(i, rows, consts):
            gate, up, da = rows
            sg = _sigmoid(gate)
            d_gate = da * up * (sg * (1.0 + gate * (1.0 - sg)))
            d_up = da * (gate * sg)
            return [jnp.concatenate([d_gate, d_up], axis=1)], []

        (dff,) = _rowwise(act_bwd, "act_bwd", Tp, BLOCK, [(sv["ff"], D_FF, 0), (sv["ff"], D_FF, 1), (d_act, None, 0)],
                          [], [(2 * D_FF, F32)])
        grads["w_ffn_in"][l] = _mm(sv["u2"], dff, "tn", "dw_ffn_in")
        du2 = _mm(dff, W_ffn_in[l], "nt", "du2")
        dh1, dg = norm_bwd_call(sv["h1"], norm_ffn[l][None, :], du2, dh2)
        grads["norm_ffn"][l] = dg[0]

        dmerged = _mm(dh1, W_out[l], "nt", "d_merged")
        grads["w_out"][l] = _mm(sv["merged"], dh1, "tn", "dw_out")

        def merge_bwd(i, rows, consts):
            g, b0, b1, b2, dm = rows
            dm = dm * (_row_ids(i, BLOCK) >= n_pad).astype(F32)
            outs, dgs = [], []
            for k, bk in enumerate((b0, b1, b2)):
                sg = _sigmoid(g[:, k * D_MODEL:(k + 1) * D_MODEL])
                outs.append(dm * sg)
                dgs.append(dm * bk * sg * (1.0 - sg))
            return outs + [jnp.concatenate(dgs, axis=1)], []

        d_ba, d_bf, d_bc, d_gates = _rowwise(
            merge_bwd, "merge_bwd", Tp, BLOCK,
            [col("gates"), (sv["ba"], None, 0), (sv["bf"], None, 0), (sv["bc"], None, 0), (dmerged, None, 0)], [],
            [(D_MODEL, F32)] * 3 + [(3 * D_MODEL, F32)])
        grads["w_branch"][l] = jnp.stack([_mm(sv["o_a"], d_ba, "tn", "dw_branch"),
                                          _mm(sv["o_f"], d_bf, "tn", "dw_branch"),
                                          _mm(sv["o_c"], d_bc, "tn", "dw_branch")])
        do_a = _mm(d_ba, W_branch[l, 0], "nt", "d_branch")
        do_f = _mm(d_bf, W_branch[l, 1], "nt", "d_branch")
        do_c = _mm(d_bc, W_branch[l, 2], "nt", "d_branch")

        def oc_bwd(i, rows, consts):
            d, hv, yv = rows
            return [d * _gelu(yv), d * hv * _gelu_grad(yv)], []

        d_hs, d_yc = _rowwise(oc_bwd, "oc_bwd", Tp, TR, [(do_c, None, 0), (sv["hs"], None, 0), col("yc")], [],
                              [(LRU_WIDTH, F32)] * 2)
        d_state = _rev_scan(sv["a_dec"], d_hs, "lru_scan")
        hs_prev = _shift_down(sv["hs"], 1)

        def lru_bwd(i, rows, consts):
            dH, hp, pr, xv = rows
            b_r, b_i, lam = consts
            valid = (_row_ids(i, TR) >= n_pad).astype(F32)
            r = _sigmoid(pr[:, :LRU_WIDTH] + b_r)
            gi = _sigmoid(pr[:, LRU_WIDTH:] + b_i)
            lsl = _log_sigmoid(lam)
            log_a = LRU_C * r * lsl
            a = jnp.exp(log_a)
            one_m_e = _neg_expm1(2.0 * log_a)
            mult = jnp.sqrt(one_m_e)
            d_inp = dH * valid
            d_mult = d_inp * gi * xv
            d_gi = d_inp * mult * xv
            d_x = d_inp * mult * gi
            d_log_a = dH * hp * a - d_mult * (1.0 - one_m_e) / mult
            d_pre_r = d_log_a * (LRU_C * lsl) * r * (1.0 - r)
            d_pre_i = d_gi * gi * (1.0 - gi)
            d_lam = _colsum(d_log_a * (LRU_C * r)) * _sigmoid(-lam)
            return [jnp.concatenate([d_pre_r, d_pre_i], axis=1), d_x], [_colsum(d_pre_r), _colsum(d_pre_i), d_lam]

        d_pre, d_xdir, d_b_r, d_b_i, d_lam = _rowwise(
            lru_bwd, "lru_bwd", Tp, TR, [(d_state, None, 0), (hs_prev, None, 0), (sv["pre"], None, 0),
                                         (sv["xconv"], None, 0)], sv["lru_consts"],
            [(2 * LRU_WIDTH, F32), (LRU_WIDTH, F32)], [LRU_WIDTH] * 3)
        grads["lru_b_r"][l], grads["lru_b_i"][l], grads["lru_lambda"][l] = d_b_r[0], d_b_i[0], d_lam[0]
        d_w_ri = _mm(sv["xconv"], d_pre, "tn", "dw_lru_gates")

        def diag_blocks(wd):
            w4 = wd.reshape(LRU_BLOCKS, LRU_BLOCK_DIM, LRU_BLOCKS, LRU_BLOCK_DIM)
            return jnp.stack([w4[b, :, b, :] for b in range(LRU_BLOCKS)])

        grads["lru_w_r"][l] = diag_blocks(d_w_ri[:, :LRU_WIDTH])
        grads["lru_w_i"][l] = diag_blocks(d_w_ri[:, LRU_WIDTH:])
        d_conv = _mm(d_pre, sv["w_ri"], "nt", "d_xconv", res=d_xdir)
        dcs = [d_conv] + [_shift_up(d_conv, s) for s in range(1, CONV_WIDTH)]

        def conv_bwd(i, rows, consts):
            w = consts[0]
            d = rows[:CONV_WIDTH]
            xsh = rows[CONV_WIDTH:]
            dxc = d[0] * w[CONV_WIDTH - 1:CONV_WIDTH, :]
            for s in range(1, CONV_WIDTH):
                dxc = dxc + d[s] * w[CONV_WIDTH - 1 - s:CONV_WIDTH - s, :]
            return [dxc], [_colsum(xsh[t] * d[0]) for t in range(CONV_WIDTH)] + [_colsum(d[0])]

        res = _rowwise(conv_bwd, "conv_bwd", Tp, TR, [(a, None, 0) for a in dcs + sv["xs"]], [sv["cw"]],
                       [(LRU_WIDTH, F32)], [LRU_WIDTH] * (CONV_WIDTH + 1))
        d_xc = res[0]
        grads["conv_w"][l] = jnp.concatenate(res[1:1 + CONV_WIDTH], axis=0)
        grads["conv_b"][l] = res[1 + CONV_WIDTH][0]

        do_f_h = _heads(do_f, 8)
        dqf, delta_f, dccol = _fox_bwd_dq(sv["qf"], sv["kf"], sv["vf"], sv["ccol"], sv["crow"], sv["o_f_h"],
                                          sv["lse_f"], do_f_h, TQ)
        dkf, dvf, dcrow = _fox_bwd_dkv(sv["qf"], sv["kf"], sv["vf"], sv["ccol"], sv["crow"], sv["lse_f"], delta_f,
                                       do_f_h, TQ)
        lanes = lambda z: jnp.pad(z.reshape(FOX_HEADS, Tp).T, ((0, 0), (0, LANE - FOX_HEADS)))
        dlogf = _rev_scan(None, lanes(dcrow), "cumsum_bwd", g2=lanes(dccol))

        def logf_bwd(i, rows, consts):
            dl, fl = rows
            valid = (_row_ids(i, TR) >= n_pad).astype(F32)
            lane_ok = (lax.broadcasted_iota(jnp.int32, (1, LANE), 1) < FOX_HEADS).astype(F32)
            dfl = dl * _sigmoid(-(fl + consts[0])) * valid * lane_ok
            return [dfl], [_colsum(dfl)]

        d_fl, d_fb = _rowwise(logf_bwd, "logf_bwd", Tp, TR, [(dlogf, None, 0), col("fl")], [sv["fbias"]],
                              [(LANE, F32)], [LANE])
        grads["fox_forget_bias"][l] = d_fb[0, :FOX_HEADS]

        dqa, dk_strip, dv_strip, d_bias_l, d_sink = _swa_bwd(sv["qa"], sv["ka"], sv["va"], bias, sv["sinks"],
                                                             sv["o_a_h"], sv["lse_a"], _heads(do_a, 8), n_pad)
        grads["swa_sinks"][l] = d_sink[:, 0, 0]
        d_bias_total = d_bias_l if d_bias_total is None else d_bias_total + d_bias_l

        def kv_parts(strip):
            cur = strip[:, :, BLOCK:, :].reshape(SWA_Q_HEADS, Tp, HEAD_DIM)
            nxt = jnp.pad(strip[:, 1:, :BLOCK, :], ((0, 0), (0, 1), (TQ - BLOCK, 0), (0, 0)))
            nxt = nxt.reshape(SWA_Q_HEADS, Tp, HEAD_DIM)
            both = jnp.stack([cur, nxt]).reshape(2, SWA_KV_HEADS, SWA_GROUP, Tp, HEAD_DIM)
            return both.transpose(3, 0, 2, 1, 4).reshape(Tp, 2 * SWA_GROUP * KV_W)

        dk_parts, dv_parts = kv_parts(dk_strip), kv_parts(dv_strip)

        def assemble(i, rows, consts):
            dg, dqa_, dqf_, dkf_, dvf_, dxc_, dyc_, dkp, dvp, dfl_ = rows

            def fold(parts):
                acc = parts[:, :KV_W]
                for p in range(1, 2 * SWA_GROUP):
                    acc = acc + parts[:, p * KV_W:(p + 1) * KV_W]
                return acc

            return [jnp.concatenate([dg, dqa_, dqf_, dkf_, dvf_, dxc_, dyc_, fold(dkp), fold(dvp), dfl_], axis=1)], []

        (dproj,) = _rowwise(
            assemble, "assemble_dproj", Tp, BLOCK,
            [(d_gates, None, 0), (_unheads(dqa), None, 0), (_unheads(dqf), None, 0), (_unheads(dkf), None, 0),
             (_unheads(dvf), None, 0), (d_xc, None, 0), (d_yc, None, 0), (dk_parts, None, 0), (dv_parts, None, 0),
             (d_fl, None, 0)], [], [(WORK_COLS, F32)])
        grads["w_in"][l] = _mm(sv["u"], dproj, "tn", "dw_in", tn_cap=384)
        du = _mm(dproj, W_in[l], "nt", "du", tk_cap=384)
        dh, dg = norm_bwd_call(sv["h"], norm_mix[l][None, :], du, dh1)
        grads["norm_mix"][l] = dg[0]

    d_table = _mm(d_bias_total.reshape(SWA_Q_HEADS, BLOCK * 2 * BLOCK), onehot, "nt", "d_rel_table", exact=True,
                  tk_cap=4096)
    g_rel = d_table[:, :REL_BUCKETS].T
    g_meta_full = dh[n_pad:first]
    grad_x = dh[first:][None]

    stack = lambda k: jnp.stack(grads[k])
    G = {n: stack(n) for n in big_names}
    cut = lambda n, ax, d: lax.slice_in_dim(G[n], d * big_w[n].shape[ax], (d + 1) * big_w[n].shape[ax], axis=ax)
    send = {
        "w_in": [jnp.concatenate(_work_cols_of_orig(d * in_shard_w, (d + 1) * in_shard_w,
                                                    lambda lo, hi: G["w_in"][..., lo:hi]), axis=-1)
                 for d in range(N_DEV)],
        "w_ffn_in": [cut("w_ffn_in", 2, d) for d in range(N_DEV)],
        "w_ffn_out": [cut("w_ffn_out", 1, d) for d in range(N_DEV)],
        "w_branch": [cut("w_branch", 3, d) for d in range(N_DEV)],
        "w_out": [cut("w_out", 1, d) for d in range(N_DEV)],
    }
    big_res = {}
    for n in big_names:
        recv = _exchange(jnp.stack([rows2d(p) for p in send[n]]), False, "scatter_" + n)
        res = _adam(recv, rows2d(big_w[n]), rows2d(big_m[n]), rows2d(big_v[n]), "adam_" + n)
        big_res[n] = [o.reshape(big_w[n].shape) for o in res]

    rep_names = ("rel_bias_table", "norm_mix", "swa_sinks", "fox_forget_bias", "conv_b", "lru_w_r", "lru_b_r",
                 "lru_w_i", "lru_b_i", "lru_lambda", "norm_ffn", "norm_final")
    rep_w = dict(rel_bias_table=rel_bias_table, norm_mix=norm_mix, swa_sinks=swa_sinks,
                 fox_forget_bias=fox_forget_bias, conv_b=conv_b, lru_w_r=lru_w_r, lru_b_r=lru_b_r, lru_w_i=lru_w_i,
                 lru_b_i=lru_b_i, lru_lambda=lru_lambda, norm_ffn=norm_ffn, norm_final=norm_final)
    rep_m = dict(rel_bias_table=m_rel_bias_table, norm_mix=m_norm_mix, swa_sinks=m_swa_sinks,
                 fox_forget_bias=m_fox_forget_bias, conv_b=m_conv_b, lru_w_r=m_lru_w_r, lru_b_r=m_lru_b_r,
                 lru_w_i=m_lru_w_i, lru_b_i=m_lru_b_i, lru_lambda=m_lru_lambda, norm_ffn=m_norm_ffn,
                 norm_final=m_norm_final)
    rep_v = dict(rel_bias_table=v_rel_bias_table, norm_mix=v_norm_mix, swa_sinks=v_swa_sinks,
                 fox_forget_bias=v_fox_forget_bias, conv_b=v_conv_b, lru_w_r=v_lru_w_r, lru_b_r=v_lru_b_r,
                 lru_w_i=v_lru_w_i, lru_b_i=v_lru_b_i, lru_lambda=v_lru_lambda, norm_ffn=v_norm_ffn,
                 norm_final=v_norm_final)
    rep_g = {n: (g_rel if n == "rel_bias_table" else d_norm_final[0] if n == "norm_final" else stack(n))
             for n in rep_names}
    small_g = [rep_g[n] for n in rep_names] + [g_meta_full, stack("conv_w")]
    small_shapes = [rep_w[n].shape for n in rep_names] + [(N_META, D_MODEL), (DEPTH, CONV_WIDTH, LRU_WIDTH)]
    gs_all = _exchange(_pack(small_g, D_MODEL, 8), True, "gather_small_grads")
    gs_sum = _unpack(_sum_parts(gs_all, "sum_small_grads"), small_shapes)
    gsum = dict(zip(rep_names, gs_sum[:len(rep_names)]))
    g_meta_sh = lax.dynamic_slice_in_dim(gs_sum[-2], me * (D_MODEL // N_DEV), D_MODEL // N_DEV, axis=1)
    g_convw_sh = lax.dynamic_slice_in_dim(gs_sum[-1], me * (LRU_WIDTH // N_DEV), LRU_WIDTH // N_DEV, axis=2)
    sm_names = rep_names + ("meta_tokens", "conv_w")
    sm_w = [rep_w[n] for n in rep_names] + [meta_tokens, conv_w]
    sm_m = [rep_m[n] for n in rep_names] + [m_meta_tokens, m_conv_w]
    sm_v = [rep_v[n] for n in rep_names] + [v_meta_tokens, v_conv_w]
    sm_g = [gsum[n] for n in rep_names] + [g_meta_sh, g_convw_sh]
    sm_shapes = [w.shape for w in sm_w]
    sm_out = [_unpack(o, sm_shapes) for o in _adam(_pack(sm_g, D_MODEL, 8)[None], _pack(sm_w, D_MODEL, 8),
                                                   _pack(sm_m, D_MODEL, 8), _pack(sm_v, D_MODEL, 8), "adam_small")]
    sm_res = {n: [sm_out[k][j] for k in range(4)] for j, n in enumerate(sm_names)}

    order = ("meta_tokens", "rel_bias_table", "norm_mix", "w_in", "swa_sinks", "fox_forget_bias", "conv_w", "conv_b",
             "lru_w_r", "lru_b_r", "lru_w_i", "lru_b_i", "lru_lambda", "w_branch", "w_out", "norm_ffn", "w_ffn_in",
             "w_ffn_out", "norm_final")
    allres = {**big_res, **sm_res}
    outs = [loss, grad_x]
    for k in range(4):
        outs += [allres[n][k] for n in order]
    return tuple(outs)
```

```python
import functools
import math

import numpy as np
import jax
import jax.numpy as jnp
from jax import lax
from jax.experimental import pallas as pl
from jax.experimental.pallas import tpu as pltpu

F32 = jnp.float32
BF16 = jnp.bfloat16

N_DEV = 8
D_MODEL = 1024
DEPTH = 4
HEAD_DIM = 64
N_META = 16
BLOCK = 128
NEG_INF = -1e30
SWA_Q_HEADS = 8
SWA_KV_HEADS = 2
SWA_GROUP = SWA_Q_HEADS // SWA_KV_HEADS
FOX_HEADS = 8
LRU_WIDTH = D_MODEL // 2
LRU_BLOCKS = 8
LRU_BLOCK_DIM = LRU_WIDTH // LRU_BLOCKS
CONV_WIDTH = 4
LRU_C = 8.0
REL_BUCKETS = 32
REL_MAX_DIST = 128
D_FF = 2816
N_BRANCH = 3
ATT_W = SWA_Q_HEADS * HEAD_DIM
KV_W = SWA_KV_HEADS * HEAD_DIM
SCALE = HEAD_DIM ** -0.5
EPS = 1e-6

_ORIG = (("qa", ATT_W), ("ka", KV_W), ("va", KV_W), ("qf", ATT_W), ("kf", ATT_W), ("vf", ATT_W),
         ("fl", FOX_HEADS), ("xc", LRU_WIDTH), ("yc", LRU_WIDTH), ("gates", N_BRANCH * D_MODEL))
IN_COLS = sum(w for _, w in _ORIG)
_WORK = (("gates", 3072), ("qa", 512), ("qf", 512), ("kf", 512), ("vf", 512), ("xc", 512), ("yc", 512),
         ("ka", 128), ("va", 128), ("fl", 128))
WORK_COLS = sum(w for _, w in _WORK)


def _offsets(table):
    off, out = 0, {}
    for n, w in table:
        out[n] = (off, w)
        off += w
    return out


_ORIG_OFF = _offsets(_ORIG)
_WORK_OFF = _offsets(_WORK)

ADAM_LR = 0.001
ADAM_B1 = 0.9
ADAM_B2 = 0.999
ADAM_EPS = 1e-08
ADAM_WD = 0.01
ADAM_STEP = 10

VMEM_LIMIT = 52 * 1024 * 1024
LANE = 128


def _tile(n, cap, mult=LANE):
    if n <= cap:
        return n
    best = None
    for d in range(mult, cap + 1, mult):
        if n % d == 0:
            best = d
    assert best is not None, (n, cap, mult)
    return best


def _params(sem):
    return pltpu.CompilerParams(dimension_semantics=sem, vmem_limit_bytes=VMEM_LIMIT)


def _sigmoid(x):
    return 1.0 / (1.0 + jnp.exp(-x))


def _log_sigmoid(x):
    return jnp.minimum(x, 0.0) - jnp.log(1.0 + jnp.exp(-jnp.abs(x)))


def _neg_expm1(x):
    series = -x * (1.0 + x * (0.5 + x * (1.0 / 6.0 + x * (1.0 / 24.0 + x * (1.0 / 120.0)))))
    return jnp.where(x > -0.1, series, 1.0 - jnp.exp(x))


_GELU_C = math.sqrt(2.0 / math.pi)


def _gelu(x):
    return 0.5 * x * (1.0 + jnp.tanh(_GELU_C * (x + 0.044715 * x * x * x)))


def _gelu_grad(x):
    t = jnp.tanh(_GELU_C * (x + 0.044715 * x * x * x))
    return 0.5 * (1.0 + t) + 0.5 * x * (1.0 - t * t) * _GELU_C * (1.0 + 3.0 * 0.044715 * x * x)


def _xch_ops(src_ref, out_ref, send_sems, recv_sems, local_sem, gather):
    x, y, c = lax.axis_index("x"), lax.axis_index("y"), lax.axis_index("c")
    me = 4 * x + 2 * y + c

    def copy(r, mine):
        px, py, pc = x ^ ((r >> 2) & 1), y ^ ((r >> 1) & 1), c ^ (r & 1)
        pid = 4 * px + 2 * py + pc
        return pltpu.make_async_remote_copy(
            src_ref=src_ref if gather else src_ref.at[pid], dst_ref=out_ref.at[me if mine else pid],
            send_sem=send_sems.at[r - 1], recv_sem=recv_sems.at[r - 1],
            device_id=(px, py, pc), device_id_type=pl.DeviceIdType.MESH)

    def local():
        return pltpu.make_async_copy(src_ref if gather else src_ref.at[me], out_ref.at[me], local_sem)

    def start():
        local().start()
        for r in range(1, N_DEV):
            copy(r, True).start()

    def wait():
        for r in range(1, N_DEV):
            copy(r, False).wait_recv()
        for r in range(1, N_DEV):
            copy(r, True).wait_send()
        local().wait()

    return start, wait


_XCH_SEMS = [pltpu.SemaphoreType.DMA((N_DEV - 1,)), pltpu.SemaphoreType.DMA((N_DEV - 1,)), pltpu.SemaphoreType.DMA]
_ANY = pl.BlockSpec(memory_space=pl.ANY)


def _xch_shape(src, gather):
    return jax.ShapeDtypeStruct((N_DEV,) + tuple(src.shape if gather else src.shape[1:]), src.dtype)


def _exchange(src, gather, name):
    def body(src_ref, out_ref, send_sems, recv_sems, local_sem):
        start, wait = _xch_ops(src_ref, out_ref, send_sems, recv_sems, local_sem, gather)
        start()
        wait()

    return pl.pallas_call(body, name=name, in_specs=[_ANY], out_specs=_ANY, out_shape=_xch_shape(src, gather),
                          scratch_shapes=list(_XCH_SEMS))(src)


def _pcall(body, name, grid, in_specs, out_specs, out_shape, scratch_shapes, sem, args, host=()):
    if not host:
        outs = pl.pallas_call(body, name=name, grid=grid, in_specs=in_specs, out_specs=out_specs, out_shape=out_shape,
                              scratch_shapes=scratch_shapes, compiler_params=_params(sem))(*args)
        return outs, []
    n_in, n_out, n_scr, n_x = len(args), len(out_shape), len(scratch_shapes), len(host)

    def wrapped(*refs):
        ins, xin = refs[:n_in], refs[n_in:n_in + n_x]
        outs, xout = refs[n_in + n_x:n_in + n_x + n_out], refs[n_in + n_x + n_out:n_in + 2 * n_x + n_out]
        scr = refs[n_in + 2 * n_x + n_out:n_in + 2 * n_x + n_out + n_scr]
        sems = refs[n_in + 2 * n_x + n_out + n_scr:]
        first = functools.reduce(jnp.logical_and, [pl.program_id(k) == 0 for k in range(len(grid))])
        last = functools.reduce(jnp.logical_and, [pl.program_id(k) == grid[k] - 1 for k in range(len(grid))])
        ops = [_xch_ops(xin[k], xout[k], *sems[3 * k:3 * k + 3], host[k][1]) for k in range(n_x)]

        @pl.when(first)
        def _():
            for start, _ in ops:
                start()

        body(*ins, *outs, *scr)

        @pl.when(last)
        def _():
            for _, wait in ops:
                wait()

    res = pl.pallas_call(
        wrapped, name=name, grid=grid, in_specs=list(in_specs) + [_ANY] * n_x,
        out_specs=list(out_specs) + [_ANY] * n_x,
        out_shape=list(out_shape) + [_xch_shape(s, g) for s, g in host],
        scratch_shapes=list(scratch_shapes) + list(_XCH_SEMS) * n_x,
        compiler_params=_params(("arbitrary",) * len(grid)))(*args, *[s for s, _ in host])
    return res[:n_out], res[n_out:]


def _mm(a, b, mode, name, res=None, exact=False, tm_cap=1408, tn_cap=512, tk_cap=1408, host=None):
    if mode == "nn":
        (M, K), (K2, N) = a.shape, b.shape
    elif mode == "nt":
        (M, K), (N, K2) = a.shape, b.shape
    else:
        (K, M), (K2, N) = a.shape, b.shape
    assert K == K2, (a.shape, b.shape, mode)
    tm, tn, tk = _tile(M, tm_cap), _tile(N, tn_cap), _tile(K, tk_cap)
    nk = K // tk
    a_spec = {"nn": pl.BlockSpec((tm, tk), lambda i, j, k: (i, k)),
              "nt": pl.BlockSpec((tm, tk), lambda i, j, k: (i, k)),
              "tn": pl.BlockSpec((tk, tm), lambda i, j, k: (k, i))}[mode]
    b_spec = {"nn": pl.BlockSpec((tk, tn), lambda i, j, k: (k, j)),
              "nt": pl.BlockSpec((tn, tk), lambda i, j, k: (j, k)),
              "tn": pl.BlockSpec((tk, tn), lambda i, j, k: (k, j))}[mode]
    o_spec = pl.BlockSpec((tm, tn), lambda i, j, k: (i, j))
    has_res = res is not None

    def body(*refs):
        if has_res:
            a_ref, b_ref, r_ref, o_ref, acc_ref = refs
        else:
            a_ref, b_ref, o_ref, acc_ref = refs
        k = pl.program_id(2)

        @pl.when(k == 0)
        def _():
            acc_ref[...] = jnp.zeros_like(acc_ref)

        x, y = a_ref[...], b_ref[...]
        if exact:
            x, y, prec = x.astype(F32), y.astype(F32), lax.Precision.HIGHEST
        else:
            x, y, prec = x.astype(BF16), y.astype(BF16), None
        if mode == "tn":
            x = x.T
        dims = (((1,), (1,)), ((), ())) if mode == "nt" else (((1,), (0,)), ((), ()))
        acc_ref[...] += lax.dot_general(x, y, dims, precision=prec, preferred_element_type=F32)

        @pl.when(k == nk - 1)
        def _():
            if has_res:
                o_ref[...] = acc_ref[...] + r_ref[...]
            else:
                o_ref[...] = acc_ref[...]

    in_specs = [a_spec, b_spec] + ([o_spec] if has_res else [])
    args = (a, b) + ((res,) if has_res else ())
    (out,), xouts = _pcall(body, name, (M // tm, N // tn, nk), in_specs, [o_spec],
                           [jax.ShapeDtypeStruct((M, N), F32)], [pltpu.VMEM((tm, tn), F32)],
                           ("parallel", "parallel", "arbitrary"), args, host or ())
    return out if host is None else (out, xouts)


def _rowwise(fn, name, n_rows, tr, row_ins, const_ins, row_outs, red_outs=()):
    n_row_in, n_const, n_row_out, n_red = len(row_ins), len(const_ins), len(row_outs), len(red_outs)
    in_specs = []
    for arr, w, idx in row_ins:
        width = arr.shape[1] if w is None else w
        in_specs.append(pl.BlockSpec((tr, width), functools.partial(lambda i, j: (i, j), j=idx)))
    for c in const_ins:
        in_specs.append(pl.BlockSpec(c.shape, lambda i: (0, 0)))
    out_shape = [jax.ShapeDtypeStruct((n_rows, c), dt) for c, dt in row_outs]
    out_shape += [jax.ShapeDtypeStruct((1, c), F32) for c in red_outs]
    out_specs = [pl.BlockSpec((tr, c), lambda i: (i, 0)) for c, _ in row_outs]
    out_specs += [pl.BlockSpec((1, c), lambda i: (0, 0)) for c in red_outs]

    def body(*refs):
        i = pl.program_id(0)
        rows = [r[...] for r in refs[:n_row_in]]
        consts = [r[...] for r in refs[n_row_in:n_row_in + n_const]]
        outs = refs[n_row_in + n_const:]
        row_vals, red_vals = fn(i, rows, consts)
        for k in range(n_row_out):
            outs[k][...] = row_vals[k].astype(outs[k].dtype)
        if n_red:
            @pl.when(i == 0)
            def _():
                for k in range(n_red):
                    outs[n_row_out + k][...] = jnp.zeros_like(outs[n_row_out + k])

            for k in range(n_red):
                outs[n_row_out + k][...] += red_vals[k]

    res = pl.pallas_call(
        body, name=name, grid=(n_rows // tr,), in_specs=in_specs, out_specs=out_specs, out_shape=out_shape,
        compiler_params=_params(("arbitrary",)))(*[a for a, _, _ in row_ins], *const_ins)
    return res


def _row_ids(i, tr):
    return i * tr + lax.broadcasted_iota(jnp.int32, (tr, 1), 0)


def _colsum(x):
    return jnp.sum(x, axis=0, keepdims=True)


def _scan(a, b, name, b2=None, reverse=False):
    n_rows, c = b.shape
    tr = _tile(n_rows, 384)
    nblk = n_rows // tr
    has_a = a is not None
    has_b2 = b2 is not None

    def body(*refs):
        o_ref, carry = refs[-2:]
        ins = list(refs[:-2])
        a_ref = ins.pop(0) if has_a else None
        b_ref = ins.pop(0)

        @pl.when(pl.program_id(0) == 0)
        def _():
            carry[...] = jnp.zeros_like(carry)

        rows = lax.broadcasted_iota(jnp.int32, (tr, c), 0)
        bv = b_ref[...] + ins[0][...] if has_b2 else b_ref[...]
        av = a_ref[...] if has_a else None
        s = 1
        while s < tr:
            keep = rows < tr - s if reverse else rows >= s
            shift = tr - s if reverse else s
            b_sh = jnp.where(keep, pltpu.roll(bv, shift, 0), 0.0)
            if has_a:
                a_sh = jnp.where(keep, pltpu.roll(av, shift, 0), 1.0)
                bv = av * b_sh + bv
                av = av * a_sh
            else:
                bv = b_sh + bv
            s *= 2
        h = av * carry[0:1, :] + bv if has_a else carry[0:1, :] + bv
        o_ref[...] = h
        edge = 0 if reverse else tr - 1
        carry[...] = jnp.broadcast_to(h[edge:edge + 1, :], carry.shape)

    spec = pl.BlockSpec((tr, c), (lambda i: (nblk - 1 - i, 0)) if reverse else (lambda i: (i, 0)))
    args = ((a,) if has_a else ()) + (b,) + ((b2,) if has_b2 else ())
    return pl.pallas_call(
        body, name=name, grid=(n_rows // tr,), in_specs=[spec] * len(args), out_specs=spec,
        out_shape=jax.ShapeDtypeStruct((n_rows, c), F32), scratch_shapes=[pltpu.VMEM((8, c), F32)],
        compiler_params=_params(("arbitrary",)))(*args)


def _shift_down(x, s, fill=0.0):
    pad = jnp.full((s,) + x.shape[1:], fill, x.dtype)
    return jnp.concatenate([pad, x[:-s]], axis=0)


def _shift_up(x, s):
    pad = jnp.zeros((s,) + x.shape[1:], x.dtype)
    return jnp.concatenate([x[s:], pad], axis=0)


def _rev_scan(a, g, name, g2=None):
    return _scan(None if a is None else _shift_up(a, 1), g, name, b2=g2, reverse=True)


def _swa_masks(blk, n_pad):
    qi = lax.broadcasted_iota(jnp.int32, (BLOCK, 2 * BLOCK), 0)
    ki = lax.broadcasted_iota(jnp.int32, (BLOCK, 2 * BLOCK), 1)
    dist = qi + BLOCK - ki
    key_abs = (blk - 1) * BLOCK + ki
    return (dist >= 0) & (dist < BLOCK) & (key_abs >= n_pad)


def _swa_fwd(q, k, v, bias, sinks, n_pad):
    H, Tp, dh = q.shape
    tq = _tile(Tp, 384)
    nsub = tq // BLOCK
    nq = Tp // tq

    def body(q_ref, kp_ref, kc_ref, vp_ref, vc_ref, b_ref, s_ref, o_ref, lse_ref):
        i = pl.program_id(1)
        kstrip = jnp.concatenate([kp_ref[0], kc_ref[0]], axis=0).astype(BF16)
        vstrip = jnp.concatenate([vp_ref[0], vc_ref[0]], axis=0).astype(BF16)
        sink = s_ref[0, :, 0:1]
        bias_h = b_ref[0]
        for j in range(nsub):
            qj = q_ref[0, j * BLOCK:(j + 1) * BLOCK, :].astype(BF16)
            kk = kstrip[j * BLOCK:(j + 2) * BLOCK]
            vv = vstrip[j * BLOCK:(j + 2) * BLOCK]
            s = lax.dot_general(qj, kk, (((1,), (1,)), ((), ())), preferred_element_type=F32) * SCALE
            mask = _swa_masks(i * nsub + j, n_pad)
            s = jnp.where(mask, s + bias_h, NEG_INF)
            m = jnp.maximum(jnp.max(s, axis=1, keepdims=True), sink)
            p = jnp.exp(s - m)
            denom = jnp.sum(p, axis=1, keepdims=True) + jnp.exp(sink - m)
            p = (p / denom).astype(BF16)
            o_ref[0, j * BLOCK:(j + 1) * BLOCK, :] = jnp.dot(p, vv, preferred_element_type=F32)
            lse_ref[0, j * BLOCK:(j + 1) * BLOCK, :] = m + jnp.log(denom)

    prev = lambda h, i: (h // SWA_GROUP, jnp.maximum(i * nsub - 1, 0), 0)
    cur = lambda h, i: (h // SWA_GROUP, i, 0)
    return pl.pallas_call(
        body, name="swa_fwd", grid=(H, nq),
        in_specs=[pl.BlockSpec((1, tq, dh), lambda h, i: (h, i, 0)),
                  pl.BlockSpec((1, BLOCK, dh), prev), pl.BlockSpec((1, tq, dh), cur),
                  pl.BlockSpec((1, BLOCK, dh), prev), pl.BlockSpec((1, tq, dh), cur),
                  pl.BlockSpec((1, BLOCK, 2 * BLOCK), lambda h, i: (h, 0, 0)),
                  pl.BlockSpec((1, 1, LANE), lambda h, i: (h, 0, 0))],
        out_specs=[pl.BlockSpec((1, tq, dh), lambda h, i: (h, i, 0)),
                   pl.BlockSpec((1, tq, 1), lambda h, i: (h, i, 0))],
        out_shape=[jax.ShapeDtypeStruct((H, Tp, dh), F32), jax.ShapeDtypeStruct((H, Tp, 1), F32)],
        compiler_params=_params(("parallel", "arbitrary")))(q, k, k, v, v, bias, sinks)


def _swa_bwd(q, k, v, bias, sinks, o, lse, do, n_pad, host=()):
    H, Tp, dh = q.shape
    tq = _tile(Tp, 384)
    nsub = tq // BLOCK
    nq = Tp // tq
    strip = BLOCK + tq

    def body(q_ref, kp_ref, kc_ref, vp_ref, vc_ref, b_ref, s_ref, o_ref, lse_ref, do_ref,
             dq_ref, dk_ref, dv_ref, db_ref, ds_ref):
        i = pl.program_id(1)

        @pl.when(i == 0)
        def _():
            db_ref[...] = jnp.zeros_like(db_ref)
            ds_ref[...] = jnp.zeros_like(ds_ref)

        dk_ref[...] = jnp.zeros_like(dk_ref)
        dv_ref[...] = jnp.zeros_like(dv_ref)
        kstrip = jnp.concatenate([kp_ref[0], kc_ref[0]], axis=0).astype(BF16)
        vstrip = jnp.concatenate([vp_ref[0], vc_ref[0]], axis=0).astype(BF16)
        sink = s_ref[0, :, 0:1]
        bias_h = b_ref[0]
        for j in range(nsub):
            rows = slice(j * BLOCK, (j + 1) * BLOCK)
            keys = slice(j * BLOCK, (j + 2) * BLOCK)
            qj = q_ref[0, rows, :].astype(BF16)
            doj = do_ref[0, rows, :]
            lsej = lse_ref[0, rows, :]
            delta = jnp.sum(doj * o_ref[0, rows, :], axis=1, keepdims=True)
            doj = doj.astype(BF16)
            kk, vv = kstrip[keys], vstrip[keys]
            s = lax.dot_general(qj, kk, (((1,), (1,)), ((), ())), preferred_element_type=F32) * SCALE
            mask = _swa_masks(i * nsub + j, n_pad)
            s = jnp.where(mask, s + bias_h, NEG_INF)
            p = jnp.exp(s - lsej)
            p_sink = jnp.exp(sink - lsej)
            dp = lax.dot_general(doj, vv, (((1,), (1,)), ((), ())), preferred_element_type=F32)
            ds = p * (dp - delta)
            db_ref[0] += ds
            ds_ref[0] += jnp.broadcast_to(-jnp.sum(p_sink * delta, axis=0, keepdims=True), (1, LANE))
            dsb = ds.astype(BF16)
            dq_ref[0, rows, :] = jnp.dot(dsb, kk, preferred_element_type=F32) * SCALE
            dk_ref[0, 0, keys, :] += jnp.dot(ds.T.astype(BF16), qj, preferred_element_type=F32) * SCALE
            dv_ref[0, 0, keys, :] += jnp.dot(p.T.astype(BF16), doj, preferred_element_type=F32)

    prev = lambda h, i: (h // SWA_GROUP, jnp.maximum(i * nsub - 1, 0), 0)
    cur = lambda h, i: (h // SWA_GROUP, i, 0)
    qspec = pl.BlockSpec((1, tq, dh), lambda h, i: (h, i, 0))
    cspec = pl.BlockSpec((1, tq, 1), lambda h, i: (h, i, 0))
    sspec = pl.BlockSpec((1, 1, strip, dh), lambda h, i: (h, i, 0, 0))
    return _pcall(
        body, "swa_bwd", (H, nq),
        [qspec, pl.BlockSpec((1, BLOCK, dh), prev), pl.BlockSpec((1, tq, dh), cur),
         pl.BlockSpec((1, BLOCK, dh), prev), pl.BlockSpec((1, tq, dh), cur),
         pl.BlockSpec((1, BLOCK, 2 * BLOCK), lambda h, i: (h, 0, 0)),
         pl.BlockSpec((1, 1, LANE), lambda h, i: (h, 0, 0)), qspec, cspec, qspec],
        [qspec, sspec, sspec, pl.BlockSpec((1, BLOCK, 2 * BLOCK), lambda h, i: (h, 0, 0)),
         pl.BlockSpec((1, 1, LANE), lambda h, i: (h, 0, 0))],
        [jax.ShapeDtypeStruct((H, Tp, dh), F32), jax.ShapeDtypeStruct((H, nq, strip, dh), F32),
         jax.ShapeDtypeStruct((H, nq, strip, dh), F32),
         jax.ShapeDtypeStruct((H, BLOCK, 2 * BLOCK), F32), jax.ShapeDtypeStruct((H, 1, LANE), F32)],
        [], ("parallel", "arbitrary"), (q, k, k, v, v, bias, sinks, o, lse, do), host)


def _fox_scores(qb, kk, cq, ck, diag, tq):
    s = lax.dot_general(qb, kk, (((1,), (1,)), ((), ())), preferred_element_type=F32)
    s = s + cq - ck
    if diag:
        r = lax.broadcasted_iota(jnp.int32, (tq, tq), 0)
        c = lax.broadcasted_iota(jnp.int32, (tq, tq), 1)
        s = jnp.where(c <= r, s, NEG_INF)
    return s


def _fox_fwd(q, k, v, ccol, crow, tq, host=()):
    H, Tp, dh = q.shape
    nq = Tp // tq
    k4, v4 = k.reshape(H, nq, tq, dh), v.reshape(H, nq, tq, dh)

    def body(q_ref, k_ref, v_ref, cc_ref, cr_ref, o_ref, lse_ref):
        qi = pl.program_id(1)
        qb = (q_ref[0] * SCALE).astype(BF16)
        cq = cc_ref[0]

        def step(kb, carry, diag):
            m, l, acc = carry
            s = _fox_scores(qb, k_ref[0, kb].astype(BF16), cq, cr_ref[0, kb], diag, tq)
            m_new = jnp.maximum(m, jnp.max(s, axis=1, keepdims=True))
            alpha = jnp.exp(m - m_new)
            p = jnp.exp(s - m_new)
            l = alpha * l + jnp.sum(p, axis=1, keepdims=True)
            acc = alpha * acc + jnp.dot(p.astype(BF16), v_ref[0, kb].astype(BF16), preferred_element_type=F32)
            return m_new, l, acc

        init = (jnp.full((tq, 1), NEG_INF, F32), jnp.zeros((tq, 1), F32), jnp.zeros((tq, dh), F32))
        carry = lax.fori_loop(0, qi, functools.partial(step, diag=False), init)
        m, l, acc = step(qi, carry, True)
        o_ref[0] = acc / l
        lse_ref[0] = m + jnp.log(l)

    qspec = pl.BlockSpec((1, tq, dh), lambda h, i: (h, i, 0))
    cspec = pl.BlockSpec((1, tq, 1), lambda h, i: (h, i, 0))
    full4 = pl.BlockSpec((1, nq, tq, dh), lambda h, i: (h, 0, 0, 0))
    return _pcall(
        body, "fox_fwd", (H, nq),
        [qspec, full4, full4, cspec, pl.BlockSpec((1, nq, 1, tq), lambda h, i: (h, 0, 0, 0))], [qspec, cspec],
        [jax.ShapeDtypeStruct((H, Tp, dh), F32), jax.ShapeDtypeStruct((H, Tp, 1), F32)],
        [], ("parallel", "arbitrary"), (q, k4, v4, ccol, crow), host)


def _fox_bwd_dq(q, k, v, ccol, crow, o, lse, do, tq, host=()):
    H, Tp, dh = q.shape
    nq = Tp // tq
    k4, v4 = k.reshape(H, nq, tq, dh), v.reshape(H, nq, tq, dh)

    def body(q_ref, k_ref, v_ref, cc_ref, cr_ref, o_ref, lse_ref, do_ref, dq_ref, dl_ref, dc_ref):
        qi = pl.program_id(1)
        qb = (q_ref[0] * SCALE).astype(BF16)
        cq = cc_ref[0]
        dof = do_ref[0]
        delta = jnp.sum(dof * o_ref[0], axis=1, keepdims=True)
        dob = dof.astype(BF16)
        lse = lse_ref[0]

        def step(kb, carry, diag):
            dq, dc = carry
            kk = k_ref[0, kb].astype(BF16)
            s = _fox_scores(qb, kk, cq, cr_ref[0, kb], diag, tq)
            p = jnp.exp(s - lse)
            dp = lax.dot_general(dob, v_ref[0, kb].astype(BF16), (((1,), (1,)), ((), ())),
                                 preferred_element_type=F32)
            ds = p * (dp - delta)
            dq = dq + jnp.dot(ds.astype(BF16), kk, preferred_element_type=F32)
            return dq, dc + jnp.sum(ds, axis=1, keepdims=True)

        init = (jnp.zeros((tq, dh), F32), jnp.zeros((tq, 1), F32))
        dq, dc = step(qi, lax.fori_loop(0, qi, functools.partial(step, diag=False), init), True)
        dq_ref[0] = dq * SCALE
        dl_ref[0] = delta
        dc_ref[0] = dc

    qspec = pl.BlockSpec((1, tq, dh), lambda h, i: (h, i, 0))
    cspec = pl.BlockSpec((1, tq, 1), lambda h, i: (h, i, 0))
    full4 = pl.BlockSpec((1, nq, tq, dh), lambda h, i: (h, 0, 0, 0))
    return _pcall(
        body, "fox_bwd_dq", (H, nq),
        [qspec, full4, full4, cspec, pl.BlockSpec((1, nq, 1, tq), lambda h, i: (h, 0, 0, 0)), qspec, cspec, qspec],
        [qspec, cspec, cspec],
        [jax.ShapeDtypeStruct((H, Tp, dh), F32), jax.ShapeDtypeStruct((H, Tp, 1), F32),
         jax.ShapeDtypeStruct((H, Tp, 1), F32)],
        [], ("parallel", "arbitrary"), (q, k4, v4, ccol, crow, o, lse, do), host)


def _fox_bwd_dkv(q, k, v, ccol, crow, lse, delta, do, tq, host=()):
    H, Tp, dh = q.shape
    nq = Tp // tq
    q4, do4 = q.reshape(H, nq, tq, dh), do.reshape(H, nq, tq, dh)
    cc4, lse4, dl4 = (x.reshape(H, nq, tq, 1) for x in (ccol, lse, delta))

    def body(q_ref, do_ref, cc_ref, lse_ref, dl_ref, k_ref, v_ref, cr_ref, dk_ref, dv_ref, dc_ref):
        ki = pl.program_id(1)
        kk = k_ref[0].astype(BF16)
        vv = v_ref[0].astype(BF16)
        ck = cr_ref[0, 0]

        def step(qb_i, carry, diag):
            dk, dv, dc = carry
            qb = (q_ref[0, qb_i] * SCALE).astype(BF16)
            dob = do_ref[0, qb_i].astype(BF16)
            s = _fox_scores(qb, kk, cc_ref[0, qb_i], ck, diag, tq)
            p = jnp.exp(s - lse_ref[0, qb_i])
            dp = lax.dot_general(dob, vv, (((1,), (1,)), ((), ())), preferred_element_type=F32)
            ds = p * (dp - dl_ref[0, qb_i])
            dv = dv + jnp.dot(p.T.astype(BF16), dob, preferred_element_type=F32)
            dk = dk + jnp.dot(ds.T.astype(BF16), qb, preferred_element_type=F32)
            dc = dc - jnp.sum(ds, axis=0, keepdims=True)
            return dk, dv, dc

        init = (jnp.zeros((tq, dh), F32), jnp.zeros((tq, dh), F32), jnp.zeros((1, tq), F32))
        dk, dv, dc = lax.fori_loop(ki + 1, nq, functools.partial(step, diag=False), step(ki, init, True))
        dk_ref[0] = dk
        dv_ref[0] = dv
        dc_ref[0, 0] = dc

    kspec = pl.BlockSpec((1, tq, dh), lambda h, i: (h, i, 0))
    full4 = pl.BlockSpec((1, nq, tq, dh), lambda h, i: (h, 0, 0, 0))
    full1 = pl.BlockSpec((1, nq, tq, 1), lambda h, i: (h, 0, 0, 0))
    rspec = pl.BlockSpec((1, 1, 1, tq), lambda h, i: (h, i, 0, 0))
    return _pcall(
        body, "fox_bwd_dkv", (H, nq), [full4, full4, full1, full1, full1, kspec, kspec, rspec],
        [kspec, kspec, rspec],
        [jax.ShapeDtypeStruct((H, Tp, dh), F32), jax.ShapeDtypeStruct((H, Tp, dh), F32),
         jax.ShapeDtypeStruct((H, nq, 1, tq), F32)],
        [], ("parallel", "arbitrary"), (q4, do4, cc4, lse4, dl4, k, v, crow), host)


def _adam(gparts, w, m, v, name):
    P, R, C = gparts.shape
    tr = _tile(R, 256, 8)

    def body(g_ref, w_ref, m_ref, v_ref, go_ref, d_ref, mo_ref, vo_ref):
        g = g_ref[0]
        for p in range(1, P):
            g = g + g_ref[p]
        m2 = ADAM_B1 * m_ref[...] + (1.0 - ADAM_B1) * g
        v2 = ADAM_B2 * v_ref[...] + (1.0 - ADAM_B2) * (g * g)
        m_hat = m2 / (1.0 - ADAM_B1 ** ADAM_STEP)
        v_hat = v2 / (1.0 - ADAM_B2 ** ADAM_STEP)
        go_ref[...] = g
        d_ref[...] = -ADAM_LR * (m_hat / (jnp.sqrt(v_hat) + ADAM_EPS) + ADAM_WD * w_ref[...])
        mo_ref[...] = m2
        vo_ref[...] = v2

    spec = pl.BlockSpec((tr, C), lambda i: (i, 0))
    shp = jax.ShapeDtypeStruct((R, C), F32)
    return pl.pallas_call(
        body, name=name, grid=(R // tr,),
        in_specs=[pl.BlockSpec((P, tr, C), lambda i: (0, i, 0)), spec, spec, spec],
        out_specs=[spec] * 4, out_shape=[shp] * 4, compiler_params=_params(("parallel",)))(gparts, w, m, v)


def _sum_parts(gparts, name):
    P, R, C = gparts.shape
    tr = _tile(R, 256, 8)

    def body(g_ref, o_ref):
        g = g_ref[0]
        for p in range(1, P):
            g = g + g_ref[p]
        o_ref[...] = g

    return pl.pallas_call(
        body, name=name, grid=(R // tr,), in_specs=[pl.BlockSpec((P, tr, C), lambda i: (0, i, 0))],
        out_specs=pl.BlockSpec((tr, C), lambda i: (i, 0)), out_shape=jax.ShapeDtypeStruct((R, C), F32),
        compiler_params=_params(("parallel",)))(gparts)


def _pack(pieces, width, row_mult):
    flat = jnp.concatenate([p.reshape(-1) for p in pieces])
    n = flat.shape[0]
    rows = -(-n // width)
    rows = -(-rows // row_mult) * row_mult
    return jnp.pad(flat, (0, rows * width - n)).reshape(rows, width)


def _unpack(flat2d, shapes, lead=None):
    out, off = [], 0
    if lead is None:
        flat = flat2d.reshape(-1)
        for s in shapes:
            n = int(np.prod(s))
            out.append(flat[off:off + n].reshape(s))
            off += n
    else:
        flat = flat2d.reshape(lead, -1)
        for s in shapes:
            n = int(np.prod(s))
            out.append(flat[:, off:off + n].reshape((lead,) + tuple(s)))
            off += n
    return out


def _col_pieces(off, width, shard_w, fetch):
    out, c, end = [], off, off + width
    while c < end:
        d = c // shard_w
        hi = min(end, (d + 1) * shard_w)
        out.append(fetch(d, c - d * shard_w, hi - d * shard_w))
        c = hi
    return out


def _work_cols_of_orig(lo, hi, fetch):
    out = []
    for name, ow in _ORIG:
        o = _ORIG_OFF[name][0]
        a, b = max(lo, o), min(hi, o + ow)
        if a < b:
            w0 = _WORK_OFF[name][0]
            out.append(fetch(w0 + a - o, w0 + b - o))
    return out


def _heads(x, n):
    return x.reshape(x.shape[0], n, HEAD_DIM).transpose(1, 0, 2)


def _unheads(x):
    return x.transpose(1, 0, 2).reshape(x.shape[1], x.shape[0] * HEAD_DIM)


def _t5_bucket_np(dist):
    max_exact = REL_BUCKETS // 2
    d = np.maximum(dist, 0)
    scaled = (np.log(np.maximum(d, 1).astype(np.float32) / np.float32(max_exact))
              / np.float32(math.log(REL_MAX_DIST / max_exact))).astype(np.float32)
    large = np.minimum(max_exact + (scaled * np.float32(REL_BUCKETS - max_exact)).astype(np.int32), REL_BUCKETS - 1)
    return np.where(d < max_exact, d, large)


def _bucket_onehot():
    q_idx = np.arange(BLOCK)[:, None]
    k_idx = np.arange(2 * BLOCK)[None, :]
    bucket = _t5_bucket_np(q_idx + BLOCK - k_idx).reshape(-1)
    oh = np.zeros((LANE, BLOCK * 2 * BLOCK), np.float32)
    oh[bucket, np.arange(bucket.shape[0])] = 1.0
    return oh


def kernel(x, meta_tokens, rel_bias_table, norm_mix, w_in, swa_sinks, fox_forget_bias, conv_w, conv_b, lru_w_r, lru_b_r, lru_w_i, lru_b_i, lru_lambda, w_branch, w_out, norm_ffn, w_ffn_in, w_ffn_out, norm_final, loss_target, m_meta_tokens, m_rel_bias_table, m_norm_mix, m_w_in, m_swa_sinks, m_fox_forget_bias, m_conv_w, m_conv_b, m_lru_w_r, m_lru_b_r, m_lru_w_i, m_lru_b_i, m_lru_lambda, m_w_branch, m_w_out, m_norm_ffn, m_w_ffn_in, m_w_ffn_out, m_norm_final, v_meta_tokens, v_rel_bias_table, v_norm_mix, v_w_in, v_swa_sinks, v_fox_forget_bias, v_conv_w, v_conv_b, v_lru_w_r, v_lru_b_r, v_lru_w_i, v_lru_b_i, v_lru_lambda, v_w_branch, v_w_out, v_norm_ffn, v_w_ffn_in, v_w_ffn_out, v_norm_final):
    S = x.shape[1]
    T = N_META + S
    n_pad = (-T) % BLOCK
    Tp = T + n_pad
    first = n_pad + N_META
    TR = _tile(Tp, 384)
    TQ = _tile(Tp, 384)
    NQ = Tp // TQ
    me = 4 * lax.axis_index("x") + 2 * lax.axis_index("y") + lax.axis_index("c")

    big_names = ("w_in", "w_ffn_in", "w_ffn_out", "w_branch", "w_out")
    big_w = dict(w_in=w_in, w_ffn_in=w_ffn_in, w_ffn_out=w_ffn_out, w_branch=w_branch, w_out=w_out)
    big_m = dict(w_in=m_w_in, w_ffn_in=m_w_ffn_in, w_ffn_out=m_w_ffn_out, w_branch=m_w_branch, w_out=m_w_out)
    big_v = dict(w_in=v_w_in, w_ffn_in=v_w_ffn_in, w_ffn_out=v_w_ffn_out, w_branch=v_w_branch, w_out=v_w_out)
    rows2d = lambda w: w.reshape(-1, w.shape[-1])
    in_shard_w = IN_COLS // N_DEV
    cat_axis = dict(w_in=-1, w_ffn_in=-1, w_ffn_out=0, w_branch=-1, w_out=0)

    def w_send(n, l):
        return rows2d(big_w[n][l]).astype(BF16)

    def w_build(n, g):
        shard = lambda d: g[d].reshape(big_w[n].shape[1:])
        if n != "w_in":
            return jnp.concatenate([shard(d) for d in range(N_DEV)], axis=cat_axis[n])
        parts = []
        for name, width in _WORK:
            off, ow = _ORIG_OFF[name]
            parts += _col_pieces(off, ow, in_shard_w, lambda d, lo, hi: shard(d)[:, lo:hi])
            if width > ow:
                parts.append(jnp.zeros((D_MODEL, width - ow), BF16))
        return jnp.concatenate(parts, axis=-1)

    W_in, W_ffn_in, W_ffn_out, W_branch, W_out = ([None] * DEPTH for _ in range(5))
    W_in[0] = w_build("w_in", _exchange(w_send("w_in", 0), True, "gather_w_in"))

    small_sh = _exchange(_pack([meta_tokens, conv_w], D_MODEL, 8), True, "gather_small")
    g_meta, g_conv_w = _unpack(small_sh, [meta_tokens.shape, conv_w.shape], lead=N_DEV)
    meta_full = g_meta.transpose(1, 0, 2).reshape(N_META, D_MODEL)
    conv_w_full = g_conv_w.transpose(1, 2, 0, 3).reshape(DEPTH, CONV_WIDTH, LRU_WIDTH)

    onehot = jnp.asarray(_bucket_onehot())
    table_t = jnp.pad(rel_bias_table.T, ((0, 0), (0, LANE - REL_BUCKETS)))
    bias = _mm(table_t, onehot, "nn", "swa_bias", exact=True, tn_cap=4096).reshape(SWA_Q_HEADS, BLOCK, 2 * BLOCK)

    def dense_blocks(w):
        rows = []
        for b in range(LRU_BLOCKS):
            rows.append(jnp.pad(w[b], ((0, 0), (b * LRU_BLOCK_DIM, (LRU_BLOCKS - 1 - b) * LRU_BLOCK_DIM))))
        return jnp.concatenate(rows, axis=0)

    h = jnp.concatenate([jnp.zeros((n_pad, D_MODEL), F32), meta_full, x[0]], axis=0)
    saved = []

    for l in range(DEPTH):
        sv = {"h": h}
        g_mix = norm_mix[l][None, :]

        def norm_fwd(i, rows, consts):
            xx = rows[0]
            return [xx * lax.rsqrt(jnp.mean(xx * xx, axis=1, keepdims=True) + EPS) * consts[0]], []

        (u,) = _rowwise(norm_fwd, "norm_fwd", Tp, TR, [(h, None, 0)], [g_mix], [(D_MODEL, F32)])
        proj, got = _mm(u, W_in[l], "nn", "proj", tn_cap=384,
                        host=[(w_send("w_branch", l), True), (w_send("w_out", l), True)])
        W_branch[l], W_out[l] = w_build("w_branch", got[0]), w_build("w_out", got[1])
        sv["u"], sv["proj"] = u, proj

        def col(name):
            off, w = _WORK_OFF[name]
            return (proj, w, off // w)

        def cols(name):
            off, w = _WORK_OFF[name]
            return proj[:, off:off + w]

        qa, ka, va = _heads(cols("qa"), 8), _heads(cols("ka"), 2), _heads(cols("va"), 2)
        sinks_l = jnp.broadcast_to(swa_sinks[l][:, None, None], (SWA_Q_HEADS, 1, LANE))
        o_a_h, lse_a = _swa_fwd(qa, ka, va, bias, sinks_l, n_pad)
        o_a = _unheads(o_a_h)
        sv.update(qa=qa, ka=ka, va=va, sinks=sinks_l, o_a_h=o_a_h, lse_a=lse_a, o_a=o_a)

        fbias = jnp.pad(fox_forget_bias[l], (0, LANE - FOX_HEADS))[None, :]

        def logf_fwd(i, rows, consts):
            return [_log_sigmoid(rows[0] + consts[0])], []

        (logf,) = _rowwise(logf_fwd, "logf_fwd", Tp, TR, [col("fl")], [fbias], [(LANE, F32)])
        cum = _scan(None, logf, "cumsum")
        cum_h = cum[:, :FOX_HEADS].T
        ccol = cum_h[:, :, None]
        crow = jnp.where(jnp.arange(Tp)[None, :] < n_pad, -NEG_INF, cum_h).reshape(FOX_HEADS, NQ, 1, TQ)
        qf, kf, vf = _heads(cols("qf"), 8), _heads(cols("kf"), 8), _heads(cols("vf"), 8)
        ride = [("w_ffn_in", l), ("w_ffn_out", l)] + ([("w_in", l + 1)] if l + 1 < DEPTH else [])
        (o_f_h, lse_f), got = _fox_fwd(qf, kf, vf, ccol, crow, TQ, host=[(w_send(n, ll), True) for n, ll in ride])
        W_ffn_in[l], W_ffn_out[l] = w_build("w_ffn_in", got[0]), w_build("w_ffn_out", got[1])
        if l + 1 < DEPTH:
            W_in[l + 1] = w_build("w_in", got[2])
        o_f = _unheads(o_f_h)
        sv.update(fbias=fbias, qf=qf, kf=kf, vf=vf, ccol=ccol, crow=crow, o_f_h=o_f_h, lse_f=lse_f, o_f=o_f)

        xc = cols("xc")
        xs = [_shift_down(xc, CONV_WIDTH - 1 - i) for i in range(CONV_WIDTH - 1)] + [xc]
        cw = conv_w_full[l]
        cb = conv_b[l][None, :]

        def conv_fwd(i, rows, consts):
            w, b = consts
            acc = rows[0] * w[0:1, :]
            for t in range(1, CONV_WIDTH):
                acc = acc + rows[t] * w[t:t + 1, :]
            return [acc + b], []

        (xconv,) = _rowwise(conv_fwd, "conv_fwd", Tp, TR, [(a, None, 0) for a in xs], [cw, cb], [(LRU_WIDTH, F32)])
        w_ri = jnp.concatenate([dense_blocks(lru_w_r[l]), dense_blocks(lru_w_i[l])], axis=1)
        pre = _mm(xconv, w_ri, "nn", "lru_gates")
        lru_consts = [lru_b_r[l][None, :], lru_b_i[l][None, :], lru_lambda[l][None, :]]

        def lru_fwd(i, rows, consts):
            pr, xv = rows
            b_r, b_i, lam = consts
            r = _sigmoid(pr[:, :LRU_WIDTH] + b_r)
            gi = _sigmoid(pr[:, LRU_WIDTH:] + b_i)
            log_a = LRU_C * r * _log_sigmoid(lam)
            valid = (_row_ids(i, TR) >= n_pad).astype(F32)
            inp = jnp.sqrt(_neg_expm1(2.0 * log_a)) * (gi * xv) * valid
            return [jnp.exp(log_a), inp], []

        a_dec, inp = _rowwise(lru_fwd, "lru_fwd", Tp, TR, [(pre, None, 0), (xconv, None, 0)], lru_consts,
                              [(LRU_WIDTH, F32), (LRU_WIDTH, F32)])
        hs = _scan(a_dec, inp, "lru_scan")

        def oc_fwd(i, rows, consts):
            return [rows[0] * _gelu(rows[1])], []

        (o_c,) = _rowwise(oc_fwd, "oc_fwd", Tp, TR, [(hs, None, 0), col("yc")], [], [(LRU_WIDTH, F32)])
        sv.update(xs=xs, cw=cw, xconv=xconv, w_ri=w_ri, pre=pre, lru_consts=lru_consts, a_dec=a_dec, hs=hs, o_c=o_c)

        ba = _mm(o_a, W_branch[l][0], "nn", "branch")
        bf = _mm(o_f, W_branch[l][1], "nn", "branch")
        bc = _mm(o_c, W_branch[l][2], "nn", "branch")

        def merge_fwd(i, rows, consts):
            g, b0, b1, b2 = rows
            valid = (_row_ids(i, BLOCK) >= n_pad).astype(F32)
            mg = (_sigmoid(g[:, :D_MODEL]) * b0 + _sigmoid(g[:, D_MODEL:2 * D_MODEL]) * b1
                  + _sigmoid(g[:, 2 * D_MODEL:]) * b2)
            return [mg * valid], []

        (merged,) = _rowwise(merge_fwd, "merge_fwd", Tp, BLOCK,
                             [col("gates"), (ba, None, 0), (bf, None, 0), (bc, None, 0)], [], [(D_MODEL, F32)])
        h1 = _mm(merged, W_out[l], "nn", "out_proj", res=h)
        sv.update(ba=ba, bf=bf, bc=bc, merged=merged, h1=h1)

        g_ffn = norm_ffn[l][None, :]
        (u2,) = _rowwise(norm_fwd, "norm_fwd", Tp, TR, [(h1, None, 0)], [g_ffn], [(D_MODEL, F32)])
        ff = _mm(u2, W_ffn_in[l], "nn", "ffn_in")

        def act_fwd(i, rows, consts):
            gate, up = rows
            return [gate * _sigmoid(gate) * up], []

        (act,) = _rowwise(act_fwd, "act_fwd", Tp, TR, [(ff, D_FF, 0), (ff, D_FF, 1)], [], [(D_FF, F32)])
        h = _mm(act, W_ffn_out[l], "nn", "ffn_out", res=h1)
        sv.update(u2=u2, ff=ff, act=act)
        saved.append(sv)

    tgt = jnp.concatenate([jnp.zeros((first, D_MODEL), F32), loss_target[0]], axis=0)
    g_fin = norm_final[None, :]

    def head(i, rows, consts):
        xx, tg = rows
        g = consts[0]
        valid = (_row_ids(i, TR) >= first).astype(F32)
        rstd = lax.rsqrt(jnp.mean(xx * xx, axis=1, keepdims=True) + EPS)
        xhat = xx * rstd
        err = (xhat * g - tg) * valid
        loss_rows = 0.5 * jnp.mean(err * err, axis=1, keepdims=True)
        dy = err * (1.0 / D_MODEL)
        dxhat = dy * g
        dx = rstd * (dxhat - xhat * jnp.mean(dxhat * xhat, axis=1, keepdims=True))
        return [dx], [jnp.broadcast_to(_colsum(loss_rows), (1, LANE)), _colsum(dy * xhat)]

    dh, loss_part, d_norm_final = _rowwise(head, "loss_head", Tp, TR, [(h, None, 0), (tgt, None, 0)], [g_fin],
                                           [(D_MODEL, F32)], [LANE, D_MODEL])
    loss = lax.psum(loss_part[0, 0], ("x", "y", "c"))

    def norm_bwd_call(xin, g, du, dres):
        def norm_bwd(i, rows, consts):
            xx, dd, rr = rows
            gg = consts[0]
            valid = (_row_ids(i, TR) >= n_pad).astype(F32)
            rstd = lax.rsqrt(jnp.mean(xx * xx, axis=1, keepdims=True) + EPS)
            xhat = xx * rstd
            dxhat = dd * gg
            dx = rstd * (dxhat - xhat * jnp.mean(dxhat * xhat, axis=1, keepdims=True))
            return [rr + dx * valid], [_colsum(dd * xhat)]

        return _rowwise(norm_bwd, "norm_bwd", Tp, TR, [(xin, None, 0), (du, None, 0), (dres, None, 0)], [g],
                        [(D_MODEL, F32)], [D_MODEL])

    grads = {k: [None] * DEPTH for k in ("norm_mix", "w_in", "swa_sinks", "fox_forget_bias", "conv_w", "conv_b",
                                         "lru_w_r", "lru_b_r", "lru_w_i", "lru_b_i", "lru_lambda", "w_branch",
                                         "w_out", "norm_ffn", "w_ffn_in", "w_ffn_out")}
    d_bias_total = None

    def g_send(n, g):
        if n == "w_in":
            pieces = [jnp.concatenate(_work_cols_of_orig(d * in_shard_w, (d + 1) * in_shard_w,
                                                         lambda lo, hi: g[:, lo:hi]), axis=-1) for d in range(N_DEV)]
        else:
            ax = cat_axis[n] % g.ndim
            w = big_w[n].shape[1 + ax]
            pieces = [lax.slice_in_dim(g, d * w, (d + 1) * w, axis=ax) for d in range(N_DEV)]
        return jnp.stack([rows2d(p) for p in pieces])

    recv = {n: [None] * DEPTH for n in big_names}
    for l in reversed(range(DEPTH)):
        sv = saved[l]
        proj = sv["proj"]

        def col(name):
            off, w = _WORK_OFF[name]
            return (proj, w, off // w)

        dh2 = dh
        d_act = _mm(dh2, W_ffn_out[l], "nt", "d_act")
        grads["w_ffn_out"][l] = _mm(sv["act"], dh2, "tn", "dw_ffn_out")

        def act_bwd(i, rows, consts):
            gate, up, da = rows
            sg = _sigmoid(gate)
            d_gate = da * up * (sg * (1.0 + gate * (1.0 - sg)))
            d_up = da * (gate * sg)
            return [jnp.concatenate([d_gate, d_up], axis=1)], []

        (dff,) = _rowwise(act_bwd, "act_bwd", Tp, BLOCK, [(sv["ff"], D_FF, 0), (sv["ff"], D_FF, 1), (d_act, None, 0)],
                          [], [(2 * D_FF, F32)])
        grads["w_ffn_in"][l] = _mm(sv["u2"], dff, "tn", "dw_ffn_in")
        du2 = _mm(dff, W_ffn_in[l], "nt", "du2")
        dh1, dg = norm_bwd_call(sv["h1"], norm_ffn[l][None, :], du2, dh2)
        grads["norm_ffn"][l] = dg[0]

        dmerged = _mm(dh1, W_out[l], "nt", "d_merged")
        grads["w_out"][l] = _mm(sv["merged"], dh1, "tn", "dw_out")

        def merge_bwd(i, rows, consts):
            g, b0, b1, b2, dm = rows
            dm = dm * (_row_ids(i, BLOCK) >= n_pad).astype(F32)
            outs, dgs = [], []
            for k, bk in enumerate((b0, b1, b2)):
                sg = _sigmoid(g[:, k * D_MODEL:(k + 1) * D_MODEL])
                outs.append(dm * sg)
                dgs.append(dm * bk * sg * (1.0 - sg))
            return outs + [jnp.concatenate(dgs, axis=1)], []

        d_ba, d_bf, d_bc, d_gates = _rowwise(
            merge_bwd, "merge_bwd", Tp, BLOCK,
            [col("gates"), (sv["ba"], None, 0), (sv["bf"], None, 0), (sv["bc"], None, 0), (dmerged, None, 0)], [],
            [(D_MODEL, F32)] * 3 + [(3 * D_MODEL, F32)])
        grads["w_branch"][l] = jnp.stack([_mm(sv["o_a"], d_ba, "tn", "dw_branch"),
                                          _mm(sv["o_f"], d_bf, "tn", "dw_branch"),
                                          _mm(sv["o_c"], d_bc, "tn", "dw_branch")])
        do_a = _mm(d_ba, W_branch[l][0], "nt", "d_branch")
        do_f = _mm(d_bf, W_branch[l][1], "nt", "d_branch")
        do_c = _mm(d_bc, W_branch[l][2], "nt", "d_branch")

        def oc_bwd(i, rows, consts):
            d, hv, yv = rows
            return [d * _gelu(yv), d * hv * _gelu_grad(yv)], []

        d_hs, d_yc = _rowwise(oc_bwd, "oc_bwd", Tp, TR, [(do_c, None, 0), (sv["hs"], None, 0), col("yc")], [],
                              [(LRU_WIDTH, F32)] * 2)
        d_state = _rev_scan(sv["a_dec"], d_hs, "lru_scan")
        hs_prev = _shift_down(sv["hs"], 1)

        def lru_bwd(i, rows, consts):
            dH, hp, pr, xv = rows
            b_r, b_i, lam = consts
            valid = (_row_ids(i, TR) >= n_pad).astype(F32)
            r = _sigmoid(pr[:, :LRU_WIDTH] + b_r)
            gi = _sigmoid(pr[:, LRU_WIDTH:] + b_i)
            lsl = _log_sigmoid(lam)
            log_a = LRU_C * r * lsl
            a = jnp.exp(log_a)
            one_m_e = _neg_expm1(2.0 * log_a)
            mult = jnp.sqrt(one_m_e)
            d_inp = dH * valid
            d_mult = d_inp * gi * xv
            d_gi = d_inp * mult * xv
            d_x = d_inp * mult * gi
            d_log_a = dH * hp * a - d_mult * (1.0 - one_m_e) / mult
            d_pre_r = d_log_a * (LRU_C * lsl) * r * (1.0 - r)
            d_pre_i = d_gi * gi * (1.0 - gi)
            d_lam = _colsum(d_log_a * (LRU_C * r)) * _sigmoid(-lam)
            return [jnp.concatenate([d_pre_r, d_pre_i], axis=1), d_x], [_colsum(d_pre_r), _colsum(d_pre_i), d_lam]

        d_pre, d_xdir, d_b_r, d_b_i, d_lam = _rowwise(
            lru_bwd, "lru_bwd", Tp, TR, [(d_state, None, 0), (hs_prev, None, 0), (sv["pre"], None, 0),
                                         (sv["xconv"], None, 0)], sv["lru_consts"],
            [(2 * LRU_WIDTH, F32), (LRU_WIDTH, F32)], [LRU_WIDTH] * 3)
        grads["lru_b_r"][l], grads["lru_b_i"][l], grads["lru_lambda"][l] = d_b_r[0], d_b_i[0], d_lam[0]
        d_w_ri = _mm(sv["xconv"], d_pre, "tn", "dw_lru_gates")

        def diag_blocks(wd):
            w4 = wd.reshape(LRU_BLOCKS, LRU_BLOCK_DIM, LRU_BLOCKS, LRU_BLOCK_DIM)
            return jnp.stack([w4[b, :, b, :] for b in range(LRU_BLOCKS)])

        grads["lru_w_r"][l] = diag_blocks(d_w_ri[:, :LRU_WIDTH])
        grads["lru_w_i"][l] = diag_blocks(d_w_ri[:, LRU_WIDTH:])
        d_conv = _mm(d_pre, sv["w_ri"], "nt", "d_xconv", res=d_xdir)
        dcs = [d_conv] + [_shift_up(d_conv, s) for s in range(1, CONV_WIDTH)]

        def conv_bwd(i, rows, consts):
            w = consts[0]
            d = rows[:CONV_WIDTH]
            xsh = rows[CONV_WIDTH:]
            dxc = d[0] * w[CONV_WIDTH - 1:CONV_WIDTH, :]
            for s in range(1, CONV_WIDTH):
                dxc = dxc + d[s] * w[CONV_WIDTH - 1 - s:CONV_WIDTH - s, :]
            return [dxc], [_colsum(xsh[t] * d[0]) for t in range(CONV_WIDTH)] + [_colsum(d[0])]

        res = _rowwise(conv_bwd, "conv_bwd", Tp, TR, [(a, None, 0) for a in dcs + sv["xs"]], [sv["cw"]],
                       [(LRU_WIDTH, F32)], [LRU_WIDTH] * (CONV_WIDTH + 1))
        d_xc = res[0]
        grads["conv_w"][l] = jnp.concatenate(res[1:1 + CONV_WIDTH], axis=0)
        grads["conv_b"][l] = res[1 + CONV_WIDTH][0]

        do_f_h = _heads(do_f, 8)
        (dqf, delta_f, dccol), got = _fox_bwd_dq(
            sv["qf"], sv["kf"], sv["vf"], sv["ccol"], sv["crow"], sv["o_f_h"], sv["lse_f"], do_f_h, TQ,
            host=[(g_send("w_ffn_in", grads["w_ffn_in"][l]), False)])
        recv["w_ffn_in"][l] = got[0]
        ride = [("w_out", l), ("w_branch", l)] + ([("w_in", l + 1)] if l + 1 < DEPTH else [])
        (dkf, dvf, dcrow), got = _fox_bwd_dkv(
            sv["qf"], sv["kf"], sv["vf"], sv["ccol"], sv["crow"], sv["lse_f"], delta_f, do_f_h, TQ,
            host=[(g_send(n, grads[n][ll]), False) for n, ll in ride])
        for (n, ll), r in zip(ride, got):
            recv[n][ll] = r
        lanes = lambda z: jnp.pad(z.reshape(FOX_HEADS, Tp).T, ((0, 0), (0, LANE - FOX_HEADS)))
        dlogf = _rev_scan(None, lanes(dcrow), "cumsum_bwd", g2=lanes(dccol))

        def logf_bwd(i, rows, consts):
            dl, fl = rows
            valid = (_row_ids(i, TR) >= n_pad).astype(F32)
            lane_ok = (lax.broadcasted_iota(jnp.int32, (1, LANE), 1) < FOX_HEADS).astype(F32)
            dfl = dl * _sigmoid(-(fl + consts[0])) * valid * lane_ok
            return [dfl], [_colsum(dfl)]

        d_fl, d_fb = _rowwise(logf_bwd, "logf_bwd", Tp, TR, [(dlogf, None, 0), col("fl")], [sv["fbias"]],
                              [(LANE, F32)], [LANE])
        grads["fox_forget_bias"][l] = d_fb[0, :FOX_HEADS]

        (dqa, dk_strip, dv_strip, d_bias_l, d_sink), got = _swa_bwd(
            sv["qa"], sv["ka"], sv["va"], bias, sv["sinks"], sv["o_a_h"], sv["lse_a"], _heads(do_a, 8), n_pad,
            host=[(g_send("w_ffn_out", grads["w_ffn_out"][l]), False)])
        recv["w_ffn_out"][l] = got[0]
        grads["swa_sinks"][l] = d_sink[:, 0, 0]
        d_bias_total = d_bias_l if d_bias_total is None else d_bias_total + d_bias_l

        def kv_parts(strip):
            cur = strip[:, :, BLOCK:, :].reshape(SWA_Q_HEADS, Tp, HEAD_DIM)
            nxt = jnp.pad(strip[:, 1:, :BLOCK, :], ((0, 0), (0, 1), (TQ - BLOCK, 0), (0, 0)))
            nxt = nxt.reshape(SWA_Q_HEADS, Tp, HEAD_DIM)
            both = jnp.stack([cur, nxt]).reshape(2, SWA_KV_HEADS, SWA_GROUP, Tp, HEAD_DIM)
            return both.transpose(3, 0, 2, 1, 4).reshape(Tp, 2 * SWA_GROUP * KV_W)

        dk_parts, dv_parts = kv_parts(dk_strip), kv_parts(dv_strip)

        def assemble(i, rows, consts):
            dg, dqa_, dqf_, dkf_, dvf_, dxc_, dyc_, dkp, dvp, dfl_ = rows

            def fold(parts):
                acc = parts[:, :KV_W]
                for p in range(1, 2 * SWA_GROUP):
                    acc = acc + parts[:, p * KV_W:(p + 1) * KV_W]
                return acc

            return [jnp.concatenate([dg, dqa_, dqf_, dkf_, dvf_, dxc_, dyc_, fold(dkp), fold(dvp), dfl_], axis=1)], []

        (dproj,) = _rowwise(
            assemble, "assemble_dproj", Tp, BLOCK,
            [(d_gates, None, 0), (_unheads(dqa), None, 0), (_unheads(dqf), None, 0), (_unheads(dkf), None, 0),
             (_unheads(dvf), None, 0), (d_xc, None, 0), (d_yc, None, 0), (dk_parts, None, 0), (dv_parts, None, 0),
             (d_fl, None, 0)], [], [(WORK_COLS, F32)])
        grads["w_in"][l] = _mm(sv["u"], dproj, "tn", "dw_in", tn_cap=384)
        du = _mm(dproj, W_in[l], "nt", "du", tk_cap=384)
        dh, dg = norm_bwd_call(sv["h"], norm_mix[l][None, :], du, dh1)
        grads["norm_mix"][l] = dg[0]

    d_table = _mm(d_bias_total.reshape(SWA_Q_HEADS, BLOCK * 2 * BLOCK), onehot, "nt", "d_rel_table", exact=True,
                  tk_cap=4096)
    g_rel = d_table[:, :REL_BUCKETS].T
    g_meta_full = dh[n_pad:first]
    grad_x = dh[first:][None]

    stack = lambda k: jnp.stack(grads[k])
    recv["w_in"][0] = _exchange(g_send("w_in", grads["w_in"][0]), False, "scatter_w_in")
    big_res = {}
    for n in big_names:
        per_layer = [_adam(recv[n][l], rows2d(big_w[n][l]), rows2d(big_m[n][l]), rows2d(big_v[n][l]), "adam_" + n)
                     for l in range(DEPTH)]
        big_res[n] = [jnp.stack([per_layer[l][k] for l in range(DEPTH)]).reshape(big_w[n].shape) for k in range(4)]

    rep_names = ("rel_bias_table", "norm_mix", "swa_sinks", "fox_forget_bias", "conv_b", "lru_w_r", "lru_b_r",
                 "lru_w_i", "lru_b_i", "lru_lambda", "norm_ffn", "norm_final")
    rep_w = dict(rel_bias_table=rel_bias_table, norm_mix=norm_mix, swa_sinks=swa_sinks,
                 fox_forget_bias=fox_forget_bias, conv_b=conv_b, lru_w_r=lru_w_r, lru_b_r=lru_b_r, lru_w_i=lru_w_i,
                 lru_b_i=lru_b_i, lru_lambda=lru_lambda, norm_ffn=norm_ffn, norm_final=norm_final)
    rep_m = dict(rel_bias_table=m_rel_bias_table, norm_mix=m_norm_mix, swa_sinks=m_swa_sinks,
                 fox_forget_bias=m_fox_forget_bias, conv_b=m_conv_b, lru_w_r=m_lru_w_r, lru_b_r=m_lru_b_r,
                 lru_w_i=m_lru_w_i, lru_b_i=m_lru_b_i, lru_lambda=m_lru_lambda, norm_ffn=m_norm_ffn,
                 norm_final=m_norm_final)
    rep_v = dict(rel_bias_table=v_rel_bias_table, norm_mix=v_norm_mix, swa_sinks=v_swa_sinks,
                 fox_forget_bias=v_fox_forget_bias, conv_b=v_conv_b, lru_w_r=v_lru_w_r, lru_b_r=v_lru_b_r,
                 lru_w_i=v_lru_w_i, lru_b_i=v_lru_b_i, lru_lambda=v_lru_lambda, norm_ffn=v_norm_ffn,
                 norm_final=v_norm_final)
    rep_g = {n: (g_rel if n == "rel_bias_table" else d_norm_final[0] if n == "norm_final" else stack(n))
             for n in rep_names}
    small_g = [rep_g[n] for n in rep_names] + [g_meta_full, stack("conv_w")]
    small_shapes = [rep_w[n].shape for n in rep_names] + [(N_META, D_MODEL), (DEPTH, CONV_WIDTH, LRU_WIDTH)]
    gs_all = _exchange(_pack(small_g, D_MODEL, 8), True, "gather_small_grads")
    gs_sum = _unpack(_sum_parts(gs_all, "sum_small_grads"), small_shapes)
    gsum = dict(zip(rep_names, gs_sum[:len(rep_names)]))
    g_meta_sh = lax.dynamic_slice_in_dim(gs_sum[-2], me * (D_MODEL // N_DEV), D_MODEL // N_DEV, axis=1)
    g_convw_sh = lax.dynamic_slice_in_dim(gs_sum[-1], me * (LRU_WIDTH // N_DEV), LRU_WIDTH // N_DEV, axis=2)
    sm_names = rep_names + ("meta_tokens", "conv_w")
    sm_w = [rep_w[n] for n in rep_names] + [meta_tokens, conv_w]
    sm_m = [rep_m[n] for n in rep_names] + [m_meta_tokens, m_conv_w]
    sm_v = [rep_v[n] for n in rep_names] + [v_meta_tokens, v_conv_w]
    sm_g = [gsum[n] for n in rep_names] + [g_meta_sh, g_convw_sh]
    sm_shapes = [w.shape for w in sm_w]
    sm_out = [_unpack(o, sm_shapes) for o in _adam(_pack(sm_g, D_MODEL, 8)[None], _pack(sm_w, D_MODEL, 8),
                                                   _pack(sm_m, D_MODEL, 8), _pack(sm_v, D_MODEL, 8), "adam_small")]
    sm_res = {n: [sm_out[k][j] for k in range(4)] for j, n in enumerate(sm_names)}

    order = ("meta_tokens", "rel_bias_table", "norm_mix", "w_in", "swa_sinks", "fox_forget_bias", "conv_w", "conv_b",
             "lru_w_r", "lru_b_r", "lru_w_i", "lru_b_i", "lru_lambda", "w_branch", "w_out", "norm_ffn", "w_ffn_in",
             "w_ffn_out", "norm_final")
    allres = {**big_res, **sm_res}
    outs = [loss, grad_x]
    for k in range(4):
        outs += [allres[n][k] for n in order]
    return tuple(outs)
```

```python
import functools
import math

import numpy as np
import jax
import jax.numpy as jnp
from jax import lax
from jax.experimental import pallas as pl
from jax.experimental.pallas import tpu as pltpu

F32 = jnp.float32
BF16 = jnp.bfloat16

N_DEV = 8
D_MODEL = 1024
DEPTH = 4
HEAD_DIM = 64
N_META = 16
BLOCK = 128
NEG_INF = -1e30
SWA_Q_HEADS = 8
SWA_KV_HEADS = 2
SWA_GROUP = SWA_Q_HEADS // SWA_KV_HEADS
FOX_HEADS = 8
LRU_WIDTH = D_MODEL // 2
LRU_BLOCKS = 8
LRU_BLOCK_DIM = LRU_WIDTH // LRU_BLOCKS
CONV_WIDTH = 4
LRU_C = 8.0
REL_BUCKETS = 32
REL_MAX_DIST = 128
D_FF = 2816
N_BRANCH = 3
ATT_W = SWA_Q_HEADS * HEAD_DIM
KV_W = SWA_KV_HEADS * HEAD_DIM
SCALE = HEAD_DIM ** -0.5
EPS = 1e-6

_ORIG = (("qa", ATT_W), ("ka", KV_W), ("va", KV_W), ("qf", ATT_W), ("kf", ATT_W), ("vf", ATT_W),
         ("fl", FOX_HEADS), ("xc", LRU_WIDTH), ("yc", LRU_WIDTH), ("gates", N_BRANCH * D_MODEL))
IN_COLS = sum(w for _, w in _ORIG)
_WORK = (("gates", 3072), ("qa", 512), ("qf", 512), ("kf", 512), ("vf", 512), ("xc", 512), ("yc", 512),
         ("ka", 128), ("va", 128), ("fl", 128))
WORK_COLS = sum(w for _, w in _WORK)


def _offsets(table):
    off, out = 0, {}
    for n, w in table:
        out[n] = (off, w)
        off += w
    return out


_ORIG_OFF = _offsets(_ORIG)
_WORK_OFF = _offsets(_WORK)

ADAM_LR = 0.001
ADAM_B1 = 0.9
ADAM_B2 = 0.999
ADAM_EPS = 1e-08
ADAM_WD = 0.01
ADAM_STEP = 10

VMEM_LIMIT = 52 * 1024 * 1024
LANE = 128


def _tile(n, cap, mult=LANE):
    if n <= cap:
        return n
    best = None
    for d in range(mult, cap + 1, mult):
        if n % d == 0:
            best = d
    assert best is not None, (n, cap, mult)
    return best


def _params(sem):
    return pltpu.CompilerParams(dimension_semantics=sem, vmem_limit_bytes=VMEM_LIMIT)


def _sigmoid(x):
    return 1.0 / (1.0 + jnp.exp(-x))


def _log_sigmoid(x):
    return jnp.minimum(x, 0.0) - jnp.log(1.0 + jnp.exp(-jnp.abs(x)))


def _neg_expm1(x):
    series = -x * (1.0 + x * (0.5 + x * (1.0 / 6.0 + x * (1.0 / 24.0 + x * (1.0 / 120.0)))))
    return jnp.where(x > -0.1, series, 1.0 - jnp.exp(x))


_GELU_C = math.sqrt(2.0 / math.pi)


def _gelu(x):
    return 0.5 * x * (1.0 + jnp.tanh(_GELU_C * (x + 0.044715 * x * x * x)))


def _gelu_grad(x):
    t = jnp.tanh(_GELU_C * (x + 0.044715 * x * x * x))
    return 0.5 * (1.0 + t) + 0.5 * x * (1.0 - t * t) * _GELU_C * (1.0 + 3.0 * 0.044715 * x * x)


def _xch_ops(src_ref, out_ref, send_sems, recv_sems, local_sem, gather):
    x, y, c = lax.axis_index("x"), lax.axis_index("y"), lax.axis_index("c")
    me = 4 * x + 2 * y + c

    def copy(r, mine):
        px, py, pc = x ^ ((r >> 2) & 1), y ^ ((r >> 1) & 1), c ^ (r & 1)
        pid = 4 * px + 2 * py + pc
        return pltpu.make_async_remote_copy(
            src_ref=src_ref if gather else src_ref.at[pid], dst_ref=out_ref.at[me if mine else pid],
            send_sem=send_sems.at[r - 1], recv_sem=recv_sems.at[r - 1],
            device_id=(px, py, pc), device_id_type=pl.DeviceIdType.MESH)

    def local():
        return pltpu.make_async_copy(src_ref if gather else src_ref.at[me], out_ref.at[me], local_sem)

    def start():
        local().start()
        for r in range(1, N_DEV):
            copy(r, True).start()

    def wait():
        for r in range(1, N_DEV):
            copy(r, False).wait_recv()
        for r in range(1, N_DEV):
            copy(r, True).wait_send()
        local().wait()

    return start, wait


_XCH_SEMS = [pltpu.SemaphoreType.DMA((N_DEV - 1,)), pltpu.SemaphoreType.DMA((N_DEV - 1,)), pltpu.SemaphoreType.DMA]
_ANY = pl.BlockSpec(memory_space=pl.ANY)


def _xch_shape(src, gather):
    return jax.ShapeDtypeStruct((N_DEV,) + tuple(src.shape if gather else src.shape[1:]), src.dtype)


def _exchange(src, gather, name):
    def body(src_ref, out_ref, send_sems, recv_sems, local_sem):
        start, wait = _xch_ops(src_ref, out_ref, send_sems, recv_sems, local_sem, gather)
        start()
        wait()

    return pl.pallas_call(body, name=name, in_specs=[_ANY], out_specs=_ANY, out_shape=_xch_shape(src, gather),
                          scratch_shapes=list(_XCH_SEMS))(src)


def _pcall(body, name, grid, in_specs, out_specs, out_shape, scratch_shapes, sem, args, host=()):
    if not host:
        outs = pl.pallas_call(body, name=name, grid=grid, in_specs=in_specs, out_specs=out_specs, out_shape=out_shape,
                              scratch_shapes=scratch_shapes, compiler_params=_params(sem))(*args)
        return outs, []
    n_in, n_out, n_scr, n_x = len(args), len(out_shape), len(scratch_shapes), len(host)

    def wrapped(*refs):
        ins, xin = refs[:n_in], refs[n_in:n_in + n_x]
        outs, xout = refs[n_in + n_x:n_in + n_x + n_out], refs[n_in + n_x + n_out:n_in + 2 * n_x + n_out]
        scr = refs[n_in + 2 * n_x + n_out:n_in + 2 * n_x + n_out + n_scr]
        sems = refs[n_in + 2 * n_x + n_out + n_scr:]
        first = functools.reduce(jnp.logical_and, [pl.program_id(k) == 0 for k in range(len(grid))])
        last = functools.reduce(jnp.logical_and, [pl.program_id(k) == grid[k] - 1 for k in range(len(grid))])
        ops = [_xch_ops(xin[k], xout[k], *sems[3 * k:3 * k + 3], host[k][1]) for k in range(n_x)]

        @pl.when(first)
        def _():
            for start, _ in ops:
                start()

        body(*ins, *outs, *scr)

        @pl.when(last)
        def _():
            for _, wait in ops:
                wait()

    res = pl.pallas_call(
        wrapped, name=name, grid=grid, in_specs=list(in_specs) + [_ANY] * n_x,
        out_specs=list(out_specs) + [_ANY] * n_x,
        out_shape=list(out_shape) + [_xch_shape(s, g) for s, g in host],
        scratch_shapes=list(scratch_shapes) + list(_XCH_SEMS) * n_x,
        compiler_params=_params(("arbitrary",) * len(grid)))(*args, *[s for s, _ in host])
    return res[:n_out], res[n_out:]


def _mm(a, b, mode, name, res=None, exact=False, tm_cap=1408, tn_cap=512, tk_cap=1408, host=None):
    if mode == "nn":
        (M, K), (K2, N) = a.shape, b.shape
    elif mode == "nt":
        (M, K), (N, K2) = a.shape, b.shape
    else:
        (K, M), (K2, N) = a.shape, b.shape
    assert K == K2, (a.shape, b.shape, mode)
    tm, tn, tk = _tile(M, tm_cap), _tile(N, tn_cap), _tile(K, tk_cap)
    nk = K // tk
    a_spec = {"nn": pl.BlockSpec((tm, tk), lambda i, j, k: (i, k)),
              "nt": pl.BlockSpec((tm, tk), lambda i, j, k: (i, k)),
              "tn": pl.BlockSpec((tk, tm), lambda i, j, k: (k, i))}[mode]
    b_spec = {"nn": pl.BlockSpec((tk, tn), lambda i, j, k: (k, j)),
              "nt": pl.BlockSpec((tn, tk), lambda i, j, k: (j, k)),
              "tn": pl.BlockSpec((tk, tn), lambda i, j, k: (k, j))}[mode]
    o_spec = pl.BlockSpec((tm, tn), lambda i, j, k: (i, j))
    has_res = res is not None

    def body(*refs):
        if has_res:
            a_ref, b_ref, r_ref, o_ref, acc_ref = refs
        else:
            a_ref, b_ref, o_ref, acc_ref = refs
        k = pl.program_id(2)

        @pl.when(k == 0)
        def _():
            acc_ref[...] = jnp.zeros_like(acc_ref)

        x, y = a_ref[...], b_ref[...]
        if exact:
            x, y, prec = x.astype(F32), y.astype(F32), lax.Precision.HIGHEST
        else:
            x, y, prec = x.astype(BF16), y.astype(BF16), None
        if mode == "tn":
            x = x.T
        dims = (((1,), (1,)), ((), ())) if mode == "nt" else (((1,), (0,)), ((), ()))
        acc_ref[...] += lax.dot_general(x, y, dims, precision=prec, preferred_element_type=F32)

        @pl.when(k == nk - 1)
        def _():
            if has_res:
                o_ref[...] = acc_ref[...] + r_ref[...]
            else:
                o_ref[...] = acc_ref[...]

    in_specs = [a_spec, b_spec] + ([o_spec] if has_res else [])
    args = (a, b) + ((res,) if has_res else ())
    (out,), xouts = _pcall(body, name, (M // tm, N // tn, nk), in_specs, [o_spec],
                           [jax.ShapeDtypeStruct((M, N), F32)], [pltpu.VMEM((tm, tn), F32)],
                           ("parallel", "parallel", "arbitrary"), args, host or ())
    return out if host is None else (out, xouts)


def _rowwise(fn, name, n_rows, tr, row_ins, const_ins, row_outs, red_outs=()):
    n_row_in, n_const, n_row_out, n_red = len(row_ins), len(const_ins), len(row_outs), len(red_outs)
    in_specs = []
    for arr, w, idx in row_ins:
        width = arr.shape[1] if w is None else w
        in_specs.append(pl.BlockSpec((tr, width), functools.partial(lambda i, j: (i, j), j=idx)))
    for c in const_ins:
        in_specs.append(pl.BlockSpec(c.shape, lambda i: (0, 0)))
    out_shape = [jax.ShapeDtypeStruct((n_rows, c), dt) for c, dt in row_outs]
    out_shape += [jax.ShapeDtypeStruct((1, c), F32) for c in red_outs]
    out_specs = [pl.BlockSpec((tr, c), lambda i: (i, 0)) for c, _ in row_outs]
    out_specs += [pl.BlockSpec((1, c), lambda i: (0, 0)) for c in red_outs]

    def body(*refs):
        i = pl.program_id(0)
        rows = [r[...] for r in refs[:n_row_in]]
        consts = [r[...] for r in refs[n_row_in:n_row_in + n_const]]
        outs = refs[n_row_in + n_const:]
        row_vals, red_vals = fn(i, rows, consts)
        for k in range(n_row_out):
            outs[k][...] = row_vals[k].astype(outs[k].dtype)
        if n_red:
            @pl.when(i == 0)
            def _():
                for k in range(n_red):
                    outs[n_row_out + k][...] = jnp.zeros_like(outs[n_row_out + k])

            for k in range(n_red):
                outs[n_row_out + k][...] += red_vals[k]

    res = pl.pallas_call(
        body, name=name, grid=(n_rows // tr,), in_specs=in_specs, out_specs=out_specs, out_shape=out_shape,
        compiler_params=_params(("arbitrary",)))(*[a for a, _, _ in row_ins], *const_ins)
    return res


def _row_ids(i, tr):
    return i * tr + lax.broadcasted_iota(jnp.int32, (tr, 1), 0)


def _colsum(x):
    return jnp.sum(x, axis=0, keepdims=True)


def _scan(a, b, name, b2=None, reverse=False):
    n_rows, c = b.shape
    tr = _tile(n_rows, 384)
    nblk = n_rows // tr
    has_a = a is not None
    has_b2 = b2 is not None

    def body(*refs):
        o_ref, carry = refs[-2:]
        ins = list(refs[:-2])
        a_ref = ins.pop(0) if has_a else None
        b_ref = ins.pop(0)

        @pl.when(pl.program_id(0) == 0)
        def _():
            carry[...] = jnp.zeros_like(carry)

        rows = lax.broadcasted_iota(jnp.int32, (tr, c), 0)
        bv = b_ref[...] + ins[0][...] if has_b2 else b_ref[...]
        av = a_ref[...] if has_a else None
        s = 1
        while s < tr:
            keep = rows < tr - s if reverse else rows >= s
            shift = tr - s if reverse else s
            b_sh = jnp.where(keep, pltpu.roll(bv, shift, 0), 0.0)
            if has_a:
                a_sh = jnp.where(keep, pltpu.roll(av, shift, 0), 1.0)
                bv = av * b_sh + bv
                av = av * a_sh
            else:
                bv = b_sh + bv
            s *= 2
        h = av * carry[0:1, :] + bv if has_a else carry[0:1, :] + bv
        o_ref[...] = h
        edge = 0 if reverse else tr - 1
        carry[...] = jnp.broadcast_to(h[edge:edge + 1, :], carry.shape)

    spec = pl.BlockSpec((tr, c), (lambda i: (nblk - 1 - i, 0)) if reverse else (lambda i: (i, 0)))
    args = ((a,) if has_a else ()) + (b,) + ((b2,) if has_b2 else ())
    return pl.pallas_call(
        body, name=name, grid=(n_rows // tr,), in_specs=[spec] * len(args), out_specs=spec,
        out_shape=jax.ShapeDtypeStruct((n_rows, c), F32), scratch_shapes=[pltpu.VMEM((8, c), F32)],
        compiler_params=_params(("arbitrary",)))(*args)


def _shift_down(x, s, fill=0.0):
    pad = jnp.full((s,) + x.shape[1:], fill, x.dtype)
    return jnp.concatenate([pad, x[:-s]], axis=0)


def _shift_up(x, s):
    pad = jnp.zeros((s,) + x.shape[1:], x.dtype)
    return jnp.concatenate([x[s:], pad], axis=0)


def _rev_scan(a, g, name, g2=None):
    return _scan(None if a is None else _shift_up(a, 1), g, name, b2=g2, reverse=True)


def _swa_masks(blk, n_pad):
    qi = lax.broadcasted_iota(jnp.int32, (BLOCK, 2 * BLOCK), 0)
    ki = lax.broadcasted_iota(jnp.int32, (BLOCK, 2 * BLOCK), 1)
    dist = qi + BLOCK - ki
    key_abs = (blk - 1) * BLOCK + ki
    return (dist >= 0) & (dist < BLOCK) & (key_abs >= n_pad)


def _swa_fwd(q, k, v, bias, sinks, n_pad):
    H, Tp, dh = q.shape
    tq = _tile(Tp, 384)
    nsub = tq // BLOCK
    nq = Tp // tq

    def body(q_ref, kp_ref, kc_ref, vp_ref, vc_ref, b_ref, s_ref, o_ref, lse_ref):
        i = pl.program_id(1)
        kstrip = jnp.concatenate([kp_ref[0], kc_ref[0]], axis=0).astype(BF16)
        vstrip = jnp.concatenate([vp_ref[0], vc_ref[0]], axis=0).astype(BF16)
        sink = s_ref[0, :, 0:1]
        bias_h = b_ref[0]
        for j in range(nsub):
            qj = q_ref[0, j * BLOCK:(j + 1) * BLOCK, :].astype(BF16)
            kk = kstrip[j * BLOCK:(j + 2) * BLOCK]
            vv = vstrip[j * BLOCK:(j + 2) * BLOCK]
            s = lax.dot_general(qj, kk, (((1,), (1,)), ((), ())), preferred_element_type=F32) * SCALE
            mask = _swa_masks(i * nsub + j, n_pad)
            s = jnp.where(mask, s + bias_h, NEG_INF)
            m = jnp.maximum(jnp.max(s, axis=1, keepdims=True), sink)
            p = jnp.exp(s - m)
            denom = jnp.sum(p, axis=1, keepdims=True) + jnp.exp(sink - m)
            p = (p / denom).astype(BF16)
            o_ref[0, j * BLOCK:(j + 1) * BLOCK, :] = jnp.dot(p, vv, preferred_element_type=F32)
            lse_ref[0, j * BLOCK:(j + 1) * BLOCK, :] = m + jnp.log(denom)

    prev = lambda h, i: (h // SWA_GROUP, jnp.maximum(i * nsub - 1, 0), 0)
    cur = lambda h, i: (h // SWA_GROUP, i, 0)
    return pl.pallas_call(
        body, name="swa_fwd", grid=(H, nq),
        in_specs=[pl.BlockSpec((1, tq, dh), lambda h, i: (h, i, 0)),
                  pl.BlockSpec((1, BLOCK, dh), prev), pl.BlockSpec((1, tq, dh), cur),
                  pl.BlockSpec((1, BLOCK, dh), prev), pl.BlockSpec((1, tq, dh), cur),
                  pl.BlockSpec((1, BLOCK, 2 * BLOCK), lambda h, i: (h, 0, 0)),
                  pl.BlockSpec((1, 1, LANE), lambda h, i: (h, 0, 0))],
        out_specs=[pl.BlockSpec((1, tq, dh), lambda h, i: (h, i, 0)),
                   pl.BlockSpec((1, tq, 1), lambda h, i: (h, i, 0))],
        out_shape=[jax.ShapeDtypeStruct((H, Tp, dh), F32), jax.ShapeDtypeStruct((H, Tp, 1), F32)],
        compiler_params=_params(("parallel", "arbitrary")))(q, k, k, v, v, bias, sinks)


def _swa_bwd(q, k, v, bias, sinks, o, lse, do, n_pad, host=()):
    H, Tp, dh = q.shape
    tq = _tile(Tp, 384)
    nsub = tq // BLOCK
    nq = Tp // tq
    strip = BLOCK + tq

    def body(q_ref, kp_ref, kc_ref, vp_ref, vc_ref, b_ref, s_ref, o_ref, lse_ref, do_ref,
             dq_ref, dk_ref, dv_ref, db_ref, ds_ref):
        i = pl.program_id(1)

        @pl.when(i == 0)
        def _():
            db_ref[...] = jnp.zeros_like(db_ref)
            ds_ref[...] = jnp.zeros_like(ds_ref)

        dk_ref[...] = jnp.zeros_like(dk_ref)
        dv_ref[...] = jnp.zeros_like(dv_ref)
        kstrip = jnp.concatenate([kp_ref[0], kc_ref[0]], axis=0).astype(BF16)
        vstrip = jnp.concatenate([vp_ref[0], vc_ref[0]], axis=0).astype(BF16)
        sink = s_ref[0, :, 0:1]
        bias_h = b_ref[0]
        for j in range(nsub):
            rows = slice(j * BLOCK, (j + 1) * BLOCK)
            keys = slice(j * BLOCK, (j + 2) * BLOCK)
            qj = q_ref[0, rows, :].astype(BF16)
            doj = do_ref[0, rows, :]
            lsej = lse_ref[0, rows, :]
            delta = jnp.sum(doj * o_ref[0, rows, :], axis=1, keepdims=True)
            doj = doj.astype(BF16)
            kk, vv = kstrip[keys], vstrip[keys]
            s = lax.dot_general(qj, kk, (((1,), (1,)), ((), ())), preferred_element_type=F32) * SCALE
            mask = _swa_masks(i * nsub + j, n_pad)
            s = jnp.where(mask, s + bias_h, NEG_INF)
            p = jnp.exp(s - lsej)
            p_sink = jnp.exp(sink - lsej)
            dp = lax.dot_general(doj, vv, (((1,), (1,)), ((), ())), preferred_element_type=F32)
            ds = p * (dp - delta)
            db_ref[0] += ds
            ds_ref[0] += jnp.broadcast_to(-jnp.sum(p_sink * delta, axis=0, keepdims=True), (1, LANE))
            dsb = ds.astype(BF16)
            dq_ref[0, rows, :] = jnp.dot(dsb, kk, preferred_element_type=F32) * SCALE
            dk_ref[0, 0, keys, :] += jnp.dot(ds.T.astype(BF16), qj, preferred_element_type=F32) * SCALE
            dv_ref[0, 0, keys, :] += jnp.dot(p.T.astype(BF16), doj, preferred_element_type=F32)

    prev = lambda h, i: (h // SWA_GROUP, jnp.maximum(i * nsub - 1, 0), 0)
    cur = lambda h, i: (h // SWA_GROUP, i, 0)
    qspec = pl.BlockSpec((1, tq, dh), lambda h, i: (h, i, 0))
    cspec = pl.BlockSpec((1, tq, 1), lambda h, i: (h, i, 0))
    sspec = pl.BlockSpec((1, 1, strip, dh), lambda h, i: (h, i, 0, 0))
    return _pcall(
        body, "swa_bwd", (H, nq),
        [qspec, pl.BlockSpec((1, BLOCK, dh), prev), pl.BlockSpec((1, tq, dh), cur),
         pl.BlockSpec((1, BLOCK, dh), prev), pl.BlockSpec((1, tq, dh), cur),
         pl.BlockSpec((1, BLOCK, 2 * BLOCK), lambda h, i: (h, 0, 0)),
         pl.BlockSpec((1, 1, LANE), lambda h, i: (h, 0, 0)), qspec, cspec, qspec],
        [qspec, sspec, sspec, pl.BlockSpec((1, BLOCK, 2 * BLOCK), lambda h, i: (h, 0, 0)),
         pl.BlockSpec((1, 1, LANE), lambda h, i: (h, 0, 0))],
        [jax.ShapeDtypeStruct((H, Tp, dh), F32), jax.ShapeDtypeStruct((H, nq, strip, dh), F32),
         jax.ShapeDtypeStruct((H, nq, strip, dh), F32),
         jax.ShapeDtypeStruct((H, BLOCK, 2 * BLOCK), F32), jax.ShapeDtypeStruct((H, 1, LANE), F32)],
        [], ("parallel", "arbitrary"), (q, k, k, v, v, bias, sinks, o, lse, do), host)


def _fox_scores(qb, kk, cq, ck, diag, tq):
    s = lax.dot_general(qb, kk, (((1,), (1,)), ((), ())), preferred_element_type=F32)
    s = s + cq - ck
    if diag:
        r = lax.broadcasted_iota(jnp.int32, (tq, tq), 0)
        c = lax.broadcasted_iota(jnp.int32, (tq, tq), 1)
        s = jnp.where(c <= r, s, NEG_INF)
    return s


_FOX_PAIRS = FOX_HEADS // 2
_QF_COL, _KF_COL, _VF_COL = (_WORK_OFF[n][0] // LANE for n in ("qf", "kf", "vf"))


def _low_lanes():
    return lax.broadcasted_iota(jnp.int32, (1, LANE), 1) < HEAD_DIM


def _split_pair(x, low):
    return [jnp.where(low, x, 0.0).astype(BF16), jnp.where(low, 0.0, x).astype(BF16)]


def _fox_fwd(proj, ccol, crow, tq, host=()):
    Tp = proj.shape[0]
    nq = Tp // tq

    def body(q_ref, k_ref, v_ref, cc_ref, cr_ref, o_ref, lse_ref):
        qi = pl.program_id(1)
        low = _low_lanes()
        qh = _split_pair(q_ref[...] * SCALE, low)
        cq = [cc_ref[0], cc_ref[1]]

        def step(kb, carry, diag):
            rows = pl.ds(pl.multiple_of(kb * tq, tq), tq)
            kk, vv = k_ref[rows, :].astype(BF16), v_ref[rows, :].astype(BF16)
            stats, upd = [], []
            for e in range(2):
                m, l = carry[2 * e], carry[2 * e + 1]
                s = _fox_scores(qh[e], kk, cq[e], cr_ref[e, kb], diag, tq)
                m_new = jnp.maximum(m, jnp.max(s, axis=1, keepdims=True))
                alpha = jnp.exp(m - m_new)
                p = jnp.exp(s - m_new)
                stats += [m_new, alpha * l + jnp.sum(p, axis=1, keepdims=True)]
                upd.append(alpha * carry[4] + jnp.dot(p.astype(BF16), vv, preferred_element_type=F32))
            return (*stats, jnp.where(low, upd[0], upd[1]))

        col = lambda val: jnp.full((tq, 1), val, F32)
        init = (col(NEG_INF), col(0.0), col(NEG_INF), col(0.0), jnp.zeros((tq, LANE), F32))
        m0, l0, m1, l1, acc = step(qi, lax.fori_loop(0, qi, functools.partial(step, diag=False), init), True)
        o_ref[...] = jnp.where(low, acc / l0, acc / l1)
        lse_ref[0] = m0 + jnp.log(l0)
        lse_ref[1] = m1 + jnp.log(l1)

    cspec = pl.BlockSpec((2, tq, 1), lambda h, i: (h, i, 0))
    return _pcall(
        body, "fox_fwd", (_FOX_PAIRS, nq),
        [pl.BlockSpec((tq, LANE), lambda h, i: (i, _QF_COL + h)), pl.BlockSpec((Tp, LANE), lambda h, i: (0, _KF_COL + h)),
         pl.BlockSpec((Tp, LANE), lambda h, i: (0, _VF_COL + h)), cspec,
         pl.BlockSpec((2, nq, 1, tq), lambda h, i: (h, 0, 0, 0))],
        [pl.BlockSpec((tq, LANE), lambda h, i: (i, h)), cspec],
        [jax.ShapeDtypeStruct((Tp, FOX_HEADS * HEAD_DIM), F32), jax.ShapeDtypeStruct((FOX_HEADS, Tp, 1), F32)],
        [], ("parallel", "arbitrary"), (proj, proj, proj, ccol, crow), host)


def _fox_bwd_dq(proj, ccol, crow, o, lse, do, tq, host=()):
    Tp = proj.shape[0]
    nq = Tp // tq

    def body(q_ref, k_ref, v_ref, cc_ref, cr_ref, o_ref, lse_ref, do_ref, dq_ref, dl_ref, dc_ref):
        qi = pl.program_id(1)
        low = _low_lanes()
        qh = _split_pair(q_ref[...] * SCALE, low)
        dof = do_ref[...]
        prod = dof * o_ref[...]
        delta = [jnp.sum(jnp.where(low, prod, 0.0), axis=1, keepdims=True),
                 jnp.sum(jnp.where(low, 0.0, prod), axis=1, keepdims=True)]
        doh = _split_pair(dof, low)
        cq = [cc_ref[0], cc_ref[1]]
        lse = [lse_ref[0], lse_ref[1]]

        def step(kb, carry, diag):
            rows = pl.ds(pl.multiple_of(kb * tq, tq), tq)
            kk, vv = k_ref[rows, :].astype(BF16), v_ref[rows, :].astype(BF16)
            dqs, dcs = [], []
            for e in range(2):
                s = _fox_scores(qh[e], kk, cq[e], cr_ref[e, kb], diag, tq)
                p = jnp.exp(s - lse[e])
                dp = lax.dot_general(doh[e], vv, (((1,), (1,)), ((), ())), preferred_element_type=F32)
                ds = p * (dp - delta[e])
                dqs.append(jnp.dot(ds.astype(BF16), kk, preferred_element_type=F32))
                dcs.append(carry[1 + e] + jnp.sum(ds, axis=1, keepdims=True))
            return (carry[0] + jnp.where(low, dqs[0], dqs[1]), *dcs)

        init = (jnp.zeros((tq, LANE), F32), jnp.zeros((tq, 1), F32), jnp.zeros((tq, 1), F32))
        dq, dc0, dc1 = step(qi, lax.fori_loop(0, qi, functools.partial(step, diag=False), init), True)
        dq_ref[...] = dq * SCALE
        dl_ref[0], dl_ref[1] = delta
        dc_ref[0], dc_ref[1] = dc0, dc1

    cspec = pl.BlockSpec((2, tq, 1), lambda h, i: (h, i, 0))
    pspec = pl.BlockSpec((tq, LANE), lambda h, i: (i, h))
    stat = jax.ShapeDtypeStruct((FOX_HEADS, Tp, 1), F32)
    return _pcall(
        body, "fox_bwd_dq", (_FOX_PAIRS, nq),
        [pl.BlockSpec((tq, LANE), lambda h, i: (i, _QF_COL + h)), pl.BlockSpec((Tp, LANE), lambda h, i: (0, _KF_COL + h)),
         pl.BlockSpec((Tp, LANE), lambda h, i: (0, _VF_COL + h)), cspec,
         pl.BlockSpec((2, nq, 1, tq), lambda h, i: (h, 0, 0, 0)), pspec, cspec, pspec],
        [pspec, cspec, cspec], [jax.ShapeDtypeStruct((Tp, FOX_HEADS * HEAD_DIM), F32), stat, stat],
        [], ("parallel", "arbitrary"), (proj, proj, proj, ccol, crow, o, lse, do), host)


def _fox_bwd_dkv(proj, ccol, crow, lse, delta, do, tq, host=()):
    Tp = proj.shape[0]
    nq = Tp // tq
    pair = lambda z: z.reshape(_FOX_PAIRS, 2, Tp).transpose(0, 2, 1)
    stats = jnp.concatenate([pair(ccol), pair(lse), pair(delta)], axis=-1).reshape(_FOX_PAIRS, nq, tq, 6)

    def body(q_ref, do_ref, st_ref, k_ref, v_ref, cr_ref, dk_ref, dv_ref, dc_ref):
        ki = pl.program_id(1)
        low = _low_lanes()
        kk, vv = k_ref[...].astype(BF16), v_ref[...].astype(BF16)

        def step(qb_i, carry, diag):
            rows = pl.ds(pl.multiple_of(qb_i * tq, tq), tq)
            qh = _split_pair(q_ref[rows, :] * SCALE, low)
            doh = _split_pair(do_ref[rows, :], low)
            st = st_ref[0, qb_i]
            dk, dv = carry[0], carry[1]
            dcs = []
            for e in range(2):
                s = _fox_scores(qh[e], kk, st[:, e:e + 1], cr_ref[e, 0], diag, tq)
                p = jnp.exp(s - st[:, 2 + e:3 + e])
                dp = lax.dot_general(doh[e], vv, (((1,), (1,)), ((), ())), preferred_element_type=F32)
                ds = p * (dp - st[:, 4 + e:5 + e])
                dv = dv + jnp.dot(p.T.astype(BF16), doh[e], preferred_element_type=F32)
                dk = dk + jnp.dot(ds.T.astype(BF16), qh[e], preferred_element_type=F32)
                dcs.append(carry[2 + e] - jnp.sum(ds, axis=0, keepdims=True))
            return (dk, dv, *dcs)

        init = (jnp.zeros((tq, LANE), F32), jnp.zeros((tq, LANE), F32), jnp.zeros((1, tq), F32), jnp.zeros((1, tq), F32))
        dk, dv, dc0, dc1 = lax.fori_loop(ki + 1, nq, functools.partial(step, diag=False), step(ki, init, True))
        dk_ref[...] = dk
        dv_ref[...] = dv
        dc_ref[0, 0] = dc0
        dc_ref[1, 0] = dc1

    rspec = pl.BlockSpec((2, 1, 1, tq), lambda h, i: (h, i, 0, 0))
    pspec = pl.BlockSpec((tq, LANE), lambda h, i: (i, h))
    wide = jax.ShapeDtypeStruct((Tp, FOX_HEADS * HEAD_DIM), F32)
    return _pcall(
        body, "fox_bwd_dkv", (_FOX_PAIRS, nq),
        [pl.BlockSpec((Tp, LANE), lambda h, i: (0, _QF_COL + h)), pl.BlockSpec((Tp, LANE), lambda h, i: (0, h)),
         pl.BlockSpec((1, nq, tq, 6), lambda h, i: (h, 0, 0, 0)),
         pl.BlockSpec((tq, LANE), lambda h, i: (i, _KF_COL + h)), pl.BlockSpec((tq, LANE), lambda h, i: (i, _VF_COL + h)),
         rspec],
        [pspec, pspec, rspec], [wide, wide, jax.ShapeDtypeStruct((FOX_HEADS, nq, 1, tq), F32)],
        [], ("parallel", "arbitrary"), (proj, do, stats, proj, proj, crow), host)


def _adam(gparts, w, m, v, name):
    P, R, C = gparts.shape
    tr = _tile(R, 256, 8)

    def body(g_ref, w_ref, m_ref, v_ref, go_ref, d_ref, mo_ref, vo_ref):
        g = g_ref[0]
        for p in range(1, P):
            g = g + g_ref[p]
        m2 = ADAM_B1 * m_ref[...] + (1.0 - ADAM_B1) * g
        v2 = ADAM_B2 * v_ref[...] + (1.0 - ADAM_B2) * (g * g)
        m_hat = m2 / (1.0 - ADAM_B1 ** ADAM_STEP)
        v_hat = v2 / (1.0 - ADAM_B2 ** ADAM_STEP)
        go_ref[...] = g
        d_ref[...] = -ADAM_LR * (m_hat / (jnp.sqrt(v_hat) + ADAM_EPS) + ADAM_WD * w_ref[...])
        mo_ref[...] = m2
        vo_ref[...] = v2

    spec = pl.BlockSpec((tr, C), lambda i: (i, 0))
    shp = jax.ShapeDtypeStruct((R, C), F32)
    return pl.pallas_call(
        body, name=name, grid=(R // tr,),
        in_specs=[pl.BlockSpec((P, tr, C), lambda i: (0, i, 0)), spec, spec, spec],
        out_specs=[spec] * 4, out_shape=[shp] * 4, compiler_params=_params(("parallel",)))(gparts, w, m, v)


def _sum_parts(gparts, name):
    P, R, C = gparts.shape
    tr = _tile(R, 256, 8)

    def body(g_ref, o_ref):
        g = g_ref[0]
        for p in range(1, P):
            g = g + g_ref[p]
        o_ref[...] = g

    return pl.pallas_call(
        body, name=name, grid=(R // tr,), in_specs=[pl.BlockSpec((P, tr, C), lambda i: (0, i, 0))],
        out_specs=pl.BlockSpec((tr, C), lambda i: (i, 0)), out_shape=jax.ShapeDtypeStruct((R, C), F32),
        compiler_params=_params(("parallel",)))(gparts)


def _pack(pieces, width, row_mult):
    flat = jnp.concatenate([p.reshape(-1) for p in pieces])
    n = flat.shape[0]
    rows = -(-n // width)
    rows = -(-rows // row_mult) * row_mult
    return jnp.pad(flat, (0, rows * width - n)).reshape(rows, width)


def _unpack(flat2d, shapes, lead=None):
    out, off = [], 0
    if lead is None:
        flat = flat2d.reshape(-1)
        for s in shapes:
            n = int(np.prod(s))
            out.append(flat[off:off + n].reshape(s))
            off += n
    else:
        flat = flat2d.reshape(lead, -1)
        for s in shapes:
            n = int(np.prod(s))
            out.append(flat[:, off:off + n].reshape((lead,) + tuple(s)))
            off += n
    return out


def _col_pieces(off, width, shard_w, fetch):
    out, c, end = [], off, off + width
    while c < end:
        d = c // shard_w
        hi = min(end, (d + 1) * shard_w)
        out.append(fetch(d, c - d * shard_w, hi - d * shard_w))
        c = hi
    return out


def _work_cols_of_orig(lo, hi, fetch):
    out = []
    for name, ow in _ORIG:
        o = _ORIG_OFF[name][0]
        a, b = max(lo, o), min(hi, o + ow)
        if a < b:
            w0 = _WORK_OFF[name][0]
            out.append(fetch(w0 + a - o, w0 + b - o))
    return out


def _heads(x, n):
    return x.reshape(x.shape[0], n, HEAD_DIM).transpose(1, 0, 2)


def _unheads(x):
    return x.transpose(1, 0, 2).reshape(x.shape[1], x.shape[0] * HEAD_DIM)


def _t5_bucket_np(dist):
    max_exact = REL_BUCKETS // 2
    d = np.maximum(dist, 0)
    scaled = (np.log(np.maximum(d, 1).astype(np.float32) / np.float32(max_exact))
              / np.float32(math.log(REL_MAX_DIST / max_exact))).astype(np.float32)
    large = np.minimum(max_exact + (scaled * np.float32(REL_BUCKETS - max_exact)).astype(np.int32), REL_BUCKETS - 1)
    return np.where(d < max_exact, d, large)


def _bucket_onehot():
    q_idx = np.arange(BLOCK)[:, None]
    k_idx = np.arange(2 * BLOCK)[None, :]
    bucket = _t5_bucket_np(q_idx + BLOCK - k_idx).reshape(-1)
    oh = np.zeros((LANE, BLOCK * 2 * BLOCK), np.float32)
    oh[bucket, np.arange(bucket.shape[0])] = 1.0
    return oh


def kernel(x, meta_tokens, rel_bias_table, norm_mix, w_in, swa_sinks, fox_forget_bias, conv_w, conv_b, lru_w_r, lru_b_r, lru_w_i, lru_b_i, lru_lambda, w_branch, w_out, norm_ffn, w_ffn_in, w_ffn_out, norm_final, loss_target, m_meta_tokens, m_rel_bias_table, m_norm_mix, m_w_in, m_swa_sinks, m_fox_forget_bias, m_conv_w, m_conv_b, m_lru_w_r, m_lru_b_r, m_lru_w_i, m_lru_b_i, m_lru_lambda, m_w_branch, m_w_out, m_norm_ffn, m_w_ffn_in, m_w_ffn_out, m_norm_final, v_meta_tokens, v_rel_bias_table, v_norm_mix, v_w_in, v_swa_sinks, v_fox_forget_bias, v_conv_w, v_conv_b, v_lru_w_r, v_lru_b_r, v_lru_w_i, v_lru_b_i, v_lru_lambda, v_w_branch, v_w_out, v_norm_ffn, v_w_ffn_in, v_w_ffn_out, v_norm_final):
    S = x.shape[1]
    T = N_META + S
    n_pad = (-T) % BLOCK
    Tp = T + n_pad
    first = n_pad + N_META
    TR = _tile(Tp, 384)
    TQ = _tile(Tp, 384)
    NQ = Tp // TQ
    me = 4 * lax.axis_index("x") + 2 * lax.axis_index("y") + lax.axis_index("c")

    big_names = ("w_in", "w_ffn_in", "w_ffn_out", "w_branch", "w_out")
    big_w = dict(w_in=w_in, w_ffn_in=w_ffn_in, w_ffn_out=w_ffn_out, w_branch=w_branch, w_out=w_out)
    big_m = dict(w_in=m_w_in, w_ffn_in=m_w_ffn_in, w_ffn_out=m_w_ffn_out, w_branch=m_w_branch, w_out=m_w_out)
    big_v = dict(w_in=v_w_in, w_ffn_in=v_w_ffn_in, w_ffn_out=v_w_ffn_out, w_branch=v_w_branch, w_out=v_w_out)
    rows2d = lambda w: w.reshape(-1, w.shape[-1])
    in_shard_w = IN_COLS // N_DEV
    cat_axis = dict(w_in=-1, w_ffn_in=-1, w_ffn_out=0, w_branch=-1, w_out=0)

    def w_send(n, l):
        return rows2d(big_w[n][l]).astype(BF16)

    def w_build(n, g):
        shard = lambda d: g[d].reshape(big_w[n].shape[1:])
        if n != "w_in":
            return jnp.concatenate([shard(d) for d in range(N_DEV)], axis=cat_axis[n])
        parts = []
        for name, width in _WORK:
            off, ow = _ORIG_OFF[name]
            parts += _col_pieces(off, ow, in_shard_w, lambda d, lo, hi: shard(d)[:, lo:hi])
            if width > ow:
                parts.append(jnp.zeros((D_MODEL, width - ow), BF16))
        return jnp.concatenate(parts, axis=-1)

    W_in, W_ffn_in, W_ffn_out, W_branch, W_out = ([None] * DEPTH for _ in range(5))
    W_in[0] = w_build("w_in", _exchange(w_send("w_in", 0), True, "gather_w_in"))

    small_sh = _exchange(_pack([meta_tokens, conv_w], D_MODEL, 8), True, "gather_small")
    g_meta, g_conv_w = _unpack(small_sh, [meta_tokens.shape, conv_w.shape], lead=N_DEV)
    meta_full = g_meta.transpose(1, 0, 2).reshape(N_META, D_MODEL)
    conv_w_full = g_conv_w.transpose(1, 2, 0, 3).reshape(DEPTH, CONV_WIDTH, LRU_WIDTH)

    onehot = jnp.asarray(_bucket_onehot())
    table_t = jnp.pad(rel_bias_table.T, ((0, 0), (0, LANE - REL_BUCKETS)))
    bias = _mm(table_t, onehot, "nn", "swa_bias", exact=True, tn_cap=4096).reshape(SWA_Q_HEADS, BLOCK, 2 * BLOCK)

    def dense_blocks(w):
        rows = []
        for b in range(LRU_BLOCKS):
            rows.append(jnp.pad(w[b], ((0, 0), (b * LRU_BLOCK_DIM, (LRU_BLOCKS - 1 - b) * LRU_BLOCK_DIM))))
        return jnp.concatenate(rows, axis=0)

    h = jnp.concatenate([jnp.zeros((n_pad, D_MODEL), F32), meta_full, x[0]], axis=0)
    saved = []

    for l in range(DEPTH):
        sv = {"h": h}
        g_mix = norm_mix[l][None, :]

        def norm_fwd(i, rows, consts):
            xx = rows[0]
            return [xx * lax.rsqrt(jnp.mean(xx * xx, axis=1, keepdims=True) + EPS) * consts[0]], []

        (u,) = _rowwise(norm_fwd, "norm_fwd", Tp, TR, [(h, None, 0)], [g_mix], [(D_MODEL, F32)])
        proj, got = _mm(u, W_in[l], "nn", "proj", tn_cap=384,
                        host=[(w_send("w_branch", l), True), (w_send("w_out", l), True)])
        W_branch[l], W_out[l] = w_build("w_branch", got[0]), w_build("w_out", got[1])
        sv["u"], sv["proj"] = u, proj

        def col(name):
            off, w = _WORK_OFF[name]
            return (proj, w, off // w)

        def cols(name):
            off, w = _WORK_OFF[name]
            return proj[:, off:off + w]

        qa, ka, va = _heads(cols("qa"), 8), _heads(cols("ka"), 2), _heads(cols("va"), 2)
        sinks_l = jnp.broadcast_to(swa_sinks[l][:, None, None], (SWA_Q_HEADS, 1, LANE))
        o_a_h, lse_a = _swa_fwd(qa, ka, va, bias, sinks_l, n_pad)
        o_a = _unheads(o_a_h)
        sv.update(qa=qa, ka=ka, va=va, sinks=sinks_l, o_a_h=o_a_h, lse_a=lse_a, o_a=o_a)

        fbias = jnp.pad(fox_forget_bias[l], (0, LANE - FOX_HEADS))[None, :]

        def logf_fwd(i, rows, consts):
            return [_log_sigmoid(rows[0] + consts[0])], []

        (logf,) = _rowwise(logf_fwd, "logf_fwd", Tp, TR, [col("fl")], [fbias], [(LANE, F32)])
        cum = _scan(None, logf, "cumsum")
        cum_h = cum[:, :FOX_HEADS].T
        ccol = cum_h[:, :, None]
        crow = jnp.where(jnp.arange(Tp)[None, :] < n_pad, -NEG_INF, cum_h).reshape(FOX_HEADS, NQ, 1, TQ)
        ride = [("w_ffn_in", l), ("w_ffn_out", l)] + ([("w_in", l + 1)] if l + 1 < DEPTH else [])
        (o_f, lse_f), got = _fox_fwd(proj, ccol, crow, TQ, host=[(w_send(n, ll), True) for n, ll in ride])
        W_ffn_in[l], W_ffn_out[l] = w_build("w_ffn_in", got[0]), w_build("w_ffn_out", got[1])
        if l + 1 < DEPTH:
            W_in[l + 1] = w_build("w_in", got[2])
        sv.update(fbias=fbias, ccol=ccol, crow=crow, lse_f=lse_f, o_f=o_f)

        xc = cols("xc")
        xs = [_shift_down(xc, CONV_WIDTH - 1 - i) for i in range(CONV_WIDTH - 1)] + [xc]
        cw = conv_w_full[l]
        cb = conv_b[l][None, :]

        def conv_fwd(i, rows, consts):
            w, b = consts
            acc = rows[0] * w[0:1, :]
            for t in range(1, CONV_WIDTH):
                acc = acc + rows[t] * w[t:t + 1, :]
            return [acc + b], []

        (xconv,) = _rowwise(conv_fwd, "conv_fwd", Tp, TR, [(a, None, 0) for a in xs], [cw, cb], [(LRU_WIDTH, F32)])
        w_ri = jnp.concatenate([dense_blocks(lru_w_r[l]), dense_blocks(lru_w_i[l])], axis=1)
        pre = _mm(xconv, w_ri, "nn", "lru_gates")
        lru_consts = [lru_b_r[l][None, :], lru_b_i[l][None, :], lru_lambda[l][None, :]]

        def lru_fwd(i, rows, consts):
            pr, xv = rows
            b_r, b_i, lam = consts
            r = _sigmoid(pr[:, :LRU_WIDTH] + b_r)
            gi = _sigmoid(pr[:, LRU_WIDTH:] + b_i)
            log_a = LRU_C * r * _log_sigmoid(lam)
            valid = (_row_ids(i, TR) >= n_pad).astype(F32)
            inp = jnp.sqrt(_neg_expm1(2.0 * log_a)) * (gi * xv) * valid
            return [jnp.exp(log_a), inp], []

        a_dec, inp = _rowwise(lru_fwd, "lru_fwd", Tp, TR, [(pre, None, 0), (xconv, None, 0)], lru_consts,
                              [(LRU_WIDTH, F32), (LRU_WIDTH, F32)])
        hs = _scan(a_dec, inp, "lru_scan")

        def oc_fwd(i, rows, consts):
            return [rows[0] * _gelu(rows[1])], []

        (o_c,) = _rowwise(oc_fwd, "oc_fwd", Tp, TR, [(hs, None, 0), col("yc")], [], [(LRU_WIDTH, F32)])
        sv.update(xs=xs, cw=cw, xconv=xconv, w_ri=w_ri, pre=pre, lru_consts=lru_consts, a_dec=a_dec, hs=hs, o_c=o_c)

        ba = _mm(o_a, W_branch[l][0], "nn", "branch")
        bf = _mm(o_f, W_branch[l][1], "nn", "branch")
        bc = _mm(o_c, W_branch[l][2], "nn", "branch")

        def merge_fwd(i, rows, consts):
            g, b0, b1, b2 = rows
            valid = (_row_ids(i, BLOCK) >= n_pad).astype(F32)
            mg = (_sigmoid(g[:, :D_MODEL]) * b0 + _sigmoid(g[:, D_MODEL:2 * D_MODEL]) * b1
                  + _sigmoid(g[:, 2 * D_MODEL:]) * b2)
            return [mg * valid], []

        (merged,) = _rowwise(merge_fwd, "merge_fwd", Tp, BLOCK,
                             [col("gates"), (ba, None, 0), (bf, None, 0), (bc, None, 0)], [], [(D_MODEL, F32)])
        h1 = _mm(merged, W_out[l], "nn", "out_proj", res=h)
        sv.update(ba=ba, bf=bf, bc=bc, merged=merged, h1=h1)

        g_ffn = norm_ffn[l][None, :]
        (u2,) = _rowwise(norm_fwd, "norm_fwd", Tp, TR, [(h1, None, 0)], [g_ffn], [(D_MODEL, F32)])
        ff = _mm(u2, W_ffn_in[l], "nn", "ffn_in")

        def act_fwd(i, rows, consts):
            gate, up = rows
            return [gate * _sigmoid(gate) * up], []

        (act,) = _rowwise(act_fwd, "act_fwd", Tp, TR, [(ff, D_FF, 0), (ff, D_FF, 1)], [], [(D_FF, F32)])
        h = _mm(act, W_ffn_out[l], "nn", "ffn_out", res=h1)
        sv.update(u2=u2, ff=ff, act=act)
        saved.append(sv)

    tgt = jnp.concatenate([jnp.zeros((first, D_MODEL), F32), loss_target[0]], axis=0)
    g_fin = norm_final[None, :]

    def head(i, rows, consts):
        xx, tg = rows
        g = consts[0]
        valid = (_row_ids(i, TR) >= first).astype(F32)
        rstd = lax.rsqrt(jnp.mean(xx * xx, axis=1, keepdims=True) + EPS)
        xhat = xx * rstd
        err = (xhat * g - tg) * valid
        loss_rows = 0.5 * jnp.mean(err * err, axis=1, keepdims=True)
        dy = err * (1.0 / D_MODEL)
        dxhat = dy * g
        dx = rstd * (dxhat - xhat * jnp.mean(dxhat * xhat, axis=1, keepdims=True))
        return [dx], [jnp.broadcast_to(_colsum(loss_rows), (1, LANE)), _colsum(dy * xhat)]

    dh, loss_part, d_norm_final = _rowwise(head, "loss_head", Tp, TR, [(h, None, 0), (tgt, None, 0)], [g_fin],
                                           [(D_MODEL, F32)], [LANE, D_MODEL])
    loss = lax.psum(loss_part[0, 0], ("x", "y", "c"))

    def norm_bwd_call(xin, g, du, dres):
        def norm_bwd(i, rows, consts):
            xx, dd, rr = rows
            gg = consts[0]
            valid = (_row_ids(i, TR) >= n_pad).astype(F32)
            rstd = lax.rsqrt(jnp.mean(xx * xx, axis=1, keepdims=True) + EPS)
            xhat = xx * rstd
            dxhat = dd * gg
            dx = rstd * (dxhat - xhat * jnp.mean(dxhat * xhat, axis=1, keepdims=True))
            return [rr + dx * valid], [_colsum(dd * xhat)]

        return _rowwise(norm_bwd, "norm_bwd", Tp, TR, [(xin, None, 0), (du, None, 0), (dres, None, 0)], [g],
                        [(D_MODEL, F32)], [D_MODEL])

    grads = {k: [None] * DEPTH for k in ("norm_mix", "w_in", "swa_sinks", "fox_forget_bias", "conv_w", "conv_b",
                                         "lru_w_r", "lru_b_r", "lru_w_i", "lru_b_i", "lru_lambda", "w_branch",
                                         "w_out", "norm_ffn", "w_ffn_in", "w_ffn_out")}
    d_bias_total = None

    def g_send(n, g):
        if n == "w_in":
            pieces = [jnp.concatenate(_work_cols_of_orig(d * in_shard_w, (d + 1) * in_shard_w,
                                                         lambda lo, hi: g[:, lo:hi]), axis=-1) for d in range(N_DEV)]
        else:
            ax = cat_axis[n] % g.ndim
            w = big_w[n].shape[1 + ax]
            pieces = [lax.slice_in_dim(g, d * w, (d + 1) * w, axis=ax) for d in range(N_DEV)]
        return jnp.stack([rows2d(p) for p in pieces])

    recv = {n: [None] * DEPTH for n in big_names}
    for l in reversed(range(DEPTH)):
        sv = saved[l]
        proj = sv["proj"]

        def col(name):
            off, w = _WORK_OFF[name]
            return (proj, w, off // w)

        dh2 = dh
        d_act = _mm(dh2, W_ffn_out[l], "nt", "d_act")
        grads["w_ffn_out"][l] = _mm(sv["act"], dh2, "tn", "dw_ffn_out")

        def act_bwd(i, rows, consts):
            gate, up, da = rows
            sg = _sigmoid(gate)
            d_gate = da * up * (sg * (1.0 + gate * (1.0 - sg)))
            d_up = da * (gate * sg)
            return [jnp.concatenate([d_gate, d_up], axis=1)], []

        (dff,) = _rowwise(act_bwd, "act_bwd", Tp, BLOCK, [(sv["ff"], D_FF, 0), (sv["ff"], D_FF, 1), (d_act, None, 0)],
                          [], [(2 * D_FF, F32)])
        grads["w_ffn_in"][l] = _mm(sv["u2"], dff, "tn", "dw_ffn_in")
        du2 = _mm(dff, W_ffn_in[l], "nt", "du2")
        dh1, dg = norm_bwd_call(sv["h1"], norm_ffn[l][None, :], du2, dh2)
        grads["norm_ffn"][l] = dg[0]

        dmerged = _mm(dh1, W_out[l], "nt", "d_merged")
        grads["w_out"][l] = _mm(sv["merged"], dh1, "tn", "dw_out")

        def merge_bwd(i, rows, consts):
            g, b0, b1, b2, dm = rows
            dm = dm * (_row_ids(i, BLOCK) >= n_pad).astype(F32)
            outs, dgs = [], []
            for k, bk in enumerate((b0, b1, b2)):
                sg = _sigmoid(g[:, k * D_MODEL:(k + 1) * D_MODEL])
                outs.append(dm * sg)
                dgs.append(dm * bk * sg * (1.0 - sg))
            return outs + [jnp.concatenate(dgs, axis=1)], []

        d_ba, d_bf, d_bc, d_gates = _rowwise(
            merge_bwd, "merge_bwd", Tp, BLOCK,
            [col("gates"), (sv["ba"], None, 0), (sv["bf"], None, 0), (sv["bc"], None, 0), (dmerged, None, 0)], [],
            [(D_MODEL, F32)] * 3 + [(3 * D_MODEL, F32)])
        grads["w_branch"][l] = jnp.stack([_mm(sv["o_a"], d_ba, "tn", "dw_branch"),
                                          _mm(sv["o_f"], d_bf, "tn", "dw_branch"),
                                          _mm(sv["o_c"], d_bc, "tn", "dw_branch")])
        do_a = _mm(d_ba, W_branch[l][0], "nt", "d_branch")
        do_f = _mm(d_bf, W_branch[l][1], "nt", "d_branch")
        do_c = _mm(d_bc, W_branch[l][2], "nt", "d_branch")

        def oc_bwd(i, rows, consts):
            d, hv, yv = rows
            return [d * _gelu(yv), d * hv * _gelu_grad(yv)], []

        d_hs, d_yc = _rowwise(oc_bwd, "oc_bwd", Tp, TR, [(do_c, None, 0), (sv["hs"], None, 0), col("yc")], [],
                              [(LRU_WIDTH, F32)] * 2)
        d_state = _rev_scan(sv["a_dec"], d_hs, "lru_scan")
        hs_prev = _shift_down(sv["hs"], 1)

        def lru_bwd(i, rows, consts):
            dH, hp, pr, xv = rows
            b_r, b_i, lam = consts
            valid = (_row_ids(i, TR) >= n_pad).astype(F32)
            r = _sigmoid(pr[:, :LRU_WIDTH] + b_r)
            gi = _sigmoid(pr[:, LRU_WIDTH:] + b_i)
            lsl = _log_sigmoid(lam)
            log_a = LRU_C * r * lsl
            a = jnp.exp(log_a)
            one_m_e = _neg_expm1(2.0 * log_a)
            mult = jnp.sqrt(one_m_e)
            d_inp = dH * valid
            d_mult = d_inp * gi * xv
            d_gi = d_inp * mult * xv
            d_x = d_inp * mult * gi
            d_log_a = dH * hp * a - d_mult * (1.0 - one_m_e) / mult
            d_pre_r = d_log_a * (LRU_C * lsl) * r * (1.0 - r)
            d_pre_i = d_gi * gi * (1.0 - gi)
            d_lam = _colsum(d_log_a * (LRU_C * r)) * _sigmoid(-lam)
            return [jnp.concatenate([d_pre_r, d_pre_i], axis=1), d_x], [_colsum(d_pre_r), _colsum(d_pre_i), d_lam]

        d_pre, d_xdir, d_b_r, d_b_i, d_lam = _rowwise(
            lru_bwd, "lru_bwd", Tp, TR, [(d_state, None, 0), (hs_prev, None, 0), (sv["pre"], None, 0),
                                         (sv["xconv"], None, 0)], sv["lru_consts"],
            [(2 * LRU_WIDTH, F32), (LRU_WIDTH, F32)], [LRU_WIDTH] * 3)
        grads["lru_b_r"][l], grads["lru_b_i"][l], grads["lru_lambda"][l] = d_b_r[0], d_b_i[0], d_lam[0]
        d_w_ri = _mm(sv["xconv"], d_pre, "tn", "dw_lru_gates")

        def diag_blocks(wd):
            w4 = wd.reshape(LRU_BLOCKS, LRU_BLOCK_DIM, LRU_BLOCKS, LRU_BLOCK_DIM)
            return jnp.stack([w4[b, :, b, :] for b in range(LRU_BLOCKS)])

        grads["lru_w_r"][l] = diag_blocks(d_w_ri[:, :LRU_WIDTH])
        grads["lru_w_i"][l] = diag_blocks(d_w_ri[:, LRU_WIDTH:])
        d_conv = _mm(d_pre, sv["w_ri"], "nt", "d_xconv", res=d_xdir)
        dcs = [d_conv] + [_shift_up(d_conv, s) for s in range(1, CONV_WIDTH)]

        def conv_bwd(i, rows, consts):
            w = consts[0]
            d = rows[:CONV_WIDTH]
            xsh = rows[CONV_WIDTH:]
            dxc = d[0] * w[CONV_WIDTH - 1:CONV_WIDTH, :]
            for s in range(1, CONV_WIDTH):
                dxc = dxc + d[s] * w[CONV_WIDTH - 1 - s:CONV_WIDTH - s, :]
            return [dxc], [_colsum(xsh[t] * d[0]) for t in range(CONV_WIDTH)] + [_colsum(d[0])]

        res = _rowwise(conv_bwd, "conv_bwd", Tp, TR, [(a, None, 0) for a in dcs + sv["xs"]], [sv["cw"]],
                       [(LRU_WIDTH, F32)], [LRU_WIDTH] * (CONV_WIDTH + 1))
        d_xc = res[0]
        grads["conv_w"][l] = jnp.concatenate(res[1:1 + CONV_WIDTH], axis=0)
        grads["conv_b"][l] = res[1 + CONV_WIDTH][0]

        (dqf, delta_f, dccol), got = _fox_bwd_dq(
            proj, sv["ccol"], sv["crow"], sv["o_f"], sv["lse_f"], do_f, TQ,
            host=[(g_send("w_ffn_in", grads["w_ffn_in"][l]), False)])
        recv["w_ffn_in"][l] = got[0]
        ride = [("w_out", l), ("w_branch", l)] + ([("w_in", l + 1)] if l + 1 < DEPTH else [])
        (dkf, dvf, dcrow), got = _fox_bwd_dkv(
            proj, sv["ccol"], sv["crow"], sv["lse_f"], delta_f, do_f, TQ,
            host=[(g_send(n, grads[n][ll]), False) for n, ll in ride])
        for (n, ll), r in zip(ride, got):
            recv[n][ll] = r
        lanes = lambda z: jnp.pad(z.reshape(FOX_HEADS, Tp).T, ((0, 0), (0, LANE - FOX_HEADS)))
        dlogf = _rev_scan(None, lanes(dcrow), "cumsum_bwd", g2=lanes(dccol))

        def logf_bwd(i, rows, consts):
            dl, fl = rows
            valid = (_row_ids(i, TR) >= n_pad).astype(F32)
            lane_ok = (lax.broadcasted_iota(jnp.int32, (1, LANE), 1) < FOX_HEADS).astype(F32)
            dfl = dl * _sigmoid(-(fl + consts[0])) * valid * lane_ok
            return [dfl], [_colsum(dfl)]

        d_fl, d_fb = _rowwise(logf_bwd, "logf_bwd", Tp, TR, [(dlogf, None, 0), col("fl")], [sv["fbias"]],
                              [(LANE, F32)], [LANE])
        grads["fox_forget_bias"][l] = d_fb[0, :FOX_HEADS]

        (dqa, dk_strip, dv_strip, d_bias_l, d_sink), got = _swa_bwd(
            sv["qa"], sv["ka"], sv["va"], bias, sv["sinks"], sv["o_a_h"], sv["lse_a"], _heads(do_a, 8), n_pad,
            host=[(g_send("w_ffn_out", grads["w_ffn_out"][l]), False)])
        recv["w_ffn_out"][l] = got[0]
        grads["swa_sinks"][l] = d_sink[:, 0, 0]
        d_bias_total = d_bias_l if d_bias_total is None else d_bias_total + d_bias_l

        def kv_parts(strip):
            cur = strip[:, :, BLOCK:, :].reshape(SWA_Q_HEADS, Tp, HEAD_DIM)
            nxt = jnp.pad(strip[:, 1:, :BLOCK, :], ((0, 0), (0, 1), (TQ - BLOCK, 0), (0, 0)))
            nxt = nxt.reshape(SWA_Q_HEADS, Tp, HEAD_DIM)
            both = jnp.stack([cur, nxt]).reshape(2, SWA_KV_HEADS, SWA_GROUP, Tp, HEAD_DIM)
            return both.transpose(3, 0, 2, 1, 4).reshape(Tp, 2 * SWA_GROUP * KV_W)

        dk_parts, dv_parts = kv_parts(dk_strip), kv_parts(dv_strip)

        def assemble(i, rows, consts):
            dg, dqa_, dqf_, dkf_, dvf_, dxc_, dyc_, dkp, dvp, dfl_ = rows

            def fold(parts):
                acc = parts[:, :KV_W]
                for p in range(1, 2 * SWA_GROUP):
                    acc = acc + parts[:, p * KV_W:(p + 1) * KV_W]
                return acc

            return [jnp.concatenate([dg, dqa_, dqf_, dkf_, dvf_, dxc_, dyc_, fold(dkp), fold(dvp), dfl_], axis=1)], []

        (dproj,) = _rowwise(
            assemble, "assemble_dproj", Tp, BLOCK,
            [(d_gates, None, 0), (_unheads(dqa), None, 0), (dqf, None, 0), (dkf, None, 0),
             (dvf, None, 0), (d_xc, None, 0), (d_yc, None, 0), (dk_parts, None, 0), (dv_parts, None, 0),
             (d_fl, None, 0)], [], [(WORK_COLS, F32)])
        grads["w_in"][l] = _mm(sv["u"], dproj, "tn", "dw_in", tn_cap=384)
        du = _mm(dproj, W_in[l], "nt", "du", tk_cap=384)
        dh, dg = norm_bwd_call(sv["h"], norm_mix[l][None, :], du, dh1)
        grads["norm_mix"][l] = dg[0]

    d_table = _mm(d_bias_total.reshape(SWA_Q_HEADS, BLOCK * 2 * BLOCK), onehot, "nt", "d_rel_table", exact=True,
                  tk_cap=4096)
    g_rel = d_table[:, :REL_BUCKETS].T
    g_meta_full = dh[n_pad:first]
    grad_x = dh[first:][None]

    stack = lambda k: jnp.stack(grads[k])
    recv["w_in"][0] = _exchange(g_send("w_in", grads["w_in"][0]), False, "scatter_w_in")
    big_res = {}
    for n in big_names:
        per_layer = [_adam(recv[n][l], rows2d(big_w[n][l]), rows2d(big_m[n][l]), rows2d(big_v[n][l]), "adam_" + n)
                     for l in range(DEPTH)]
        big_res[n] = [jnp.stack([per_layer[l][k] for l in range(DEPTH)]).reshape(big_w[n].shape) for k in range(4)]

    rep_names = ("rel_bias_table", "norm_mix", "swa_sinks", "fox_forget_bias", "conv_b", "lru_w_r", "lru_b_r",
                 "lru_w_i", "lru_b_i", "lru_lambda", "norm_ffn", "norm_final")
    rep_w = dict(rel_bias_table=rel_bias_table, norm_mix=norm_mix, swa_sinks=swa_sinks,
                 fox_forget_bias=fox_forget_bias, conv_b=conv_b, lru_w_r=lru_w_r, lru_b_r=lru_b_r, lru_w_i=lru_w_i,
                 lru_b_i=lru_b_i, lru_lambda=lru_lambda, norm_ffn=norm_ffn, norm_final=norm_final)
    rep_m = dict(rel_bias_table=m_rel_bias_table, norm_mix=m_norm_mix, swa_sinks=m_swa_sinks,
                 fox_forget_bias=m_fox_forget_bias, conv_b=m_conv_b, lru_w_r=m_lru_w_r, lru_b_r=m_lru_b_r,
                 lru_w_i=m_lru_w_i, lru_b_i=m_lru_b_i, lru_lambda=m_lru_lambda, norm_ffn=m_norm_ffn,
                 norm_final=m_norm_final)
    rep_v = dict(rel_bias_table=v_rel_bias_table, norm_mix=v_norm_mix, swa_sinks=v_swa_sinks,
                 fox_forget_bias=v_fox_forget_bias, conv_b=v_conv_b, lru_w_r=v_lru_w_r, lru_b_r=v_lru_b_r,
                 lru_w_i=v_lru_w_i, lru_b_i=v_lru_b_i, lru_lambda=v_lru_lambda, norm_ffn=v_norm_ffn,
                 norm_final=v_norm_final)
    rep_g = {n: (g_rel if n == "rel_bias_table" else d_norm_final[0] if n == "norm_final" else stack(n))
             for n in rep_names}
    small_g = [rep_g[n] for n in rep_names] + [g_meta_full, stack("conv_w")]
    small_shapes = [rep_w[n].shape for n in rep_names] + [(N_META, D_MODEL), (DEPTH, CONV_WIDTH, LRU_WIDTH)]
    gs_all = _exchange(_pack(small_g, D_MODEL, 8), True, "gather_small_grads")
    gs_sum = _unpack(_sum_parts(gs_all, "sum_small_grads"), small_shapes)
    gsum = dict(zip(rep_names, gs_sum[:len(rep_names)]))
    g_meta_sh = lax.dynamic_slice_in_dim(gs_sum[-2], me * (D_MODEL // N_DEV), D_MODEL // N_DEV, axis=1)
    g_convw_sh = lax.dynamic_slice_in_dim(gs_sum[-1], me * (LRU_WIDTH // N_DEV), LRU_WIDTH // N_DEV, axis=2)
    sm_names = rep_names + ("meta_tokens", "conv_w")
    sm_w = [rep_w[n] for n in rep_names] + [meta_tokens, conv_w]
    sm_m = [rep_m[n] for n in rep_names] + [m_meta_tokens, m_conv_w]
    sm_v = [rep_v[n] for n in rep_names] + [v_meta_tokens, v_conv_w]
    sm_g = [gsum[n] for n in rep_names] + [g_meta_sh, g_convw_sh]
    sm_shapes = [w.shape for w in sm_w]
    sm_out = [_unpack(o, sm_shapes) for o in _adam(_pack(sm_g, D_MODEL, 8)[None], _pack(sm_w, D_MODEL, 8),
                                                   _pack(sm_m, D_MODEL, 8), _pack(sm_v, D_MODEL, 8), "adam_small")]
    sm_res = {n: [sm_out[k][j] for k in range(4)] for j, n in enumerate(sm_names)}

    order = ("meta_tokens", "rel_bias_table", "norm_mix", "w_in", "swa_sinks", "fox_forget_bias", "conv_w", "conv_b",
             "lru_w_r", "lru_b_r", "lru_w_i", "lru_b_i", "lru_lambda", "w_branch", "w_out", "norm_ffn", "w_ffn_in",
             "w_ffn_out", "norm_final")
    allres = {**big_res, **sm_res}
    outs = [loss, grad_x]
    for k in range(4):
        outs += [allres[n][k] for n in order]
    return tuple(outs)
```

```python
import functools
import math

import numpy as np
import jax
import jax.numpy as jnp
from jax import lax
from jax.experimental import pallas as pl
from jax.experimental.pallas import tpu as pltpu

F32 = jnp.float32
BF16 = jnp.bfloat16

N_DEV = 8
D_MODEL = 1024
DEPTH = 4
HEAD_DIM = 64
N_META = 16
BLOCK = 128
NEG_INF = -1e30
SWA_Q_HEADS = 8
SWA_KV_HEADS = 2
SWA_GROUP = SWA_Q_HEADS // SWA_KV_HEADS
FOX_HEADS = 8
LRU_WIDTH = D_MODEL // 2
LRU_BLOCKS = 8
LRU_BLOCK_DIM = LRU_WIDTH // LRU_BLOCKS
CONV_WIDTH = 4
LRU_C = 8.0
REL_BUCKETS = 32
REL_MAX_DIST = 128
D_FF = 2816
N_BRANCH = 3
ATT_W = SWA_Q_HEADS * HEAD_DIM
KV_W = SWA_KV_HEADS * HEAD_DIM
SCALE = HEAD_DIM ** -0.5
EPS = 1e-6

_ORIG = (("qa", ATT_W), ("ka", KV_W), ("va", KV_W), ("qf", ATT_W), ("kf", ATT_W), ("vf", ATT_W),
         ("fl", FOX_HEADS), ("xc", LRU_WIDTH), ("yc", LRU_WIDTH), ("gates", N_BRANCH * D_MODEL))
IN_COLS = sum(w for _, w in _ORIG)
_WORK = (("gates", 3072), ("qa", 512), ("qf", 512), ("kf", 512), ("vf", 512), ("xc", 512), ("yc", 512),
         ("ka", 128), ("va", 128), ("fl", 128))
WORK_COLS = sum(w for _, w in _WORK)


def _offsets(table):
    off, out = 0, {}
    for n, w in table:
        out[n] = (off, w)
        off += w
    return out


_ORIG_OFF = _offsets(_ORIG)
_WORK_OFF = _offsets(_WORK)

ADAM_LR = 0.001
ADAM_B1 = 0.9
ADAM_B2 = 0.999
ADAM_EPS = 1e-08
ADAM_WD = 0.01
ADAM_STEP = 10

VMEM_LIMIT = 52 * 1024 * 1024
LANE = 128


def _tile(n, cap, mult=LANE):
    if n <= cap:
        return n
    best = None
    for d in range(mult, cap + 1, mult):
        if n % d == 0:
            best = d
    assert best is not None, (n, cap, mult)
    return best


def _params(sem):
    return pltpu.CompilerParams(dimension_semantics=sem, vmem_limit_bytes=VMEM_LIMIT)


def _sigmoid(x):
    return 1.0 / (1.0 + jnp.exp(-x))


def _log_sigmoid(x):
    return jnp.minimum(x, 0.0) - jnp.log(1.0 + jnp.exp(-jnp.abs(x)))


def _neg_expm1(x):
    series = -x * (1.0 + x * (0.5 + x * (1.0 / 6.0 + x * (1.0 / 24.0 + x * (1.0 / 120.0)))))
    return jnp.where(x > -0.1, series, 1.0 - jnp.exp(x))


_GELU_C = math.sqrt(2.0 / math.pi)


def _gelu(x):
    return 0.5 * x * (1.0 + jnp.tanh(_GELU_C * (x + 0.044715 * x * x * x)))


def _gelu_grad(x):
    t = jnp.tanh(_GELU_C * (x + 0.044715 * x * x * x))
    return 0.5 * (1.0 + t) + 0.5 * x * (1.0 - t * t) * _GELU_C * (1.0 + 3.0 * 0.044715 * x * x)


def _xch_ops(src_ref, out_ref, send_sems, recv_sems, local_sem, gather):
    x, y, c = lax.axis_index("x"), lax.axis_index("y"), lax.axis_index("c")
    me = 4 * x + 2 * y + c

    def copy(r, mine):
        px, py, pc = x ^ ((r >> 2) & 1), y ^ ((r >> 1) & 1), c ^ (r & 1)
        pid = 4 * px + 2 * py + pc
        return pltpu.make_async_remote_copy(
            src_ref=src_ref if gather else src_ref.at[pid], dst_ref=out_ref.at[me if mine else pid],
            send_sem=send_sems.at[r - 1], recv_sem=recv_sems.at[r - 1],
            device_id=(px, py, pc), device_id_type=pl.DeviceIdType.MESH)

    def local():
        return pltpu.make_async_copy(src_ref if gather else src_ref.at[me], out_ref.at[me], local_sem)

    def start():
        local().start()
        for r in range(1, N_DEV):
            copy(r, True).start()

    def wait():
        for r in range(1, N_DEV):
            copy(r, False).wait_recv()
        for r in range(1, N_DEV):
            copy(r, True).wait_send()
        local().wait()

    return start, wait


_XCH_SEMS = [pltpu.SemaphoreType.DMA((N_DEV - 1,)), pltpu.SemaphoreType.DMA((N_DEV - 1,)), pltpu.SemaphoreType.DMA]
_ANY = pl.BlockSpec(memory_space=pl.ANY)


def _xch_shape(src, gather):
    return jax.ShapeDtypeStruct((N_DEV,) + tuple(src.shape if gather else src.shape[1:]), src.dtype)


def _exchange(src, gather, name):
    def body(src_ref, out_ref, send_sems, recv_sems, local_sem):
        start, wait = _xch_ops(src_ref, out_ref, send_sems, recv_sems, local_sem, gather)
        start()
        wait()

    return pl.pallas_call(body, name=name, in_specs=[_ANY], out_specs=_ANY, out_shape=_xch_shape(src, gather),
                          scratch_shapes=list(_XCH_SEMS))(src)


def _pcall(body, name, grid, in_specs, out_specs, out_shape, scratch_shapes, sem, args, host=()):
    if not host:
        outs = pl.pallas_call(body, name=name, grid=grid, in_specs=in_specs, out_specs=out_specs, out_shape=out_shape,
                              scratch_shapes=scratch_shapes, compiler_params=_params(sem))(*args)
        return outs, []
    n_in, n_out, n_scr, n_x = len(args), len(out_shape), len(scratch_shapes), len(host)

    def wrapped(*refs):
        ins, xin = refs[:n_in], refs[n_in:n_in + n_x]
        outs, xout = refs[n_in + n_x:n_in + n_x + n_out], refs[n_in + n_x + n_out:n_in + 2 * n_x + n_out]
        scr = refs[n_in + 2 * n_x + n_out:n_in + 2 * n_x + n_out + n_scr]
        sems = refs[n_in + 2 * n_x + n_out + n_scr:]
        first = functools.reduce(jnp.logical_and, [pl.program_id(k) == 0 for k in range(len(grid))])
        last = functools.reduce(jnp.logical_and, [pl.program_id(k) == grid[k] - 1 for k in range(len(grid))])
        ops = [_xch_ops(xin[k], xout[k], *sems[3 * k:3 * k + 3], host[k][1]) for k in range(n_x)]

        @pl.when(first)
        def _():
            for start, _ in ops:
                start()

        body(*ins, *outs, *scr)

        @pl.when(last)
        def _():
            for _, wait in ops:
                wait()

    res = pl.pallas_call(
        wrapped, name=name, grid=grid, in_specs=list(in_specs) + [_ANY] * n_x,
        out_specs=list(out_specs) + [_ANY] * n_x,
        out_shape=list(out_shape) + [_xch_shape(s, g) for s, g in host],
        scratch_shapes=list(scratch_shapes) + list(_XCH_SEMS) * n_x,
        compiler_params=_params(("arbitrary",) * len(grid)))(*args, *[s for s, _ in host])
    return res[:n_out], res[n_out:]


def _mm(a, b, mode, name, res=None, exact=False, tm_cap=1408, tn_cap=512, tk_cap=1408, host=None):
    if mode == "nn":
        (M, K), (K2, N) = a.shape, b.shape
    elif mode == "nt":
        (M, K), (N, K2) = a.shape, b.shape
    else:
        (K, M), (K2, N) = a.shape, b.shape
    assert K == K2, (a.shape, b.shape, mode)
    tm, tn, tk = _tile(M, tm_cap), _tile(N, tn_cap), _tile(K, tk_cap)
    nk = K // tk
    a_spec = {"nn": pl.BlockSpec((tm, tk), lambda i, j, k: (i, k)),
              "nt": pl.BlockSpec((tm, tk), lambda i, j, k: (i, k)),
              "tn": pl.BlockSpec((tk, tm), lambda i, j, k: (k, i))}[mode]
    b_spec = {"nn": pl.BlockSpec((tk, tn), lambda i, j, k: (k, j)),
              "nt": pl.BlockSpec((tn, tk), lambda i, j, k: (j, k)),
              "tn": pl.BlockSpec((tk, tn), lambda i, j, k: (k, j))}[mode]
    o_spec = pl.BlockSpec((tm, tn), lambda i, j, k: (i, j))
    has_res = res is not None

    def body(*refs):
        if has_res:
            a_ref, b_ref, r_ref, o_ref, acc_ref = refs
        else:
            a_ref, b_ref, o_ref, acc_ref = refs
        k = pl.program_id(2)

        @pl.when(k == 0)
        def _():
            acc_ref[...] = jnp.zeros_like(acc_ref)

        x, y = a_ref[...], b_ref[...]
        if exact:
            x, y, prec = x.astype(F32), y.astype(F32), lax.Precision.HIGHEST
        else:
            x, y, prec = x.astype(BF16), y.astype(BF16), None
        if mode == "tn":
            x = x.T
        dims = (((1,), (1,)), ((), ())) if mode == "nt" else (((1,), (0,)), ((), ()))
        acc_ref[...] += lax.dot_general(x, y, dims, precision=prec, preferred_element_type=F32)

        @pl.when(k == nk - 1)
        def _():
            if has_res:
                o_ref[...] = acc_ref[...] + r_ref[...]
            else:
                o_ref[...] = acc_ref[...]

    in_specs = [a_spec, b_spec] + ([o_spec] if has_res else [])
    args = (a, b) + ((res,) if has_res else ())
    (out,), xouts = _pcall(body, name, (M // tm, N // tn, nk), in_specs, [o_spec],
                           [jax.ShapeDtypeStruct((M, N), F32)], [pltpu.VMEM((tm, tn), F32)],
                           ("parallel", "parallel", "arbitrary"), args, host or ())
    return out if host is None else (out, xouts)


def _rowwise(fn, name, n_rows, tr, row_ins, const_ins, row_outs, red_outs=()):
    n_row_in, n_const, n_row_out, n_red = len(row_ins), len(const_ins), len(row_outs), len(red_outs)
    in_specs = []
    for arr, w, idx in row_ins:
        width = arr.shape[1] if w is None else w
        in_specs.append(pl.BlockSpec((tr, width), functools.partial(lambda i, j: (i, j), j=idx)))
    for c in const_ins:
        in_specs.append(pl.BlockSpec(c.shape, lambda i: (0, 0)))
    out_shape = [jax.ShapeDtypeStruct((n_rows, c), dt) for c, dt in row_outs]
    out_shape += [jax.ShapeDtypeStruct((1, c), F32) for c in red_outs]
    out_specs = [pl.BlockSpec((tr, c), lambda i: (i, 0)) for c, _ in row_outs]
    out_specs += [pl.BlockSpec((1, c), lambda i: (0, 0)) for c in red_outs]

    def body(*refs):
        i = pl.program_id(0)
        rows = [r[...] for r in refs[:n_row_in]]
        consts = [r[...] for r in refs[n_row_in:n_row_in + n_const]]
        outs = refs[n_row_in + n_const:]
        row_vals, red_vals = fn(i, rows, consts)
        for k in range(n_row_out):
            outs[k][...] = row_vals[k].astype(outs[k].dtype)
        if n_red:
            @pl.when(i == 0)
            def _():
                for k in range(n_red):
                    outs[n_row_out + k][...] = jnp.zeros_like(outs[n_row_out + k])

            for k in range(n_red):
                outs[n_row_out + k][...] += red_vals[k]

    res = pl.pallas_call(
        body, name=name, grid=(n_rows // tr,), in_specs=in_specs, out_specs=out_specs, out_shape=out_shape,
        compiler_params=_params(("arbitrary",)))(*[a for a, _, _ in row_ins], *const_ins)
    return res


def _row_ids(i, tr):
    return i * tr + lax.broadcasted_iota(jnp.int32, (tr, 1), 0)


def _colsum(x):
    return jnp.sum(x, axis=0, keepdims=True)


def _scan(a, b, name, b2=None, reverse=False):
    n_rows, c = b.shape
    tr = _tile(n_rows, 384)
    nblk = n_rows // tr
    has_a = a is not None
    has_b2 = b2 is not None

    def body(*refs):
        o_ref, carry = refs[-2:]
        ins = list(refs[:-2])
        a_ref = ins.pop(0) if has_a else None
        b_ref = ins.pop(0)

        @pl.when(pl.program_id(0) == 0)
        def _():
            carry[...] = jnp.zeros_like(carry)

        rows = lax.broadcasted_iota(jnp.int32, (tr, c), 0)
        bv = b_ref[...] + ins[0][...] if has_b2 else b_ref[...]
        av = a_ref[...] if has_a else None
        s = 1
        while s < tr:
            keep = rows < tr - s if reverse else rows >= s
            shift = tr - s if reverse else s
            b_sh = jnp.where(keep, pltpu.roll(bv, shift, 0), 0.0)
            if has_a:
                a_sh = jnp.where(keep, pltpu.roll(av, shift, 0), 1.0)
                bv = av * b_sh + bv
                av = av * a_sh
            else:
                bv = b_sh + bv
            s *= 2
        h = av * carry[0:1, :] + bv if has_a else carry[0:1, :] + bv
        o_ref[...] = h
        edge = 0 if reverse else tr - 1
        carry[...] = jnp.broadcast_to(h[edge:edge + 1, :], carry.shape)

    spec = pl.BlockSpec((tr, c), (lambda i: (nblk - 1 - i, 0)) if reverse else (lambda i: (i, 0)))
    args = ((a,) if has_a else ()) + (b,) + ((b2,) if has_b2 else ())
    return pl.pallas_call(
        body, name=name, grid=(n_rows // tr,), in_specs=[spec] * len(args), out_specs=spec,
        out_shape=jax.ShapeDtypeStruct((n_rows, c), F32), scratch_shapes=[pltpu.VMEM((8, c), F32)],
        compiler_params=_params(("arbitrary",)))(*args)


def _shift_down(x, s, fill=0.0):
    pad = jnp.full((s,) + x.shape[1:], fill, x.dtype)
    return jnp.concatenate([pad, x[:-s]], axis=0)


def _shift_up(x, s):
    pad = jnp.zeros((s,) + x.shape[1:], x.dtype)
    return jnp.concatenate([x[s:], pad], axis=0)


def _rev_scan(a, g, name, g2=None):
    return _scan(None if a is None else _shift_up(a, 1), g, name, b2=g2, reverse=True)


def _swa_masks(blk, n_pad):
    qi = lax.broadcasted_iota(jnp.int32, (BLOCK, 2 * BLOCK), 0)
    ki = lax.broadcasted_iota(jnp.int32, (BLOCK, 2 * BLOCK), 1)
    dist = qi + BLOCK - ki
    key_abs = (blk - 1) * BLOCK + ki
    return (dist >= 0) & (dist < BLOCK) & (key_abs >= n_pad)


def _swa_fwd(q, k, v, bias, sinks, n_pad):
    H, Tp, dh = q.shape
    tq = _tile(Tp, 384)
    nsub = tq // BLOCK
    nq = Tp // tq

    def body(q_ref, kp_ref, kc_ref, vp_ref, vc_ref, b_ref, s_ref, o_ref, lse_ref):
        i = pl.program_id(1)
        kstrip = jnp.concatenate([kp_ref[0], kc_ref[0]], axis=0).astype(BF16)
        vstrip = jnp.concatenate([vp_ref[0], vc_ref[0]], axis=0).astype(BF16)
        sink = s_ref[0, :, 0:1]
        bias_h = b_ref[0]
        for j in range(nsub):
            qj = q_ref[0, j * BLOCK:(j + 1) * BLOCK, :].astype(BF16)
            kk = kstrip[j * BLOCK:(j + 2) * BLOCK]
            vv = vstrip[j * BLOCK:(j + 2) * BLOCK]
            s = lax.dot_general(qj, kk, (((1,), (1,)), ((), ())), preferred_element_type=F32) * SCALE
            mask = _swa_masks(i * nsub + j, n_pad)
            s = jnp.where(mask, s + bias_h, NEG_INF)
            m = jnp.maximum(jnp.max(s, axis=1, keepdims=True), sink)
            p = jnp.exp(s - m)
            denom = jnp.sum(p, axis=1, keepdims=True) + jnp.exp(sink - m)
            p = (p / denom).astype(BF16)
            o_ref[0, j * BLOCK:(j + 1) * BLOCK, :] = jnp.dot(p, vv, preferred_element_type=F32)
            lse_ref[0, j * BLOCK:(j + 1) * BLOCK, :] = m + jnp.log(denom)

    prev = lambda h, i: (h // SWA_GROUP, jnp.maximum(i * nsub - 1, 0), 0)
    cur = lambda h, i: (h // SWA_GROUP, i, 0)
    return pl.pallas_call(
        body, name="swa_fwd", grid=(H, nq),
        in_specs=[pl.BlockSpec((1, tq, dh), lambda h, i: (h, i, 0)),
                  pl.BlockSpec((1, BLOCK, dh), prev), pl.BlockSpec((1, tq, dh), cur),
                  pl.BlockSpec((1, BLOCK, dh), prev), pl.BlockSpec((1, tq, dh), cur),
                  pl.BlockSpec((1, BLOCK, 2 * BLOCK), lambda h, i: (h, 0, 0)),
                  pl.BlockSpec((1, 1, LANE), lambda h, i: (h, 0, 0))],
        out_specs=[pl.BlockSpec((1, tq, dh), lambda h, i: (h, i, 0)),
                   pl.BlockSpec((1, tq, 1), lambda h, i: (h, i, 0))],
        out_shape=[jax.ShapeDtypeStruct((H, Tp, dh), F32), jax.ShapeDtypeStruct((H, Tp, 1), F32)],
        compiler_params=_params(("parallel", "arbitrary")))(q, k, k, v, v, bias, sinks)


def _swa_bwd(q, k, v, bias, sinks, o, lse, do, n_pad, host=()):
    H, Tp, dh = q.shape
    tq = _tile(Tp, 384)
    nsub = tq // BLOCK
    nq = Tp // tq
    strip = BLOCK + tq

    def body(q_ref, kp_ref, kc_ref, vp_ref, vc_ref, b_ref, s_ref, o_ref, lse_ref, do_ref,
             dq_ref, dk_ref, dv_ref, db_ref, ds_ref):
        i = pl.program_id(1)

        @pl.when(i == 0)
        def _():
            db_ref[...] = jnp.zeros_like(db_ref)
            ds_ref[...] = jnp.zeros_like(ds_ref)

        dk_ref[...] = jnp.zeros_like(dk_ref)
        dv_ref[...] = jnp.zeros_like(dv_ref)
        kstrip = jnp.concatenate([kp_ref[0], kc_ref[0]], axis=0).astype(BF16)
        vstrip = jnp.concatenate([vp_ref[0], vc_ref[0]], axis=0).astype(BF16)
        sink = s_ref[0, :, 0:1]
        bias_h = b_ref[0]
        for j in range(nsub):
            rows = slice(j * BLOCK, (j + 1) * BLOCK)
            keys = slice(j * BLOCK, (j + 2) * BLOCK)
            qj = q_ref[0, rows, :].astype(BF16)
            doj = do_ref[0, rows, :]
            lsej = lse_ref[0, rows, :]
            delta = jnp.sum(doj * o_ref[0, rows, :], axis=1, keepdims=True)
            doj = doj.astype(BF16)
            kk, vv = kstrip[keys], vstrip[keys]
            s = lax.dot_general(qj, kk, (((1,), (1,)), ((), ())), preferred_element_type=F32) * SCALE
            mask = _swa_masks(i * nsub + j, n_pad)
            s = jnp.where(mask, s + bias_h, NEG_INF)
            p = jnp.exp(s - lsej)
            p_sink = jnp.exp(sink - lsej)
            dp = lax.dot_general(doj, vv, (((1,), (1,)), ((), ())), preferred_element_type=F32)
            ds = p * (dp - delta)
            db_ref[0] += ds
            ds_ref[0] += jnp.broadcast_to(-jnp.sum(p_sink * delta, axis=0, keepdims=True), (1, LANE))
            dsb = ds.astype(BF16)
            dq_ref[0, rows, :] = jnp.dot(dsb, kk, preferred_element_type=F32) * SCALE
            dk_ref[0, 0, keys, :] += jnp.dot(ds.T.astype(BF16), qj, preferred_element_type=F32) * SCALE
            dv_ref[0, 0, keys, :] += jnp.dot(p.T.astype(BF16), doj, preferred_element_type=F32)

    prev = lambda h, i: (h // SWA_GROUP, jnp.maximum(i * nsub - 1, 0), 0)
    cur = lambda h, i: (h // SWA_GROUP, i, 0)
    qspec = pl.BlockSpec((1, tq, dh), lambda h, i: (h, i, 0))
    cspec = pl.BlockSpec((1, tq, 1), lambda h, i: (h, i, 0))
    sspec = pl.BlockSpec((1, 1, strip, dh), lambda h, i: (h, i, 0, 0))
    return _pcall(
        body, "swa_bwd", (H, nq),
        [qspec, pl.BlockSpec((1, BLOCK, dh), prev), pl.BlockSpec((1, tq, dh), cur),
         pl.BlockSpec((1, BLOCK, dh), prev), pl.BlockSpec((1, tq, dh), cur),
         pl.BlockSpec((1, BLOCK, 2 * BLOCK), lambda h, i: (h, 0, 0)),
         pl.BlockSpec((1, 1, LANE), lambda h, i: (h, 0, 0)), qspec, cspec, qspec],
        [qspec, sspec, sspec, pl.BlockSpec((1, BLOCK, 2 * BLOCK), lambda h, i: (h, 0, 0)),
         pl.BlockSpec((1, 1, LANE), lambda h, i: (h, 0, 0))],
        [jax.ShapeDtypeStruct((H, Tp, dh), F32), jax.ShapeDtypeStruct((H, nq, strip, dh), F32),
         jax.ShapeDtypeStruct((H, nq, strip, dh), F32),
         jax.ShapeDtypeStruct((H, BLOCK, 2 * BLOCK), F32), jax.ShapeDtypeStruct((H, 1, LANE), F32)],
        [], ("parallel", "arbitrary"), (q, k, k, v, v, bias, sinks, o, lse, do), host)


def _fox_scores(qb, kk, cq, ck, diag, tq):
    s = lax.dot_general(qb, kk, (((1,), (1,)), ((), ())), preferred_element_type=F32)
    s = s + cq - ck
    return _fox_causal(s, tq) if diag else s


def _fox_causal(s, tq):
    r = lax.broadcasted_iota(jnp.int32, (tq, tq), 0)
    c = lax.broadcasted_iota(jnp.int32, (tq, tq), 1)
    return jnp.where(c <= r, s, NEG_INF)


_FOX_PAIRS = FOX_HEADS // 2
_QF_COL, _KF_COL, _VF_COL = (_WORK_OFF[n][0] // LANE for n in ("qf", "kf", "vf"))


def _low_lanes():
    return lax.broadcasted_iota(jnp.int32, (1, LANE), 1) < HEAD_DIM


def _split_pair(x, low):
    return [jnp.where(low, x, 0.0).astype(BF16), jnp.where(low, 0.0, x).astype(BF16)]


def _fox_fwd(proj, ccol, crow, tq, host=()):
    Tp = proj.shape[0]
    nq = Tp // tq

    def body(q_ref, k_ref, v_ref, cc_ref, cr_ref, o_ref, lse_ref):
        qi = pl.program_id(1)
        low = _low_lanes()
        qh = _split_pair(q_ref[...] * SCALE, low)
        cq = [cc_ref[0], cc_ref[1]]

        def scores(kb):
            kk = k_ref[pl.ds(pl.multiple_of(kb * tq, tq), tq), :].astype(BF16)
            return [_fox_scores(qh[e], kk, cq[e], cr_ref[e, kb], False, tq) for e in range(2)]

        def softmax_pv(kb, s, carry, diag):
            vv = v_ref[pl.ds(pl.multiple_of(kb * tq, tq), tq), :].astype(BF16)
            stats, upd = [], []
            for e in range(2):
                m, l = carry[2 * e], carry[2 * e + 1]
                se = _fox_causal(s[e], tq) if diag else s[e]
                m_new = jnp.maximum(m, jnp.max(se, axis=1, keepdims=True))
                alpha = jnp.exp(m - m_new)
                p = jnp.exp(se - m_new)
                stats += [m_new, alpha * l + jnp.sum(p, axis=1, keepdims=True)]
                upd.append(alpha * carry[4] + jnp.dot(p.astype(BF16), vv, preferred_element_type=F32))
            return (*stats, jnp.where(low, upd[0], upd[1]))

        def step(kb, c):
            s_next = scores(kb + 1)
            return (*softmax_pv(kb, c[5:], c[:5], False), *s_next)

        col = lambda val: jnp.full((tq, 1), val, F32)
        init = (col(NEG_INF), col(0.0), col(NEG_INF), col(0.0), jnp.zeros((tq, LANE), F32), *scores(0))
        c = lax.fori_loop(0, qi, step, init)
        m0, l0, m1, l1, acc = softmax_pv(qi, c[5:], c[:5], True)
        o_ref[...] = jnp.where(low, acc / l0, acc / l1)
        lse_ref[0] = m0 + jnp.log(l0)
        lse_ref[1] = m1 + jnp.log(l1)

    cspec = pl.BlockSpec((2, tq, 1), lambda h, i: (h, i, 0))
    return _pcall(
        body, "fox_fwd", (_FOX_PAIRS, nq),
        [pl.BlockSpec((tq, LANE), lambda h, i: (i, _QF_COL + h)), pl.BlockSpec((Tp, LANE), lambda h, i: (0, _KF_COL + h)),
         pl.BlockSpec((Tp, LANE), lambda h, i: (0, _VF_COL + h)), cspec,
         pl.BlockSpec((2, nq, 1, tq), lambda h, i: (h, 0, 0, 0))],
        [pl.BlockSpec((tq, LANE), lambda h, i: (i, h)), cspec],
        [jax.ShapeDtypeStruct((Tp, FOX_HEADS * HEAD_DIM), F32), jax.ShapeDtypeStruct((FOX_HEADS, Tp, 1), F32)],
        [], ("parallel", "arbitrary"), (proj, proj, proj, ccol, crow), host)


def _fox_bwd_dq(proj, ccol, crow, o, lse, do, tq, host=()):
    Tp = proj.shape[0]
    nq = Tp // tq

    def body(q_ref, k_ref, v_ref, cc_ref, cr_ref, o_ref, lse_ref, do_ref, dq_ref, dl_ref, dc_ref):
        qi = pl.program_id(1)
        low = _low_lanes()
        qh = _split_pair(q_ref[...] * SCALE, low)
        dof = do_ref[...]
        prod = dof * o_ref[...]
        delta = [jnp.sum(jnp.where(low, prod, 0.0), axis=1, keepdims=True),
                 jnp.sum(jnp.where(low, 0.0, prod), axis=1, keepdims=True)]
        doh = _split_pair(dof, low)
        cq = [cc_ref[0], cc_ref[1]]
        lse = [lse_ref[0], lse_ref[1]]

        def scores(kb):
            kk = k_ref[pl.ds(pl.multiple_of(kb * tq, tq), tq), :].astype(BF16)
            return [_fox_scores(qh[e], kk, cq[e], cr_ref[e, kb], False, tq) for e in range(2)]

        def grads(kb, s, carry, diag):
            rows = pl.ds(pl.multiple_of(kb * tq, tq), tq)
            kk, vv = k_ref[rows, :].astype(BF16), v_ref[rows, :].astype(BF16)
            dqs, dcs = [], []
            for e in range(2):
                se = _fox_causal(s[e], tq) if diag else s[e]
                p = jnp.exp(se - lse[e])
                dp = lax.dot_general(doh[e], vv, (((1,), (1,)), ((), ())), preferred_element_type=F32)
                ds = p * (dp - delta[e])
                dqs.append(jnp.dot(ds.astype(BF16), kk, preferred_element_type=F32))
                dcs.append(carry[1 + e] + jnp.sum(ds, axis=1, keepdims=True))
            return (carry[0] + jnp.where(low, dqs[0], dqs[1]), *dcs)

        def step(kb, c):
            s_next = scores(kb + 1)
            return (*grads(kb, c[3:], c[:3], False), *s_next)

        init = (jnp.zeros((tq, LANE), F32), jnp.zeros((tq, 1), F32), jnp.zeros((tq, 1), F32), *scores(0))
        c = lax.fori_loop(0, qi, step, init)
        dq, dc0, dc1 = grads(qi, c[3:], c[:3], True)
        dq_ref[...] = dq * SCALE
        dl_ref[0], dl_ref[1] = delta
        dc_ref[0], dc_ref[1] = dc0, dc1

    cspec = pl.BlockSpec((2, tq, 1), lambda h, i: (h, i, 0))
    pspec = pl.BlockSpec((tq, LANE), lambda h, i: (i, h))
    stat = jax.ShapeDtypeStruct((FOX_HEADS, Tp, 1), F32)
    return _pcall(
        body, "fox_bwd_dq", (_FOX_PAIRS, nq),
        [pl.BlockSpec((tq, LANE), lambda h, i: (i, _QF_COL + h)), pl.BlockSpec((Tp, LANE), lambda h, i: (0, _KF_COL + h)),
         pl.BlockSpec((Tp, LANE), lambda h, i: (0, _VF_COL + h)), cspec,
         pl.BlockSpec((2, nq, 1, tq), lambda h, i: (h, 0, 0, 0)), pspec, cspec, pspec],
        [pspec, cspec, cspec], [jax.ShapeDtypeStruct((Tp, FOX_HEADS * HEAD_DIM), F32), stat, stat],
        [], ("parallel", "arbitrary"), (proj, proj, proj, ccol, crow, o, lse, do), host)


def _fox_bwd_dkv(proj, ccol, crow, lse, delta, do, tq, host=()):
    Tp = proj.shape[0]
    nq = Tp // tq
    pair = lambda z: z.reshape(_FOX_PAIRS, 2, Tp).transpose(0, 2, 1)
    stats = jnp.concatenate([pair(ccol), pair(lse), pair(delta)], axis=-1).reshape(_FOX_PAIRS, nq, tq, 6)

    def body(q_ref, do_ref, st_ref, k_ref, v_ref, cr_ref, dk_ref, dv_ref, dc_ref):
        ki = pl.program_id(1)
        low = _low_lanes()
        kk, vv = k_ref[...].astype(BF16), v_ref[...].astype(BF16)

        def step(qb_i, carry, diag):
            rows = pl.ds(pl.multiple_of(qb_i * tq, tq), tq)
            qh = _split_pair(q_ref[rows, :] * SCALE, low)
            doh = _split_pair(do_ref[rows, :], low)
            st = st_ref[0, qb_i]
            dk, dv = carry[0], carry[1]
            dcs = []
            for e in range(2):
                s = _fox_scores(qh[e], kk, st[:, e:e + 1], cr_ref[e, 0], diag, tq)
                p = jnp.exp(s - st[:, 2 + e:3 + e])
                dp = lax.dot_general(doh[e], vv, (((1,), (1,)), ((), ())), preferred_element_type=F32)
                ds = p * (dp - st[:, 4 + e:5 + e])
                dv = dv + jnp.dot(p.T.astype(BF16), doh[e], preferred_element_type=F32)
                dk = dk + jnp.dot(ds.T.astype(BF16), qh[e], preferred_element_type=F32)
                dcs.append(carry[2 + e] - jnp.sum(ds, axis=0, keepdims=True))
            return (dk, dv, *dcs)

        init = (jnp.zeros((tq, LANE), F32), jnp.zeros((tq, LANE), F32), jnp.zeros((1, tq), F32), jnp.zeros((1, tq), F32))
        dk, dv, dc0, dc1 = lax.fori_loop(ki + 1, nq, functools.partial(step, diag=False), step(ki, init, True))
        dk_ref[...] = dk
        dv_ref[...] = dv
        dc_ref[0, 0] = dc0
        dc_ref[1, 0] = dc1

    rspec = pl.BlockSpec((2, 1, 1, tq), lambda h, i: (h, i, 0, 0))
    pspec = pl.BlockSpec((tq, LANE), lambda h, i: (i, h))
    wide = jax.ShapeDtypeStruct((Tp, FOX_HEADS * HEAD_DIM), F32)
    return _pcall(
        body, "fox_bwd_dkv", (_FOX_PAIRS, nq),
        [pl.BlockSpec((Tp, LANE), lambda h, i: (0, _QF_COL + h)), pl.BlockSpec((Tp, LANE), lambda h, i: (0, h)),
         pl.BlockSpec((1, nq, tq, 6), lambda h, i: (h, 0, 0, 0)),
         pl.BlockSpec((tq, LANE), lambda h, i: (i, _KF_COL + h)), pl.BlockSpec((tq, LANE), lambda h, i: (i, _VF_COL + h)),
         rspec],
        [pspec, pspec, rspec], [wide, wide, jax.ShapeDtypeStruct((FOX_HEADS, nq, 1, tq), F32)],
        [], ("parallel", "arbitrary"), (proj, do, stats, proj, proj, crow), host)


def _adam(gparts, w, m, v, name):
    P, R, C = gparts.shape
    tr = _tile(R, 256, 8)

    def body(g_ref, w_ref, m_ref, v_ref, go_ref, d_ref, mo_ref, vo_ref):
        g = g_ref[0]
        for p in range(1, P):
            g = g + g_ref[p]
        m2 = ADAM_B1 * m_ref[...] + (1.0 - ADAM_B1) * g
        v2 = ADAM_B2 * v_ref[...] + (1.0 - ADAM_B2) * (g * g)
        m_hat = m2 / (1.0 - ADAM_B1 ** ADAM_STEP)
        v_hat = v2 / (1.0 - ADAM_B2 ** ADAM_STEP)
        go_ref[...] = g
        d_ref[...] = -ADAM_LR * (m_hat / (jnp.sqrt(v_hat) + ADAM_EPS) + ADAM_WD * w_ref[...])
        mo_ref[...] = m2
        vo_ref[...] = v2

    spec = pl.BlockSpec((tr, C), lambda i: (i, 0))
    shp = jax.ShapeDtypeStruct((R, C), F32)
    return pl.pallas_call(
        body, name=name, grid=(R // tr,),
        in_specs=[pl.BlockSpec((P, tr, C), lambda i: (0, i, 0)), spec, spec, spec],
        out_specs=[spec] * 4, out_shape=[shp] * 4, compiler_params=_params(("parallel",)))(gparts, w, m, v)


def _sum_parts(gparts, name):
    P, R, C = gparts.shape
    tr = _tile(R, 256, 8)

    def body(g_ref, o_ref):
        g = g_ref[0]
        for p in range(1, P):
            g = g + g_ref[p]
        o_ref[...] = g

    return pl.pallas_call(
        body, name=name, grid=(R // tr,), in_specs=[pl.BlockSpec((P, tr, C), lambda i: (0, i, 0))],
        out_specs=pl.BlockSpec((tr, C), lambda i: (i, 0)), out_shape=jax.ShapeDtypeStruct((R, C), F32),
        compiler_params=_params(("parallel",)))(gparts)


def _pack(pieces, width, row_mult):
    flat = jnp.concatenate([p.reshape(-1) for p in pieces])
    n = flat.shape[0]
    rows = -(-n // width)
    rows = -(-rows // row_mult) * row_mult
    return jnp.pad(flat, (0, rows * width - n)).reshape(rows, width)


def _unpack(flat2d, shapes, lead=None):
    out, off = [], 0
    if lead is None:
        flat = flat2d.reshape(-1)
        for s in shapes:
            n = int(np.prod(s))
            out.append(flat[off:off + n].reshape(s))
            off += n
    else:
        flat = flat2d.reshape(lead, -1)
        for s in shapes:
            n = int(np.prod(s))
            out.append(flat[:, off:off + n].reshape((lead,) + tuple(s)))
            off += n
    return out


def _col_pieces(off, width, shard_w, fetch):
    out, c, end = [], off, off + width
    while c < end:
        d = c // shard_w
        hi = min(end, (d + 1) * shard_w)
        out.append(fetch(d, c - d * shard_w, hi - d * shard_w))
        c = hi
    return out


def _work_cols_of_orig(lo, hi, fetch):
    out = []
    for name, ow in _ORIG:
        o = _ORIG_OFF[name][0]
        a, b = max(lo, o), min(hi, o + ow)
        if a < b:
            w0 = _WORK_OFF[name][0]
            out.append(fetch(w0 + a - o, w0 + b - o))
    return out


def _heads(x, n):
    return x.reshape(x.shape[0], n, HEAD_DIM).transpose(1, 0, 2)


def _unheads(x):
    return x.transpose(1, 0, 2).reshape(x.shape[1], x.shape[0] * HEAD_DIM)


def _t5_bucket_np(dist):
    max_exact = REL_BUCKETS // 2
    d = np.maximum(dist, 0)
    scaled = (np.log(np.maximum(d, 1).astype(np.float32) / np.float32(max_exact))
              / np.float32(math.log(REL_MAX_DIST / max_exact))).astype(np.float32)
    large = np.minimum(max_exact + (scaled * np.float32(REL_BUCKETS - max_exact)).astype(np.int32), REL_BUCKETS - 1)
    return np.where(d < max_exact, d, large)


def _bucket_onehot():
    q_idx = np.arange(BLOCK)[:, None]
    k_idx = np.arange(2 * BLOCK)[None, :]
    bucket = _t5_bucket_np(q_idx + BLOCK - k_idx).reshape(-1)
    oh = np.zeros((LANE, BLOCK * 2 * BLOCK), np.float32)
    oh[bucket, np.arange(bucket.shape[0])] = 1.0
    return oh


def kernel(x, meta_tokens, rel_bias_table, norm_mix, w_in, swa_sinks, fox_forget_bias, conv_w, conv_b, lru_w_r, lru_b_r, lru_w_i, lru_b_i, lru_lambda, w_branch, w_out, norm_ffn, w_ffn_in, w_ffn_out, norm_final, loss_target, m_meta_tokens, m_rel_bias_table, m_norm_mix, m_w_in, m_swa_sinks, m_fox_forget_bias, m_conv_w, m_conv_b, m_lru_w_r, m_lru_b_r, m_lru_w_i, m_lru_b_i, m_lru_lambda, m_w_branch, m_w_out, m_norm_ffn, m_w_ffn_in, m_w_ffn_out, m_norm_final, v_meta_tokens, v_rel_bias_table, v_norm_mix, v_w_in, v_swa_sinks, v_fox_forget_bias, v_conv_w, v_conv_b, v_lru_w_r, v_lru_b_r, v_lru_w_i, v_lru_b_i, v_lru_lambda, v_w_branch, v_w_out, v_norm_ffn, v_w_ffn_in, v_w_ffn_out, v_norm_final):
    S = x.shape[1]
    T = N_META + S
    n_pad = (-T) % BLOCK
    Tp = T + n_pad
    first = n_pad + N_META
    TR = _tile(Tp, 384)
    TQ = _tile(Tp, 384)
    NQ = Tp // TQ
    me = 4 * lax.axis_index("x") + 2 * lax.axis_index("y") + lax.axis_index("c")

    big_names = ("w_in", "w_ffn_in", "w_ffn_out", "w_branch", "w_out")
    big_w = dict(w_in=w_in, w_ffn_in=w_ffn_in, w_ffn_out=w_ffn_out, w_branch=w_branch, w_out=w_out)
    big_m = dict(w_in=m_w_in, w_ffn_in=m_w_ffn_in, w_ffn_out=m_w_ffn_out, w_branch=m_w_branch, w_out=m_w_out)
    big_v = dict(w_in=v_w_in, w_ffn_in=v_w_ffn_in, w_ffn_out=v_w_ffn_out, w_branch=v_w_branch, w_out=v_w_out)
    rows2d = lambda w: w.reshape(-1, w.shape[-1])
    in_shard_w = IN_COLS // N_DEV
    cat_axis = dict(w_in=-1, w_ffn_in=-1, w_ffn_out=0, w_branch=-1, w_out=0)

    def w_send(n, l):
        return rows2d(big_w[n][l]).astype(BF16)

    def w_build(n, g):
        shard = lambda d: g[d].reshape(big_w[n].shape[1:])
        if n != "w_in":
            return jnp.concatenate([shard(d) for d in range(N_DEV)], axis=cat_axis[n])
        parts = []
        for name, width in _WORK:
            off, ow = _ORIG_OFF[name]
            parts += _col_pieces(off, ow, in_shard_w, lambda d, lo, hi: shard(d)[:, lo:hi])
            if width > ow:
                parts.append(jnp.zeros((D_MODEL, width - ow), BF16))
        return jnp.concatenate(parts, axis=-1)

    W_in, W_ffn_in, W_ffn_out, W_branch, W_out = ([None] * DEPTH for _ in range(5))
    W_in[0] = w_build("w_in", _exchange(w_send("w_in", 0), True, "gather_w_in"))

    small_sh = _exchange(_pack([meta_tokens, conv_w], D_MODEL, 8), True, "gather_small")
    g_meta, g_conv_w = _unpack(small_sh, [meta_tokens.shape, conv_w.shape], lead=N_DEV)
    meta_full = g_meta.transpose(1, 0, 2).reshape(N_META, D_MODEL)
    conv_w_full = g_conv_w.transpose(1, 2, 0, 3).reshape(DEPTH, CONV_WIDTH, LRU_WIDTH)

    onehot = jnp.asarray(_bucket_onehot())
    table_t = jnp.pad(rel_bias_table.T, ((0, 0), (0, LANE - REL_BUCKETS)))
    bias = _mm(table_t, onehot, "nn", "swa_bias", exact=True, tn_cap=4096).reshape(SWA_Q_HEADS, BLOCK, 2 * BLOCK)

    def dense_blocks(w):
        rows = []
        for b in range(LRU_BLOCKS):
            rows.append(jnp.pad(w[b], ((0, 0), (b * LRU_BLOCK_DIM, (LRU_BLOCKS - 1 - b) * LRU_BLOCK_DIM))))
        return jnp.concatenate(rows, axis=0)

    h = jnp.concatenate([jnp.zeros((n_pad, D_MODEL), F32), meta_full, x[0]], axis=0)
    saved = []

    for l in range(DEPTH):
        sv = {"h": h}
        g_mix = norm_mix[l][None, :]

        def norm_fwd(i, rows, consts):
            xx = rows[0]
            return [xx * lax.rsqrt(jnp.mean(xx * xx, axis=1, keepdims=True) + EPS) * consts[0]], []

        (u,) = _rowwise(norm_fwd, "norm_fwd", Tp, TR, [(h, None, 0)], [g_mix], [(D_MODEL, BF16)])
        proj, got = _mm(u, W_in[l], "nn", "proj", tn_cap=384,
                        host=[(w_send("w_branch", l), True), (w_send("w_out", l), True)])
        W_branch[l], W_out[l] = w_build("w_branch", got[0]), w_build("w_out", got[1])
        sv["u"], sv["proj"] = u, proj

        def col(name):
            off, w = _WORK_OFF[name]
            return (proj, w, off // w)

        def cols(name):
            off, w = _WORK_OFF[name]
            return proj[:, off:off + w]

        qa, ka, va = _heads(cols("qa"), 8), _heads(cols("ka"), 2), _heads(cols("va"), 2)
        sinks_l = jnp.broadcast_to(swa_sinks[l][:, None, None], (SWA_Q_HEADS, 1, LANE))
        o_a_h, lse_a = _swa_fwd(qa, ka, va, bias, sinks_l, n_pad)
        o_a = _unheads(o_a_h)
        sv.update(qa=qa, ka=ka, va=va, sinks=sinks_l, o_a_h=o_a_h, lse_a=lse_a, o_a=o_a)

        fbias = jnp.pad(fox_forget_bias[l], (0, LANE - FOX_HEADS))[None, :]

        def logf_fwd(i, rows, consts):
            return [_log_sigmoid(rows[0] + consts[0])], []

        (logf,) = _rowwise(logf_fwd, "logf_fwd", Tp, TR, [col("fl")], [fbias], [(LANE, F32)])
        cum = _scan(None, logf, "cumsum")
        cum_h = cum[:, :FOX_HEADS].T
        ccol = cum_h[:, :, None]
        crow = jnp.where(jnp.arange(Tp)[None, :] < n_pad, -NEG_INF, cum_h).reshape(FOX_HEADS, NQ, 1, TQ)
        ride = [("w_ffn_in", l), ("w_ffn_out", l)] + ([("w_in", l + 1)] if l + 1 < DEPTH else [])
        (o_f, lse_f), got = _fox_fwd(proj, ccol, crow, TQ, host=[(w_send(n, ll), True) for n, ll in ride])
        W_ffn_in[l], W_ffn_out[l] = w_build("w_ffn_in", got[0]), w_build("w_ffn_out", got[1])
        if l + 1 < DEPTH:
            W_in[l + 1] = w_build("w_in", got[2])
        sv.update(fbias=fbias, ccol=ccol, crow=crow, lse_f=lse_f, o_f=o_f)

        xc = cols("xc")
        xs = [_shift_down(xc, CONV_WIDTH - 1 - i) for i in range(CONV_WIDTH - 1)] + [xc]
        cw = conv_w_full[l]
        cb = conv_b[l][None, :]

        def conv_fwd(i, rows, consts):
            w, b = consts
            acc = rows[0] * w[0:1, :]
            for t in range(1, CONV_WIDTH):
                acc = acc + rows[t] * w[t:t + 1, :]
            return [acc + b], []

        (xconv,) = _rowwise(conv_fwd, "conv_fwd", Tp, TR, [(a, None, 0) for a in xs], [cw, cb], [(LRU_WIDTH, F32)])
        w_ri = jnp.concatenate([dense_blocks(lru_w_r[l]), dense_blocks(lru_w_i[l])], axis=1)
        pre = _mm(xconv, w_ri, "nn", "lru_gates")
        lru_consts = [lru_b_r[l][None, :], lru_b_i[l][None, :], lru_lambda[l][None, :]]

        def lru_fwd(i, rows, consts):
            pr, xv = rows
            b_r, b_i, lam = consts
            r = _sigmoid(pr[:, :LRU_WIDTH] + b_r)
            gi = _sigmoid(pr[:, LRU_WIDTH:] + b_i)
            log_a = LRU_C * r * _log_sigmoid(lam)
            valid = (_row_ids(i, TR) >= n_pad).astype(F32)
            inp = jnp.sqrt(_neg_expm1(2.0 * log_a)) * (gi * xv) * valid
            return [jnp.exp(log_a), inp], []

        a_dec, inp = _rowwise(lru_fwd, "lru_fwd", Tp, TR, [(pre, None, 0), (xconv, None, 0)], lru_consts,
                              [(LRU_WIDTH, F32), (LRU_WIDTH, F32)])
        hs = _scan(a_dec, inp, "lru_scan")

        def oc_fwd(i, rows, consts):
            return [rows[0] * _gelu(rows[1])], []

        (o_c,) = _rowwise(oc_fwd, "oc_fwd", Tp, TR, [(hs, None, 0), col("yc")], [], [(LRU_WIDTH, BF16)])
        sv.update(xs=xs, cw=cw, xconv=xconv, w_ri=w_ri, pre=pre, lru_consts=lru_consts, a_dec=a_dec, hs=hs, o_c=o_c)

        ba = _mm(o_a, W_branch[l][0], "nn", "branch")
        bf = _mm(o_f, W_branch[l][1], "nn", "branch")
        bc = _mm(o_c, W_branch[l][2], "nn", "branch")

        def merge_fwd(i, rows, consts):
            g, b0, b1, b2 = rows
            valid = (_row_ids(i, BLOCK) >= n_pad).astype(F32)
            mg = (_sigmoid(g[:, :D_MODEL]) * b0 + _sigmoid(g[:, D_MODEL:2 * D_MODEL]) * b1
                  + _sigmoid(g[:, 2 * D_MODEL:]) * b2)
            return [mg * valid], []

        (merged,) = _rowwise(merge_fwd, "merge_fwd", Tp, BLOCK,
                             [col("gates"), (ba, None, 0), (bf, None, 0), (bc, None, 0)], [], [(D_MODEL, BF16)])
        h1 = _mm(merged, W_out[l], "nn", "out_proj", res=h)
        sv.update(ba=ba, bf=bf, bc=bc, merged=merged, h1=h1)

        g_ffn = norm_ffn[l][None, :]
        (u2,) = _rowwise(norm_fwd, "norm_fwd", Tp, TR, [(h1, None, 0)], [g_ffn], [(D_MODEL, BF16)])
        ff = _mm(u2, W_ffn_in[l], "nn", "ffn_in")

        def act_fwd(i, rows, consts):
            gate, up = rows
            return [gate * _sigmoid(gate) * up], []

        (act,) = _rowwise(act_fwd, "act_fwd", Tp, TR, [(ff, D_FF, 0), (ff, D_FF, 1)], [], [(D_FF, BF16)])
        h = _mm(act, W_ffn_out[l], "nn", "ffn_out", res=h1)
        sv.update(u2=u2, ff=ff, act=act)
        saved.append(sv)

    tgt = jnp.concatenate([jnp.zeros((first, D_MODEL), F32), loss_target[0]], axis=0)
    g_fin = norm_final[None, :]

    def head(i, rows, consts):
        xx, tg = rows
        g = consts[0]
        valid = (_row_ids(i, TR) >= first).astype(F32)
        rstd = lax.rsqrt(jnp.mean(xx * xx, axis=1, keepdims=True) + EPS)
        xhat = xx * rstd
        err = (xhat * g - tg) * valid
        loss_rows = 0.5 * jnp.mean(err * err, axis=1, keepdims=True)
        dy = err * (1.0 / D_MODEL)
        dxhat = dy * g
        dx = rstd * (dxhat - xhat * jnp.mean(dxhat * xhat, axis=1, keepdims=True))
        return [dx], [jnp.broadcast_to(_colsum(loss_rows), (1, LANE)), _colsum(dy * xhat)]

    dh, loss_part, d_norm_final = _rowwise(head, "loss_head", Tp, TR, [(h, None, 0), (tgt, None, 0)], [g_fin],
                                           [(D_MODEL, F32)], [LANE, D_MODEL])
    loss = lax.psum(loss_part[0, 0], ("x", "y", "c"))

    def norm_bwd_call(xin, g, du, dres):
        def norm_bwd(i, rows, consts):
            xx, dd, rr = rows
            gg = consts[0]
            valid = (_row_ids(i, TR) >= n_pad).astype(F32)
            rstd = lax.rsqrt(jnp.mean(xx * xx, axis=1, keepdims=True) + EPS)
            xhat = xx * rstd
            dxhat = dd * gg
            dx = rstd * (dxhat - xhat * jnp.mean(dxhat * xhat, axis=1, keepdims=True))
            return [rr + dx * valid], [_colsum(dd * xhat)]

        return _rowwise(norm_bwd, "norm_bwd", Tp, TR, [(xin, None, 0), (du, None, 0), (dres, None, 0)], [g],
                        [(D_MODEL, F32)], [D_MODEL])

    grads = {k: [None] * DEPTH for k in ("norm_mix", "w_in", "swa_sinks", "fox_forget_bias", "conv_w", "conv_b",
                                         "lru_w_r", "lru_b_r", "lru_w_i", "lru_b_i", "lru_lambda", "w_branch",
                                         "w_out", "norm_ffn", "w_ffn_in", "w_ffn_out")}
    d_bias_total = None

    def g_send(n, g):
        if n == "w_in":
            pieces = [jnp.concatenate(_work_cols_of_orig(d * in_shard_w, (d + 1) * in_shard_w,
                                                         lambda lo, hi: g[:, lo:hi]), axis=-1) for d in range(N_DEV)]
        else:
            ax = cat_axis[n] % g.ndim
            w = big_w[n].shape[1 + ax]
            pieces = [lax.slice_in_dim(g, d * w, (d + 1) * w, axis=ax) for d in range(N_DEV)]
        return jnp.stack([rows2d(p) for p in pieces])

    recv = {n: [None] * DEPTH for n in big_names}
    for l in reversed(range(DEPTH)):
        sv = saved[l]
        proj = sv["proj"]

        def col(name):
            off, w = _WORK_OFF[name]
            return (proj, w, off // w)

        dh2 = dh
        d_act = _mm(dh2, W_ffn_out[l], "nt", "d_act")
        grads["w_ffn_out"][l] = _mm(sv["act"], dh2, "tn", "dw_ffn_out")

        def act_bwd(i, rows, consts):
            gate, up, da = rows
            sg = _sigmoid(gate)
            d_gate = da * up * (sg * (1.0 + gate * (1.0 - sg)))
            d_up = da * (gate * sg)
            return [jnp.concatenate([d_gate, d_up], axis=1)], []

        (dff,) = _rowwise(act_bwd, "act_bwd", Tp, BLOCK, [(sv["ff"], D_FF, 0), (sv["ff"], D_FF, 1), (d_act, None, 0)],
                          [], [(2 * D_FF, BF16)])
        grads["w_ffn_in"][l] = _mm(sv["u2"], dff, "tn", "dw_ffn_in")
        du2 = _mm(dff, W_ffn_in[l], "nt", "du2")
        dh1, dg = norm_bwd_call(sv["h1"], norm_ffn[l][None, :], du2, dh2)
        grads["norm_ffn"][l] = dg[0]

        dmerged = _mm(dh1, W_out[l], "nt", "d_merged")
        grads["w_out"][l] = _mm(sv["merged"], dh1, "tn", "dw_out")

        def merge_bwd(i, rows, consts):
            g, b0, b1, b2, dm = rows
            dm = dm * (_row_ids(i, BLOCK) >= n_pad).astype(F32)
            outs, dgs = [], []
            for k, bk in enumerate((b0, b1, b2)):
                sg = _sigmoid(g[:, k * D_MODEL:(k + 1) * D_MODEL])
                outs.append(dm * sg)
                dgs.append(dm * bk * sg * (1.0 - sg))
            return outs + [jnp.concatenate(dgs, axis=1)], []

        d_ba, d_bf, d_bc, d_gates = _rowwise(
            merge_bwd, "merge_bwd", Tp, BLOCK,
            [col("gates"), (sv["ba"], None, 0), (sv["bf"], None, 0), (sv["bc"], None, 0), (dmerged, None, 0)], [],
            [(D_MODEL, BF16)] * 3 + [(3 * D_MODEL, BF16)])
        grads["w_branch"][l] = jnp.stack([_mm(sv["o_a"], d_ba, "tn", "dw_branch"),
                                          _mm(sv["o_f"], d_bf, "tn", "dw_branch"),
                                          _mm(sv["o_c"], d_bc, "tn", "dw_branch")])
        do_a = _mm(d_ba, W_branch[l][0], "nt", "d_branch")
        do_f = _mm(d_bf, W_branch[l][1], "nt", "d_branch")
        do_c = _mm(d_bc, W_branch[l][2], "nt", "d_branch")

        def oc_bwd(i, rows, consts):
            d, hv, yv = rows
            return [d * _gelu(yv), d * hv * _gelu_grad(yv)], []

        d_hs, d_yc = _rowwise(oc_bwd, "oc_bwd", Tp, TR, [(do_c, None, 0), (sv["hs"], None, 0), col("yc")], [],
                              [(LRU_WIDTH, F32)] * 2)
        d_state = _rev_scan(sv["a_dec"], d_hs, "lru_scan")
        hs_prev = _shift_down(sv["hs"], 1)

        def lru_bwd(i, rows, consts):
            dH, hp, pr, xv = rows
            b_r, b_i, lam = consts
            valid = (_row_ids(i, TR) >= n_pad).astype(F32)
            r = _sigmoid(pr[:, :LRU_WIDTH] + b_r)
            gi = _sigmoid(pr[:, LRU_WIDTH:] + b_i)
            lsl = _log_sigmoid(lam)
            log_a = LRU_C * r * lsl
            a = jnp.exp(log_a)
            one_m_e = _neg_expm1(2.0 * log_a)
            mult = jnp.sqrt(one_m_e)
            d_inp = dH * valid
            d_mult = d_inp * gi * xv
            d_gi = d_inp * mult * xv
            d_x = d_inp * mult * gi
            d_log_a = dH * hp * a - d_mult * (1.0 - one_m_e) / mult
            d_pre_r = d_log_a * (LRU_C * lsl) * r * (1.0 - r)
            d_pre_i = d_gi * gi * (1.0 - gi)
            d_lam = _colsum(d_log_a * (LRU_C * r)) * _sigmoid(-lam)
            return [jnp.concatenate([d_pre_r, d_pre_i], axis=1), d_x], [_colsum(d_pre_r), _colsum(d_pre_i), d_lam]

        d_pre, d_xdir, d_b_r, d_b_i, d_lam = _rowwise(
            lru_bwd, "lru_bwd", Tp, TR, [(d_state, None, 0), (hs_prev, None, 0), (sv["pre"], None, 0),
                                         (sv["xconv"], None, 0)], sv["lru_consts"],
            [(2 * LRU_WIDTH, BF16), (LRU_WIDTH, F32)], [LRU_WIDTH] * 3)
        grads["lru_b_r"][l], grads["lru_b_i"][l], grads["lru_lambda"][l] = d_b_r[0], d_b_i[0], d_lam[0]
        d_w_ri = _mm(sv["xconv"], d_pre, "tn", "dw_lru_gates")

        def diag_blocks(wd):
            w4 = wd.reshape(LRU_BLOCKS, LRU_BLOCK_DIM, LRU_BLOCKS, LRU_BLOCK_DIM)
            return jnp.stack([w4[b, :, b, :] for b in range(LRU_BLOCKS)])

        grads["lru_w_r"][l] = diag_blocks(d_w_ri[:, :LRU_WIDTH])
        grads["lru_w_i"][l] = diag_blocks(d_w_ri[:, LRU_WIDTH:])
        d_conv = _mm(d_pre, sv["w_ri"], "nt", "d_xconv", res=d_xdir)
        dcs = [d_conv] + [_shift_up(d_conv, s) for s in range(1, CONV_WIDTH)]

        def conv_bwd(i, rows, consts):
            w = consts[0]
            d = rows[:CONV_WIDTH]
            xsh = rows[CONV_WIDTH:]
            dxc = d[0] * w[CONV_WIDTH - 1:CONV_WIDTH, :]
            for s in range(1, CONV_WIDTH):
                dxc = dxc + d[s] * w[CONV_WIDTH - 1 - s:CONV_WIDTH - s, :]
            return [dxc], [_colsum(xsh[t] * d[0]) for t in range(CONV_WIDTH)] + [_colsum(d[0])]

        res = _rowwise(conv_bwd, "conv_bwd", Tp, TR, [(a, None, 0) for a in dcs + sv["xs"]], [sv["cw"]],
                       [(LRU_WIDTH, F32)], [LRU_WIDTH] * (CONV_WIDTH + 1))
        d_xc = res[0]
        grads["conv_w"][l] = jnp.concatenate(res[1:1 + CONV_WIDTH], axis=0)
        grads["conv_b"][l] = res[1 + CONV_WIDTH][0]

        (dqf, delta_f, dccol), got = _fox_bwd_dq(
            proj, sv["ccol"], sv["crow"], sv["o_f"], sv["lse_f"], do_f, TQ,
            host=[(g_send("w_ffn_in", grads["w_ffn_in"][l]), False)])
        recv["w_ffn_in"][l] = got[0]
        ride = [("w_out", l), ("w_branch", l)] + ([("w_in", l + 1)] if l + 1 < DEPTH else [])
        (dkf, dvf, dcrow), got = _fox_bwd_dkv(
            proj, sv["ccol"], sv["crow"], sv["lse_f"], delta_f, do_f, TQ,
            host=[(g_send(n, grads[n][ll]), False) for n, ll in ride])
        for (n, ll), r in zip(ride, got):
            recv[n][ll] = r
        lanes = lambda z: jnp.pad(z.reshape(FOX_HEADS, Tp).T, ((0, 0), (0, LANE - FOX_HEADS)))
        dlogf = _rev_scan(None, lanes(dcrow), "cumsum_bwd", g2=lanes(dccol))

        def logf_bwd(i, rows, consts):
            dl, fl = rows
            valid = (_row_ids(i, TR) >= n_pad).astype(F32)
            lane_ok = (lax.broadcasted_iota(jnp.int32, (1, LANE), 1) < FOX_HEADS).astype(F32)
            dfl = dl * _sigmoid(-(fl + consts[0])) * valid * lane_ok
            return [dfl], [_colsum(dfl)]

        d_fl, d_fb = _rowwise(logf_bwd, "logf_bwd", Tp, TR, [(dlogf, None, 0), col("fl")], [sv["fbias"]],
                              [(LANE, F32)], [LANE])
        grads["fox_forget_bias"][l] = d_fb[0, :FOX_HEADS]

        (dqa, dk_strip, dv_strip, d_bias_l, d_sink), got = _swa_bwd(
            sv["qa"], sv["ka"], sv["va"], bias, sv["sinks"], sv["o_a_h"], sv["lse_a"], _heads(do_a, 8), n_pad,
            host=[(g_send("w_ffn_out", grads["w_ffn_out"][l]), False)])
        recv["w_ffn_out"][l] = got[0]
        grads["swa_sinks"][l] = d_sink[:, 0, 0]
        d_bias_total = d_bias_l if d_bias_total is None else d_bias_total + d_bias_l

        def kv_parts(strip):
            cur = strip[:, :, BLOCK:, :].reshape(SWA_Q_HEADS, Tp, HEAD_DIM)
            nxt = jnp.pad(strip[:, 1:, :BLOCK, :], ((0, 0), (0, 1), (TQ - BLOCK, 0), (0, 0)))
            nxt = nxt.reshape(SWA_Q_HEADS, Tp, HEAD_DIM)
            both = jnp.stack([cur, nxt]).reshape(2, SWA_KV_HEADS, SWA_GROUP, Tp, HEAD_DIM)
            return both.transpose(3, 0, 2, 1, 4).reshape(Tp, 2 * SWA_GROUP * KV_W)

        dk_parts, dv_parts = kv_parts(dk_strip), kv_parts(dv_strip)

        def assemble(i, rows, consts):
            dg, dqa_, dqf_, dkf_, dvf_, dxc_, dyc_, dkp, dvp, dfl_ = rows

            def fold(parts):
                acc = parts[:, :KV_W]
                for p in range(1, 2 * SWA_GROUP):
                    acc = acc + parts[:, p * KV_W:(p + 1) * KV_W]
                return acc

            parts = [dg, dqa_, dqf_, dkf_, dvf_, dxc_, dyc_, fold(dkp), fold(dvp), dfl_]
            return [jnp.concatenate([p.astype(BF16) for p in parts], axis=1)], []

        (dproj,) = _rowwise(
            assemble, "assemble_dproj", Tp, BLOCK,
            [(d_gates, None, 0), (_unheads(dqa), None, 0), (dqf, None, 0), (dkf, None, 0),
             (dvf, None, 0), (d_xc, None, 0), (d_yc, None, 0), (dk_parts, None, 0), (dv_parts, None, 0),
             (d_fl, None, 0)], [], [(WORK_COLS, BF16)])
        grads["w_in"][l] = _mm(sv["u"], dproj, "tn", "dw_in", tn_cap=384)
        if l == 0:
            du, got = _mm(dproj, W_in[l], "nt", "du", tk_cap=384,
                          host=[(g_send("w_in", grads["w_in"][0]), False)])
            recv["w_in"][0] = got[0]
        else:
            du = _mm(dproj, W_in[l], "nt", "du", tk_cap=384)
        dh, dg = norm_bwd_call(sv["h"], norm_mix[l][None, :], du, dh1)
        grads["norm_mix"][l] = dg[0]

    d_table = _mm(d_bias_total.reshape(SWA_Q_HEADS, BLOCK * 2 * BLOCK), onehot, "nt", "d_rel_table", exact=True,
                  tk_cap=4096)
    g_rel = d_table[:, :REL_BUCKETS].T
    g_meta_full = dh[n_pad:first]
    grad_x = dh[first:][None]

    stack = lambda k: jnp.stack(grads[k])
    big_res = {}
    for n in big_names:
        per_layer = [_adam(recv[n][l], rows2d(big_w[n][l]), rows2d(big_m[n][l]), rows2d(big_v[n][l]), "adam_" + n)
                     for l in range(DEPTH)]
        big_res[n] = [jnp.stack([per_layer[l][k] for l in range(DEPTH)]).reshape(big_w[n].shape) for k in range(4)]

    rep_names = ("rel_bias_table", "norm_mix", "swa_sinks", "fox_forget_bias", "conv_b", "lru_w_r", "lru_b_r",
                 "lru_w_i", "lru_b_i", "lru_lambda", "norm_ffn", "norm_final")
    rep_w = dict(rel_bias_table=rel_bias_table, norm_mix=norm_mix, swa_sinks=swa_sinks,
                 fox_forget_bias=fox_forget_bias, conv_b=conv_b, lru_w_r=lru_w_r, lru_b_r=lru_b_r, lru_w_i=lru_w_i,
                 lru_b_i=lru_b_i, lru_lambda=lru_lambda, norm_ffn=norm_ffn, norm_final=norm_final)
    rep_m = dict(rel_bias_table=m_rel_bias_table, norm_mix=m_norm_mix, swa_sinks=m_swa_sinks,
                 fox_forget_bias=m_fox_forget_bias, conv_b=m_conv_b, lru_w_r=m_lru_w_r, lru_b_r=m_lru_b_r,
                 lru_w_i=m_lru_w_i, lru_b_i=m_lru_b_i, lru_lambda=m_lru_lambda, norm_ffn=m_norm_ffn,
                 norm_final=m_norm_final)
    rep_v = dict(rel_bias_table=v_rel_bias_table, norm_mix=v_norm_mix, swa_sinks=v_swa_sinks,
                 fox_forget_bias=v_fox_forget_bias, conv_b=v_conv_b, lru_w_r=v_lru_w_r, lru_b_r=v_lru_b_r,
                 lru_w_i=v_lru_w_i, lru_b_i=v_lru_b_i, lru_lambda=v_lru_lambda, norm_ffn=v_norm_ffn,
                 norm_final=v_norm_final)
    rep_g = {n: (g_rel if n == "rel_bias_table" else d_norm_final[0] if n == "norm_final" else stack(n))
             for n in rep_names}
    small_g = [rep_g[n] for n in rep_names] + [g_meta_full, stack("conv_w")]
    small_shapes = [rep_w[n].shape for n in rep_names] + [(N_META, D_MODEL), (DEPTH, CONV_WIDTH, LRU_WIDTH)]
    gs_all = _exchange(_pack(small_g, D_MODEL, 8), True, "gather_small_grads")
    gs_sum = _unpack(_sum_parts(gs_all, "sum_small_grads"), small_shapes)
    gsum = dict(zip(rep_names, gs_sum[:len(rep_names)]))
    g_meta_sh = lax.dynamic_slice_in_dim(gs_sum[-2], me * (D_MODEL // N_DEV), D_MODEL // N_DEV, axis=1)
    g_convw_sh = lax.dynamic_slice_in_dim(gs_sum[-1], me * (LRU_WIDTH // N_DEV), LRU_WIDTH // N_DEV, axis=2)
    sm_names = rep_names + ("meta_tokens", "conv_w")
    sm_w = [rep_w[n] for n in rep_names] + [meta_tokens, conv_w]
    sm_m = [rep_m[n] for n in rep_names] + [m_meta_tokens, m_conv_w]
    sm_v = [rep_v[n] for n in rep_names] + [v_meta_tokens, v_conv_w]
    sm_g = [gsum[n] for n in rep_names] + [g_meta_sh, g_convw_sh]
    sm_shapes = [w.shape for w in sm_w]
    sm_out = [_unpack(o, sm_shapes) for o in _adam(_pack(sm_g, D_MODEL, 8)[None], _pack(sm_w, D_MODEL, 8),
                                                   _pack(sm_m, D_MODEL, 8), _pack(sm_v, D_MODEL, 8), "adam_small")]
    sm_res = {n: [sm_out[k][j] for k in range(4)] for j, n in enumerate(sm_names)}

    order = ("meta_tokens", "rel_bias_table", "norm_mix", "w_in", "swa_sinks", "fox_forget_bias", "conv_w", "conv_b",
             "lru_w_r", "lru_b_r", "lru_w_i", "lru_b_i", "lru_lambda", "w_branch", "w_out", "norm_ffn", "w_ffn_in",
             "w_ffn_out", "norm_final")
    allres = {**big_res, **sm_res}
    outs = [loss, grad_x]
    for k in range(4):
        outs += [allres[n][k] for n in order]
    return tuple(outs)
```

```python
import functools
import math

import numpy as np
import jax
import jax.numpy as jnp
from jax import lax
from jax.experimental import pallas as pl
from jax.experimental.pallas import tpu as pltpu

F32 = jnp.float32
BF16 = jnp.bfloat16

N_DEV = 8
D_MODEL = 1024
DEPTH = 4
HEAD_DIM = 64
N_META = 16
BLOCK = 128
NEG_INF = -1e30
SWA_Q_HEADS = 8
SWA_KV_HEADS = 2
SWA_GROUP = SWA_Q_HEADS // SWA_KV_HEADS
FOX_HEADS = 8
LRU_WIDTH = D_MODEL // 2
LRU_BLOCKS = 8
LRU_BLOCK_DIM = LRU_WIDTH // LRU_BLOCKS
CONV_WIDTH = 4
LRU_C = 8.0
REL_BUCKETS = 32
REL_MAX_DIST = 128
D_FF = 2816
N_BRANCH = 3
ATT_W = SWA_Q_HEADS * HEAD_DIM
KV_W = SWA_KV_HEADS * HEAD_DIM
SCALE = HEAD_DIM ** -0.5
EPS = 1e-6

_ORIG = (("qa", ATT_W), ("ka", KV_W), ("va", KV_W), ("qf", ATT_W), ("kf", ATT_W), ("vf", ATT_W),
         ("fl", FOX_HEADS), ("xc", LRU_WIDTH), ("yc", LRU_WIDTH), ("gates", N_BRANCH * D_MODEL))
IN_COLS = sum(w for _, w in _ORIG)
_WORK = (("gates", 3072), ("qa", 512), ("qf", 512), ("kf", 512), ("vf", 512), ("xc", 512), ("yc", 512),
         ("ka", 128), ("va", 128), ("fl", 128))
WORK_COLS = sum(w for _, w in _WORK)


def _offsets(table):
    off, out = 0, {}
    for n, w in table:
        out[n] = (off, w)
        off += w
    return out


_ORIG_OFF = _offsets(_ORIG)
_WORK_OFF = _offsets(_WORK)

ADAM_LR = 0.001
ADAM_B1 = 0.9
ADAM_B2 = 0.999
ADAM_EPS = 1e-08
ADAM_WD = 0.01
ADAM_STEP = 10

VMEM_LIMIT = 52 * 1024 * 1024
LANE = 128


def _tile(n, cap, mult=LANE):
    if n <= cap:
        return n
    best = None
    for d in range(mult, cap + 1, mult):
        if n % d == 0:
            best = d
    assert best is not None, (n, cap, mult)
    return best


def _params(sem):
    return pltpu.CompilerParams(dimension_semantics=sem, vmem_limit_bytes=VMEM_LIMIT)


def _sigmoid(x):
    return 1.0 / (1.0 + jnp.exp(-x))


def _log_sigmoid(x):
    return jnp.minimum(x, 0.0) - jnp.log(1.0 + jnp.exp(-jnp.abs(x)))


def _neg_expm1(x):
    series = -x * (1.0 + x * (0.5 + x * (1.0 / 6.0 + x * (1.0 / 24.0 + x * (1.0 / 120.0)))))
    return jnp.where(x > -0.1, series, 1.0 - jnp.exp(x))


_GELU_C = math.sqrt(2.0 / math.pi)


def _gelu(x):
    return 0.5 * x * (1.0 + jnp.tanh(_GELU_C * (x + 0.044715 * x * x * x)))


def _gelu_grad(x):
    t = jnp.tanh(_GELU_C * (x + 0.044715 * x * x * x))
    return 0.5 * (1.0 + t) + 0.5 * x * (1.0 - t * t) * _GELU_C * (1.0 + 3.0 * 0.044715 * x * x)


def _xch_ops(src_ref, out_ref, send_sems, recv_sems, local_sem, gather):
    x, y, c = lax.axis_index("x"), lax.axis_index("y"), lax.axis_index("c")
    me = 4 * x + 2 * y + c

    def copy(r, mine):
        px, py, pc = x ^ ((r >> 2) & 1), y ^ ((r >> 1) & 1), c ^ (r & 1)
        pid = 4 * px + 2 * py + pc
        return pltpu.make_async_remote_copy(
            src_ref=src_ref if gather else src_ref.at[pid], dst_ref=out_ref.at[me if mine else pid],
            send_sem=send_sems.at[r - 1], recv_sem=recv_sems.at[r - 1],
            device_id=(px, py, pc), device_id_type=pl.DeviceIdType.MESH)

    def local():
        return pltpu.make_async_copy(src_ref if gather else src_ref.at[me], out_ref.at[me], local_sem)

    def start():
        local().start()
        for r in range(1, N_DEV):
            copy(r, True).start()

    def wait():
        for r in range(1, N_DEV):
            copy(r, False).wait_recv()
        for r in range(1, N_DEV):
            copy(r, True).wait_send()
        local().wait()

    return start, wait


_XCH_SEMS = [pltpu.SemaphoreType.DMA((N_DEV - 1,)), pltpu.SemaphoreType.DMA((N_DEV - 1,)), pltpu.SemaphoreType.DMA]
_ANY = pl.BlockSpec(memory_space=pl.ANY)


def _xch_shape(src, gather):
    return jax.ShapeDtypeStruct((N_DEV,) + tuple(src.shape if gather else src.shape[1:]), src.dtype)


def _exchange(src, gather, name):
    def body(src_ref, out_ref, send_sems, recv_sems, local_sem):
        start, wait = _xch_ops(src_ref, out_ref, send_sems, recv_sems, local_sem, gather)
        start()
        wait()

    return pl.pallas_call(body, name=name, in_specs=[_ANY], out_specs=_ANY, out_shape=_xch_shape(src, gather),
                          scratch_shapes=list(_XCH_SEMS))(src)


def _pcall(body, name, grid, in_specs, out_specs, out_shape, scratch_shapes, sem, args, host=()):
    if not host:
        outs = pl.pallas_call(body, name=name, grid=grid, in_specs=in_specs, out_specs=out_specs, out_shape=out_shape,
                              scratch_shapes=scratch_shapes, compiler_params=_params(sem))(*args)
        return outs, []
    n_in, n_out, n_scr, n_x = len(args), len(out_shape), len(scratch_shapes), len(host)

    def wrapped(*refs):
        ins, xin = refs[:n_in], refs[n_in:n_in + n_x]
        outs, xout = refs[n_in + n_x:n_in + n_x + n_out], refs[n_in + n_x + n_out:n_in + 2 * n_x + n_out]
        scr = refs[n_in + 2 * n_x + n_out:n_in + 2 * n_x + n_out + n_scr]
        sems = refs[n_in + 2 * n_x + n_out + n_scr:]
        first = functools.reduce(jnp.logical_and, [pl.program_id(k) == 0 for k in range(len(grid))])
        last = functools.reduce(jnp.logical_and, [pl.program_id(k) == grid[k] - 1 for k in range(len(grid))])
        ops = [_xch_ops(xin[k], xout[k], *sems[3 * k:3 * k + 3], host[k][1]) for k in range(n_x)]

        @pl.when(first)
        def _():
            for start, _ in ops:
                start()

        body(*ins, *outs, *scr)

        @pl.when(last)
        def _():
            for _, wait in ops:
                wait()

    res = pl.pallas_call(
        wrapped, name=name, grid=grid, in_specs=list(in_specs) + [_ANY] * n_x,
        out_specs=list(out_specs) + [_ANY] * n_x,
        out_shape=list(out_shape) + [_xch_shape(s, g) for s, g in host],
        scratch_shapes=list(scratch_shapes) + list(_XCH_SEMS) * n_x,
        compiler_params=_params(("arbitrary",) * len(grid)))(*args, *[s for s, _ in host])
    return res[:n_out], res[n_out:]


def _mm(a, b, mode, name, res=None, exact=False, tm_cap=1408, tn_cap=512, tk_cap=1408, host=None):
    if mode == "nn":
        (M, K), (K2, N) = a.shape, b.shape
    elif mode == "nt":
        (M, K), (N, K2) = a.shape, b.shape
    else:
        (K, M), (K2, N) = a.shape, b.shape
    assert K == K2, (a.shape, b.shape, mode)
    tm, tn, tk = _tile(M, tm_cap), _tile(N, tn_cap), _tile(K, tk_cap)
    nk = K // tk
    a_spec = {"nn": pl.BlockSpec((tm, tk), lambda i, j, k: (i, k)),
              "nt": pl.BlockSpec((tm, tk), lambda i, j, k: (i, k)),
              "tn": pl.BlockSpec((tk, tm), lambda i, j, k: (k, i))}[mode]
    b_spec = {"nn": pl.BlockSpec((tk, tn), lambda i, j, k: (k, j)),
              "nt": pl.BlockSpec((tn, tk), lambda i, j, k: (j, k)),
              "tn": pl.BlockSpec((tk, tn), lambda i, j, k: (k, j))}[mode]
    o_spec = pl.BlockSpec((tm, tn), lambda i, j, k: (i, j))
    has_res = res is not None

    def body(*refs):
        if has_res:
            a_ref, b_ref, r_ref, o_ref, acc_ref = refs
        else:
            a_ref, b_ref, o_ref, acc_ref = refs
        k = pl.program_id(2)

        @pl.when(k == 0)
        def _():
            acc_ref[...] = jnp.zeros_like(acc_ref)

        x, y = a_ref[...], b_ref[...]
        if exact:
            x, y, prec = x.astype(F32), y.astype(F32), lax.Precision.HIGHEST
        else:
            x, y, prec = x.astype(BF16), y.astype(BF16), None
        if mode == "tn":
            x = x.T
        dims = (((1,), (1,)), ((), ())) if mode == "nt" else (((1,), (0,)), ((), ()))
        acc_ref[...] += lax.dot_general(x, y, dims, precision=prec, preferred_element_type=F32)

        @pl.when(k == nk - 1)
        def _():
            if has_res:
                o_ref[...] = acc_ref[...] + r_ref[...]
            else:
                o_ref[...] = acc_ref[...]

    in_specs = [a_spec, b_spec] + ([o_spec] if has_res else [])
    args = (a, b) + ((res,) if has_res else ())
    (out,), xouts = _pcall(body, name, (M // tm, N // tn, nk), in_specs, [o_spec],
                           [jax.ShapeDtypeStruct((M, N), F32)], [pltpu.VMEM((tm, tn), F32)],
                           ("parallel", "parallel", "arbitrary"), args, host or ())
    return out if host is None else (out, xouts)


def _rowwise(fn, name, n_rows, tr, row_ins, const_ins, row_outs, red_outs=()):
    n_row_in, n_const, n_row_out, n_red = len(row_ins), len(const_ins), len(row_outs), len(red_outs)
    in_specs = []
    for arr, w, idx in row_ins:
        width = arr.shape[1] if w is None else w
        in_specs.append(pl.BlockSpec((tr, width), functools.partial(lambda i, j: (i, j), j=idx)))
    for c in const_ins:
        in_specs.append(pl.BlockSpec(c.shape, lambda i: (0, 0)))
    out_shape = [jax.ShapeDtypeStruct((n_rows, c), dt) for c, dt in row_outs]
    out_shape += [jax.ShapeDtypeStruct((1, c), F32) for c in red_outs]
    out_specs = [pl.BlockSpec((tr, c), lambda i: (i, 0)) for c, _ in row_outs]
    out_specs += [pl.BlockSpec((1, c), lambda i: (0, 0)) for c in red_outs]

    def body(*refs):
        i = pl.program_id(0)
        rows = [r[...] for r in refs[:n_row_in]]
        consts = [r[...] for r in refs[n_row_in:n_row_in + n_const]]
        outs = refs[n_row_in + n_const:]
        row_vals, red_vals = fn(i, rows, consts)
        for k in range(n_row_out):
            outs[k][...] = row_vals[k].astype(outs[k].dtype)
        if n_red:
            @pl.when(i == 0)
            def _():
                for k in range(n_red):
                    outs[n_row_out + k][...] = jnp.zeros_like(outs[n_row_out + k])

            for k in range(n_red):
                outs[n_row_out + k][...] += red_vals[k]

    res = pl.pallas_call(
        body, name=name, grid=(n_rows // tr,), in_specs=in_specs, out_specs=out_specs, out_shape=out_shape,
        compiler_params=_params(("arbitrary",)))(*[a for a, _, _ in row_ins], *const_ins)
    return res


def _row_ids(i, tr):
    return i * tr + lax.broadcasted_iota(jnp.int32, (tr, 1), 0)


def _colsum(x):
    return jnp.sum(x, axis=0, keepdims=True)


def _scan(a, b, name, b2=None, reverse=False):
    n_rows, c = b.shape
    tr = _tile(n_rows, 384)
    nblk = n_rows // tr
    has_a = a is not None
    has_b2 = b2 is not None

    def body(*refs):
        o_ref, carry = refs[-2:]
        ins = list(refs[:-2])
        a_ref = ins.pop(0) if has_a else None
        b_ref = ins.pop(0)

        @pl.when(pl.program_id(0) == 0)
        def _():
            carry[...] = jnp.zeros_like(carry)

        rows = lax.broadcasted_iota(jnp.int32, (tr, c), 0)
        bv = b_ref[...] + ins[0][...] if has_b2 else b_ref[...]
        av = a_ref[...] if has_a else None
        s = 1
        while s < tr:
            keep = rows < tr - s if reverse else rows >= s
            shift = tr - s if reverse else s
            b_sh = jnp.where(keep, pltpu.roll(bv, shift, 0), 0.0)
            if has_a:
                a_sh = jnp.where(keep, pltpu.roll(av, shift, 0), 1.0)
                bv = av * b_sh + bv
                av = av * a_sh
            else:
                bv = b_sh + bv
            s *= 2
        h = av * carry[0:1, :] + bv if has_a else carry[0:1, :] + bv
        o_ref[...] = h
        edge = 0 if reverse else tr - 1
        carry[...] = jnp.broadcast_to(h[edge:edge + 1, :], carry.shape)

    spec = pl.BlockSpec((tr, c), (lambda i: (nblk - 1 - i, 0)) if reverse else (lambda i: (i, 0)))
    args = ((a,) if has_a else ()) + (b,) + ((b2,) if has_b2 else ())
    return pl.pallas_call(
        body, name=name, grid=(n_rows // tr,), in_specs=[spec] * len(args), out_specs=spec,
        out_shape=jax.ShapeDtypeStruct((n_rows, c), F32), scratch_shapes=[pltpu.VMEM((8, c), F32)],
        compiler_params=_params(("arbitrary",)))(*args)


def _shift_down(x, s, fill=0.0):
    pad = jnp.full((s,) + x.shape[1:], fill, x.dtype)
    return jnp.concatenate([pad, x[:-s]], axis=0)


def _shift_up(x, s):
    pad = jnp.zeros((s,) + x.shape[1:], x.dtype)
    return jnp.concatenate([x[s:], pad], axis=0)


def _rev_scan(a, g, name, g2=None):
    return _scan(None if a is None else _shift_up(a, 1), g, name, b2=g2, reverse=True)


def _swa_masks(blk, n_pad):
    qi = lax.broadcasted_iota(jnp.int32, (BLOCK, 2 * BLOCK), 0)
    ki = lax.broadcasted_iota(jnp.int32, (BLOCK, 2 * BLOCK), 1)
    dist = qi + BLOCK - ki
    key_abs = (blk - 1) * BLOCK + ki
    return (dist >= 0) & (dist < BLOCK) & (key_abs >= n_pad)


_QA_COL = _WORK_OFF["qa"][0] // ATT_W
_KA_COL, _VA_COL = (_WORK_OFF[n][0] // LANE for n in ("ka", "va"))


def _swa_strips(prev_ref, cur_ref):
    x = jnp.concatenate([prev_ref[...], cur_ref[...]], axis=0)
    return [x.astype(BF16), pltpu.roll(x, HEAD_DIM, 1).astype(BF16)]


def _swa_swapped(h):
    return 0 if h % 2 == h // SWA_GROUP else 1


def _swa_specs(tq, nsub):
    prev = lambda col: pl.BlockSpec((BLOCK, LANE), lambda i: (jnp.maximum(i * nsub - 1, 0), col))
    cur = lambda col: pl.BlockSpec((tq, LANE), lambda i: (i, col))
    return [pl.BlockSpec((tq, ATT_W), lambda i: (i, _QA_COL)), prev(_KA_COL), cur(_KA_COL), prev(_VA_COL), cur(_VA_COL),
            pl.BlockSpec((SWA_Q_HEADS, BLOCK, 2 * BLOCK), lambda i: (0, 0, 0)),
            pl.BlockSpec((SWA_Q_HEADS, 1, LANE), lambda i: (0, 0, 0))]


def _swa_fwd(proj, bias, sinks, n_pad):
    Tp = proj.shape[0]
    tq = _tile(Tp, 384)
    nsub = tq // BLOCK

    def body(q_ref, kp_ref, kc_ref, vp_ref, vc_ref, b_ref, s_ref, o_ref, lse_ref):
        i = pl.program_id(0)
        low = _low_lanes()
        ks, vs = _swa_strips(kp_ref, kc_ref), _swa_strips(vp_ref, vc_ref)
        for j in range(nsub):
            rows, keys = slice(j * BLOCK, (j + 1) * BLOCK), slice(j * BLOCK, (j + 2) * BLOCK)
            mask = _swa_masks(i * nsub + j, n_pad)
            for hb in range(SWA_Q_HEADS // 2):
                pair = slice(hb * LANE, (hb + 1) * LANE)
                qh = _split_pair(q_ref[rows, pair] * SCALE, low)
                outs = []
                for e in range(2):
                    h = 2 * hb + e
                    kk, vv = ks[_swa_swapped(h)][keys], vs[_swa_swapped(h)][keys]
                    s = lax.dot_general(qh[e], kk, (((1,), (1,)), ((), ())), preferred_element_type=F32)
                    s = jnp.where(mask, s + b_ref[h], NEG_INF)
                    sink = s_ref[h, :, 0:1]
                    m = jnp.maximum(jnp.max(s, axis=1, keepdims=True), sink)
                    p = jnp.exp(s - m)
                    denom = jnp.sum(p, axis=1, keepdims=True) + jnp.exp(sink - m)
                    outs.append(jnp.dot((p / denom).astype(BF16), vv, preferred_element_type=F32))
                    lse_ref[h, rows, :] = m + jnp.log(denom)
                o_ref[rows, pair] = jnp.where(low, outs[0], outs[1])

    return pl.pallas_call(
        body, name="swa_fwd", grid=(Tp // tq,), in_specs=_swa_specs(tq, nsub),
        out_specs=[pl.BlockSpec((tq, ATT_W), lambda i: (i, 0)), pl.BlockSpec((SWA_Q_HEADS, tq, 1), lambda i: (0, i, 0))],
        out_shape=[jax.ShapeDtypeStruct((Tp, ATT_W), F32), jax.ShapeDtypeStruct((SWA_Q_HEADS, Tp, 1), F32)],
        compiler_params=_params(("arbitrary",)))(proj, proj, proj, proj, proj, bias, sinks)


def _swa_bwd(proj, bias, sinks, o, lse, do, n_pad, host=()):
    Tp = proj.shape[0]
    tq = _tile(Tp, 384)
    nsub = tq // BLOCK
    nq = Tp // tq
    strip = BLOCK + tq

    def body(q_ref, kp_ref, kc_ref, vp_ref, vc_ref, b_ref, s_ref, o_ref, lse_ref, do_ref,
             dq_ref, dk_ref, dv_ref, db_ref, ds_ref, dk_sw, dv_sw):
        i = pl.program_id(0)

        @pl.when(i == 0)
        def _():
            db_ref[...] = jnp.zeros_like(db_ref)
            ds_ref[...] = jnp.zeros_like(ds_ref)

        for ref in (dk_ref, dv_ref, dk_sw, dv_sw):
            ref[...] = jnp.zeros_like(ref)
        low = _low_lanes()
        ks, vs = _swa_strips(kp_ref, kc_ref), _swa_strips(vp_ref, vc_ref)
        for j in range(nsub):
            rows, keys = slice(j * BLOCK, (j + 1) * BLOCK), slice(j * BLOCK, (j + 2) * BLOCK)
            mask = _swa_masks(i * nsub + j, n_pad)
            for hb in range(SWA_Q_HEADS // 2):
                pair = slice(hb * LANE, (hb + 1) * LANE)
                qh = _split_pair(q_ref[rows, pair] * SCALE, low)
                dof = do_ref[rows, pair]
                prod = dof * o_ref[rows, pair]
                doh = _split_pair(dof, low)
                dqs = []
                for e in range(2):
                    h = 2 * hb + e
                    sw = _swa_swapped(h)
                    kk, vv = ks[sw][keys], vs[sw][keys]
                    delta = jnp.sum(jnp.where(low, prod, 0.0) if e == 0 else jnp.where(low, 0.0, prod),
                                    axis=1, keepdims=True)
                    lse_h = lse_ref[h, rows, :]
                    s = lax.dot_general(qh[e], kk, (((1,), (1,)), ((), ())), preferred_element_type=F32)
                    s = jnp.where(mask, s + b_ref[h], NEG_INF)
                    p = jnp.exp(s - lse_h)
                    p_sink = jnp.exp(s_ref[h, :, 0:1] - lse_h)
                    dp = lax.dot_general(doh[e], vv, (((1,), (1,)), ((), ())), preferred_element_type=F32)
                    ds = p * (dp - delta)
                    db_ref[h] += ds
                    ds_ref[h] += jnp.broadcast_to(-jnp.sum(p_sink * delta, axis=0, keepdims=True), (1, LANE))
                    dqs.append(jnp.dot(ds.astype(BF16), kk, preferred_element_type=F32))
                    dk_h = jnp.dot(ds.T.astype(BF16), qh[e], preferred_element_type=F32)
                    dv_h = jnp.dot(p.T.astype(BF16), doh[e], preferred_element_type=F32)
                    if sw:
                        dk_sw[keys, :] += dk_h
                        dv_sw[keys, :] += dv_h
                    else:
                        dk_ref[0, keys, :] += dk_h
                        dv_ref[0, keys, :] += dv_h
                dq_ref[rows, pair] = jnp.where(low, dqs[0], dqs[1]) * SCALE
        dk_ref[0] += pltpu.roll(dk_sw[...], HEAD_DIM, 1)
        dv_ref[0] += pltpu.roll(dv_sw[...], HEAD_DIM, 1)

    wide = pl.BlockSpec((tq, ATT_W), lambda i: (i, 0))
    sspec = pl.BlockSpec((1, strip, LANE), lambda i: (i, 0, 0))
    return _pcall(
        body, "swa_bwd", (nq,),
        _swa_specs(tq, nsub) + [wide, pl.BlockSpec((SWA_Q_HEADS, tq, 1), lambda i: (0, i, 0)), wide],
        [wide, sspec, sspec, pl.BlockSpec((SWA_Q_HEADS, BLOCK, 2 * BLOCK), lambda i: (0, 0, 0)),
         pl.BlockSpec((SWA_Q_HEADS, 1, LANE), lambda i: (0, 0, 0))],
        [jax.ShapeDtypeStruct((Tp, ATT_W), F32), jax.ShapeDtypeStruct((nq, strip, LANE), F32),
         jax.ShapeDtypeStruct((nq, strip, LANE), F32),
         jax.ShapeDtypeStruct((SWA_Q_HEADS, BLOCK, 2 * BLOCK), F32), jax.ShapeDtypeStruct((SWA_Q_HEADS, 1, LANE), F32)],
        [pltpu.VMEM((strip, LANE), F32), pltpu.VMEM((strip, LANE), F32)], ("arbitrary",),
        (proj, proj, proj, proj, proj, bias, sinks, o, lse, do), host)


def _fox_scores(qb, kk, cq, ck, diag, tq):
    s = lax.dot_general(qb, kk, (((1,), (1,)), ((), ())), preferred_element_type=F32)
    s = s + cq - ck
    return _fox_causal(s, tq) if diag else s


def _fox_causal(s, tq):
    r = lax.broadcasted_iota(jnp.int32, (tq, tq), 0)
    c = lax.broadcasted_iota(jnp.int32, (tq, tq), 1)
    return jnp.where(c <= r, s, NEG_INF)


_FOX_PAIRS = FOX_HEADS // 2
_QF_COL, _KF_COL, _VF_COL = (_WORK_OFF[n][0] // LANE for n in ("qf", "kf", "vf"))


def _low_lanes():
    return lax.broadcasted_iota(jnp.int32, (1, LANE), 1) < HEAD_DIM


def _split_pair(x, low):
    return [jnp.where(low, x, 0.0).astype(BF16), jnp.where(low, 0.0, x).astype(BF16)]


def _fox_fwd(proj, ccol, crow, tq, host=()):
    Tp = proj.shape[0]
    nq = Tp // tq

    def body(q_ref, k_ref, v_ref, cc_ref, cr_ref, o_ref, lse_ref):
        qi = pl.program_id(1)
        low = _low_lanes()
        qh = _split_pair(q_ref[...] * SCALE, low)
        cq = [cc_ref[0], cc_ref[1]]

        def scores(kb):
            kk = k_ref[pl.ds(pl.multiple_of(kb * tq, tq), tq), :].astype(BF16)
            return [_fox_scores(qh[e], kk, cq[e], cr_ref[e, kb], False, tq) for e in range(2)]

        def softmax_pv(kb, s, carry, diag):
            vv = v_ref[pl.ds(pl.multiple_of(kb * tq, tq), tq), :].astype(BF16)
            stats, upd = [], []
            for e in range(2):
                m, l = carry[2 * e], carry[2 * e + 1]
                se = _fox_causal(s[e], tq) if diag else s[e]
                m_new = jnp.maximum(m, jnp.max(se, axis=1, keepdims=True))
                alpha = jnp.exp(m - m_new)
                p = jnp.exp(se - m_new)
                stats += [m_new, alpha * l + jnp.sum(p, axis=1, keepdims=True)]
                upd.append(alpha * carry[4] + jnp.dot(p.astype(BF16), vv, preferred_element_type=F32))
            return (*stats, jnp.where(low, upd[0], upd[1]))

        def step(kb, c):
            s_next = scores(kb + 1)
            return (*softmax_pv(kb, c[5:], c[:5], False), *s_next)

        col = lambda val: jnp.full((tq, 1), val, F32)
        init = (col(NEG_INF), col(0.0), col(NEG_INF), col(0.0), jnp.zeros((tq, LANE), F32), *scores(0))
        c = lax.fori_loop(0, qi, step, init)
        m0, l0, m1, l1, acc = softmax_pv(qi, c[5:], c[:5], True)
        o_ref[...] = jnp.where(low, acc / l0, acc / l1)
        lse_ref[0] = m0 + jnp.log(l0)
        lse_ref[1] = m1 + jnp.log(l1)

    cspec = pl.BlockSpec((2, tq, 1), lambda h, i: (h, i, 0))
    return _pcall(
        body, "fox_fwd", (_FOX_PAIRS, nq),
        [pl.BlockSpec((tq, LANE), lambda h, i: (i, _QF_COL + h)), pl.BlockSpec((Tp, LANE), lambda h, i: (0, _KF_COL + h)),
         pl.BlockSpec((Tp, LANE), lambda h, i: (0, _VF_COL + h)), cspec,
         pl.BlockSpec((2, nq, 1, tq), lambda h, i: (h, 0, 0, 0))],
        [pl.BlockSpec((tq, LANE), lambda h, i: (i, h)), cspec],
        [jax.ShapeDtypeStruct((Tp, FOX_HEADS * HEAD_DIM), F32), jax.ShapeDtypeStruct((FOX_HEADS, Tp, 1), F32)],
        [], ("parallel", "arbitrary"), (proj, proj, proj, ccol, crow), host)


def _fox_bwd_dq(proj, ccol, crow, o, lse, do, tq, host=()):
    Tp = proj.shape[0]
    nq = Tp // tq

    def body(q_ref, k_ref, v_ref, cc_ref, cr_ref, o_ref, lse_ref, do_ref, dq_ref, dl_ref, dc_ref):
        qi = pl.program_id(1)
        low = _low_lanes()
        qh = _split_pair(q_ref[...] * SCALE, low)
        dof = do_ref[...]
        prod = dof * o_ref[...]
        delta = [jnp.sum(jnp.where(low, prod, 0.0), axis=1, keepdims=True),
                 jnp.sum(jnp.where(low, 0.0, prod), axis=1, keepdims=True)]
        doh = _split_pair(dof, low)
        cq = [cc_ref[0], cc_ref[1]]
        lse = [lse_ref[0], lse_ref[1]]

        def step(kb, carry, diag):
            rows = pl.ds(pl.multiple_of(kb * tq, tq), tq)
            kk, vv = k_ref[rows, :].astype(BF16), v_ref[rows, :].astype(BF16)
            dqs, dcs = [], []
            for e in range(2):
                s = _fox_scores(qh[e], kk, cq[e], cr_ref[e, kb], diag, tq)
                p = jnp.exp(s - lse[e])
                dp = lax.dot_general(doh[e], vv, (((1,), (1,)), ((), ())), preferred_element_type=F32)
                ds = p * (dp - delta[e])
                dqs.append(jnp.dot(ds.astype(BF16), kk, preferred_element_type=F32))
                dcs.append(carry[1 + e] + jnp.sum(ds, axis=1, keepdims=True))
            return (carry[0] + jnp.where(low, dqs[0], dqs[1]), *dcs)

        init = (jnp.zeros((tq, LANE), F32), jnp.zeros((tq, 1), F32), jnp.zeros((tq, 1), F32))
        dq, dc0, dc1 = step(qi, lax.fori_loop(0, qi, functools.partial(step, diag=False), init), True)
        dq_ref[...] = dq * SCALE
        dl_ref[0], dl_ref[1] = delta
        dc_ref[0], dc_ref[1] = dc0, dc1

    cspec = pl.BlockSpec((2, tq, 1), lambda h, i: (h, i, 0))
    pspec = pl.BlockSpec((tq, LANE), lambda h, i: (i, h))
    stat = jax.ShapeDtypeStruct((FOX_HEADS, Tp, 1), F32)
    return _pcall(
        body, "fox_bwd_dq", (_FOX_PAIRS, nq),
        [pl.BlockSpec((tq, LANE), lambda h, i: (i, _QF_COL + h)), pl.BlockSpec((Tp, LANE), lambda h, i: (0, _KF_COL + h)),
         pl.BlockSpec((Tp, LANE), lambda h, i: (0, _VF_COL + h)), cspec,
         pl.BlockSpec((2, nq, 1, tq), lambda h, i: (h, 0, 0, 0)), pspec, cspec, pspec],
        [pspec, cspec, cspec], [jax.ShapeDtypeStruct((Tp, FOX_HEADS * HEAD_DIM), F32), stat, stat],
        [], ("parallel", "arbitrary"), (proj, proj, proj, ccol, crow, o, lse, do), host)


def _fox_bwd_dkv(proj, ccol, crow, lse, delta, do, tq, host=()):
    Tp = proj.shape[0]
    nq = Tp // tq
    pair = lambda z: z.reshape(_FOX_PAIRS, 2, Tp).transpose(0, 2, 1)
    stats = jnp.concatenate([pair(ccol), pair(lse), pair(delta)], axis=-1).reshape(_FOX_PAIRS, nq, tq, 6)

    def body(q_ref, do_ref, st_ref, k_ref, v_ref, cr_ref, dk_ref, dv_ref, dc_ref):
        ki = pl.program_id(1)
        low = _low_lanes()
        kk, vv = k_ref[...].astype(BF16), v_ref[...].astype(BF16)

        def step(qb_i, carry, diag):
            rows = pl.ds(pl.multiple_of(qb_i * tq, tq), tq)
            qh = _split_pair(q_ref[rows, :] * SCALE, low)
            doh = _split_pair(do_ref[rows, :], low)
            st = st_ref[0, qb_i]
            dk, dv = carry[0], carry[1]
            dcs = []
            for e in range(2):
                s = _fox_scores(qh[e], kk, st[:, e:e + 1], cr_ref[e, 0], diag, tq)
                p = jnp.exp(s - st[:, 2 + e:3 + e])
                dp = lax.dot_general(doh[e], vv, (((1,), (1,)), ((), ())), preferred_element_type=F32)
                ds = p * (dp - st[:, 4 + e:5 + e])
                dv = dv + jnp.dot(p.T.astype(BF16), doh[e], preferred_element_type=F32)
                dk = dk + jnp.dot(ds.T.astype(BF16), qh[e], preferred_element_type=F32)
                dcs.append(carry[2 + e] - jnp.sum(ds, axis=0, keepdims=True))
            return (dk, dv, *dcs)

        init = (jnp.zeros((tq, LANE), F32), jnp.zeros((tq, LANE), F32), jnp.zeros((1, tq), F32), jnp.zeros((1, tq), F32))
        dk, dv, dc0, dc1 = lax.fori_loop(ki + 1, nq, functools.partial(step, diag=False), step(ki, init, True))
        dk_ref[...] = dk
        dv_ref[...] = dv
        dc_ref[0, 0] = dc0
        dc_ref[1, 0] = dc1

    rspec = pl.BlockSpec((2, 1, 1, tq), lambda h, i: (h, i, 0, 0))
    pspec = pl.BlockSpec((tq, LANE), lambda h, i: (i, h))
    wide = jax.ShapeDtypeStruct((Tp, FOX_HEADS * HEAD_DIM), F32)
    return _pcall(
        body, "fox_bwd_dkv", (_FOX_PAIRS, nq),
        [pl.BlockSpec((Tp, LANE), lambda h, i: (0, _QF_COL + h)), pl.BlockSpec((Tp, LANE), lambda h, i: (0, h)),
         pl.BlockSpec((1, nq, tq, 6), lambda h, i: (h, 0, 0, 0)),
         pl.BlockSpec((tq, LANE), lambda h, i: (i, _KF_COL + h)), pl.BlockSpec((tq, LANE), lambda h, i: (i, _VF_COL + h)),
         rspec],
        [pspec, pspec, rspec], [wide, wide, jax.ShapeDtypeStruct((FOX_HEADS, nq, 1, tq), F32)],
        [], ("parallel", "arbitrary"), (proj, do, stats, proj, proj, crow), host)


def _adam(gparts, w, m, v, name):
    P, R, C = gparts.shape
    tr = _tile(R, 256, 8)

    def body(g_ref, w_ref, m_ref, v_ref, go_ref, d_ref, mo_ref, vo_ref):
        g = g_ref[0]
        for p in range(1, P):
            g = g + g_ref[p]
        m2 = ADAM_B1 * m_ref[...] + (1.0 - ADAM_B1) * g
        v2 = ADAM_B2 * v_ref[...] + (1.0 - ADAM_B2) * (g * g)
        m_hat = m2 / (1.0 - ADAM_B1 ** ADAM_STEP)
        v_hat = v2 / (1.0 - ADAM_B2 ** ADAM_STEP)
        go_ref[...] = g
        d_ref[...] = -ADAM_LR * (m_hat / (jnp.sqrt(v_hat) + ADAM_EPS) + ADAM_WD * w_ref[...])
        mo_ref[...] = m2
        vo_ref[...] = v2

    spec = pl.BlockSpec((tr, C), lambda i: (i, 0))
    shp = jax.ShapeDtypeStruct((R, C), F32)
    return pl.pallas_call(
        body, name=name, grid=(R // tr,),
        in_specs=[pl.BlockSpec((P, tr, C), lambda i: (0, i, 0)), spec, spec, spec],
        out_specs=[spec] * 4, out_shape=[shp] * 4, compiler_params=_params(("parallel",)))(gparts, w, m, v)


def _sum_parts(gparts, name):
    P, R, C = gparts.shape
    tr = _tile(R, 256, 8)

    def body(g_ref, o_ref):
        g = g_ref[0]
        for p in range(1, P):
            g = g + g_ref[p]
        o_ref[...] = g

    return pl.pallas_call(
        body, name=name, grid=(R // tr,), in_specs=[pl.BlockSpec((P, tr, C), lambda i: (0, i, 0))],
        out_specs=pl.BlockSpec((tr, C), lambda i: (i, 0)), out_shape=jax.ShapeDtypeStruct((R, C), F32),
        compiler_params=_params(("parallel",)))(gparts)


def _pack(pieces, width, row_mult):
    flat = jnp.concatenate([p.reshape(-1) for p in pieces])
    n = flat.shape[0]
    rows = -(-n // width)
    rows = -(-rows // row_mult) * row_mult
    return jnp.pad(flat, (0, rows * width - n)).reshape(rows, width)


def _unpack(flat2d, shapes, lead=None):
    out, off = [], 0
    if lead is None:
        flat = flat2d.reshape(-1)
        for s in shapes:
            n = int(np.prod(s))
            out.append(flat[off:off + n].reshape(s))
            off += n
    else:
        flat = flat2d.reshape(lead, -1)
        for s in shapes:
            n = int(np.prod(s))
            out.append(flat[:, off:off + n].reshape((lead,) + tuple(s)))
            off += n
    return out


def _col_pieces(off, width, shard_w, fetch):
    out, c, end = [], off, off + width
    while c < end:
        d = c // shard_w
        hi = min(end, (d + 1) * shard_w)
        out.append(fetch(d, c - d * shard_w, hi - d * shard_w))
        c = hi
    return out


def _work_cols_of_orig(lo, hi, fetch):
    out = []
    for name, ow in _ORIG:
        o = _ORIG_OFF[name][0]
        a, b = max(lo, o), min(hi, o + ow)
        if a < b:
            w0 = _WORK_OFF[name][0]
            out.append(fetch(w0 + a - o, w0 + b - o))
    return out


def _t5_bucket_np(dist):
    max_exact = REL_BUCKETS // 2
    d = np.maximum(dist, 0)
    scaled = (np.log(np.maximum(d, 1).astype(np.float32) / np.float32(max_exact))
              / np.float32(math.log(REL_MAX_DIST / max_exact))).astype(np.float32)
    large = np.minimum(max_exact + (scaled * np.float32(REL_BUCKETS - max_exact)).astype(np.int32), REL_BUCKETS - 1)
    return np.where(d < max_exact, d, large)


def _bucket_onehot():
    q_idx = np.arange(BLOCK)[:, None]
    k_idx = np.arange(2 * BLOCK)[None, :]
    bucket = _t5_bucket_np(q_idx + BLOCK - k_idx).reshape(-1)
    oh = np.zeros((LANE, BLOCK * 2 * BLOCK), np.float32)
    oh[bucket, np.arange(bucket.shape[0])] = 1.0
    return oh


def kernel(x, meta_tokens, rel_bias_table, norm_mix, w_in, swa_sinks, fox_forget_bias, conv_w, conv_b, lru_w_r, lru_b_r, lru_w_i, lru_b_i, lru_lambda, w_branch, w_out, norm_ffn, w_ffn_in, w_ffn_out, norm_final, loss_target, m_meta_tokens, m_rel_bias_table, m_norm_mix, m_w_in, m_swa_sinks, m_fox_forget_bias, m_conv_w, m_conv_b, m_lru_w_r, m_lru_b_r, m_lru_w_i, m_lru_b_i, m_lru_lambda, m_w_branch, m_w_out, m_norm_ffn, m_w_ffn_in, m_w_ffn_out, m_norm_final, v_meta_tokens, v_rel_bias_table, v_norm_mix, v_w_in, v_swa_sinks, v_fox_forget_bias, v_conv_w, v_conv_b, v_lru_w_r, v_lru_b_r, v_lru_w_i, v_lru_b_i, v_lru_lambda, v_w_branch, v_w_out, v_norm_ffn, v_w_ffn_in, v_w_ffn_out, v_norm_final):
    S = x.shape[1]
    T = N_META + S
    n_pad = (-T) % BLOCK
    Tp = T + n_pad
    first = n_pad + N_META
    TR = _tile(Tp, 384)
    TQ = _tile(Tp, 384)
    NQ = Tp // TQ
    me = 4 * lax.axis_index("x") + 2 * lax.axis_index("y") + lax.axis_index("c")

    big_names = ("w_in", "w_ffn_in", "w_ffn_out", "w_branch", "w_out")
    big_w = dict(w_in=w_in, w_ffn_in=w_ffn_in, w_ffn_out=w_ffn_out, w_branch=w_branch, w_out=w_out)
    big_m = dict(w_in=m_w_in, w_ffn_in=m_w_ffn_in, w_ffn_out=m_w_ffn_out, w_branch=m_w_branch, w_out=m_w_out)
    big_v = dict(w_in=v_w_in, w_ffn_in=v_w_ffn_in, w_ffn_out=v_w_ffn_out, w_branch=v_w_branch, w_out=v_w_out)
    rows2d = lambda w: w.reshape(-1, w.shape[-1])
    in_shard_w = IN_COLS // N_DEV
    cat_axis = dict(w_in=-1, w_ffn_in=-1, w_ffn_out=0, w_branch=-1, w_out=0)

    def w_send(n, l):
        return rows2d(big_w[n][l]).astype(BF16)

    def w_build(n, g):
        shard = lambda d: g[d].reshape(big_w[n].shape[1:])
        if n != "w_in":
            return jnp.concatenate([shard(d) for d in range(N_DEV)], axis=cat_axis[n])
        parts = []
        for name, width in _WORK:
            off, ow = _ORIG_OFF[name]
            parts += _col_pieces(off, ow, in_shard_w, lambda d, lo, hi: shard(d)[:, lo:hi])
            if width > ow:
                parts.append(jnp.zeros((D_MODEL, width - ow), BF16))
        return jnp.concatenate(parts, axis=-1)

    W_in, W_ffn_in, W_ffn_out, W_branch, W_out = ([None] * DEPTH for _ in range(5))
    W_in[0] = w_build("w_in", _exchange(w_send("w_in", 0), True, "gather_w_in"))

    small_sh = _exchange(_pack([meta_tokens, conv_w], D_MODEL, 8), True, "gather_small")
    g_meta, g_conv_w = _unpack(small_sh, [meta_tokens.shape, conv_w.shape], lead=N_DEV)
    meta_full = g_meta.transpose(1, 0, 2).reshape(N_META, D_MODEL)
    conv_w_full = g_conv_w.transpose(1, 2, 0, 3).reshape(DEPTH, CONV_WIDTH, LRU_WIDTH)

    onehot = jnp.asarray(_bucket_onehot())
    table_t = jnp.pad(rel_bias_table.T, ((0, 0), (0, LANE - REL_BUCKETS)))
    bias = _mm(table_t, onehot, "nn", "swa_bias", exact=True, tn_cap=4096).reshape(SWA_Q_HEADS, BLOCK, 2 * BLOCK)

    def dense_blocks(w):
        rows = []
        for b in range(LRU_BLOCKS):
            rows.append(jnp.pad(w[b], ((0, 0), (b * LRU_BLOCK_DIM, (LRU_BLOCKS - 1 - b) * LRU_BLOCK_DIM))))
        return jnp.concatenate(rows, axis=0)

    h = jnp.concatenate([jnp.zeros((n_pad, D_MODEL), F32), meta_full, x[0]], axis=0)
    saved = []

    for l in range(DEPTH):
        sv = {"h": h}
        g_mix = norm_mix[l][None, :]

        def norm_fwd(i, rows, consts):
            xx = rows[0]
            return [xx * lax.rsqrt(jnp.mean(xx * xx, axis=1, keepdims=True) + EPS) * consts[0]], []

        (u,) = _rowwise(norm_fwd, "norm_fwd", Tp, TR, [(h, None, 0)], [g_mix], [(D_MODEL, BF16)])
        proj, got = _mm(u, W_in[l], "nn", "proj", tn_cap=384,
                        host=[(w_send("w_branch", l), True), (w_send("w_out", l), True)])
        W_branch[l], W_out[l] = w_build("w_branch", got[0]), w_build("w_out", got[1])
        sv["u"], sv["proj"] = u, proj

        def col(name):
            off, w = _WORK_OFF[name]
            return (proj, w, off // w)

        def cols(name):
            off, w = _WORK_OFF[name]
            return proj[:, off:off + w]

        sinks_l =jnp.broadcast_to(swa_sinks[l][:, None, None], (SWA_Q_HEADS, 1, LANE))
        o_a, lse_a = _swa_fwd(proj, bias, sinks_l, n_pad)
        sv.update(sinks=sinks_l, lse_a=lse_a, o_a=o_a)

        fbias = jnp.pad(fox_forget_bias[l], (0, LANE - FOX_HEADS))[None, :]

        def logf_fwd(i, rows, consts):
            return [_log_sigmoid(rows[0] + consts[0])], []

        (logf,) = _rowwise(logf_fwd, "logf_fwd", Tp, TR, [col("fl")], [fbias], [(LANE, F32)])
        cum = _scan(None, logf, "cumsum")
        cum_h = cum[:, :FOX_HEADS].T
        ccol = cum_h[:, :, None]
        crow = jnp.where(jnp.arange(Tp)[None, :] < n_pad, -NEG_INF, cum_h).reshape(FOX_HEADS, NQ, 1, TQ)
        ride = [("w_ffn_in", l), ("w_ffn_out", l)] + ([("w_in", l + 1)] if l + 1 < DEPTH else [])
        (o_f, lse_f), got = _fox_fwd(proj, ccol, crow, TQ, host=[(w_send(n, ll), True) for n, ll in ride])
        W_ffn_in[l], W_ffn_out[l] = w_build("w_ffn_in", got[0]), w_build("w_ffn_out", got[1])
        if l + 1 < DEPTH:
            W_in[l + 1] = w_build("w_in", got[2])
        sv.update(fbias=fbias, ccol=ccol, crow=crow, lse_f=lse_f, o_f=o_f)

        xc = cols("xc")
        xs = [_shift_down(xc, CONV_WIDTH - 1 - i) for i in range(CONV_WIDTH - 1)] + [xc]
        cw = conv_w_full[l]
        cb = conv_b[l][None, :]

        def conv_fwd(i, rows, consts):
            w, b = consts
            acc = rows[0] * w[0:1, :]
            for t in range(1, CONV_WIDTH):
                acc = acc + rows[t] * w[t:t + 1, :]
            return [acc + b], []

        (xconv,) = _rowwise(conv_fwd, "conv_fwd", Tp, TR, [(a, None, 0) for a in xs], [cw, cb], [(LRU_WIDTH, F32)])
        w_ri = jnp.concatenate([dense_blocks(lru_w_r[l]), dense_blocks(lru_w_i[l])], axis=1)
        pre = _mm(xconv, w_ri, "nn", "lru_gates")
        lru_consts = [lru_b_r[l][None, :], lru_b_i[l][None, :], lru_lambda[l][None, :]]

        def lru_fwd(i, rows, consts):
            pr, xv = rows
            b_r, b_i, lam = consts
            r = _sigmoid(pr[:, :LRU_WIDTH] + b_r)
            gi = _sigmoid(pr[:, LRU_WIDTH:] + b_i)
            log_a = LRU_C * r * _log_sigmoid(lam)
            valid = (_row_ids(i, TR) >= n_pad).astype(F32)
            inp = jnp.sqrt(_neg_expm1(2.0 * log_a)) * (gi * xv) * valid
            return [jnp.exp(log_a), inp], []

        a_dec, inp = _rowwise(lru_fwd, "lru_fwd", Tp, TR, [(pre, None, 0), (xconv, None, 0)], lru_consts,
                              [(LRU_WIDTH, F32), (LRU_WIDTH, F32)])
        hs = _scan(a_dec, inp, "lru_scan")

        def oc_fwd(i, rows, consts):
            return [rows[0] * _gelu(rows[1])], []

        (o_c,) = _rowwise(oc_fwd, "oc_fwd", Tp, TR, [(hs, None, 0), col("yc")], [], [(LRU_WIDTH, BF16)])
        sv.update(xs=xs, cw=cw, xconv=xconv, w_ri=w_ri, pre=pre, lru_consts=lru_consts, a_dec=a_dec, hs=hs, o_c=o_c)

        ba = _mm(o_a, W_branch[l][0], "nn", "branch")
        bf = _mm(o_f, W_branch[l][1], "nn", "branch")
        bc = _mm(o_c, W_branch[l][2], "nn", "branch")

        def merge_fwd(i, rows, consts):
            g, b0, b1, b2 = rows
            valid = (_row_ids(i, BLOCK) >= n_pad).astype(F32)
            mg = (_sigmoid(g[:, :D_MODEL]) * b0 + _sigmoid(g[:, D_MODEL:2 * D_MODEL]) * b1
                  + _sigmoid(g[:, 2 * D_MODEL:]) * b2)
            return [mg * valid], []

        (merged,) = _rowwise(merge_fwd, "merge_fwd", Tp, BLOCK,
                             [col("gates"), (ba, None, 0), (bf, None, 0), (bc, None, 0)], [], [(D_MODEL, BF16)])
        h1 = _mm(merged, W_out[l], "nn", "out_proj", res=h)
        sv.update(ba=ba, bf=bf, bc=bc, merged=merged, h1=h1)

        g_ffn = norm_ffn[l][None, :]
        (u2,) = _rowwise(norm_fwd, "norm_fwd", Tp, TR, [(h1, None, 0)], [g_ffn], [(D_MODEL, BF16)])
        ff = _mm(u2, W_ffn_in[l], "nn", "ffn_in")

        def act_fwd(i, rows, consts):
            gate, up = rows
            return [gate * _sigmoid(gate) * up], []

        (act,) = _rowwise(act_fwd, "act_fwd", Tp, TR, [(ff, D_FF, 0), (ff, D_FF, 1)], [], [(D_FF, BF16)])
        h = _mm(act, W_ffn_out[l], "nn", "ffn_out", res=h1)
        sv.update(u2=u2, ff=ff, act=act)
        saved.append(sv)

    tgt = jnp.concatenate([jnp.zeros((first, D_MODEL), F32), loss_target[0]], axis=0)
    g_fin = norm_final[None, :]

    def head(i, rows, consts):
        xx, tg = rows
        g = consts[0]
        valid = (_row_ids(i, TR) >= first).astype(F32)
        rstd = lax.rsqrt(jnp.mean(xx * xx, axis=1, keepdims=True) + EPS)
        xhat = xx * rstd
        err = (xhat * g - tg) * valid
        loss_rows = 0.5 * jnp.mean(err * err, axis=1, keepdims=True)
        dy = err * (1.0 / D_MODEL)
        dxhat = dy * g
        dx = rstd * (dxhat - xhat * jnp.mean(dxhat * xhat, axis=1, keepdims=True))
        return [dx], [jnp.broadcast_to(_colsum(loss_rows), (1, LANE)), _colsum(dy * xhat)]

    dh, loss_part, d_norm_final = _rowwise(head, "loss_head", Tp, TR, [(h, None, 0), (tgt, None, 0)], [g_fin],
                                           [(D_MODEL, F32)], [LANE, D_MODEL])
    loss = lax.psum(loss_part[0, 0], ("x", "y", "c"))

    def norm_bwd_call(xin, g, du, dres):
        def norm_bwd(i, rows, consts):
            xx, dd, rr = rows
            gg = consts[0]
            valid = (_row_ids(i, TR) >= n_pad).astype(F32)
            rstd = lax.rsqrt(jnp.mean(xx * xx, axis=1, keepdims=True) + EPS)
            xhat = xx * rstd
            dxhat = dd * gg
            dx = rstd * (dxhat - xhat * jnp.mean(dxhat * xhat, axis=1, keepdims=True))
            return [rr + dx * valid], [_colsum(dd * xhat)]

        return _rowwise(norm_bwd, "norm_bwd", Tp, TR, [(xin, None, 0), (du, None, 0), (dres, None, 0)], [g],
                        [(D_MODEL, F32)], [D_MODEL])

    grads = {k: [None] * DEPTH for k in ("norm_mix", "w_in", "swa_sinks", "fox_forget_bias", "conv_w", "conv_b",
                                         "lru_w_r", "lru_b_r", "lru_w_i", "lru_b_i", "lru_lambda", "w_branch",
                                         "w_out", "norm_ffn", "w_ffn_in", "w_ffn_out")}
    d_bias_total = None

    def g_send(n, g):
        if n == "w_in":
            pieces = [jnp.concatenate(_work_cols_of_orig(d * in_shard_w, (d + 1) * in_shard_w,
                                                         lambda lo, hi: g[:, lo:hi]), axis=-1) for d in range(N_DEV)]
        else:
            ax = cat_axis[n] % g.ndim
            w = big_w[n].shape[1 + ax]
            pieces = [lax.slice_in_dim(g, d * w, (d + 1) * w, axis=ax) for d in range(N_DEV)]
        return jnp.stack([rows2d(p) for p in pieces])

    recv = {n: [None] * DEPTH for n in big_names}
    for l in reversed(range(DEPTH)):
        sv = saved[l]
        proj = sv["proj"]

        def col(name):
            off, w = _WORK_OFF[name]
            return (proj, w, off // w)

        dh2 = dh
        d_act = _mm(dh2, W_ffn_out[l], "nt", "d_act")
        grads["w_ffn_out"][l] = _mm(sv["act"], dh2, "tn", "dw_ffn_out")

        def act_bwd(i, rows, consts):
            gate, up, da = rows
            sg = _sigmoid(gate)
            d_gate = da * up * (sg * (1.0 + gate * (1.0 - sg)))
            d_up = da * (gate * sg)
            return [jnp.concatenate([d_gate, d_up], axis=1)], []

        (dff,) = _rowwise(act_bwd, "act_bwd", Tp, BLOCK, [(sv["ff"], D_FF, 0), (sv["ff"], D_FF, 1), (d_act, None, 0)],
                          [], [(2 * D_FF, BF16)])
        grads["w_ffn_in"][l] = _mm(sv["u2"], dff, "tn", "dw_ffn_in")
        du2 = _mm(dff, W_ffn_in[l], "nt", "du2")
        dh1, dg = norm_bwd_call(sv["h1"], norm_ffn[l][None, :], du2, dh2)
        grads["norm_ffn"][l] = dg[0]

        dmerged = _mm(dh1, W_out[l], "nt", "d_merged")
        grads["w_out"][l] = _mm(sv["merged"], dh1, "tn", "dw_out")

        def merge_bwd(i, rows, consts):
            g, b0, b1, b2, dm = rows
            dm = dm * (_row_ids(i, BLOCK) >= n_pad).astype(F32)
            outs, dgs = [], []
            for k, bk in enumerate((b0, b1, b2)):
                sg = _sigmoid(g[:, k * D_MODEL:(k + 1) * D_MODEL])
                outs.append(dm * sg)
                dgs.append(dm * bk * sg * (1.0 - sg))
            return outs + [jnp.concatenate(dgs, axis=1)], []

        d_ba, d_bf, d_bc, d_gates = _rowwise(
            merge_bwd, "merge_bwd", Tp, BLOCK,
            [col("gates"), (sv["ba"], None, 0), (sv["bf"], None, 0), (sv["bc"], None, 0), (dmerged, None, 0)], [],
            [(D_MODEL, BF16)] * 3 + [(3 * D_MODEL, BF16)])
        grads["w_branch"][l] = jnp.stack([_mm(sv["o_a"], d_ba, "tn", "dw_branch"),
                                          _mm(sv["o_f"], d_bf, "tn", "dw_branch"),
                                          _mm(sv["o_c"], d_bc, "tn", "dw_branch")])
        do_a = _mm(d_ba, W_branch[l][0], "nt", "d_branch")
        do_f = _mm(d_bf, W_branch[l][1], "nt", "d_branch")
        do_c = _mm(d_bc, W_branch[l][2], "nt", "d_branch")

        def oc_bwd(i, rows, consts):
            d, hv, yv = rows
            return [d * _gelu(yv), d * hv * _gelu_grad(yv)], []

        d_hs, d_yc = _rowwise(oc_bwd, "oc_bwd", Tp, TR, [(do_c, None, 0), (sv["hs"], None, 0), col("yc")], [],
                              [(LRU_WIDTH, F32)] * 2)
        d_state = _rev_scan(sv["a_dec"], d_hs, "lru_scan")
        hs_prev = _shift_down(sv["hs"], 1)

        def lru_bwd(i, rows, consts):
            dH, hp, pr, xv = rows
            b_r, b_i, lam = consts
            valid = (_row_ids(i, TR) >= n_pad).astype(F32)
            r = _sigmoid(pr[:, :LRU_WIDTH] + b_r)
            gi = _sigmoid(pr[:, LRU_WIDTH:] + b_i)
            lsl = _log_sigmoid(lam)
            log_a = LRU_C * r * lsl
            a = jnp.exp(log_a)
            one_m_e = _neg_expm1(2.0 * log_a)
            mult = jnp.sqrt(one_m_e)
            d_inp = dH * valid
            d_mult = d_inp * gi * xv
            d_gi = d_inp * mult * xv
            d_x = d_inp * mult * gi
            d_log_a = dH * hp * a - d_mult * (1.0 - one_m_e) / mult
            d_pre_r = d_log_a * (LRU_C * lsl) * r * (1.0 - r)
            d_pre_i = d_gi * gi * (1.0 - gi)
            d_lam = _colsum(d_log_a * (LRU_C * r)) * _sigmoid(-lam)
            return [jnp.concatenate([d_pre_r, d_pre_i], axis=1), d_x], [_colsum(d_pre_r), _colsum(d_pre_i), d_lam]

        d_pre, d_xdir, d_b_r, d_b_i, d_lam = _rowwise(
            lru_bwd, "lru_bwd", Tp, TR, [(d_state, None, 0), (hs_prev, None, 0), (sv["pre"], None, 0),
                                         (sv["xconv"], None, 0)], sv["lru_consts"],
            [(2 * LRU_WIDTH, BF16), (LRU_WIDTH, F32)], [LRU_WIDTH] * 3)
        grads["lru_b_r"][l], grads["lru_b_i"][l], grads["lru_lambda"][l] = d_b_r[0], d_b_i[0], d_lam[0]
        d_w_ri = _mm(sv["xconv"], d_pre, "tn", "dw_lru_gates")

        def diag_blocks(wd):
            w4 = wd.reshape(LRU_BLOCKS, LRU_BLOCK_DIM, LRU_BLOCKS, LRU_BLOCK_DIM)
            return jnp.stack([w4[b, :, b, :] for b in range(LRU_BLOCKS)])

        grads["lru_w_r"][l] = diag_blocks(d_w_ri[:, :LRU_WIDTH])
        grads["lru_w_i"][l] = diag_blocks(d_w_ri[:, LRU_WIDTH:])
        d_conv = _mm(d_pre, sv["w_ri"], "nt", "d_xconv", res=d_xdir)
        dcs = [d_conv] + [_shift_up(d_conv, s) for s in range(1, CONV_WIDTH)]

        def conv_bwd(i, rows, consts):
            w = consts[0]
            d = rows[:CONV_WIDTH]
            xsh = rows[CONV_WIDTH:]
            dxc = d[0] * w[CONV_WIDTH - 1:CONV_WIDTH, :]
            for s in range(1, CONV_WIDTH):
                dxc = dxc + d[s] * w[CONV_WIDTH - 1 - s:CONV_WIDTH - s, :]
            return [dxc], [_colsum(xsh[t] * d[0]) for t in range(CONV_WIDTH)] + [_colsum(d[0])]

        res = _rowwise(conv_bwd, "conv_bwd", Tp, TR, [(a, None, 0) for a in dcs + sv["xs"]], [sv["cw"]],
                       [(LRU_WIDTH, F32)], [LRU_WIDTH] * (CONV_WIDTH + 1))
        d_xc = res[0]
        grads["conv_w"][l] = jnp.concatenate(res[1:1 + CONV_WIDTH], axis=0)
        grads["conv_b"][l] = res[1 + CONV_WIDTH][0]

        (dqf, delta_f, dccol), got = _fox_bwd_dq(
            proj, sv["ccol"], sv["crow"], sv["o_f"], sv["lse_f"], do_f, TQ,
            host=[(g_send("w_ffn_in", grads["w_ffn_in"][l]), False)])
        recv["w_ffn_in"][l] = got[0]
        ride = [("w_out", l), ("w_branch", l)] + ([("w_in", l + 1)] if l + 1 < DEPTH else [])
        (dkf, dvf, dcrow), got = _fox_bwd_dkv(
            proj, sv["ccol"], sv["crow"], sv["lse_f"], delta_f, do_f, TQ,
            host=[(g_send(n, grads[n][ll]), False) for n, ll in ride])
        for (n, ll), r in zip(ride, got):
            recv[n][ll] = r
        lanes = lambda z: jnp.pad(z.reshape(FOX_HEADS, Tp).T, ((0, 0), (0, LANE - FOX_HEADS)))
        dlogf = _rev_scan(None, lanes(dcrow), "cumsum_bwd", g2=lanes(dccol))

        def logf_bwd(i, rows, consts):
            dl, fl = rows
            valid = (_row_ids(i, TR) >= n_pad).astype(F32)
            lane_ok = (lax.broadcasted_iota(jnp.int32, (1, LANE), 1) < FOX_HEADS).astype(F32)
            dfl = dl * _sigmoid(-(fl + consts[0])) * valid * lane_ok
            return [dfl], [_colsum(dfl)]

        d_fl, d_fb = _rowwise(logf_bwd, "logf_bwd", Tp, TR, [(dlogf, None, 0), col("fl")], [sv["fbias"]],
                              [(LANE, F32)], [LANE])
        grads["fox_forget_bias"][l] = d_fb[0, :FOX_HEADS]

        (dqa, dk_strip, dv_strip, d_bias_l, d_sink), got = _swa_bwd(
            proj, bias, sv["sinks"], sv["o_a"], sv["lse_a"], do_a, n_pad,
            host=[(g_send("w_ffn_out", grads["w_ffn_out"][l]), False)])
        recv["w_ffn_out"][l] = got[0]
        grads["swa_sinks"][l] = d_sink[:, 0, 0]
        d_bias_total = d_bias_l if d_bias_total is None else d_bias_total + d_bias_l

        def kv_parts(strip):
            own = strip[:, BLOCK:, :].reshape(Tp, KV_W)
            nxt = jnp.pad(strip[1:, :BLOCK, :], ((0, 1), (TQ - BLOCK, 0), (0, 0))).reshape(Tp, KV_W)
            return [(own, None, 0), (nxt, None, 0)]

        def assemble(i, rows, consts):
            dg, dqa_, dqf_, dkf_, dvf_, dxc_, dyc_, dk0, dk1, dv0, dv1, dfl_ = rows
            parts = [dg, dqa_, dqf_, dkf_, dvf_, dxc_, dyc_, dk0 + dk1, dv0 + dv1, dfl_]
            return [jnp.concatenate([p.astype(BF16) for p in parts], axis=1)], []

        (dproj,) = _rowwise(
            assemble, "assemble_dproj", Tp, BLOCK,
            [(d_gates, None, 0), (dqa, None, 0), (dqf, None, 0), (dkf, None, 0), (dvf, None, 0), (d_xc, None, 0),
             (d_yc, None, 0)] + kv_parts(dk_strip) + kv_parts(dv_strip) + [(d_fl, None, 0)], [],
            [(WORK_COLS, BF16)])
        grads["w_in"][l] = _mm(sv["u"], dproj, "tn", "dw_in", tn_cap=384)
        if l == 0:
            du, got = _mm(dproj, W_in[l], "nt", "du", tk_cap=384,
                          host=[(g_send("w_in", grads["w_in"][0]), False)])
            recv["w_in"][0] = got[0]
        else:
            du = _mm(dproj, W_in[l], "nt", "du", tk_cap=384)
        dh, dg = norm_bwd_call(sv["h"], norm_mix[l][None, :], du, dh1)
        grads["norm_mix"][l] = dg[0]

    d_table = _mm(d_bias_total.reshape(SWA_Q_HEADS, BLOCK * 2 * BLOCK), onehot, "nt", "d_rel_table", exact=True,
                  tk_cap=4096)
    g_rel = d_table[:, :REL_BUCKETS].T
    g_meta_full = dh[n_pad:first]
    grad_x = dh[first:][None]

    stack = lambda k: jnp.stack(grads[k])
    big_res = {}
    for n in big_names:
        per_layer = [_adam(recv[n][l], rows2d(big_w[n][l]), rows2d(big_m[n][l]), rows2d(big_v[n][l]), "adam_" + n)
                     for l in range(DEPTH)]
        big_res[n] = [jnp.stack([per_layer[l][k] for l in range(DEPTH)]).reshape(big_w[n].shape) for k in range(4)]

    rep_names = ("rel_bias_table", "norm_mix", "swa_sinks", "fox_forget_bias", "conv_b", "lru_w_r", "lru_b_r",
                 "lru_w_i", "lru_b_i", "lru_lambda", "norm_ffn", "norm_final")
    rep_w = dict(rel_bias_table=rel_bias_table, norm_mix=norm_mix, swa_sinks=swa_sinks,
                 fox_forget_bias=fox_forget_bias, conv_b=conv_b, lru_w_r=lru_w_r, lru_b_r=lru_b_r, lru_w_i=lru_w_i,
                 lru_b_i=lru_b_i, lru_lambda=lru_lambda, norm_ffn=norm_ffn, norm_final=norm_final)
    rep_m = dict(rel_bias_table=m_rel_bias_table, norm_mix=m_norm_mix, swa_sinks=m_swa_sinks,
                 fox_forget_bias=m_fox_forget_bias, conv_b=m_conv_b, lru_w_r=m_lru_w_r, lru_b_r=m_lru_b_r,
                 lru_w_i=m_lru_w_i, lru_b_i=m_lru_b_i, lru_lambda=m_lru_lambda, norm_ffn=m_norm_ffn,
                 norm_final=m_norm_final)
    rep_v = dict(rel_bias_table=v_rel_bias_table, norm_mix=v_norm_mix, swa_sinks=v_swa_sinks,
                 fox_forget_bias=v_fox_forget_bias, conv_b=v_conv_b, lru_w_r=v_lru_w_r, lru_b_r=v_lru_b_r,
                 lru_w_i=v_lru_w_i, lru_b_i=v_lru_b_i, lru_lambda=v_lru_lambda, norm_ffn=v_norm_ffn,
                 norm_final=v_norm_final)
    rep_g = {n: (g_rel if n == "rel_bias_table" else d_norm_final[0] if n == "norm_final" else stack(n))
             for n in rep_names}
    small_g = [rep_g[n] for n in rep_names] + [g_meta_full, stack("conv_w")]
    small_shapes = [rep_w[n].shape for n in rep_names] + [(N_META, D_MODEL), (DEPTH, CONV_WIDTH, LRU_WIDTH)]
    gs_all = _exchange(_pack(small_g, D_MODEL, 8), True, "gather_small_grads")
    gs_sum = _unpack(_sum_parts(gs_all, "sum_small_grads"), small_shapes)
    gsum = dict(zip(rep_names, gs_sum[:len(rep_names)]))
    g_meta_sh = lax.dynamic_slice_in_dim(gs_sum[-2], me * (D_MODEL // N_DEV), D_MODEL // N_DEV, axis=1)
    g_convw_sh = lax.dynamic_slice_in_dim(gs_sum[-1], me * (LRU_WIDTH // N_DEV), LRU_WIDTH // N_DEV, axis=2)
    sm_names = rep_names + ("meta_tokens", "conv_w")
    sm_w = [rep_w[n] for n in rep_names] + [meta_tokens, conv_w]
    sm_m = [rep_m[n] for n in rep_names] + [m_meta_tokens, m_conv_w]
    sm_v = [rep_v[n] for n in rep_names] + [v_meta_tokens, v_conv_w]
    sm_g = [gsum[n] for n in rep_names] + [g_meta_sh, g_convw_sh]
    sm_shapes = [w.shape for w in sm_w]
    sm_out = [_unpack(o, sm_shapes) for o in _adam(_pack(sm_g, D_MODEL, 8)[None], _pack(sm_w, D_MODEL, 8),
                                                   _pack(sm_m, D_MODEL, 8), _pack(sm_v, D_MODEL, 8), "adam_small")]
    sm_res = {n: [sm_out[k][j] for k in range(4)] for j, n in enumerate(sm_names)}

    order = ("meta_tokens", "rel_bias_table", "norm_mix", "w_in", "swa_sinks", "fox_forget_bias", "conv_w", "conv_b",
             "lru_w_r", "lru_b_r", "lru_w_i", "lru_b_i", "lru_lambda", "w_branch", "w_out", "norm_ffn", "w_ffn_in",
             "w_ffn_out", "norm_final")
    allres = {**big_res, **sm_res}
    outs = [loss, grad_x]
    for k in range(4):
        outs += [allres[n][k] for n in order]
    return tuple(outs)
```

```python
import functools
import math

import numpy as np
import jax
import jax.numpy as jnp
from jax import lax
from jax.experimental import pallas as pl
from jax.experimental.pallas import tpu as pltpu

F32 = jnp.float32
BF16 = jnp.bfloat16

N_DEV = 8
D_MODEL = 1024
DEPTH = 4
HEAD_DIM = 64
N_META = 16
BLOCK = 128
NEG_INF = -1e30
SWA_Q_HEADS = 8
SWA_KV_HEADS = 2
SWA_GROUP = SWA_Q_HEADS // SWA_KV_HEADS
FOX_HEADS = 8
LRU_WIDTH = D_MODEL // 2
LRU_BLOCKS = 8
LRU_BLOCK_DIM = LRU_WIDTH // LRU_BLOCKS
CONV_WIDTH = 4
LRU_C = 8.0
REL_BUCKETS = 32
REL_MAX_DIST = 128
D_FF = 2816
N_BRANCH = 3
ATT_W = SWA_Q_HEADS * HEAD_DIM
KV_W = SWA_KV_HEADS * HEAD_DIM
SCALE = HEAD_DIM ** -0.5
EPS = 1e-6

_ORIG = (("qa", ATT_W), ("ka", KV_W), ("va", KV_W), ("qf", ATT_W), ("kf", ATT_W), ("vf", ATT_W),
         ("fl", FOX_HEADS), ("xc", LRU_WIDTH), ("yc", LRU_WIDTH), ("gates", N_BRANCH * D_MODEL))
IN_COLS = sum(w for _, w in _ORIG)
_WORK = (("gates", 3072), ("qa", 512), ("qf", 512), ("kf", 512), ("vf", 512), ("xc", 512), ("yc", 512),
         ("ka", 128), ("va", 128), ("fl", 128))
WORK_COLS = sum(w for _, w in _WORK)


def _offsets(table):
    off, out = 0, {}
    for n, w in table:
        out[n] = (off, w)
        off += w
    return out


_ORIG_OFF = _offsets(_ORIG)
_WORK_OFF = _offsets(_WORK)

ADAM_LR = 0.001
ADAM_B1 = 0.9
ADAM_B2 = 0.999
ADAM_EPS = 1e-08
ADAM_WD = 0.01
ADAM_STEP = 10

VMEM_LIMIT = 62 * 1024 * 1024
FOX_TILE = 1408
LANE = 128


def _tile(n, cap, mult=LANE):
    if n <= cap:
        return n
    best = None
    for d in range(mult, cap + 1, mult):
        if n % d == 0:
            best = d
    assert best is not None, (n, cap, mult)
    return best


def _params(sem):
    return pltpu.CompilerParams(dimension_semantics=sem, vmem_limit_bytes=VMEM_LIMIT)


def _sigmoid(x):
    return 1.0 / (1.0 + jnp.exp(-x))


def _log_sigmoid(x):
    return jnp.minimum(x, 0.0) - jnp.log(1.0 + jnp.exp(-jnp.abs(x)))


def _neg_expm1(x):
    series = -x * (1.0 + x * (0.5 + x * (1.0 / 6.0 + x * (1.0 / 24.0 + x * (1.0 / 120.0)))))
    return jnp.where(x > -0.1, series, 1.0 - jnp.exp(x))


_GELU_C = math.sqrt(2.0 / math.pi)


def _gelu(x):
    return 0.5 * x * (1.0 + jnp.tanh(_GELU_C * (x + 0.044715 * x * x * x)))


def _gelu_grad(x):
    t = jnp.tanh(_GELU_C * (x + 0.044715 * x * x * x))
    return 0.5 * (1.0 + t) + 0.5 * x * (1.0 - t * t) * _GELU_C * (1.0 + 3.0 * 0.044715 * x * x)


def _xch_ops(src_ref, out_ref, send_sems, recv_sems, local_sem, gather):
    x, y, c = lax.axis_index("x"), lax.axis_index("y"), lax.axis_index("c")
    me = 4 * x + 2 * y + c

    def copy(r, mine):
        px, py, pc = x ^ ((r >> 2) & 1), y ^ ((r >> 1) & 1), c ^ (r & 1)
        pid = 4 * px + 2 * py + pc
        return pltpu.make_async_remote_copy(
            src_ref=src_ref if gather else src_ref.at[pid], dst_ref=out_ref.at[me if mine else pid],
            send_sem=send_sems.at[r - 1], recv_sem=recv_sems.at[r - 1],
            device_id=(px, py, pc), device_id_type=pl.DeviceIdType.MESH)

    def local():
        return pltpu.make_async_copy(src_ref if gather else src_ref.at[me], out_ref.at[me], local_sem)

    def start():
        local().start()
        for r in range(1, N_DEV):
            copy(r, True).start()

    def wait():
        for r in range(1, N_DEV):
            copy(r, False).wait_recv()
        for r in range(1, N_DEV):
            copy(r, True).wait_send()
        local().wait()

    return start, wait


_XCH_SEMS = [pltpu.SemaphoreType.DMA((N_DEV - 1,)), pltpu.SemaphoreType.DMA((N_DEV - 1,)), pltpu.SemaphoreType.DMA]
_ANY = pl.BlockSpec(memory_space=pl.ANY)


def _xch_shape(src, gather):
    return jax.ShapeDtypeStruct((N_DEV,) + tuple(src.shape if gather else src.shape[1:]), src.dtype)


def _exchange(src, gather, name):
    def body(src_ref, out_ref, send_sems, recv_sems, local_sem):
        start, wait = _xch_ops(src_ref, out_ref, send_sems, recv_sems, local_sem, gather)
        start()
        wait()

    return pl.pallas_call(body, name=name, in_specs=[_ANY], out_specs=_ANY, out_shape=_xch_shape(src, gather),
                          scratch_shapes=list(_XCH_SEMS))(src)


def _pcall(body, name, grid, in_specs, out_specs, out_shape, scratch_shapes, sem, args, host=()):
    if not host:
        outs = pl.pallas_call(body, name=name, grid=grid, in_specs=in_specs, out_specs=out_specs, out_shape=out_shape,
                              scratch_shapes=scratch_shapes, compiler_params=_params(sem))(*args)
        return outs, []
    n_in, n_out, n_scr, n_x = len(args), len(out_shape), len(scratch_shapes), len(host)

    def wrapped(*refs):
        ins, xin = refs[:n_in], refs[n_in:n_in + n_x]
        outs, xout = refs[n_in + n_x:n_in + n_x + n_out], refs[n_in + n_x + n_out:n_in + 2 * n_x + n_out]
        scr = refs[n_in + 2 * n_x + n_out:n_in + 2 * n_x + n_out + n_scr]
        sems = refs[n_in + 2 * n_x + n_out + n_scr:]
        first = functools.reduce(jnp.logical_and, [pl.program_id(k) == 0 for k in range(len(grid))])
        last = functools.reduce(jnp.logical_and, [pl.program_id(k) == grid[k] - 1 for k in range(len(grid))])
        ops = [_xch_ops(xin[k], xout[k], *sems[3 * k:3 * k + 3], host[k][1]) for k in range(n_x)]

        @pl.when(first)
        def _():
            for start, _ in ops:
                start()

        body(*ins, *outs, *scr)

        @pl.when(last)
        def _():
            for _, wait in ops:
                wait()

    res = pl.pallas_call(
        wrapped, name=name, grid=grid, in_specs=list(in_specs) + [_ANY] * n_x,
        out_specs=list(out_specs) + [_ANY] * n_x,
        out_shape=list(out_shape) + [_xch_shape(s, g) for s, g in host],
        scratch_shapes=list(scratch_shapes) + list(_XCH_SEMS) * n_x,
        compiler_params=_params(("arbitrary",) * len(grid)))(*args, *[s for s, _ in host])
    return res[:n_out], res[n_out:]


def _mm(a, b, mode, name, res=None, exact=False, tm_cap=1408, tn_cap=512, tk_cap=1408, host=None):
    if mode == "nn":
        (M, K), (K2, N) = a.shape, b.shape
    elif mode == "nt":
        (M, K), (N, K2) = a.shape, b.shape
    else:
        (K, M), (K2, N) = a.shape, b.shape
    assert K == K2, (a.shape, b.shape, mode)
    tm, tn, tk = _tile(M, tm_cap), _tile(N, tn_cap), _tile(K, tk_cap)
    nk = K // tk
    a_spec = {"nn": pl.BlockSpec((tm, tk), lambda i, j, k: (i, k)),
              "nt": pl.BlockSpec((tm, tk), lambda i, j, k: (i, k)),
              "tn": pl.BlockSpec((tk, tm), lambda i, j, k: (k, i))}[mode]
    b_spec = {"nn": pl.BlockSpec((tk, tn), lambda i, j, k: (k, j)),
              "nt": pl.BlockSpec((tn, tk), lambda i, j, k: (j, k)),
              "tn": pl.BlockSpec((tk, tn), lambda i, j, k: (k, j))}[mode]
    o_spec = pl.BlockSpec((tm, tn), lambda i, j, k: (i, j))
    has_res = res is not None

    def body(*refs):
        if has_res:
            a_ref, b_ref, r_ref, o_ref, acc_ref = refs
        else:
            a_ref, b_ref, o_ref, acc_ref = refs
        k = pl.program_id(2)

        @pl.when(k == 0)
        def _():
            acc_ref[...] = jnp.zeros_like(acc_ref)

        x, y = a_ref[...], b_ref[...]
        if exact:
            x, y, prec = x.astype(F32), y.astype(F32), lax.Precision.HIGHEST
        else:
            x, y, prec = x.astype(BF16), y.astype(BF16), None
        if mode == "tn":
            x = x.T
        dims = (((1,), (1,)), ((), ())) if mode == "nt" else (((1,), (0,)), ((), ()))
        acc_ref[...] += lax.dot_general(x, y, dims, precision=prec, preferred_element_type=F32)

        @pl.when(k == nk - 1)
        def _():
            if has_res:
                o_ref[...] = acc_ref[...] + r_ref[...]
            else:
                o_ref[...] = acc_ref[...]

    in_specs = [a_spec, b_spec] + ([o_spec] if has_res else [])
    args = (a, b) + ((res,) if has_res else ())
    (out,), xouts = _pcall(body, name, (M // tm, N // tn, nk), in_specs, [o_spec],
                           [jax.ShapeDtypeStruct((M, N), F32)], [pltpu.VMEM((tm, tn), F32)],
                           ("parallel", "parallel", "arbitrary"), args, host or ())
    return out if host is None else (out, xouts)


def _rowwise(fn, name, n_rows, tr, row_ins, const_ins, row_outs, red_outs=()):
    n_row_in, n_const, n_row_out, n_red = len(row_ins), len(const_ins), len(row_outs), len(red_outs)
    in_specs = []
    for arr, w, idx in row_ins:
        width = arr.shape[1] if w is None else w
        in_specs.append(pl.BlockSpec((tr, width), functools.partial(lambda i, j: (i, j), j=idx)))
    for c in const_ins:
        in_specs.append(pl.BlockSpec(c.shape, lambda i: (0, 0)))
    out_shape = [jax.ShapeDtypeStruct((n_rows, c), dt) for c, dt in row_outs]
    out_shape += [jax.ShapeDtypeStruct((1, c), F32) for c in red_outs]
    out_specs = [pl.BlockSpec((tr, c), lambda i: (i, 0)) for c, _ in row_outs]
    out_specs += [pl.BlockSpec((1, c), lambda i: (0, 0)) for c in red_outs]

    def body(*refs):
        i = pl.program_id(0)
        rows = [r[...] for r in refs[:n_row_in]]
        consts = [r[...] for r in refs[n_row_in:n_row_in + n_const]]
        outs = refs[n_row_in + n_const:]
        row_vals, red_vals = fn(i, rows, consts)
        for k in range(n_row_out):
            outs[k][...] = row_vals[k].astype(outs[k].dtype)
        if n_red:
            @pl.when(i == 0)
            def _():
                for k in range(n_red):
                    outs[n_row_out + k][...] = jnp.zeros_like(outs[n_row_out + k])

            for k in range(n_red):
                outs[n_row_out + k][...] += red_vals[k]

    res = pl.pallas_call(
        body, name=name, grid=(n_rows // tr,), in_specs=in_specs, out_specs=out_specs, out_shape=out_shape,
        compiler_params=_params(("arbitrary",)))(*[a for a, _, _ in row_ins], *const_ins)
    return res


def _row_ids(i, tr):
    return i * tr + lax.broadcasted_iota(jnp.int32, (tr, 1), 0)


def _colsum(x):
    return jnp.sum(x, axis=0, keepdims=True)


def _scan(a, b, name, b2=None, reverse=False):
    n_rows, c = b.shape
    tr = _tile(n_rows, 384)
    nblk = n_rows // tr
    has_a = a is not None
    has_b2 = b2 is not None

    def body(*refs):
        o_ref, carry = refs[-2:]
        ins = list(refs[:-2])
        a_ref = ins.pop(0) if has_a else None
        b_ref = ins.pop(0)

        @pl.when(pl.program_id(0) == 0)
        def _():
            carry[...] = jnp.zeros_like(carry)

        rows = lax.broadcasted_iota(jnp.int32, (tr, c), 0)
        bv = b_ref[...] + ins[0][...] if has_b2 else b_ref[...]
        av = a_ref[...] if has_a else None
        s = 1
        while s < tr:
            keep = rows < tr - s if reverse else rows >= s
            shift = tr - s if reverse else s
            b_sh = jnp.where(keep, pltpu.roll(bv, shift, 0), 0.0)
            if has_a:
                a_sh = jnp.where(keep, pltpu.roll(av, shift, 0), 1.0)
                bv = av * b_sh + bv
                av = av * a_sh
            else:
                bv = b_sh + bv
            s *= 2
        h = av * carry[0:1, :] + bv if has_a else carry[0:1, :] + bv
        o_ref[...] = h
        edge = 0 if reverse else tr - 1
        carry[...] = jnp.broadcast_to(h[edge:edge + 1, :], carry.shape)

    spec = pl.BlockSpec((tr, c), (lambda i: (nblk - 1 - i, 0)) if reverse else (lambda i: (i, 0)))
    args = ((a,) if has_a else ()) + (b,) + ((b2,) if has_b2 else ())
    return pl.pallas_call(
        body, name=name, grid=(n_rows // tr,), in_specs=[spec] * len(args), out_specs=spec,
        out_shape=jax.ShapeDtypeStruct((n_rows, c), F32), scratch_shapes=[pltpu.VMEM((8, c), F32)],
        compiler_params=_params(("arbitrary",)))(*args)


def _shift_down(x, s, fill=0.0):
    pad = jnp.full((s,) + x.shape[1:], fill, x.dtype)
    return jnp.concatenate([pad, x[:-s]], axis=0)


def _shift_up(x, s):
    pad = jnp.zeros((s,) + x.shape[1:], x.dtype)
    return jnp.concatenate([x[s:], pad], axis=0)


def _rev_scan(a, g, name, g2=None):
    return _scan(None if a is None else _shift_up(a, 1), g, name, b2=g2, reverse=True)


def _swa_masks(blk, n_pad):
    qi = lax.broadcasted_iota(jnp.int32, (BLOCK, 2 * BLOCK), 0)
    ki = lax.broadcasted_iota(jnp.int32, (BLOCK, 2 * BLOCK), 1)
    dist = qi + BLOCK - ki
    key_abs = (blk - 1) * BLOCK + ki
    return (dist >= 0) & (dist < BLOCK) & (key_abs >= n_pad)


_QA_COL = _WORK_OFF["qa"][0] // ATT_W
_KA_COL, _VA_COL = (_WORK_OFF[n][0] // LANE for n in ("ka", "va"))


def _swa_strips(prev_ref, cur_ref):
    x = jnp.concatenate([prev_ref[...], cur_ref[...]], axis=0)
    return [x.astype(BF16), pltpu.roll(x, HEAD_DIM, 1).astype(BF16)]


def _swa_swapped(h):
    return 0 if h % 2 == h // SWA_GROUP else 1


def _swa_specs(tq, nsub):
    prev = lambda col: pl.BlockSpec((BLOCK, LANE), lambda i: (jnp.maximum(i * nsub - 1, 0), col))
    cur = lambda col: pl.BlockSpec((tq, LANE), lambda i: (i, col))
    return [pl.BlockSpec((tq, ATT_W), lambda i: (i, _QA_COL)), prev(_KA_COL), cur(_KA_COL), prev(_VA_COL), cur(_VA_COL),
            pl.BlockSpec((SWA_Q_HEADS, BLOCK, 2 * BLOCK), lambda i: (0, 0, 0)),
            pl.BlockSpec((SWA_Q_HEADS, 1, LANE), lambda i: (0, 0, 0))]


def _swa_fwd(proj, bias, sinks, n_pad):
    Tp = proj.shape[0]
    tq = _tile(Tp, 384)
    nsub = tq // BLOCK

    def body(q_ref, kp_ref, kc_ref, vp_ref, vc_ref, b_ref, s_ref, o_ref, lse_ref):
        i = pl.program_id(0)
        low = _low_lanes()
        ks, vs = _swa_strips(kp_ref, kc_ref), _swa_strips(vp_ref, vc_ref)
        for j in range(nsub):
            rows, keys = slice(j * BLOCK, (j + 1) * BLOCK), slice(j * BLOCK, (j + 2) * BLOCK)
            mask = _swa_masks(i * nsub + j, n_pad)
            for hb in range(SWA_Q_HEADS // 2):
                pair = slice(hb * LANE, (hb + 1) * LANE)
                qh = _split_pair(q_ref[rows, pair] * SCALE, low)
                outs = []
                for e in range(2):
                    h = 2 * hb + e
                    kk, vv = ks[_swa_swapped(h)][keys], vs[_swa_swapped(h)][keys]
                    s = lax.dot_general(qh[e], kk, (((1,), (1,)), ((), ())), preferred_element_type=F32)
                    s = jnp.where(mask, s + b_ref[h], NEG_INF)
                    sink = s_ref[h, :, 0:1]
                    m = jnp.maximum(jnp.max(s, axis=1, keepdims=True), sink)
                    p = jnp.exp(s - m)
                    denom = jnp.sum(p, axis=1, keepdims=True) + jnp.exp(sink - m)
                    outs.append(jnp.dot((p / denom).astype(BF16), vv, preferred_element_type=F32))
                    lse_ref[h, rows, :] = m + jnp.log(denom)
                o_ref[rows, pair] = jnp.where(low, outs[0], outs[1])

    return pl.pallas_call(
        body, name="swa_fwd", grid=(Tp // tq,), in_specs=_swa_specs(tq, nsub),
        out_specs=[pl.BlockSpec((tq, ATT_W), lambda i: (i, 0)), pl.BlockSpec((SWA_Q_HEADS, tq, 1), lambda i: (0, i, 0))],
        out_shape=[jax.ShapeDtypeStruct((Tp, ATT_W), F32), jax.ShapeDtypeStruct((SWA_Q_HEADS, Tp, 1), F32)],
        compiler_params=_params(("arbitrary",)))(proj, proj, proj, proj, proj, bias, sinks)


def _swa_bwd(proj, bias, sinks, o, lse, do, n_pad, host=()):
    Tp = proj.shape[0]
    tq = _tile(Tp, 384)
    nsub = tq // BLOCK
    nq = Tp // tq
    strip = BLOCK + tq

    def body(q_ref, kp_ref, kc_ref, vp_ref, vc_ref, b_ref, s_ref, o_ref, lse_ref, do_ref,
             dq_ref, dk_ref, dv_ref, db_ref, ds_ref, dk_sw, dv_sw):
        i = pl.program_id(0)

        @pl.when(i == 0)
        def _():
            db_ref[...] = jnp.zeros_like(db_ref)
            ds_ref[...] = jnp.zeros_like(ds_ref)

        for ref in (dk_ref, dv_ref, dk_sw, dv_sw):
            ref[...] = jnp.zeros_like(ref)
        low = _low_lanes()
        ks, vs = _swa_strips(kp_ref, kc_ref), _swa_strips(vp_ref, vc_ref)
        for j in range(nsub):
            rows, keys = slice(j * BLOCK, (j + 1) * BLOCK), slice(j * BLOCK, (j + 2) * BLOCK)
            mask = _swa_masks(i * nsub + j, n_pad)
            for hb in range(SWA_Q_HEADS // 2):
                pair = slice(hb * LANE, (hb + 1) * LANE)
                qh = _split_pair(q_ref[rows, pair] * SCALE, low)
                dof = do_ref[rows, pair]
                prod = dof * o_ref[rows, pair]
                doh = _split_pair(dof, low)
                dqs = []
                for e in range(2):
                    h = 2 * hb + e
                    sw = _swa_swapped(h)
                    kk, vv = ks[sw][keys], vs[sw][keys]
                    delta = jnp.sum(jnp.where(low, prod, 0.0) if e == 0 else jnp.where(low, 0.0, prod),
                                    axis=1, keepdims=True)
                    lse_h = lse_ref[h, rows, :]
                    s = lax.dot_general(qh[e], kk, (((1,), (1,)), ((), ())), preferred_element_type=F32)
                    s = jnp.where(mask, s + b_ref[h], NEG_INF)
                    p = jnp.exp(s - lse_h)
                    p_sink = jnp.exp(s_ref[h, :, 0:1] - lse_h)
                    dp = lax.dot_general(doh[e], vv, (((1,), (1,)), ((), ())), preferred_element_type=F32)
                    ds = p * (dp - delta)
                    db_ref[h] += ds
                    ds_ref[h] += jnp.broadcast_to(-jnp.sum(p_sink * delta, axis=0, keepdims=True), (1, LANE))
                    dqs.append(jnp.dot(ds.astype(BF16), kk, preferred_element_type=F32))
                    dk_h = jnp.dot(ds.T.astype(BF16), qh[e], preferred_element_type=F32)
                    dv_h = jnp.dot(p.T.astype(BF16), doh[e], preferred_element_type=F32)
                    if sw:
                        dk_sw[keys, :] += dk_h
                        dv_sw[keys, :] += dv_h
                    else:
                        dk_ref[0, keys, :] += dk_h
                        dv_ref[0, keys, :] += dv_h
                dq_ref[rows, pair] = jnp.where(low, dqs[0], dqs[1]) * SCALE
        dk_ref[0] += pltpu.roll(dk_sw[...], HEAD_DIM, 1)
        dv_ref[0] += pltpu.roll(dv_sw[...], HEAD_DIM, 1)

    wide = pl.BlockSpec((tq, ATT_W), lambda i: (i, 0))
    sspec = pl.BlockSpec((1, strip, LANE), lambda i: (i, 0, 0))
    return _pcall(
        body, "swa_bwd", (nq,),
        _swa_specs(tq, nsub) + [wide, pl.BlockSpec((SWA_Q_HEADS, tq, 1), lambda i: (0, i, 0)), wide],
        [wide, sspec, sspec, pl.BlockSpec((SWA_Q_HEADS, BLOCK, 2 * BLOCK), lambda i: (0, 0, 0)),
         pl.BlockSpec((SWA_Q_HEADS, 1, LANE), lambda i: (0, 0, 0))],
        [jax.ShapeDtypeStruct((Tp, ATT_W), F32), jax.ShapeDtypeStruct((nq, strip, LANE), F32),
         jax.ShapeDtypeStruct((nq, strip, LANE), F32),
         jax.ShapeDtypeStruct((SWA_Q_HEADS, BLOCK, 2 * BLOCK), F32), jax.ShapeDtypeStruct((SWA_Q_HEADS, 1, LANE), F32)],
        [pltpu.VMEM((strip, LANE), F32), pltpu.VMEM((strip, LANE), F32)], ("arbitrary",),
        (proj, proj, proj, proj, proj, bias, sinks, o, lse, do), host)


def _fox_scores(qb, kk, cq, ck, diag, tq):
    s = lax.dot_general(qb, kk, (((1,), (1,)), ((), ())), preferred_element_type=F32)
    s = s + cq - ck
    return _fox_causal(s, tq) if diag else s


def _fox_causal(s, tq):
    r = lax.broadcasted_iota(jnp.int32, (tq, tq), 0)
    c = lax.broadcasted_iota(jnp.int32, (tq, tq), 1)
    return jnp.where(c <= r, s, NEG_INF)


_FOX_PAIRS = FOX_HEADS // 2
_QF_COL, _KF_COL, _VF_COL = (_WORK_OFF[n][0] // LANE for n in ("qf", "kf", "vf"))


def _low_lanes():
    return lax.broadcasted_iota(jnp.int32, (1, LANE), 1) < HEAD_DIM


def _split_pair(x, low):
    return [jnp.where(low, x, 0.0).astype(BF16), jnp.where(low, 0.0, x).astype(BF16)]


def _fox_fwd(proj, ccol, crow, tq, host=(), prefetch=True):
    Tp = proj.shape[0]
    nq = Tp // tq

    def body(q_ref, k_ref, v_ref, cc_ref, cr_ref, o_ref, lse_ref):
        qi = pl.program_id(1)
        low = _low_lanes()
        qh = _split_pair(q_ref[...] * SCALE, low)
        cq = [cc_ref[0], cc_ref[1]]

        def scores(kb):
            kk = k_ref[pl.ds(pl.multiple_of(kb * tq, tq), tq), :].astype(BF16)
            return [_fox_scores(qh[e], kk, cq[e], cr_ref[e, kb], False, tq) for e in range(2)]

        def softmax_pv(kb, s, carry, diag):
            vv = v_ref[pl.ds(pl.multiple_of(kb * tq, tq), tq), :].astype(BF16)
            stats, upd = [], []
            for e in range(2):
                m, l = carry[2 * e], carry[2 * e + 1]
                se = _fox_causal(s[e], tq) if diag else s[e]
                m_new = jnp.maximum(m, jnp.max(se, axis=1, keepdims=True))
                alpha = jnp.exp(m - m_new)
                p = jnp.exp(se - m_new)
                stats += [m_new, alpha * l + jnp.sum(p, axis=1, keepdims=True)]
                upd.append(alpha * carry[4] + jnp.dot(p.astype(BF16), vv, preferred_element_type=F32))
            return (*stats, jnp.where(low, upd[0], upd[1]))

        col = lambda val: jnp.full((tq, 1), val, F32)
        init = (col(NEG_INF), col(0.0), col(NEG_INF), col(0.0), jnp.zeros((tq, LANE), F32))
        if prefetch:
            def step(kb, c):
                s_next = scores(kb + 1)
                return (*softmax_pv(kb, c[5:], c[:5], False), *s_next)

            c = lax.fori_loop(0, qi, step, (*init, *scores(0)))
            m0, l0, m1, l1, acc = softmax_pv(qi, c[5:], c[:5], True)
        else:
            c = lax.fori_loop(0, qi, lambda kb, c: softmax_pv(kb, scores(kb), c, False), init)
            m0, l0, m1, l1, acc = softmax_pv(qi, scores(qi), c, True)
        o_ref[...] = jnp.where(low, acc / l0, acc / l1)
        lse_ref[0] = m0 + jnp.log(l0)
        lse_ref[1] = m1 + jnp.log(l1)

    cspec = pl.BlockSpec((2, tq, 1), lambda h, i: (h, i, 0))
    return _pcall(
        body, "fox_fwd", (_FOX_PAIRS, nq),
        [pl.BlockSpec((tq, LANE), lambda h, i: (i, _QF_COL + h)), pl.BlockSpec((Tp, LANE), lambda h, i: (0, _KF_COL + h)),
         pl.BlockSpec((Tp, LANE), lambda h, i: (0, _VF_COL + h)), cspec,
         pl.BlockSpec((2, nq, 1, tq), lambda h, i: (h, 0, 0, 0))],
        [pl.BlockSpec((tq, LANE), lambda h, i: (i, h)), cspec],
        [jax.ShapeDtypeStruct((Tp, FOX_HEADS * HEAD_DIM), F32), jax.ShapeDtypeStruct((FOX_HEADS, Tp, 1), F32)],
        [], ("parallel", "arbitrary"), (proj, proj, proj, ccol, crow), host)


def _fox_bwd_dq(proj, ccol, crow, o, lse, do, tq, host=()):
    Tp = proj.shape[0]
    nq = Tp // tq

    def body(q_ref, k_ref, v_ref, cc_ref, cr_ref, o_ref, lse_ref, do_ref, dq_ref, dl_ref, dc_ref):
        qi = pl.program_id(1)
        low = _low_lanes()
        qh = _split_pair(q_ref[...] * SCALE, low)
        dof = do_ref[...]
        prod = dof * o_ref[...]
        delta = [jnp.sum(jnp.where(low, prod, 0.0), axis=1, keepdims=True),
                 jnp.sum(jnp.where(low, 0.0, prod), axis=1, keepdims=True)]
        doh = _split_pair(dof, low)
        cq = [cc_ref[0], cc_ref[1]]
        lse = [lse_ref[0], lse_ref[1]]

        def step(kb, carry, diag):
            rows = pl.ds(pl.multiple_of(kb * tq, tq), tq)
            kk, vv = k_ref[rows, :].astype(BF16), v_ref[rows, :].astype(BF16)
            dqs, dcs = [], []
            for e in range(2):
                s = _fox_scores(qh[e], kk, cq[e], cr_ref[e, kb], diag, tq)
                p = jnp.exp(s - lse[e])
                dp = lax.dot_general(doh[e], vv, (((1,), (1,)), ((), ())), preferred_element_type=F32)
                ds = p * (dp - delta[e])
                dqs.append(jnp.dot(ds.astype(BF16), kk, preferred_element_type=F32))
                dcs.append(carry[1 + e] + jnp.sum(ds, axis=1, keepdims=True))
            return (carry[0] + jnp.where(low, dqs[0], dqs[1]), *dcs)

        init = (jnp.zeros((tq, LANE), F32), jnp.zeros((tq, 1), F32), jnp.zeros((tq, 1), F32))
        dq, dc0, dc1 = step(qi, lax.fori_loop(0, qi, functools.partial(step, diag=False), init), True)
        dq_ref[...] = dq * SCALE
        dl_ref[0], dl_ref[1] = delta
        dc_ref[0], dc_ref[1] = dc0, dc1

    cspec = pl.BlockSpec((2, tq, 1), lambda h, i: (h, i, 0))
    pspec = pl.BlockSpec((tq, LANE), lambda h, i: (i, h))
    stat = jax.ShapeDtypeStruct((FOX_HEADS, Tp, 1), F32)
    return _pcall(
        body, "fox_bwd_dq", (_FOX_PAIRS, nq),
        [pl.BlockSpec((tq, LANE), lambda h, i: (i, _QF_COL + h)), pl.BlockSpec((Tp, LANE), lambda h, i: (0, _KF_COL + h)),
         pl.BlockSpec((Tp, LANE), lambda h, i: (0, _VF_COL + h)), cspec,
         pl.BlockSpec((2, nq, 1, tq), lambda h, i: (h, 0, 0, 0)), pspec, cspec, pspec],
        [pspec, cspec, cspec], [jax.ShapeDtypeStruct((Tp, FOX_HEADS * HEAD_DIM), F32), stat, stat],
        [], ("parallel", "arbitrary"), (proj, proj, proj, ccol, crow, o, lse, do), host)


def _fox_bwd_dkv(proj, ccol, crow, lse, delta, do, tq, host=()):
    Tp = proj.shape[0]
    nq = Tp // tq
    pair = lambda z: z.reshape(_FOX_PAIRS, 2, Tp).transpose(0, 2, 1)
    stats = jnp.concatenate([pair(ccol), pair(lse), pair(delta)], axis=-1).reshape(_FOX_PAIRS, nq, tq, 6)

    def body(q_ref, do_ref, st_ref, k_ref, v_ref, cr_ref, dk_ref, dv_ref, dc_ref):
        ki = pl.program_id(1)
        low = _low_lanes()
        kk, vv = k_ref[...].astype(BF16), v_ref[...].astype(BF16)

        def step(qb_i, carry, diag):
            rows = pl.ds(pl.multiple_of(qb_i * tq, tq), tq)
            qh = _split_pair(q_ref[rows, :] * SCALE, low)
            doh = _split_pair(do_ref[rows, :], low)
            st = st_ref[0, qb_i]
            dk, dv = carry[0], carry[1]
            dcs = []
            for e in range(2):
                s = _fox_scores(qh[e], kk, st[:, e:e + 1], cr_ref[e, 0], diag, tq)
                p = jnp.exp(s - st[:, 2 + e:3 + e])
                dp = lax.dot_general(doh[e], vv, (((1,), (1,)), ((), ())), preferred_element_type=F32)
                ds = p * (dp - st[:, 4 + e:5 + e])
                dv = dv + jnp.dot(p.T.astype(BF16), doh[e], preferred_element_type=F32)
                dk = dk + jnp.dot(ds.T.astype(BF16), qh[e], preferred_element_type=F32)
                dcs.append(carry[2 + e] - jnp.sum(ds, axis=0, keepdims=True))
            return (dk, dv, *dcs)

        init = (jnp.zeros((tq, LANE), F32), jnp.zeros((tq, LANE), F32), jnp.zeros((1, tq), F32), jnp.zeros((1, tq), F32))
        dk, dv, dc0, dc1 = lax.fori_loop(ki + 1, nq, functools.partial(step, diag=False), step(ki, init, True))
        dk_ref[...] = dk
        dv_ref[...] = dv
        dc_ref[0, 0] = dc0
        dc_ref[1, 0] = dc1

    rspec = pl.BlockSpec((2, 1, 1, tq), lambda h, i: (h, i, 0, 0))
    pspec = pl.BlockSpec((tq, LANE), lambda h, i: (i, h))
    wide = jax.ShapeDtypeStruct((Tp, FOX_HEADS * HEAD_DIM), F32)
    return _pcall(
        body, "fox_bwd_dkv", (_FOX_PAIRS, nq),
        [pl.BlockSpec((Tp, LANE), lambda h, i: (0, _QF_COL + h)), pl.BlockSpec((Tp, LANE), lambda h, i: (0, h)),
         pl.BlockSpec((1, nq, tq, 6), lambda h, i: (h, 0, 0, 0)),
         pl.BlockSpec((tq, LANE), lambda h, i: (i, _KF_COL + h)), pl.BlockSpec((tq, LANE), lambda h, i: (i, _VF_COL + h)),
         rspec],
        [pspec, pspec, rspec], [wide, wide, jax.ShapeDtypeStruct((FOX_HEADS, nq, 1, tq), F32)],
        [], ("parallel", "arbitrary"), (proj, do, stats, proj, proj, crow), host)


def _adam(gparts, w, m, v, name):
    P, R, C = gparts.shape
    tr = _tile(R, 256, 8)

    def body(g_ref, w_ref, m_ref, v_ref, go_ref, d_ref, mo_ref, vo_ref):
        g = g_ref[0]
        for p in range(1, P):
            g = g + g_ref[p]
        m2 = ADAM_B1 * m_ref[...] + (1.0 - ADAM_B1) * g
        v2 = ADAM_B2 * v_ref[...] + (1.0 - ADAM_B2) * (g * g)
        m_hat = m2 / (1.0 - ADAM_B1 ** ADAM_STEP)
        v_hat = v2 / (1.0 - ADAM_B2 ** ADAM_STEP)
        go_ref[...] = g
        d_ref[...] = -ADAM_LR * (m_hat / (jnp.sqrt(v_hat) + ADAM_EPS) + ADAM_WD * w_ref[...])
        mo_ref[...] = m2
        vo_ref[...] = v2

    spec = pl.BlockSpec((tr, C), lambda i: (i, 0))
    shp = jax.ShapeDtypeStruct((R, C), F32)
    return pl.pallas_call(
        body, name=name, grid=(R // tr,),
        in_specs=[pl.BlockSpec((P, tr, C), lambda i: (0, i, 0)), spec, spec, spec],
        out_specs=[spec] * 4, out_shape=[shp] * 4, compiler_params=_params(("parallel",)))(gparts, w, m, v)


def _sum_parts(gparts, name):
    P, R, C = gparts.shape
    tr = _tile(R, 256, 8)

    def body(g_ref, o_ref):
        g = g_ref[0]
        for p in range(1, P):
            g = g + g_ref[p]
        o_ref[...] = g

    return pl.pallas_call(
        body, name=name, grid=(R // tr,), in_specs=[pl.BlockSpec((P, tr, C), lambda i: (0, i, 0))],
        out_specs=pl.BlockSpec((tr, C), lambda i: (i, 0)), out_shape=jax.ShapeDtypeStruct((R, C), F32),
        compiler_params=_params(("parallel",)))(gparts)


def _pack(pieces, width, row_mult):
    flat = jnp.concatenate([p.reshape(-1) for p in pieces])
    n = flat.shape[0]
    rows = -(-n // width)
    rows = -(-rows // row_mult) * row_mult
    return jnp.pad(flat, (0, rows * width - n)).reshape(rows, width)


def _unpack(flat2d, shapes, lead=None):
    out, off = [], 0
    if lead is None:
        flat = flat2d.reshape(-1)
        for s in shapes:
            n = int(np.prod(s))
            out.append(flat[off:off + n].reshape(s))
            off += n
    else:
        flat = flat2d.reshape(lead, -1)
        for s in shapes:
            n = int(np.prod(s))
            out.append(flat[:, off:off + n].reshape((lead,) + tuple(s)))
            off += n
    return out


def _col_pieces(off, width, shard_w, fetch):
    out, c, end = [], off, off + width
    while c < end:
        d = c // shard_w
        hi = min(end, (d + 1) * shard_w)
        out.append(fetch(d, c - d * shard_w, hi - d * shard_w))
        c = hi
    return out


def _work_cols_of_orig(lo, hi, fetch):
    out = []
    for name, ow in _ORIG:
        o = _ORIG_OFF[name][0]
        a, b = max(lo, o), min(hi, o + ow)
        if a < b:
            w0 = _WORK_OFF[name][0]
            out.append(fetch(w0 + a - o, w0 + b - o))
    return out


def _t5_bucket_np(dist):
    max_exact = REL_BUCKETS // 2
    d = np.maximum(dist, 0)
    scaled = (np.log(np.maximum(d, 1).astype(np.float32) / np.float32(max_exact))
              / np.float32(math.log(REL_MAX_DIST / max_exact))).astype(np.float32)
    large = np.minimum(max_exact + (scaled * np.float32(REL_BUCKETS - max_exact)).astype(np.int32), REL_BUCKETS - 1)
    return np.where(d < max_exact, d, large)


def _bucket_onehot():
    q_idx = np.arange(BLOCK)[:, None]
    k_idx = np.arange(2 * BLOCK)[None, :]
    bucket = _t5_bucket_np(q_idx + BLOCK - k_idx).reshape(-1)
    oh = np.zeros((LANE, BLOCK * 2 * BLOCK), np.float32)
    oh[bucket, np.arange(bucket.shape[0])] = 1.0
    return oh


def kernel(x, meta_tokens, rel_bias_table, norm_mix, w_in, swa_sinks, fox_forget_bias, conv_w, conv_b, lru_w_r, lru_b_r, lru_w_i, lru_b_i, lru_lambda, w_branch, w_out, norm_ffn, w_ffn_in, w_ffn_out, norm_final, loss_target, m_meta_tokens, m_rel_bias_table, m_norm_mix, m_w_in, m_swa_sinks, m_fox_forget_bias, m_conv_w, m_conv_b, m_lru_w_r, m_lru_b_r, m_lru_w_i, m_lru_b_i, m_lru_lambda, m_w_branch, m_w_out, m_norm_ffn, m_w_ffn_in, m_w_ffn_out, m_norm_final, v_meta_tokens, v_rel_bias_table, v_norm_mix, v_w_in, v_swa_sinks, v_fox_forget_bias, v_conv_w, v_conv_b, v_lru_w_r, v_lru_b_r, v_lru_w_i, v_lru_b_i, v_lru_lambda, v_w_branch, v_w_out, v_norm_ffn, v_w_ffn_in, v_w_ffn_out, v_norm_final):
    S = x.shape[1]
    T = N_META + S
    n_pad = (-T) % BLOCK
    Tp = T + n_pad
    first = n_pad + N_META
    TR = _tile(Tp, 384)
    TQ = _tile(Tp, 384)
    TQF = _tile(Tp, FOX_TILE)
    me =4 * lax.axis_index("x") + 2 * lax.axis_index("y") + lax.axis_index("c")

    big_names = ("w_in", "w_ffn_in", "w_ffn_out", "w_branch", "w_out")
    big_w = dict(w_in=w_in, w_ffn_in=w_ffn_in, w_ffn_out=w_ffn_out, w_branch=w_branch, w_out=w_out)
    big_m = dict(w_in=m_w_in, w_ffn_in=m_w_ffn_in, w_ffn_out=m_w_ffn_out, w_branch=m_w_branch, w_out=m_w_out)
    big_v = dict(w_in=v_w_in, w_ffn_in=v_w_ffn_in, w_ffn_out=v_w_ffn_out, w_branch=v_w_branch, w_out=v_w_out)
    rows2d = lambda w: w.reshape(-1, w.shape[-1])
    in_shard_w = IN_COLS // N_DEV
    cat_axis = dict(w_in=-1, w_ffn_in=-1, w_ffn_out=0, w_branch=-1, w_out=0)

    def w_send(n, l):
        return rows2d(big_w[n][l]).astype(BF16)

    def w_build(n, g):
        shard = lambda d: g[d].reshape(big_w[n].shape[1:])
        if cat_axis[n] == 0:
            return g.reshape(-1, g.shape[-1])
        if n != "w_in":
            return jnp.concatenate([shard(d) for d in range(N_DEV)], axis=cat_axis[n])
        parts = []
        for name, width in _WORK:
            off, ow = _ORIG_OFF[name]
            parts += _col_pieces(off, ow, in_shard_w, lambda d, lo, hi: shard(d)[:, lo:hi])
            if width > ow:
                parts.append(jnp.zeros((D_MODEL, width - ow), BF16))
        return jnp.concatenate(parts, axis=-1)

    W_in, W_ffn_in, W_ffn_out, W_branch, W_out = ([None] * DEPTH for _ in range(5))
    W_in[0] = w_build("w_in", _exchange(w_send("w_in", 0), True, "gather_w_in"))

    small_sh = _exchange(_pack([meta_tokens, conv_w], D_MODEL, 8), True, "gather_small")
    g_meta, g_conv_w = _unpack(small_sh, [meta_tokens.shape, conv_w.shape], lead=N_DEV)
    meta_full = g_meta.transpose(1, 0, 2).reshape(N_META, D_MODEL)
    conv_w_full = g_conv_w.transpose(1, 2, 0, 3).reshape(DEPTH, CONV_WIDTH, LRU_WIDTH)

    onehot = jnp.asarray(_bucket_onehot())
    table_t = jnp.pad(rel_bias_table.T, ((0, 0), (0, LANE - REL_BUCKETS)))
    bias = _mm(table_t, onehot, "nn", "swa_bias", exact=True, tn_cap=4096).reshape(SWA_Q_HEADS, BLOCK, 2 * BLOCK)

    def dense_blocks(w):
        rows = []
        for b in range(LRU_BLOCKS):
            rows.append(jnp.pad(w[b], ((0, 0), (b * LRU_BLOCK_DIM, (LRU_BLOCKS - 1 - b) * LRU_BLOCK_DIM))))
        return jnp.concatenate(rows, axis=0)

    h = jnp.concatenate([jnp.zeros((n_pad, D_MODEL), F32), meta_full, x[0]], axis=0)
    saved = []

    for l in range(DEPTH):
        sv = {"h": h}
        g_mix = norm_mix[l][None, :]

        def norm_fwd(i, rows, consts):
            xx = rows[0]
            return [xx * lax.rsqrt(jnp.mean(xx * xx, axis=1, keepdims=True) + EPS) * consts[0]], []

        (u,) = _rowwise(norm_fwd, "norm_fwd", Tp, TR, [(h, None, 0)], [g_mix], [(D_MODEL, BF16)])
        proj, got = _mm(u, W_in[l], "nn", "proj", tn_cap=384,
                        host=[(w_send("w_branch", l), True), (w_send("w_out", l), True)])
        W_branch[l], W_out[l] = w_build("w_branch", got[0]), w_build("w_out", got[1])
        sv["u"], sv["proj"] = u, proj

        def col(name):
            off, w = _WORK_OFF[name]
            return (proj, w, off // w)

        def cols(name):
            off, w = _WORK_OFF[name]
            return proj[:, off:off + w]

        sinks_l =jnp.broadcast_to(swa_sinks[l][:, None, None], (SWA_Q_HEADS, 1, LANE))
        o_a, lse_a = _swa_fwd(proj, bias, sinks_l, n_pad)
        sv.update(sinks=sinks_l, lse_a=lse_a, o_a=o_a)

        fbias = jnp.pad(fox_forget_bias[l], (0, LANE - FOX_HEADS))[None, :]

        def logf_fwd(i, rows, consts):
            return [_log_sigmoid(rows[0] + consts[0])], []

        (logf,) = _rowwise(logf_fwd, "logf_fwd", Tp, TR, [col("fl")], [fbias], [(LANE, F32)])
        cum = _scan(None, logf, "cumsum")
        cum_h = cum[:, :FOX_HEADS].T
        ccol = cum_h[:, :, None]
        crow = jnp.where(jnp.arange(Tp)[None, :] < n_pad, -NEG_INF, cum_h).reshape(FOX_HEADS, Tp // TQF, 1, TQF)
        ride = [("w_ffn_in", l), ("w_ffn_out", l)] + ([("w_in", l + 1)] if l + 1 < DEPTH else [])
        (o_f, lse_f), got = _fox_fwd(proj, ccol, crow, TQF, host=[(w_send(n, ll), True) for n, ll in ride],
                                     prefetch=TQF <= 384)
        W_ffn_in[l], W_ffn_out[l] = w_build("w_ffn_in", got[0]), w_build("w_ffn_out", got[1])
        if l + 1 < DEPTH:
            W_in[l + 1] = w_build("w_in", got[2])
        sv.update(fbias=fbias, ccol=ccol, crow=crow, lse_f=lse_f, o_f=o_f)

        xc = cols("xc")
        xs = [_shift_down(xc, CONV_WIDTH - 1 - i) for i in range(CONV_WIDTH - 1)] + [xc]
        cw = conv_w_full[l]
        cb = conv_b[l][None, :]

        def conv_fwd(i, rows, consts):
            w, b = consts
            acc = rows[0] * w[0:1, :]
            for t in range(1, CONV_WIDTH):
                acc = acc + rows[t] * w[t:t + 1, :]
            return [acc + b], []

        (xconv,) = _rowwise(conv_fwd, "conv_fwd", Tp, TR, [(a, None, 0) for a in xs], [cw, cb], [(LRU_WIDTH, F32)])
        w_ri = jnp.concatenate([dense_blocks(lru_w_r[l]), dense_blocks(lru_w_i[l])], axis=1)
        pre = _mm(xconv, w_ri, "nn", "lru_gates")
        lru_consts = [lru_b_r[l][None, :], lru_b_i[l][None, :], lru_lambda[l][None, :]]

        def lru_fwd(i, rows, consts):
            pr, xv = rows
            b_r, b_i, lam = consts
            r = _sigmoid(pr[:, :LRU_WIDTH] + b_r)
            gi = _sigmoid(pr[:, LRU_WIDTH:] + b_i)
            log_a = LRU_C * r * _log_sigmoid(lam)
            valid = (_row_ids(i, TR) >= n_pad).astype(F32)
            inp = jnp.sqrt(_neg_expm1(2.0 * log_a)) * (gi * xv) * valid
            return [jnp.exp(log_a), inp], []

        a_dec, inp = _rowwise(lru_fwd, "lru_fwd", Tp, TR, [(pre, None, 0), (xconv, None, 0)], lru_consts,
                              [(LRU_WIDTH, F32), (LRU_WIDTH, F32)])
        hs = _scan(a_dec, inp, "lru_scan")

        def oc_fwd(i, rows, consts):
            return [rows[0] * _gelu(rows[1])], []

        (o_c,) = _rowwise(oc_fwd, "oc_fwd", Tp, TR, [(hs, None, 0), col("yc")], [], [(LRU_WIDTH, BF16)])
        sv.update(xs=xs, cw=cw, xconv=xconv, w_ri=w_ri, pre=pre, lru_consts=lru_consts, a_dec=a_dec, hs=hs, o_c=o_c)

        ba = _mm(o_a, W_branch[l][0], "nn", "branch")
        bf = _mm(o_f, W_branch[l][1], "nn", "branch")
        bc = _mm(o_c, W_branch[l][2], "nn", "branch")

        def merge_fwd(i, rows, consts):
            g, b0, b1, b2 = rows
            valid = (_row_ids(i, BLOCK) >= n_pad).astype(F32)
            mg = (_sigmoid(g[:, :D_MODEL]) * b0 + _sigmoid(g[:, D_MODEL:2 * D_MODEL]) * b1
                  + _sigmoid(g[:, 2 * D_MODEL:]) * b2)
            return [mg * valid], []

        (merged,) = _rowwise(merge_fwd, "merge_fwd", Tp, BLOCK,
                             [col("gates"), (ba, None, 0), (bf, None, 0), (bc, None, 0)], [], [(D_MODEL, BF16)])
        h1 = _mm(merged, W_out[l], "nn", "out_proj", res=h)
        sv.update(ba=ba, bf=bf, bc=bc, merged=merged, h1=h1)

        g_ffn = norm_ffn[l][None, :]
        (u2,) = _rowwise(norm_fwd, "norm_fwd", Tp, TR, [(h1, None, 0)], [g_ffn], [(D_MODEL, BF16)])
        ff = _mm(u2, W_ffn_in[l], "nn", "ffn_in")

        def act_fwd(i, rows, consts):
            gate, up = rows
            return [gate * _sigmoid(gate) * up], []

        (act,) = _rowwise(act_fwd, "act_fwd", Tp, TR, [(ff, D_FF, 0), (ff, D_FF, 1)], [], [(D_FF, BF16)])
        h = _mm(act, W_ffn_out[l], "nn", "ffn_out", res=h1)
        sv.update(u2=u2, ff=ff, act=act)
        saved.append(sv)

    tgt = jnp.concatenate([jnp.zeros((first, D_MODEL), F32), loss_target[0]], axis=0)
    g_fin = norm_final[None, :]

    def head(i, rows, consts):
        xx, tg = rows
        g = consts[0]
        valid = (_row_ids(i, TR) >= first).astype(F32)
        rstd = lax.rsqrt(jnp.mean(xx * xx, axis=1, keepdims=True) + EPS)
        xhat = xx * rstd
        err = (xhat * g - tg) * valid
        loss_rows = 0.5 * jnp.mean(err * err, axis=1, keepdims=True)
        dy = err * (1.0 / D_MODEL)
        dxhat = dy * g
        dx = rstd * (dxhat - xhat * jnp.mean(dxhat * xhat, axis=1, keepdims=True))
        return [dx], [jnp.broadcast_to(_colsum(loss_rows), (1, LANE)), _colsum(dy * xhat)]

    dh, loss_part, d_norm_final = _rowwise(head, "loss_head", Tp, TR, [(h, None, 0), (tgt, None, 0)], [g_fin],
                                           [(D_MODEL, F32)], [LANE, D_MODEL])
    loss = lax.psum(loss_part[0, 0], ("x", "y", "c"))

    def norm_bwd_call(xin, g, du, dres):
        def norm_bwd(i, rows, consts):
            xx, dd, rr = rows
            gg = consts[0]
            valid = (_row_ids(i, TR) >= n_pad).astype(F32)
            rstd = lax.rsqrt(jnp.mean(xx * xx, axis=1, keepdims=True) + EPS)
            xhat = xx * rstd
            dxhat = dd * gg
            dx = rstd * (dxhat - xhat * jnp.mean(dxhat * xhat, axis=1, keepdims=True))
            return [rr + dx * valid], [_colsum(dd * xhat)]

        return _rowwise(norm_bwd, "norm_bwd", Tp, TR, [(xin, None, 0), (du, None, 0), (dres, None, 0)], [g],
                        [(D_MODEL, F32)], [D_MODEL])

    grads = {k: [None] * DEPTH for k in ("norm_mix", "w_in", "swa_sinks", "fox_forget_bias", "conv_w", "conv_b",
                                         "lru_w_r", "lru_b_r", "lru_w_i", "lru_b_i", "lru_lambda", "w_branch",
                                         "w_out", "norm_ffn", "w_ffn_in", "w_ffn_out")}
    d_bias_total = None

    def g_send(n, g):
        if n == "w_in":
            pieces = [jnp.concatenate(_work_cols_of_orig(d * in_shard_w, (d + 1) * in_shard_w,
                                                         lambda lo, hi: g[:, lo:hi]), axis=-1) for d in range(N_DEV)]
        elif cat_axis[n] == 0:
            return g.reshape(N_DEV, -1, g.shape[-1])
        else:
            ax = cat_axis[n] % g.ndim
            w = big_w[n].shape[1 + ax]
            pieces = [lax.slice_in_dim(g, d * w, (d + 1) * w, axis=ax) for d in range(N_DEV)]
        return jnp.stack([rows2d(p) for p in pieces])

    recv = {n: [None] * DEPTH for n in big_names}
    for l in reversed(range(DEPTH)):
        sv = saved[l]
        proj = sv["proj"]

        def col(name):
            off, w = _WORK_OFF[name]
            return (proj, w, off // w)

        dh2 = dh
        d_act = _mm(dh2, W_ffn_out[l], "nt", "d_act")
        grads["w_ffn_out"][l] = _mm(sv["act"], dh2, "tn", "dw_ffn_out")

        def act_bwd(i, rows, consts):
            gate, up, da = rows
            sg = _sigmoid(gate)
            d_gate = da * up * (sg * (1.0 + gate * (1.0 - sg)))
            d_up = da * (gate * sg)
            return [jnp.concatenate([d_gate, d_up], axis=1)], []

        (dff,) = _rowwise(act_bwd, "act_bwd", Tp, BLOCK, [(sv["ff"], D_FF, 0), (sv["ff"], D_FF, 1), (d_act, None, 0)],
                          [], [(2 * D_FF, BF16)])
        grads["w_ffn_in"][l] = _mm(sv["u2"], dff, "tn", "dw_ffn_in")
        du2 = _mm(dff, W_ffn_in[l], "nt", "du2")
        dh1, dg = norm_bwd_call(sv["h1"], norm_ffn[l][None, :], du2, dh2)
        grads["norm_ffn"][l] = dg[0]

        dmerged = _mm(dh1, W_out[l], "nt", "d_merged")
        grads["w_out"][l] = _mm(sv["merged"], dh1, "tn", "dw_out")

        def merge_bwd(i, rows, consts):
            g, b0, b1, b2, dm = rows
            dm = dm * (_row_ids(i, BLOCK) >= n_pad).astype(F32)
            outs, dgs = [], []
            for k, bk in enumerate((b0, b1, b2)):
                sg = _sigmoid(g[:, k * D_MODEL:(k + 1) * D_MODEL])
                outs.append(dm * sg)
                dgs.append(dm * bk * sg * (1.0 - sg))
            return outs + [jnp.concatenate(dgs, axis=1)], []

        d_ba, d_bf, d_bc, d_gates = _rowwise(
            merge_bwd, "merge_bwd", Tp, BLOCK,
            [col("gates"), (sv["ba"], None, 0), (sv["bf"], None, 0), (sv["bc"], None, 0), (dmerged, None, 0)], [],
            [(D_MODEL, BF16)] * 3 + [(3 * D_MODEL, BF16)])
        grads["w_branch"][l] = jnp.stack([_mm(sv["o_a"], d_ba, "tn", "dw_branch"),
                                          _mm(sv["o_f"], d_bf, "tn", "dw_branch"),
                                          _mm(sv["o_c"], d_bc, "tn", "dw_branch")])
        do_a = _mm(d_ba, W_branch[l][0], "nt", "d_branch")
        do_f = _mm(d_bf, W_branch[l][1], "nt", "d_branch")
        do_c = _mm(d_bc, W_branch[l][2], "nt", "d_branch")

        def oc_bwd(i, rows, consts):
            d, hv, yv = rows
            return [d * _gelu(yv), d * hv * _gelu_grad(yv)], []

        d_hs, d_yc = _rowwise(oc_bwd, "oc_bwd", Tp, TR, [(do_c, None, 0), (sv["hs"], None, 0), col("yc")], [],
                              [(LRU_WIDTH, F32)] * 2)
        d_state = _rev_scan(sv["a_dec"], d_hs, "lru_scan")
        hs_prev = _shift_down(sv["hs"], 1)

        def lru_bwd(i, rows, consts):
            dH, hp, pr, xv = rows
            b_r, b_i, lam = consts
            valid = (_row_ids(i, TR) >= n_pad).astype(F32)
            r = _sigmoid(pr[:, :LRU_WIDTH] + b_r)
            gi = _sigmoid(pr[:, LRU_WIDTH:] + b_i)
            lsl = _log_sigmoid(lam)
            log_a = LRU_C * r * lsl
            a = jnp.exp(log_a)
            one_m_e = _neg_expm1(2.0 * log_a)
            mult = jnp.sqrt(one_m_e)
            d_inp = dH * valid
            d_mult = d_inp * gi * xv
            d_gi = d_inp * mult * xv
            d_x = d_inp * mult * gi
            d_log_a = dH * hp * a - d_mult * (1.0 - one_m_e) / mult
            d_pre_r = d_log_a * (LRU_C * lsl) * r * (1.0 - r)
            d_pre_i = d_gi * gi * (1.0 - gi)
            d_lam = _colsum(d_log_a * (LRU_C * r)) * _sigmoid(-lam)
            return [jnp.concatenate([d_pre_r, d_pre_i], axis=1), d_x], [_colsum(d_pre_r), _colsum(d_pre_i), d_lam]

        d_pre, d_xdir, d_b_r, d_b_i, d_lam = _rowwise(
            lru_bwd, "lru_bwd", Tp, TR, [(d_state, None, 0), (hs_prev, None, 0), (sv["pre"], None, 0),
                                         (sv["xconv"], None, 0)], sv["lru_consts"],
            [(2 * LRU_WIDTH, BF16), (LRU_WIDTH, F32)], [LRU_WIDTH] * 3)
        grads["lru_b_r"][l], grads["lru_b_i"][l], grads["lru_lambda"][l] = d_b_r[0], d_b_i[0], d_lam[0]
        d_w_ri = _mm(sv["xconv"], d_pre, "tn", "dw_lru_gates")

        def diag_blocks(wd):
            w4 = wd.reshape(LRU_BLOCKS, LRU_BLOCK_DIM, LRU_BLOCKS, LRU_BLOCK_DIM)
            return jnp.stack([w4[b, :, b, :] for b in range(LRU_BLOCKS)])

        grads["lru_w_r"][l] = diag_blocks(d_w_ri[:, :LRU_WIDTH])
        grads["lru_w_i"][l] = diag_blocks(d_w_ri[:, LRU_WIDTH:])
        d_conv = _mm(d_pre, sv["w_ri"], "nt", "d_xconv", res=d_xdir)
        dcs = [d_conv] + [_shift_up(d_conv, s) for s in range(1, CONV_WIDTH)]

        def conv_bwd(i, rows, consts):
            w = consts[0]
            d = rows[:CONV_WIDTH]
            xsh = rows[CONV_WIDTH:]
            dxc = d[0] * w[CONV_WIDTH - 1:CONV_WIDTH, :]
            for s in range(1, CONV_WIDTH):
                dxc = dxc + d[s] * w[CONV_WIDTH - 1 - s:CONV_WIDTH - s, :]
            return [dxc], [_colsum(xsh[t] * d[0]) for t in range(CONV_WIDTH)] + [_colsum(d[0])]

        res = _rowwise(conv_bwd, "conv_bwd", Tp, TR, [(a, None, 0) for a in dcs + sv["xs"]], [sv["cw"]],
                       [(LRU_WIDTH, F32)], [LRU_WIDTH] * (CONV_WIDTH + 1))
        d_xc = res[0]
        grads["conv_w"][l] = jnp.concatenate(res[1:1 + CONV_WIDTH], axis=0)
        grads["conv_b"][l] = res[1 + CONV_WIDTH][0]

        (dqf, delta_f, dccol), got = _fox_bwd_dq(
            proj, sv["ccol"], sv["crow"], sv["o_f"], sv["lse_f"], do_f, TQF,
            host=[(g_send("w_ffn_in", grads["w_ffn_in"][l]), False)])
        recv["w_ffn_in"][l] = got[0]
        ride = [("w_out", l), ("w_branch", l)] + ([("w_in", l + 1)] if l + 1 < DEPTH else [])
        (dkf, dvf, dcrow), got = _fox_bwd_dkv(
            proj, sv["ccol"], sv["crow"], sv["lse_f"], delta_f, do_f, TQF,
            host=[(g_send(n, grads[n][ll]), False) for n, ll in ride])
        for (n, ll), r in zip(ride, got):
            recv[n][ll] = r
        lanes = lambda z: jnp.pad(z.reshape(FOX_HEADS, Tp).T, ((0, 0), (0, LANE - FOX_HEADS)))
        dlogf = _rev_scan(None, lanes(dcrow), "cumsum_bwd", g2=lanes(dccol))

        def logf_bwd(i, rows, consts):
            dl, fl = rows
            valid = (_row_ids(i, TR) >= n_pad).astype(F32)
            lane_ok = (lax.broadcasted_iota(jnp.int32, (1, LANE), 1) < FOX_HEADS).astype(F32)
            dfl = dl * _sigmoid(-(fl + consts[0])) * valid * lane_ok
            return [dfl], [_colsum(dfl)]

        d_fl, d_fb = _rowwise(logf_bwd, "logf_bwd", Tp, TR, [(dlogf, None, 0), col("fl")], [sv["fbias"]],
                              [(LANE, F32)], [LANE])
        grads["fox_forget_bias"][l] = d_fb[0, :FOX_HEADS]

        (dqa, dk_strip, dv_strip, d_bias_l, d_sink), got = _swa_bwd(
            proj, bias, sv["sinks"], sv["o_a"], sv["lse_a"], do_a, n_pad,
            host=[(g_send("w_ffn_out", grads["w_ffn_out"][l]), False)])
        recv["w_ffn_out"][l] = got[0]
        grads["swa_sinks"][l] = d_sink[:, 0, 0]
        d_bias_total = d_bias_l if d_bias_total is None else d_bias_total + d_bias_l

        def kv_parts(strip):
            own = strip[:, BLOCK:, :].reshape(Tp, KV_W)
            nxt = jnp.pad(strip[1:, :BLOCK, :], ((0, 1), (TQ - BLOCK, 0), (0, 0))).reshape(Tp, KV_W)
            return [(own, None, 0), (nxt, None, 0)]

        def assemble(i, rows, consts):
            dg, dqa_, dqf_, dkf_, dvf_, dxc_, dyc_, dk0, dk1, dv0, dv1, dfl_ = rows
            parts = [dg, dqa_, dqf_, dkf_, dvf_, dxc_, dyc_, dk0 + dk1, dv0 + dv1, dfl_]
            return [jnp.concatenate([p.astype(BF16) for p in parts], axis=1)], []

        (dproj,) = _rowwise(
            assemble, "assemble_dproj", Tp, BLOCK,
            [(d_gates, None, 0), (dqa, None, 0), (dqf, None, 0), (dkf, None, 0), (dvf, None, 0), (d_xc, None, 0),
             (d_yc, None, 0)] + kv_parts(dk_strip) + kv_parts(dv_strip) + [(d_fl, None, 0)], [],
            [(WORK_COLS, BF16)])
        grads["w_in"][l] = _mm(sv["u"], dproj, "tn", "dw_in", tn_cap=384)
        if l == 0:
            du, got = _mm(dproj, W_in[l], "nt", "du", tk_cap=2176,
                          host=[(g_send("w_in", grads["w_in"][0]), False)])
            recv["w_in"][0] = got[0]
        else:
            du = _mm(dproj, W_in[l], "nt", "du", tk_cap=2176)
        dh, dg = norm_bwd_call(sv["h"], norm_mix[l][None, :], du, dh1)
        grads["norm_mix"][l] = dg[0]

    d_table = _mm(d_bias_total.reshape(SWA_Q_HEADS, BLOCK * 2 * BLOCK), onehot, "nt", "d_rel_table", exact=True,
                  tk_cap=4096)
    g_rel = d_table[:, :REL_BUCKETS].T
    g_meta_full = dh[n_pad:first]
    grad_x = dh[first:][None]

    stack = lambda k: jnp.stack(grads[k])
    big_res = {}
    for n in big_names:
        per_layer = [_adam(recv[n][l], rows2d(big_w[n][l]), rows2d(big_m[n][l]), rows2d(big_v[n][l]), "adam_" + n)
                     for l in range(DEPTH)]
        big_res[n] = [jnp.stack([per_layer[l][k] for l in range(DEPTH)]).reshape(big_w[n].shape) for k in range(4)]

    rep_names = ("rel_bias_table", "norm_mix", "swa_sinks", "fox_forget_bias", "conv_b", "lru_w_r", "lru_b_r",
                 "lru_w_i", "lru_b_i", "lru_lambda", "norm_ffn", "norm_final")
    rep_w = dict(rel_bias_table=rel_bias_table, norm_mix=norm_mix, swa_sinks=swa_sinks,
                 fox_forget_bias=fox_forget_bias, conv_b=conv_b, lru_w_r=lru_w_r, lru_b_r=lru_b_r, lru_w_i=lru_w_i,
                 lru_b_i=lru_b_i, lru_lambda=lru_lambda, norm_ffn=norm_ffn, norm_final=norm_final)
    rep_m = dict(rel_bias_table=m_rel_bias_table, norm_mix=m_norm_mix, swa_sinks=m_swa_sinks,
                 fox_forget_bias=m_fox_forget_bias, conv_b=m_conv_b, lru_w_r=m_lru_w_r, lru_b_r=m_lru_b_r,
                 lru_w_i=m_lru_w_i, lru_b_i=m_lru_b_i, lru_lambda=m_lru_lambda, norm_ffn=m_norm_ffn,
                 norm_final=m_norm_final)
    rep_v = dict(rel_bias_table=v_rel_bias_table, norm_mix=v_norm_mix, swa_sinks=v_swa_sinks,
                 fox_forget_bias=v_fox_forget_bias, conv_b=v_conv_b, lru_w_r=v_lru_w_r, lru_b_r=v_lru_b_r,
                 lru_w_i=v_lru_w_i, lru_b_i=v_lru_b_i, lru_lambda=v_lru_lambda, norm_ffn=v_norm_ffn,
                 norm_final=v_norm_final)
    rep_g = {n: (g_rel if n == "rel_bias_table" else d_norm_final[0] if n == "norm_final" else stack(n))
             for n in rep_names}
    small_g = [rep_g[n] for n in rep_names] + [g_meta_full, stack("conv_w")]
    small_shapes = [rep_w[n].shape for n in rep_names] + [(N_META, D_MODEL), (DEPTH, CONV_WIDTH, LRU_WIDTH)]
    gs_all = _exchange(_pack(small_g, D_MODEL, 8), True, "gather_small_grads")
    gs_sum = _unpack(_sum_parts(gs_all, "sum_small_grads"), small_shapes)
    gsum = dict(zip(rep_names, gs_sum[:len(rep_names)]))
    g_meta_sh = lax.dynamic_slice_in_dim(gs_sum[-2], me * (D_MODEL // N_DEV), D_MODEL // N_DEV, axis=1)
    g_convw_sh = lax.dynamic_slice_in_dim(gs_sum[-1], me * (LRU_WIDTH // N_DEV), LRU_WIDTH // N_DEV, axis=2)
    sm_names = rep_names + ("meta_tokens", "conv_w")
    sm_w = [rep_w[n] for n in rep_names] + [meta_tokens, conv_w]
    sm_m = [rep_m[n] for n in rep_names] + [m_meta_tokens, m_conv_w]
    sm_v = [rep_v[n] for n in rep_names] + [v_meta_tokens, v_conv_w]
    sm_g = [gsum[n] for n in rep_names] + [g_meta_sh, g_convw_sh]
    sm_shapes = [w.shape for w in sm_w]
    sm_out = [_unpack(o, sm_shapes) for o in _adam(_pack(sm_g, D_MODEL, 8)[None], _pack(sm_w, D_MODEL, 8),
                                                   _pack(sm_m, D_MODEL, 8), _pack(sm_v, D_MODEL, 8), "adam_small")]
    sm_res = {n: [sm_out[k][j] for k in range(4)] for j, n in enumerate(sm_names)}

    order = ("meta_tokens", "rel_bias_table", "norm_mix", "w_in", "swa_sinks", "fox_forget_bias", "conv_w", "conv_b",
             "lru_w_r", "lru_b_r", "lru_w_i", "lru_b_i", "lru_lambda", "w_branch", "w_out", "norm_ffn", "w_ffn_in",
             "w_ffn_out", "norm_final")
    allres = {**big_res, **sm_res}
    outs = [loss, grad_x]
    for k in range(4):
        outs += [allres[n][k] for n in order]
    return tuple(outs)
```

```python
import functools
import math

import numpy as np
import jax
import jax.numpy as jnp
from jax import lax
from jax.experimental import pallas as pl
from jax.experimental.pallas import tpu as pltpu

F32 = jnp.float32
BF16 = jnp.bfloat16

N_DEV = 8
D_MODEL = 1024
DEPTH = 4
HEAD_DIM = 64
N_META = 16
BLOCK = 128
NEG_INF = -1e30
SWA_Q_HEADS = 8
SWA_KV_HEADS = 2
SWA_GROUP = SWA_Q_HEADS // SWA_KV_HEADS
FOX_HEADS = 8
LRU_WIDTH = D_MODEL // 2
LRU_BLOCKS = 8
LRU_BLOCK_DIM = LRU_WIDTH // LRU_BLOCKS
CONV_WIDTH = 4
LRU_C = 8.0
REL_BUCKETS = 32
REL_MAX_DIST = 128
D_FF = 2816
N_BRANCH = 3
ATT_W = SWA_Q_HEADS * HEAD_DIM
KV_W = SWA_KV_HEADS * HEAD_DIM
SCALE = HEAD_DIM ** -0.5
EPS = 1e-6

_ORIG = (("qa", ATT_W), ("ka", KV_W), ("va", KV_W), ("qf", ATT_W), ("kf", ATT_W), ("vf", ATT_W),
         ("fl", FOX_HEADS), ("xc", LRU_WIDTH), ("yc", LRU_WIDTH), ("gates", N_BRANCH * D_MODEL))
IN_COLS = sum(w for _, w in _ORIG)
_WORK = (("gates", 3072), ("qa", 512), ("qf", 512), ("kf", 512), ("vf", 512), ("xc", 512), ("yc", 512),
         ("ka", 128), ("va", 128), ("fl", 128))
WORK_COLS = sum(w for _, w in _WORK)


def _offsets(table):
    off, out = 0, {}
    for n, w in table:
        out[n] = (off, w)
        off += w
    return out


_ORIG_OFF = _offsets(_ORIG)
_WORK_OFF = _offsets(_WORK)

ADAM_LR = 0.001
ADAM_B1 = 0.9
ADAM_B2 = 0.999
ADAM_EPS = 1e-08
ADAM_WD = 0.01
ADAM_STEP = 10

VMEM_LIMIT = 62 * 1024 * 1024
FOX_TILE = 1408
LANE = 128


def _tile(n, cap, mult=LANE):
    if n <= cap:
        return n
    best = None
    for d in range(mult, cap + 1, mult):
        if n % d == 0:
            best = d
    assert best is not None, (n, cap, mult)
    return best


def _params(sem):
    return pltpu.CompilerParams(dimension_semantics=sem, vmem_limit_bytes=VMEM_LIMIT)


def _sigmoid(x):
    return 1.0 / (1.0 + jnp.exp(-x))


def _log_sigmoid(x):
    return jnp.minimum(x, 0.0) - jnp.log(1.0 + jnp.exp(-jnp.abs(x)))


def _neg_expm1(x):
    series = -x * (1.0 + x * (0.5 + x * (1.0 / 6.0 + x * (1.0 / 24.0 + x * (1.0 / 120.0)))))
    return jnp.where(x > -0.1, series, 1.0 - jnp.exp(x))


_GELU_C = math.sqrt(2.0 / math.pi)


def _gelu(x):
    return 0.5 * x * (1.0 + jnp.tanh(_GELU_C * (x + 0.044715 * x * x * x)))


def _gelu_grad(x):
    t = jnp.tanh(_GELU_C * (x + 0.044715 * x * x * x))
    return 0.5 * (1.0 + t) + 0.5 * x * (1.0 - t * t) * _GELU_C * (1.0 + 3.0 * 0.044715 * x * x)


def _xch_ops(src_ref, out_ref, send_sems, recv_sems, local_sem, gather):
    x, y, c = lax.axis_index("x"), lax.axis_index("y"), lax.axis_index("c")
    me = 4 * x + 2 * y + c

    def copy(r, mine):
        px, py, pc = x ^ ((r >> 2) & 1), y ^ ((r >> 1) & 1), c ^ (r & 1)
        pid = 4 * px + 2 * py + pc
        return pltpu.make_async_remote_copy(
            src_ref=src_ref if gather else src_ref.at[pid], dst_ref=out_ref.at[me if mine else pid],
            send_sem=send_sems.at[r - 1], recv_sem=recv_sems.at[r - 1],
            device_id=(px, py, pc), device_id_type=pl.DeviceIdType.MESH)

    def local():
        return pltpu.make_async_copy(src_ref if gather else src_ref.at[me], out_ref.at[me], local_sem)

    def start():
        local().start()
        for r in range(1, N_DEV):
            copy(r, True).start()

    def wait():
        for r in range(1, N_DEV):
            copy(r, False).wait_recv()
        for r in range(1, N_DEV):
            copy(r, True).wait_send()
        local().wait()

    return start, wait


_XCH_SEMS = [pltpu.SemaphoreType.DMA((N_DEV - 1,)), pltpu.SemaphoreType.DMA((N_DEV - 1,)), pltpu.SemaphoreType.DMA]
_ANY = pl.BlockSpec(memory_space=pl.ANY)


def _xch_shape(src, gather):
    return jax.ShapeDtypeStruct((N_DEV,) + tuple(src.shape if gather else src.shape[1:]), src.dtype)


def _exchange(src, gather, name):
    def body(src_ref, out_ref, send_sems, recv_sems, local_sem):
        start, wait = _xch_ops(src_ref, out_ref, send_sems, recv_sems, local_sem, gather)
        start()
        wait()

    return pl.pallas_call(body, name=name, in_specs=[_ANY], out_specs=_ANY, out_shape=_xch_shape(src, gather),
                          scratch_shapes=list(_XCH_SEMS))(src)


def _pcall(body, name, grid, in_specs, out_specs, out_shape, scratch_shapes, sem, args, host=()):
    if not host:
        outs = pl.pallas_call(body, name=name, grid=grid, in_specs=in_specs, out_specs=out_specs, out_shape=out_shape,
                              scratch_shapes=scratch_shapes, compiler_params=_params(sem))(*args)
        return outs, []
    n_in, n_out, n_scr, n_x = len(args), len(out_shape), len(scratch_shapes), len(host)

    def wrapped(*refs):
        ins, xin = refs[:n_in], refs[n_in:n_in + n_x]
        outs, xout = refs[n_in + n_x:n_in + n_x + n_out], refs[n_in + n_x + n_out:n_in + 2 * n_x + n_out]
        scr = refs[n_in + 2 * n_x + n_out:n_in + 2 * n_x + n_out + n_scr]
        sems = refs[n_in + 2 * n_x + n_out + n_scr:]
        first = functools.reduce(jnp.logical_and, [pl.program_id(k) == 0 for k in range(len(grid))])
        last = functools.reduce(jnp.logical_and, [pl.program_id(k) == grid[k] - 1 for k in range(len(grid))])
        ops = [_xch_ops(xin[k], xout[k], *sems[3 * k:3 * k + 3], host[k][1]) for k in range(n_x)]

        @pl.when(first)
        def _():
            for start, _ in ops:
                start()

        body(*ins, *outs, *scr)

        @pl.when(last)
        def _():
            for _, wait in ops:
                wait()

    res = pl.pallas_call(
        wrapped, name=name, grid=grid, in_specs=list(in_specs) + [_ANY] * n_x,
        out_specs=list(out_specs) + [_ANY] * n_x,
        out_shape=list(out_shape) + [_xch_shape(s, g) for s, g in host],
        scratch_shapes=list(scratch_shapes) + list(_XCH_SEMS) * n_x,
        compiler_params=_params(("arbitrary",) * len(grid)))(*args, *[s for s, _ in host])
    return res[:n_out], res[n_out:]


def _mm(a, b, mode, name, res=None, exact=False, tm_cap=1408, tn_cap=512, tk_cap=1408, host=None, out_dtype=F32):
    if mode == "nn":
        (M, K), (K2, N) = a.shape, b.shape
    elif mode == "nt":
        (M, K), (N, K2) = a.shape, b.shape
    else:
        (K, M), (K2, N) = a.shape, b.shape
    assert K == K2, (a.shape, b.shape, mode)
    tm, tn, tk = _tile(M, tm_cap), _tile(N, tn_cap), _tile(K, tk_cap)
    nk = K // tk
    a_spec = {"nn": pl.BlockSpec((tm, tk), lambda i, j, k: (i, k)),
              "nt": pl.BlockSpec((tm, tk), lambda i, j, k: (i, k)),
              "tn": pl.BlockSpec((tk, tm), lambda i, j, k: (k, i))}[mode]
    b_spec = {"nn": pl.BlockSpec((tk, tn), lambda i, j, k: (k, j)),
              "nt": pl.BlockSpec((tn, tk), lambda i, j, k: (j, k)),
              "tn": pl.BlockSpec((tk, tn), lambda i, j, k: (k, j))}[mode]
    o_spec = pl.BlockSpec((tm, tn), lambda i, j, k: (i, j))
    has_res = res is not None

    def body(*refs):
        if has_res:
            a_ref, b_ref, r_ref, o_ref, acc_ref = refs
        else:
            a_ref, b_ref, o_ref, acc_ref = refs
        k = pl.program_id(2)

        @pl.when(k == 0)
        def _():
            acc_ref[...] = jnp.zeros_like(acc_ref)

        x, y = a_ref[...], b_ref[...]
        if exact:
            x, y, prec = x.astype(F32), y.astype(F32), lax.Precision.HIGHEST
        else:
            x, y, prec = x.astype(BF16), y.astype(BF16), None
        if mode == "tn":
            x = x.T
        dims = (((1,), (1,)), ((), ())) if mode == "nt" else (((1,), (0,)), ((), ()))
        acc_ref[...] += lax.dot_general(x, y, dims, precision=prec, preferred_element_type=F32)

        @pl.when(k == nk - 1)
        def _():
            if has_res:
                o_ref[...] = (acc_ref[...] + r_ref[...]).astype(out_dtype)
            else:
                o_ref[...] = acc_ref[...].astype(out_dtype)

    in_specs = [a_spec, b_spec] + ([o_spec] if has_res else [])
    args = (a, b) + ((res,) if has_res else ())
    (out,), xouts = _pcall(body, name, (M // tm, N // tn, nk), in_specs, [o_spec],
                           [jax.ShapeDtypeStruct((M, N), out_dtype)], [pltpu.VMEM((tm, tn), F32)],
                           ("parallel", "parallel", "arbitrary"), args, host or ())
    return out if host is None else (out, xouts)


def _rowwise(fn, name, n_rows, tr, row_ins, const_ins, row_outs, red_outs=()):
    n_row_in, n_const, n_row_out, n_red = len(row_ins), len(const_ins), len(row_outs), len(red_outs)
    in_specs = []
    for arr, w, idx in row_ins:
        width = arr.shape[1] if w is None else w
        in_specs.append(pl.BlockSpec((tr, width), functools.partial(lambda i, j: (i, j), j=idx)))
    for c in const_ins:
        in_specs.append(pl.BlockSpec(c.shape, lambda i: (0, 0)))
    out_shape = [jax.ShapeDtypeStruct((n_rows, c), dt) for c, dt in row_outs]
    out_shape += [jax.ShapeDtypeStruct((1, c), F32) for c in red_outs]
    out_specs = [pl.BlockSpec((tr, c), lambda i: (i, 0)) for c, _ in row_outs]
    out_specs += [pl.BlockSpec((1, c), lambda i: (0, 0)) for c in red_outs]

    def body(*refs):
        i = pl.program_id(0)
        rows = [r[...] for r in refs[:n_row_in]]
        consts = [r[...] for r in refs[n_row_in:n_row_in + n_const]]
        outs = refs[n_row_in + n_const:]
        row_vals, red_vals = fn(i, rows, consts)
        for k in range(n_row_out):
            outs[k][...] = row_vals[k].astype(outs[k].dtype)
        if n_red:
            @pl.when(i == 0)
            def _():
                for k in range(n_red):
                    outs[n_row_out + k][...] = jnp.zeros_like(outs[n_row_out + k])

            for k in range(n_red):
                outs[n_row_out + k][...] += red_vals[k]

    res = pl.pallas_call(
        body, name=name, grid=(n_rows // tr,), in_specs=in_specs, out_specs=out_specs, out_shape=out_shape,
        compiler_params=_params(("arbitrary",)))(*[a for a, _, _ in row_ins], *const_ins)
    return res


def _row_ids(i, tr):
    return i * tr + lax.broadcasted_iota(jnp.int32, (tr, 1), 0)


def _colsum(x):
    return jnp.sum(x, axis=0, keepdims=True)


def _scan(a, b, name, b2=None, reverse=False):
    n_rows, c = b.shape
    tr = _tile(n_rows, 384)
    nblk = n_rows // tr
    has_a = a is not None
    has_b2 = b2 is not None

    def body(*refs):
        o_ref, carry = refs[-2:]
        ins = list(refs[:-2])
        a_ref = ins.pop(0) if has_a else None
        b_ref = ins.pop(0)

        @pl.when(pl.program_id(0) == 0)
        def _():
            carry[...] = jnp.zeros_like(carry)

        rows = lax.broadcasted_iota(jnp.int32, (tr, c), 0)
        bv = b_ref[...] + ins[0][...] if has_b2 else b_ref[...]
        av = a_ref[...] if has_a else None
        s = 1
        while s < tr:
            keep = rows < tr - s if reverse else rows >= s
            shift = tr - s if reverse else s
            b_sh = jnp.where(keep, pltpu.roll(bv, shift, 0), 0.0)
            if has_a:
                a_sh = jnp.where(keep, pltpu.roll(av, shift, 0), 1.0)
                bv = av * b_sh + bv
                av = av * a_sh
            else:
                bv = b_sh + bv
            s *= 2
        h = av * carry[0:1, :] + bv if has_a else carry[0:1, :] + bv
        o_ref[...] = h
        edge = 0 if reverse else tr - 1
        carry[...] = jnp.broadcast_to(h[edge:edge + 1, :], carry.shape)

    spec = pl.BlockSpec((tr, c), (lambda i: (nblk - 1 - i, 0)) if reverse else (lambda i: (i, 0)))
    args = ((a,) if has_a else ()) + (b,) + ((b2,) if has_b2 else ())
    return pl.pallas_call(
        body, name=name, grid=(n_rows // tr,), in_specs=[spec] * len(args), out_specs=spec,
        out_shape=jax.ShapeDtypeStruct((n_rows, c), F32), scratch_shapes=[pltpu.VMEM((8, c), F32)],
        compiler_params=_params(("arbitrary",)))(*args)


def _shift_down(x, s, fill=0.0):
    pad = jnp.full((s,) + x.shape[1:], fill, x.dtype)
    return jnp.concatenate([pad, x[:-s]], axis=0)


def _shift_up(x, s):
    pad = jnp.zeros((s,) + x.shape[1:], x.dtype)
    return jnp.concatenate([x[s:], pad], axis=0)


def _rev_scan(a, g, name, g2=None):
    return _scan(None if a is None else _shift_up(a, 1), g, name, b2=g2, reverse=True)


def _swa_masks(blk, n_pad):
    qi = lax.broadcasted_iota(jnp.int32, (BLOCK, 2 * BLOCK), 0)
    ki = lax.broadcasted_iota(jnp.int32, (BLOCK, 2 * BLOCK), 1)
    dist = qi + BLOCK - ki
    key_abs = (blk - 1) * BLOCK + ki
    return (dist >= 0) & (dist < BLOCK) & (key_abs >= n_pad)


_QA_COL = _WORK_OFF["qa"][0] // ATT_W
_KA_COL, _VA_COL = (_WORK_OFF[n][0] // LANE for n in ("ka", "va"))


def _swa_strips(prev_ref, cur_ref):
    x = jnp.concatenate([prev_ref[...], cur_ref[...]], axis=0)
    return [x.astype(BF16), pltpu.roll(x, HEAD_DIM, 1).astype(BF16)]


def _swa_swapped(h):
    return 0 if h % 2 == h // SWA_GROUP else 1


def _swa_specs(tq, nsub):
    prev = lambda col: pl.BlockSpec((BLOCK, LANE), lambda i: (jnp.maximum(i * nsub - 1, 0), col))
    cur = lambda col: pl.BlockSpec((tq, LANE), lambda i: (i, col))
    return [pl.BlockSpec((tq, ATT_W), lambda i: (i, _QA_COL)), prev(_KA_COL), cur(_KA_COL), prev(_VA_COL), cur(_VA_COL),
            pl.BlockSpec((SWA_Q_HEADS, BLOCK, 2 * BLOCK), lambda i: (0, 0, 0)),
            pl.BlockSpec((SWA_Q_HEADS, 1, LANE), lambda i: (0, 0, 0))]


def _swa_fwd(proj, bias, sinks, n_pad, host=()):
    Tp = proj.shape[0]
    tq = _tile(Tp, 384)
    nsub = tq // BLOCK

    def body(q_ref, kp_ref, kc_ref, vp_ref, vc_ref, b_ref, s_ref, o_ref, lse_ref):
        i = pl.program_id(0)
        low = _low_lanes()
        ks, vs = _swa_strips(kp_ref, kc_ref), _swa_strips(vp_ref, vc_ref)
        for j in range(nsub):
            rows, keys = slice(j * BLOCK, (j + 1) * BLOCK), slice(j * BLOCK, (j + 2) * BLOCK)
            mask = _swa_masks(i * nsub + j, n_pad)
            for hb in range(SWA_Q_HEADS // 2):
                pair = slice(hb * LANE, (hb + 1) * LANE)
                qh = _split_pair(q_ref[rows, pair] * SCALE, low)
                outs = []
                for e in range(2):
                    h = 2 * hb + e
                    kk, vv = ks[_swa_swapped(h)][keys], vs[_swa_swapped(h)][keys]
                    s = lax.dot_general(qh[e], kk, (((1,), (1,)), ((), ())), preferred_element_type=F32)
                    s = jnp.where(mask, s + b_ref[h], NEG_INF)
                    sink = s_ref[h, :, 0:1]
                    m = jnp.maximum(jnp.max(s, axis=1, keepdims=True), sink)
                    p = jnp.exp(s - m)
                    denom = jnp.sum(p, axis=1, keepdims=True) + jnp.exp(sink - m)
                    outs.append(jnp.dot((p / denom).astype(BF16), vv, preferred_element_type=F32))
                    lse_ref[h, rows, :] = m + jnp.log(denom)
                o_ref[rows, pair] = jnp.where(low, outs[0], outs[1])

    return _pcall(
        body, "swa_fwd", (Tp // tq,), _swa_specs(tq, nsub),
        [pl.BlockSpec((tq, ATT_W), lambda i: (i, 0)), pl.BlockSpec((SWA_Q_HEADS, tq, 1), lambda i: (0, i, 0))],
        [jax.ShapeDtypeStruct((Tp, ATT_W), F32), jax.ShapeDtypeStruct((SWA_Q_HEADS, Tp, 1), F32)],
        [], ("arbitrary",), (proj, proj, proj, proj, proj, bias, sinks), host)


def _swa_bwd(proj, bias, sinks, o, lse, do, n_pad, host=()):
    Tp = proj.shape[0]
    tq = _tile(Tp, 384)
    nsub = tq // BLOCK
    nq = Tp // tq
    strip = BLOCK + tq

    def body(q_ref, kp_ref, kc_ref, vp_ref, vc_ref, b_ref, s_ref, o_ref, lse_ref, do_ref,
             dq_ref, dk_ref, dv_ref, db_ref, ds_ref, dk_sw, dv_sw):
        i = pl.program_id(0)

        @pl.when(i == 0)
        def _():
            db_ref[...] = jnp.zeros_like(db_ref)
            ds_ref[...] = jnp.zeros_like(ds_ref)

        for ref in (dk_ref, dv_ref, dk_sw, dv_sw):
            ref[...] = jnp.zeros_like(ref)
        low = _low_lanes()
        ks, vs = _swa_strips(kp_ref, kc_ref), _swa_strips(vp_ref, vc_ref)
        for j in range(nsub):
            rows, keys = slice(j * BLOCK, (j + 1) * BLOCK), slice(j * BLOCK, (j + 2) * BLOCK)
            mask = _swa_masks(i * nsub + j, n_pad)
            for hb in range(SWA_Q_HEADS // 2):
                pair = slice(hb * LANE, (hb + 1) * LANE)
                qh = _split_pair(q_ref[rows, pair] * SCALE, low)
                dof = do_ref[rows, pair]
                prod = dof * o_ref[rows, pair]
                doh = _split_pair(dof, low)
                dqs = []
                for e in range(2):
                    h = 2 * hb + e
                    sw = _swa_swapped(h)
                    kk, vv = ks[sw][keys], vs[sw][keys]
                    delta = jnp.sum(jnp.where(low, prod, 0.0) if e == 0 else jnp.where(low, 0.0, prod),
                                    axis=1, keepdims=True)
                    lse_h = lse_ref[h, rows, :]
                    s = lax.dot_general(qh[e], kk, (((1,), (1,)), ((), ())), preferred_element_type=F32)
                    s = jnp.where(mask, s + b_ref[h], NEG_INF)
                    p = jnp.exp(s - lse_h)
                    p_sink = jnp.exp(s_ref[h, :, 0:1] - lse_h)
                    dp = lax.dot_general(doh[e], vv, (((1,), (1,)), ((), ())), preferred_element_type=F32)
                    ds = p * (dp - delta)
                    db_ref[h] += ds
                    ds_ref[h] += jnp.broadcast_to(-jnp.sum(p_sink * delta, axis=0, keepdims=True), (1, LANE))
                    dqs.append(jnp.dot(ds.astype(BF16), kk, preferred_element_type=F32))
                    dk_h = jnp.dot(ds.T.astype(BF16), qh[e], preferred_element_type=F32)
                    dv_h = jnp.dot(p.T.astype(BF16), doh[e], preferred_element_type=F32)
                    if sw:
                        dk_sw[keys, :] += dk_h
                        dv_sw[keys, :] += dv_h
                    else:
                        dk_ref[0, keys, :] += dk_h
                        dv_ref[0, keys, :] += dv_h
                dq_ref[rows, pair] = jnp.where(low, dqs[0], dqs[1]) * SCALE
        dk_ref[0] += pltpu.roll(dk_sw[...], HEAD_DIM, 1)
        dv_ref[0] += pltpu.roll(dv_sw[...], HEAD_DIM, 1)

    wide = pl.BlockSpec((tq, ATT_W), lambda i: (i, 0))
    sspec = pl.BlockSpec((1, strip, LANE), lambda i: (i, 0, 0))
    return _pcall(
        body, "swa_bwd", (nq,),
        _swa_specs(tq, nsub) + [wide, pl.BlockSpec((SWA_Q_HEADS, tq, 1), lambda i: (0, i, 0)), wide],
        [wide, sspec, sspec, pl.BlockSpec((SWA_Q_HEADS, BLOCK, 2 * BLOCK), lambda i: (0, 0, 0)),
         pl.BlockSpec((SWA_Q_HEADS, 1, LANE), lambda i: (0, 0, 0))],
        [jax.ShapeDtypeStruct((Tp, ATT_W), F32), jax.ShapeDtypeStruct((nq, strip, LANE), F32),
         jax.ShapeDtypeStruct((nq, strip, LANE), F32),
         jax.ShapeDtypeStruct((SWA_Q_HEADS, BLOCK, 2 * BLOCK), F32), jax.ShapeDtypeStruct((SWA_Q_HEADS, 1, LANE), F32)],
        [pltpu.VMEM((strip, LANE), F32), pltpu.VMEM((strip, LANE), F32)], ("arbitrary",),
        (proj, proj, proj, proj, proj, bias, sinks, o, lse, do), host)


def _fox_scores(qb, kk, cq, ck, diag, tq):
    s = lax.dot_general(qb, kk, (((1,), (1,)), ((), ())), preferred_element_type=F32)
    s = s + cq - ck
    return _fox_causal(s, tq) if diag else s


def _fox_causal(s, tq):
    r = lax.broadcasted_iota(jnp.int32, (tq, tq), 0)
    c = lax.broadcasted_iota(jnp.int32, (tq, tq), 1)
    return jnp.where(c <= r, s, NEG_INF)


_FOX_PAIRS = FOX_HEADS // 2
_QF_COL, _KF_COL, _VF_COL = (_WORK_OFF[n][0] // LANE for n in ("qf", "kf", "vf"))


def _low_lanes():
    return lax.broadcasted_iota(jnp.int32, (1, LANE), 1) < HEAD_DIM


def _split_pair(x, low):
    return [jnp.where(low, x, 0.0).astype(BF16), jnp.where(low, 0.0, x).astype(BF16)]


def _fox_fwd(proj, ccol, crow, tq, host=(), prefetch=True):
    Tp = proj.shape[0]
    nq = Tp // tq

    def body(q_ref, k_ref, v_ref, cc_ref, cr_ref, o_ref, lse_ref):
        qi = pl.program_id(1)
        low = _low_lanes()
        qh = _split_pair(q_ref[...] * SCALE, low)
        cq = [cc_ref[0], cc_ref[1]]

        def scores(kb):
            kk = k_ref[pl.ds(pl.multiple_of(kb * tq, tq), tq), :].astype(BF16)
            return [_fox_scores(qh[e], kk, cq[e], cr_ref[e, kb], False, tq) for e in range(2)]

        def softmax_pv(kb, s, carry, diag):
            vv = v_ref[pl.ds(pl.multiple_of(kb * tq, tq), tq), :].astype(BF16)
            stats, upd = [], []
            for e in range(2):
                m, l = carry[2 * e], carry[2 * e + 1]
                se = _fox_causal(s[e], tq) if diag else s[e]
                m_new = jnp.maximum(m, jnp.max(se, axis=1, keepdims=True))
                alpha = jnp.exp(m - m_new)
                p = jnp.exp(se - m_new)
                stats += [m_new, alpha * l + jnp.sum(p, axis=1, keepdims=True)]
                upd.append(alpha * carry[4] + jnp.dot(p.astype(BF16), vv, preferred_element_type=F32))
            return (*stats, jnp.where(low, upd[0], upd[1]))

        col = lambda val: jnp.full((tq, 1), val, F32)
        init = (col(NEG_INF), col(0.0), col(NEG_INF), col(0.0), jnp.zeros((tq, LANE), F32))
        if prefetch:
            def step(kb, c):
                s_next = scores(kb + 1)
                return (*softmax_pv(kb, c[5:], c[:5], False), *s_next)

            c = lax.fori_loop(0, qi, step, (*init, *scores(0)))
            m0, l0, m1, l1, acc = softmax_pv(qi, c[5:], c[:5], True)
        else:
            c = lax.fori_loop(0, qi, lambda kb, c: softmax_pv(kb, scores(kb), c, False), init)
            m0, l0, m1, l1, acc = softmax_pv(qi, scores(qi), c, True)
        o_ref[...] = jnp.where(low, acc / l0, acc / l1)
        lse_ref[0] = m0 + jnp.log(l0)
        lse_ref[1] = m1 + jnp.log(l1)

    cspec = pl.BlockSpec((2, tq, 1), lambda h, i: (h, i, 0))
    return _pcall(
        body, "fox_fwd", (_FOX_PAIRS, nq),
        [pl.BlockSpec((tq, LANE), lambda h, i: (i, _QF_COL + h)), pl.BlockSpec((Tp, LANE), lambda h, i: (0, _KF_COL + h)),
         pl.BlockSpec((Tp, LANE), lambda h, i: (0, _VF_COL + h)), cspec,
         pl.BlockSpec((2, nq, 1, tq), lambda h, i: (h, 0, 0, 0))],
        [pl.BlockSpec((tq, LANE), lambda h, i: (i, h)), cspec],
        [jax.ShapeDtypeStruct((Tp, FOX_HEADS * HEAD_DIM), F32), jax.ShapeDtypeStruct((FOX_HEADS, Tp, 1), F32)],
        [], ("parallel", "arbitrary"), (proj, proj, proj, ccol, crow), host)


def _fox_bwd_dq(proj, ccol, crow, o, lse, do, tq, host=()):
    Tp = proj.shape[0]
    nq = Tp // tq

    def body(q_ref, k_ref, v_ref, cc_ref, cr_ref, o_ref, lse_ref, do_ref, dq_ref, dl_ref, dc_ref):
        qi = pl.program_id(1)
        low = _low_lanes()
        qh = _split_pair(q_ref[...] * SCALE, low)
        dof = do_ref[...]
        prod = dof * o_ref[...]
        delta = [jnp.sum(jnp.where(low, prod, 0.0), axis=1, keepdims=True),
                 jnp.sum(jnp.where(low, 0.0, prod), axis=1, keepdims=True)]
        doh = _split_pair(dof, low)
        cq = [cc_ref[0], cc_ref[1]]
        lse = [lse_ref[0], lse_ref[1]]

        def step(kb, carry, diag):
            rows = pl.ds(pl.multiple_of(kb * tq, tq), tq)
            kk, vv = k_ref[rows, :].astype(BF16), v_ref[rows, :].astype(BF16)
            dqs, dcs = [], []
            for e in range(2):
                s = _fox_scores(qh[e], kk, cq[e], cr_ref[e, kb], diag, tq)
                p = jnp.exp(s - lse[e])
                dp = lax.dot_general(doh[e], vv, (((1,), (1,)), ((), ())), preferred_element_type=F32)
                ds = p * (dp - delta[e])
                dqs.append(jnp.dot(ds.astype(BF16), kk, preferred_element_type=F32))
                dcs.append(carry[1 + e] + jnp.sum(ds, axis=1, keepdims=True))
            return (carry[0] + jnp.where(low, dqs[0], dqs[1]), *dcs)

        init = (jnp.zeros((tq, LANE), F32), jnp.zeros((tq, 1), F32), jnp.zeros((tq, 1), F32))
        dq, dc0, dc1 = step(qi, lax.fori_loop(0, qi, functools.partial(step, diag=False), init), True)
        dq_ref[...] = dq * SCALE
        dl_ref[0], dl_ref[1] = delta
        dc_ref[0], dc_ref[1] = dc0, dc1

    cspec = pl.BlockSpec((2, tq, 1), lambda h, i: (h, i, 0))
    pspec = pl.BlockSpec((tq, LANE), lambda h, i: (i, h))
    stat = jax.ShapeDtypeStruct((FOX_HEADS, Tp, 1), F32)
    return _pcall(
        body, "fox_bwd_dq", (_FOX_PAIRS, nq),
        [pl.BlockSpec((tq, LANE), lambda h, i: (i, _QF_COL + h)), pl.BlockSpec((Tp, LANE), lambda h, i: (0, _KF_COL + h)),
         pl.BlockSpec((Tp, LANE), lambda h, i: (0, _VF_COL + h)), cspec,
         pl.BlockSpec((2, nq, 1, tq), lambda h, i: (h, 0, 0, 0)), pspec, cspec, pspec],
        [pspec, cspec, cspec], [jax.ShapeDtypeStruct((Tp, FOX_HEADS * HEAD_DIM), F32), stat, stat],
        [], ("parallel", "arbitrary"), (proj, proj, proj, ccol, crow, o, lse, do), host)


def _fox_bwd_dkv(proj, ccol, crow, lse, delta, do, tq, host=()):
    Tp = proj.shape[0]
    nq = Tp // tq
    pair = lambda z: z.reshape(_FOX_PAIRS, 2, Tp).transpose(0, 2, 1)
    stats = jnp.concatenate([pair(ccol), pair(lse), pair(delta)], axis=-1).reshape(_FOX_PAIRS, nq, tq, 6)

    def body(q_ref, do_ref, st_ref, k_ref, v_ref, cr_ref, dk_ref, dv_ref, dc_ref):
        ki = pl.program_id(1)
        low = _low_lanes()
        kk, vv = k_ref[...].astype(BF16), v_ref[...].astype(BF16)

        def step(qb_i, carry, diag):
            rows = pl.ds(pl.multiple_of(qb_i * tq, tq), tq)
            qh = _split_pair(q_ref[rows, :] * SCALE, low)
            doh = _split_pair(do_ref[rows, :], low)
            st = st_ref[0, qb_i]
            dk, dv = carry[0], carry[1]
            dcs = []
            for e in range(2):
                s = _fox_scores(qh[e], kk, st[:, e:e + 1], cr_ref[e, 0], diag, tq)
                p = jnp.exp(s - st[:, 2 + e:3 + e])
                dp = lax.dot_general(doh[e], vv, (((1,), (1,)), ((), ())), preferred_element_type=F32)
                ds = p * (dp - st[:, 4 + e:5 + e])
                dv = dv + jnp.dot(p.T.astype(BF16), doh[e], preferred_element_type=F32)
                dk = dk + jnp.dot(ds.T.astype(BF16), qh[e], preferred_element_type=F32)
                dcs.append(carry[2 + e] - jnp.sum(ds, axis=0, keepdims=True))
            return (dk, dv, *dcs)

        init = (jnp.zeros((tq, LANE), F32), jnp.zeros((tq, LANE), F32), jnp.zeros((1, tq), F32), jnp.zeros((1, tq), F32))
        dk, dv, dc0, dc1 = lax.fori_loop(ki + 1, nq, functools.partial(step, diag=False), step(ki, init, True))
        dk_ref[...] = dk
        dv_ref[...] = dv
        dc_ref[0, 0] = dc0
        dc_ref[1, 0] = dc1

    rspec = pl.BlockSpec((2, 1, 1, tq), lambda h, i: (h, i, 0, 0))
    pspec = pl.BlockSpec((tq, LANE), lambda h, i: (i, h))
    wide = jax.ShapeDtypeStruct((Tp, FOX_HEADS * HEAD_DIM), F32)
    return _pcall(
        body, "fox_bwd_dkv", (_FOX_PAIRS, nq),
        [pl.BlockSpec((Tp, LANE), lambda h, i: (0, _QF_COL + h)), pl.BlockSpec((Tp, LANE), lambda h, i: (0, h)),
         pl.BlockSpec((1, nq, tq, 6), lambda h, i: (h, 0, 0, 0)),
         pl.BlockSpec((tq, LANE), lambda h, i: (i, _KF_COL + h)), pl.BlockSpec((tq, LANE), lambda h, i: (i, _VF_COL + h)),
         rspec],
        [pspec, pspec, rspec], [wide, wide, jax.ShapeDtypeStruct((FOX_HEADS, nq, 1, tq), F32)],
        [], ("parallel", "arbitrary"), (proj, do, stats, proj, proj, crow), host)


def _adam(gparts, w, m, v, name):
    P, R, C = gparts.shape
    tr = _tile(R, 256, 8)

    def body(g_ref, w_ref, m_ref, v_ref, go_ref, d_ref, mo_ref, vo_ref):
        g = g_ref[0].astype(F32)
        for p in range(1, P):
            g = g + g_ref[p].astype(F32)
        m2 =ADAM_B1 * m_ref[...] + (1.0 - ADAM_B1) * g
        v2 = ADAM_B2 * v_ref[...] + (1.0 - ADAM_B2) * (g * g)
        m_hat = m2 / (1.0 - ADAM_B1 ** ADAM_STEP)
        v_hat = v2 / (1.0 - ADAM_B2 ** ADAM_STEP)
        go_ref[...] = g
        d_ref[...] = -ADAM_LR * (m_hat / (jnp.sqrt(v_hat) + ADAM_EPS) + ADAM_WD * w_ref[...])
        mo_ref[...] = m2
        vo_ref[...] = v2

    spec = pl.BlockSpec((tr, C), lambda i: (i, 0))
    shp = jax.ShapeDtypeStruct((R, C), F32)
    return pl.pallas_call(
        body, name=name, grid=(R // tr,),
        in_specs=[pl.BlockSpec((P, tr, C), lambda i: (0, i, 0)), spec, spec, spec],
        out_specs=[spec] * 4, out_shape=[shp] * 4, compiler_params=_params(("parallel",)))(gparts, w, m, v)


def _sum_parts(gparts, name):
    P, R, C = gparts.shape
    tr = _tile(R, 256, 8)

    def body(g_ref, o_ref):
        g = g_ref[0]
        for p in range(1, P):
            g = g + g_ref[p]
        o_ref[...] = g

    return pl.pallas_call(
        body, name=name, grid=(R // tr,), in_specs=[pl.BlockSpec((P, tr, C), lambda i: (0, i, 0))],
        out_specs=pl.BlockSpec((tr, C), lambda i: (i, 0)), out_shape=jax.ShapeDtypeStruct((R, C), F32),
        compiler_params=_params(("parallel",)))(gparts)


def _pack(pieces, width, row_mult):
    flat = jnp.concatenate([p.reshape(-1) for p in pieces])
    n = flat.shape[0]
    rows = -(-n // width)
    rows = -(-rows // row_mult) * row_mult
    return jnp.pad(flat, (0, rows * width - n)).reshape(rows, width)


def _unpack(flat2d, shapes, lead=None):
    out, off = [], 0
    if lead is None:
        flat = flat2d.reshape(-1)
        for s in shapes:
            n = int(np.prod(s))
            out.append(flat[off:off + n].reshape(s))
            off += n
    else:
        flat = flat2d.reshape(lead, -1)
        for s in shapes:
            n = int(np.prod(s))
            out.append(flat[:, off:off + n].reshape((lead,) + tuple(s)))
            off += n
    return out


def _col_pieces(off, width, shard_w, fetch):
    out, c, end = [], off, off + width
    while c < end:
        d = c // shard_w
        hi = min(end, (d + 1) * shard_w)
        out.append(fetch(d, c - d * shard_w, hi - d * shard_w))
        c = hi
    return out


def _work_cols_of_orig(lo, hi, fetch):
    out = []
    for name, ow in _ORIG:
        o = _ORIG_OFF[name][0]
        a, b = max(lo, o), min(hi, o + ow)
        if a < b:
            w0 = _WORK_OFF[name][0]
            out.append(fetch(w0 + a - o, w0 + b - o))
    return out


def _t5_bucket_np(dist):
    max_exact = REL_BUCKETS // 2
    d = np.maximum(dist, 0)
    scaled = (np.log(np.maximum(d, 1).astype(np.float32) / np.float32(max_exact))
              / np.float32(math.log(REL_MAX_DIST / max_exact))).astype(np.float32)
    large = np.minimum(max_exact + (scaled * np.float32(REL_BUCKETS - max_exact)).astype(np.int32), REL_BUCKETS - 1)
    return np.where(d < max_exact, d, large)


def _bucket_onehot():
    q_idx = np.arange(BLOCK)[:, None]
    k_idx = np.arange(2 * BLOCK)[None, :]
    bucket = _t5_bucket_np(q_idx + BLOCK - k_idx).reshape(-1)
    oh = np.zeros((LANE, BLOCK * 2 * BLOCK), np.float32)
    oh[bucket, np.arange(bucket.shape[0])] = 1.0
    return oh


def kernel(x, meta_tokens, rel_bias_table, norm_mix, w_in, swa_sinks, fox_forget_bias, conv_w, conv_b, lru_w_r, lru_b_r, lru_w_i, lru_b_i, lru_lambda, w_branch, w_out, norm_ffn, w_ffn_in, w_ffn_out, norm_final, loss_target, m_meta_tokens, m_rel_bias_table, m_norm_mix, m_w_in, m_swa_sinks, m_fox_forget_bias, m_conv_w, m_conv_b, m_lru_w_r, m_lru_b_r, m_lru_w_i, m_lru_b_i, m_lru_lambda, m_w_branch, m_w_out, m_norm_ffn, m_w_ffn_in, m_w_ffn_out, m_norm_final, v_meta_tokens, v_rel_bias_table, v_norm_mix, v_w_in, v_swa_sinks, v_fox_forget_bias, v_conv_w, v_conv_b, v_lru_w_r, v_lru_b_r, v_lru_w_i, v_lru_b_i, v_lru_lambda, v_w_branch, v_w_out, v_norm_ffn, v_w_ffn_in, v_w_ffn_out, v_norm_final):
    S = x.shape[1]
    T = N_META + S
    n_pad = (-T) % BLOCK
    Tp = T + n_pad
    first = n_pad + N_META
    TR = _tile(Tp, 384)
    TQ = _tile(Tp, 384)
    TQF = _tile(Tp, FOX_TILE)
    me =4 * lax.axis_index("x") + 2 * lax.axis_index("y") + lax.axis_index("c")

    big_names = ("w_in", "w_ffn_in", "w_ffn_out", "w_branch", "w_out")
    big_w = dict(w_in=w_in, w_ffn_in=w_ffn_in, w_ffn_out=w_ffn_out, w_branch=w_branch, w_out=w_out)
    big_m = dict(w_in=m_w_in, w_ffn_in=m_w_ffn_in, w_ffn_out=m_w_ffn_out, w_branch=m_w_branch, w_out=m_w_out)
    big_v = dict(w_in=v_w_in, w_ffn_in=v_w_ffn_in, w_ffn_out=v_w_ffn_out, w_branch=v_w_branch, w_out=v_w_out)
    rows2d = lambda w: w.reshape(-1, w.shape[-1])
    in_shard_w = IN_COLS // N_DEV
    cat_axis = dict(w_in=-1, w_ffn_in=-1, w_ffn_out=0, w_branch=-1, w_out=0)

    def w_send(n, l):
        return rows2d(big_w[n][l]).astype(BF16)

    def w_build(n, g):
        shard = lambda d: g[d].reshape(big_w[n].shape[1:])
        if cat_axis[n] == 0:
            return g.reshape(-1, g.shape[-1])
        if n != "w_in":
            return jnp.concatenate([shard(d) for d in range(N_DEV)], axis=cat_axis[n])
        parts = []
        for name, width in _WORK:
            off, ow = _ORIG_OFF[name]
            parts += _col_pieces(off, ow, in_shard_w, lambda d, lo, hi: shard(d)[:, lo:hi])
            if width > ow:
                parts.append(jnp.zeros((D_MODEL, width - ow), BF16))
        return jnp.concatenate(parts, axis=-1)

    W_in, W_ffn_in, W_ffn_out, W_branch, W_out = ([None] * DEPTH for _ in range(5))
    W_in[0] = w_build("w_in", _exchange(w_send("w_in", 0), True, "gather_w_in"))

    small_sh = _exchange(_pack([meta_tokens, conv_w], D_MODEL, 8), True, "gather_small")
    g_meta, g_conv_w = _unpack(small_sh, [meta_tokens.shape, conv_w.shape], lead=N_DEV)
    meta_full = g_meta.transpose(1, 0, 2).reshape(N_META, D_MODEL)
    conv_w_full = g_conv_w.transpose(1, 2, 0, 3).reshape(DEPTH, CONV_WIDTH, LRU_WIDTH)

    onehot = jnp.asarray(_bucket_onehot())
    table_t = jnp.pad(rel_bias_table.T, ((0, 0), (0, LANE - REL_BUCKETS)))
    bias = _mm(table_t, onehot, "nn", "swa_bias", exact=True, tn_cap=4096).reshape(SWA_Q_HEADS, BLOCK, 2 * BLOCK)

    def dense_blocks(w):
        rows = []
        for b in range(LRU_BLOCKS):
            rows.append(jnp.pad(w[b], ((0, 0), (b * LRU_BLOCK_DIM, (LRU_BLOCKS - 1 - b) * LRU_BLOCK_DIM))))
        return jnp.concatenate(rows, axis=0)

    h = jnp.concatenate([jnp.zeros((n_pad, D_MODEL), F32), meta_full, x[0]], axis=0)
    saved = []

    for l in range(DEPTH):
        sv = {"h": h}
        g_mix = norm_mix[l][None, :]

        def norm_fwd(i, rows, consts):
            xx = rows[0]
            return [xx * lax.rsqrt(jnp.mean(xx * xx, axis=1, keepdims=True) + EPS) * consts[0]], []

        (u,) = _rowwise(norm_fwd, "norm_fwd", Tp, TR, [(h, None, 0)], [g_mix], [(D_MODEL, BF16)])
        proj, got = _mm(u, W_in[l], "nn", "proj", tn_cap=384,
                        host=[(w_send("w_branch", l), True), (w_send("w_out", l), True)])
        W_branch[l], W_out[l] = w_build("w_branch", got[0]), w_build("w_out", got[1])
        sv["u"], sv["proj"] = u, proj

        def col(name):
            off, w = _WORK_OFF[name]
            return (proj, w, off // w)

        def cols(name):
            off, w = _WORK_OFF[name]
            return proj[:, off:off + w]

        sinks_l =jnp.broadcast_to(swa_sinks[l][:, None, None], (SWA_Q_HEADS, 1, LANE))
        (o_a, lse_a), got = _swa_fwd(proj, bias, sinks_l, n_pad, host=[(w_send("w_ffn_out", l), True)])
        W_ffn_out[l] = w_build("w_ffn_out", got[0])
        sv.update(sinks=sinks_l, lse_a=lse_a, o_a=o_a)

        fbias = jnp.pad(fox_forget_bias[l], (0, LANE - FOX_HEADS))[None, :]

        def logf_fwd(i, rows, consts):
            return [_log_sigmoid(rows[0] + consts[0])], []

        (logf,) = _rowwise(logf_fwd, "logf_fwd", Tp, TR, [col("fl")], [fbias], [(LANE, F32)])
        cum = _scan(None, logf, "cumsum")
        cum_h = cum[:, :FOX_HEADS].T
        ccol = cum_h[:, :, None]
        crow = jnp.where(jnp.arange(Tp)[None, :] < n_pad, -NEG_INF, cum_h).reshape(FOX_HEADS, Tp // TQF, 1, TQF)
        ride = [("w_ffn_in", l)] + ([("w_in", l + 1)] if l + 1 < DEPTH else [])
        (o_f, lse_f), got = _fox_fwd(proj, ccol, crow, TQF, host=[(w_send(n, ll), True) for n, ll in ride],
                                     prefetch=TQF <= 384)
        W_ffn_in[l] = w_build("w_ffn_in", got[0])
        if l + 1 < DEPTH:
            W_in[l + 1] = w_build("w_in", got[1])
        sv.update(fbias=fbias, ccol=ccol, crow=crow, lse_f=lse_f, o_f=o_f)

        xc = cols("xc")
        xs = [_shift_down(xc, CONV_WIDTH - 1 - i) for i in range(CONV_WIDTH - 1)] + [xc]
        cw = conv_w_full[l]
        cb = conv_b[l][None, :]

        def conv_fwd(i, rows, consts):
            w, b = consts
            acc = rows[0] * w[0:1, :]
            for t in range(1, CONV_WIDTH):
                acc = acc + rows[t] * w[t:t + 1, :]
            return [acc + b], []

        (xconv,) = _rowwise(conv_fwd, "conv_fwd", Tp, TR, [(a, None, 0) for a in xs], [cw, cb], [(LRU_WIDTH, F32)])
        w_ri = jnp.concatenate([dense_blocks(lru_w_r[l]), dense_blocks(lru_w_i[l])], axis=1)
        pre = _mm(xconv, w_ri, "nn", "lru_gates")
        lru_consts = [lru_b_r[l][None, :], lru_b_i[l][None, :], lru_lambda[l][None, :]]

        def lru_fwd(i, rows, consts):
            pr, xv = rows
            b_r, b_i, lam = consts
            r = _sigmoid(pr[:, :LRU_WIDTH] + b_r)
            gi = _sigmoid(pr[:, LRU_WIDTH:] + b_i)
            log_a = LRU_C * r * _log_sigmoid(lam)
            valid = (_row_ids(i, TR) >= n_pad).astype(F32)
            inp = jnp.sqrt(_neg_expm1(2.0 * log_a)) * (gi * xv) * valid
            return [jnp.exp(log_a), inp], []

        a_dec, inp = _rowwise(lru_fwd, "lru_fwd", Tp, TR, [(pre, None, 0), (xconv, None, 0)], lru_consts,
                              [(LRU_WIDTH, F32), (LRU_WIDTH, F32)])
        hs = _scan(a_dec, inp, "lru_scan")

        def oc_fwd(i, rows, consts):
            return [rows[0] * _gelu(rows[1])], []

        (o_c,) = _rowwise(oc_fwd, "oc_fwd", Tp, TR, [(hs, None, 0), col("yc")], [], [(LRU_WIDTH, BF16)])
        sv.update(xs=xs, cw=cw, xconv=xconv, w_ri=w_ri, pre=pre, lru_consts=lru_consts, a_dec=a_dec, hs=hs, o_c=o_c)

        ba = _mm(o_a, W_branch[l][0], "nn", "branch")
        bf = _mm(o_f, W_branch[l][1], "nn", "branch")
        bc = _mm(o_c, W_branch[l][2], "nn", "branch")

        def merge_fwd(i, rows, consts):
            g, b0, b1, b2 = rows
            valid = (_row_ids(i, BLOCK) >= n_pad).astype(F32)
            mg = (_sigmoid(g[:, :D_MODEL]) * b0 + _sigmoid(g[:, D_MODEL:2 * D_MODEL]) * b1
                  + _sigmoid(g[:, 2 * D_MODEL:]) * b2)
            return [mg * valid], []

        (merged,) = _rowwise(merge_fwd, "merge_fwd", Tp, BLOCK,
                             [col("gates"), (ba, None, 0), (bf, None, 0), (bc, None, 0)], [], [(D_MODEL, BF16)])
        h1 = _mm(merged, W_out[l], "nn", "out_proj", res=h)
        sv.update(ba=ba, bf=bf, bc=bc, merged=merged, h1=h1)

        g_ffn = norm_ffn[l][None, :]
        (u2,) = _rowwise(norm_fwd, "norm_fwd", Tp, TR, [(h1, None, 0)], [g_ffn], [(D_MODEL, BF16)])
        ff = _mm(u2, W_ffn_in[l], "nn", "ffn_in")

        def act_fwd(i, rows, consts):
            gate, up = rows
            return [gate * _sigmoid(gate) * up], []

        (act,) = _rowwise(act_fwd, "act_fwd", Tp, TR, [(ff, D_FF, 0), (ff, D_FF, 1)], [], [(D_FF, BF16)])
        h = _mm(act, W_ffn_out[l], "nn", "ffn_out", res=h1)
        sv.update(u2=u2, ff=ff, act=act)
        saved.append(sv)

    tgt = jnp.concatenate([jnp.zeros((first, D_MODEL), F32), loss_target[0]], axis=0)
    g_fin = norm_final[None, :]

    def head(i, rows, consts):
        xx, tg = rows
        g = consts[0]
        valid = (_row_ids(i, TR) >= first).astype(F32)
        rstd = lax.rsqrt(jnp.mean(xx * xx, axis=1, keepdims=True) + EPS)
        xhat = xx * rstd
        err = (xhat * g - tg) * valid
        loss_rows = 0.5 * jnp.mean(err * err, axis=1, keepdims=True)
        dy = err * (1.0 / D_MODEL)
        dxhat = dy * g
        dx = rstd * (dxhat - xhat * jnp.mean(dxhat * xhat, axis=1, keepdims=True))
        return [dx], [jnp.broadcast_to(_colsum(loss_rows), (1, LANE)), _colsum(dy * xhat)]

    dh, loss_part, d_norm_final = _rowwise(head, "loss_head", Tp, TR, [(h, None, 0), (tgt, None, 0)], [g_fin],
                                           [(D_MODEL, F32)], [LANE, D_MODEL])
    loss = lax.psum(loss_part[0, 0], ("x", "y", "c"))

    def norm_bwd_call(xin, g, du, dres):
        def norm_bwd(i, rows, consts):
            xx, dd, rr = rows
            gg = consts[0]
            valid = (_row_ids(i, TR) >= n_pad).astype(F32)
            rstd = lax.rsqrt(jnp.mean(xx * xx, axis=1, keepdims=True) + EPS)
            xhat = xx * rstd
            dxhat = dd * gg
            dx = rstd * (dxhat - xhat * jnp.mean(dxhat * xhat, axis=1, keepdims=True))
            return [rr + dx * valid], [_colsum(dd * xhat)]

        return _rowwise(norm_bwd, "norm_bwd", Tp, TR, [(xin, None, 0), (du, None, 0), (dres, None, 0)], [g],
                        [(D_MODEL, F32)], [D_MODEL])

    grads = {k: [None] * DEPTH for k in ("norm_mix", "w_in", "swa_sinks", "fox_forget_bias", "conv_w", "conv_b",
                                         "lru_w_r", "lru_b_r", "lru_w_i", "lru_b_i", "lru_lambda", "w_branch",
                                         "w_out", "norm_ffn", "w_ffn_in", "w_ffn_out")}
    d_bias_total = None

    def g_send(n, g):
        if n == "w_in":
            pieces = [jnp.concatenate(_work_cols_of_orig(d * in_shard_w, (d + 1) * in_shard_w,
                                                         lambda lo, hi: g[:, lo:hi]), axis=-1) for d in range(N_DEV)]
        elif cat_axis[n] == 0:
            return g.reshape(N_DEV, -1, g.shape[-1])
        else:
            ax = cat_axis[n] % g.ndim
            w = big_w[n].shape[1 + ax]
            pieces = [lax.slice_in_dim(g, d * w, (d + 1) * w, axis=ax) for d in range(N_DEV)]
        return jnp.stack([rows2d(p) for p in pieces])

    recv = {n: [None] * DEPTH for n in big_names}
    for l in reversed(range(DEPTH)):
        sv = saved[l]
        proj = sv["proj"]

        def col(name):
            off, w = _WORK_OFF[name]
            return (proj, w, off // w)

        dh2 = dh
        d_act = _mm(dh2, W_ffn_out[l], "nt", "d_act")
        grads["w_ffn_out"][l] = _mm(sv["act"], dh2, "tn", "dw_ffn_out", out_dtype=BF16)

        def act_bwd(i, rows, consts):
            gate, up, da = rows
            sg = _sigmoid(gate)
            d_gate = da * up * (sg * (1.0 + gate * (1.0 - sg)))
            d_up = da * (gate * sg)
            return [jnp.concatenate([d_gate, d_up], axis=1)], []

        (dff,) = _rowwise(act_bwd, "act_bwd", Tp, BLOCK, [(sv["ff"], D_FF, 0), (sv["ff"], D_FF, 1), (d_act, None, 0)],
                          [], [(2 * D_FF, BF16)])
        grads["w_ffn_in"][l] = _mm(sv["u2"], dff, "tn", "dw_ffn_in", out_dtype=BF16)
        du2 = _mm(dff, W_ffn_in[l], "nt", "du2")
        dh1, dg = norm_bwd_call(sv["h1"], norm_ffn[l][None, :], du2, dh2)
        grads["norm_ffn"][l] = dg[0]

        dmerged = _mm(dh1, W_out[l], "nt", "d_merged")
        grads["w_out"][l] = _mm(sv["merged"], dh1, "tn", "dw_out", out_dtype=BF16)

        def merge_bwd(i, rows, consts):
            g, b0, b1, b2, dm = rows
            dm = dm * (_row_ids(i, BLOCK) >= n_pad).astype(F32)
            outs, dgs = [], []
            for k, bk in enumerate((b0, b1, b2)):
                sg = _sigmoid(g[:, k * D_MODEL:(k + 1) * D_MODEL])
                outs.append(dm * sg)
                dgs.append(dm * bk * sg * (1.0 - sg))
            return outs + [jnp.concatenate(dgs, axis=1)], []

        d_ba, d_bf, d_bc, d_gates = _rowwise(
            merge_bwd, "merge_bwd", Tp, BLOCK,
            [col("gates"), (sv["ba"], None, 0), (sv["bf"], None, 0), (sv["bc"], None, 0), (dmerged, None, 0)], [],
            [(D_MODEL, BF16)] * 3 + [(3 * D_MODEL, BF16)])
        grads["w_branch"][l] = jnp.stack([_mm(sv["o_a"], d_ba, "tn", "dw_branch", out_dtype=BF16),
                                          _mm(sv["o_f"], d_bf, "tn", "dw_branch", out_dtype=BF16),
                                          _mm(sv["o_c"], d_bc, "tn", "dw_branch", out_dtype=BF16)])
        do_a = _mm(d_ba, W_branch[l][0], "nt", "d_branch")
        do_f = _mm(d_bf, W_branch[l][1], "nt", "d_branch")
        do_c = _mm(d_bc, W_branch[l][2], "nt", "d_branch")

        def oc_bwd(i, rows, consts):
            d, hv, yv = rows
            return [d * _gelu(yv), d * hv * _gelu_grad(yv)], []

        d_hs, d_yc = _rowwise(oc_bwd, "oc_bwd", Tp, TR, [(do_c, None, 0), (sv["hs"], None, 0), col("yc")], [],
                              [(LRU_WIDTH, F32)] * 2)
        d_state = _rev_scan(sv["a_dec"], d_hs, "lru_scan")
        hs_prev = _shift_down(sv["hs"], 1)

        def lru_bwd(i, rows, consts):
            dH, hp, pr, xv = rows
            b_r, b_i, lam = consts
            valid = (_row_ids(i, TR) >= n_pad).astype(F32)
            r = _sigmoid(pr[:, :LRU_WIDTH] + b_r)
            gi = _sigmoid(pr[:, LRU_WIDTH:] + b_i)
            lsl = _log_sigmoid(lam)
            log_a = LRU_C * r * lsl
            a = jnp.exp(log_a)
            one_m_e = _neg_expm1(2.0 * log_a)
            mult = jnp.sqrt(one_m_e)
            d_inp = dH * valid
            d_mult = d_inp * gi * xv
            d_gi = d_inp * mult * xv
            d_x = d_inp * mult * gi
            d_log_a = dH * hp * a - d_mult * (1.0 - one_m_e) / mult
            d_pre_r = d_log_a * (LRU_C * lsl) * r * (1.0 - r)
            d_pre_i = d_gi * gi * (1.0 - gi)
            d_lam = _colsum(d_log_a * (LRU_C * r)) * _sigmoid(-lam)
            return [jnp.concatenate([d_pre_r, d_pre_i], axis=1), d_x], [_colsum(d_pre_r), _colsum(d_pre_i), d_lam]

        d_pre, d_xdir, d_b_r, d_b_i, d_lam = _rowwise(
            lru_bwd, "lru_bwd", Tp, TR, [(d_state, None, 0), (hs_prev, None, 0), (sv["pre"], None, 0),
                                         (sv["xconv"], None, 0)], sv["lru_consts"],
            [(2 * LRU_WIDTH, BF16), (LRU_WIDTH, F32)], [LRU_WIDTH] * 3)
        grads["lru_b_r"][l], grads["lru_b_i"][l], grads["lru_lambda"][l] = d_b_r[0], d_b_i[0], d_lam[0]
        d_w_ri = _mm(sv["xconv"], d_pre, "tn", "dw_lru_gates")

        def diag_blocks(wd):
            w4 = wd.reshape(LRU_BLOCKS, LRU_BLOCK_DIM, LRU_BLOCKS, LRU_BLOCK_DIM)
            return jnp.stack([w4[b, :, b, :] for b in range(LRU_BLOCKS)])

        grads["lru_w_r"][l] = diag_blocks(d_w_ri[:, :LRU_WIDTH])
        grads["lru_w_i"][l] = diag_blocks(d_w_ri[:, LRU_WIDTH:])
        d_conv = _mm(d_pre, sv["w_ri"], "nt", "d_xconv", res=d_xdir)
        dcs = [d_conv] + [_shift_up(d_conv, s) for s in range(1, CONV_WIDTH)]

        def conv_bwd(i, rows, consts):
            w = consts[0]
            d = rows[:CONV_WIDTH]
            xsh = rows[CONV_WIDTH:]
            dxc = d[0] * w[CONV_WIDTH - 1:CONV_WIDTH, :]
            for s in range(1, CONV_WIDTH):
                dxc = dxc + d[s] * w[CONV_WIDTH - 1 - s:CONV_WIDTH - s, :]
            return [dxc], [_colsum(xsh[t] * d[0]) for t in range(CONV_WIDTH)] + [_colsum(d[0])]

        res = _rowwise(conv_bwd, "conv_bwd", Tp, TR, [(a, None, 0) for a in dcs + sv["xs"]], [sv["cw"]],
                       [(LRU_WIDTH, F32)], [LRU_WIDTH] * (CONV_WIDTH + 1))
        d_xc = res[0]
        grads["conv_w"][l] = jnp.concatenate(res[1:1 + CONV_WIDTH], axis=0)
        grads["conv_b"][l] = res[1 + CONV_WIDTH][0]

        (dqf, delta_f, dccol), got = _fox_bwd_dq(
            proj, sv["ccol"], sv["crow"], sv["o_f"], sv["lse_f"], do_f, TQF,
            host=[(g_send("w_ffn_in", grads["w_ffn_in"][l]), False)])
        recv["w_ffn_in"][l] = got[0]
        ride = [("w_out", l), ("w_branch", l)] + ([("w_in", l + 1)] if l + 1 < DEPTH else [])
        (dkf, dvf, dcrow), got = _fox_bwd_dkv(
            proj, sv["ccol"], sv["crow"], sv["lse_f"], delta_f, do_f, TQF,
            host=[(g_send(n, grads[n][ll]), False) for n, ll in ride])
        for (n, ll), r in zip(ride, got):
            recv[n][ll] = r
        lanes = lambda z: jnp.pad(z.reshape(FOX_HEADS, Tp).T, ((0, 0), (0, LANE - FOX_HEADS)))
        dlogf = _rev_scan(None, lanes(dcrow), "cumsum_bwd", g2=lanes(dccol))

        def logf_bwd(i, rows, consts):
            dl, fl = rows
            valid = (_row_ids(i, TR) >= n_pad).astype(F32)
            lane_ok = (lax.broadcasted_iota(jnp.int32, (1, LANE), 1) < FOX_HEADS).astype(F32)
            dfl = dl * _sigmoid(-(fl + consts[0])) * valid * lane_ok
            return [dfl], [_colsum(dfl)]

        d_fl, d_fb = _rowwise(logf_bwd, "logf_bwd", Tp, TR, [(dlogf, None, 0), col("fl")], [sv["fbias"]],
                              [(LANE, F32)], [LANE])
        grads["fox_forget_bias"][l] = d_fb[0, :FOX_HEADS]

        (dqa, dk_strip, dv_strip, d_bias_l, d_sink), got = _swa_bwd(
            proj, bias, sv["sinks"], sv["o_a"], sv["lse_a"], do_a, n_pad,
            host=[(g_send("w_ffn_out", grads["w_ffn_out"][l]), False)])
        recv["w_ffn_out"][l] = got[0]
        grads["swa_sinks"][l] = d_sink[:, 0, 0]
        d_bias_total = d_bias_l if d_bias_total is None else d_bias_total + d_bias_l

        def kv_parts(strip):
            own = strip[:, BLOCK:, :].reshape(Tp, KV_W)
            nxt = jnp.pad(strip[1:, :BLOCK, :], ((0, 1), (TQ - BLOCK, 0), (0, 0))).reshape(Tp, KV_W)
            return [(own, None, 0), (nxt, None, 0)]

        def assemble(i, rows, consts):
            dg, dqa_, dqf_, dkf_, dvf_, dxc_, dyc_, dk0, dk1, dv0, dv1, dfl_ = rows
            parts = [dg, dqa_, dqf_, dkf_, dvf_, dxc_, dyc_, dk0 + dk1, dv0 + dv1, dfl_]
            return [jnp.concatenate([p.astype(BF16) for p in parts], axis=1)], []

        (dproj,) = _rowwise(
            assemble, "assemble_dproj", Tp, BLOCK,
            [(d_gates, None, 0), (dqa, None, 0), (dqf, None, 0), (dkf, None, 0), (dvf, None, 0), (d_xc, None, 0),
             (d_yc, None, 0)] + kv_parts(dk_strip) + kv_parts(dv_strip) + [(d_fl, None, 0)], [],
            [(WORK_COLS, BF16)])
        grads["w_in"][l] = _mm(sv["u"], dproj, "tn", "dw_in", tn_cap=384, out_dtype=BF16)
        if l == 0:
            du, got = _mm(dproj, W_in[l], "nt", "du", tk_cap=2176,
                          host=[(g_send("w_in", grads["w_in"][0]), False)])
            recv["w_in"][0] = got[0]
        else:
            du = _mm(dproj, W_in[l], "nt", "du", tk_cap=2176)
        dh, dg = norm_bwd_call(sv["h"], norm_mix[l][None, :], du, dh1)
        grads["norm_mix"][l] = dg[0]

    d_table = _mm(d_bias_total.reshape(SWA_Q_HEADS, BLOCK * 2 * BLOCK), onehot, "nt", "d_rel_table", exact=True,
                  tk_cap=4096)
    g_rel = d_table[:, :REL_BUCKETS].T
    g_meta_full = dh[n_pad:first]
    grad_x = dh[first:][None]

    stack = lambda k: jnp.stack(grads[k])
    big_res = {}
    for n in big_names:
        per_layer = [_adam(recv[n][l], rows2d(big_w[n][l]), rows2d(big_m[n][l]), rows2d(big_v[n][l]), "adam_" + n)
                     for l in range(DEPTH)]
        big_res[n] = [jnp.stack([per_layer[l][k] for l in range(DEPTH)]).reshape(big_w[n].shape) for k in range(4)]

    rep_names = ("rel_bias_table", "norm_mix", "swa_sinks", "fox_forget_bias", "conv_b", "lru_w_r", "lru_b_r",
                 "lru_w_i", "lru_b_i", "lru_lambda", "norm_ffn", "norm_final")
    rep_w = dict(rel_bias_table=rel_bias_table, norm_mix=norm_mix, swa_sinks=swa_sinks,
                 fox_forget_bias=fox_forget_bias, conv_b=conv_b, lru_w_r=lru_w_r, lru_b_r=lru_b_r, lru_w_i=lru_w_i,
                 lru_b_i=lru_b_i, lru_lambda=lru_lambda, norm_ffn=norm_ffn, norm_final=norm_final)
    rep_m = dict(rel_bias_table=m_rel_bias_table, norm_mix=m_norm_mix, swa_sinks=m_swa_sinks,
                 fox_forget_bias=m_fox_forget_bias, conv_b=m_conv_b, lru_w_r=m_lru_w_r, lru_b_r=m_lru_b_r,
                 lru_w_i=m_lru_w_i, lru_b_i=m_lru_b_i, lru_lambda=m_lru_lambda, norm_ffn=m_norm_ffn,
                 norm_final=m_norm_final)
    rep_v = dict(rel_bias_table=v_rel_bias_table, norm_mix=v_norm_mix, swa_sinks=v_swa_sinks,
                 fox_forget_bias=v_fox_forget_bias, conv_b=v_conv_b, lru_w_r=v_lru_w_r, lru_b_r=v_lru_b_r,
                 lru_w_i=v_lru_w_i, lru_b_i=v_lru_b_i, lru_lambda=v_lru_lambda, norm_ffn=v_norm_ffn,
                 norm_final=v_norm_final)
    rep_g = {n: (g_rel if n == "rel_bias_table" else d_norm_final[0] if n == "norm_final" else stack(n))
             for n in rep_names}
    small_g = [rep_g[n] for n in rep_names] + [g_meta_full, stack("conv_w")]
    small_shapes = [rep_w[n].shape for n in rep_names] + [(N_META, D_MODEL), (DEPTH, CONV_WIDTH, LRU_WIDTH)]
    gs_all = _exchange(_pack(small_g, D_MODEL, 8), True, "gather_small_grads")
    gs_sum = _unpack(_sum_parts(gs_all, "sum_small_grads"), small_shapes)
    gsum = dict(zip(rep_names, gs_sum[:len(rep_names)]))
    g_meta_sh = lax.dynamic_slice_in_dim(gs_sum[-2], me * (D_MODEL // N_DEV), D_MODEL // N_DEV, axis=1)
    g_convw_sh = lax.dynamic_slice_in_dim(gs_sum[-1], me * (LRU_WIDTH // N_DEV), LRU_WIDTH // N_DEV, axis=2)
    sm_names = rep_names + ("meta_tokens", "conv_w")
    sm_w = [rep_w[n] for n in rep_names] + [meta_tokens, conv_w]
    sm_m = [rep_m[n] for n in rep_names] + [m_meta_tokens, m_conv_w]
    sm_v = [rep_v[n] for n in rep_names] + [v_meta_tokens, v_conv_w]
    sm_g = [gsum[n] for n in rep_names] + [g_meta_sh, g_convw_sh]
    sm_shapes = [w.shape for w in sm_w]
    sm_out = [_unpack(o, sm_shapes) for o in _adam(_pack(sm_g, D_MODEL, 8)[None], _pack(sm_w, D_MODEL, 8),
                                                   _pack(sm_m, D_MODEL, 8), _pack(sm_v, D_MODEL, 8), "adam_small")]
    sm_res = {n: [sm_out[k][j] for k in range(4)] for j, n in enumerate(sm_names)}

    order = ("meta_tokens", "rel_bias_table", "norm_mix", "w_in", "swa_sinks", "fox_forget_bias", "conv_w", "conv_b",
             "lru_w_r", "lru_b_r", "lru_w_i", "lru_b_i", "lru_lambda", "w_branch", "w_out", "norm_ffn", "w_ffn_in",
             "w_ffn_out", "norm_final")
    allres = {**big_res, **sm_res}
    outs = [loss, grad_x]
    for k in range(4):
        outs += [allres[n][k] for n in order]
    return tuple(outs)
```

```python
import functools
import math

import numpy as np
import jax
import jax.numpy as jnp
from jax import lax
from jax.experimental import pallas as pl
from jax.experimental.pallas import tpu as pltpu

F32 = jnp.float32
BF16 = jnp.bfloat16

N_DEV = 8
D_MODEL = 1024
DEPTH = 4
HEAD_DIM = 64
N_META = 16
BLOCK = 128
NEG_INF = -1e30
SWA_Q_HEADS = 8
SWA_KV_HEADS = 2
SWA_GROUP = SWA_Q_HEADS // SWA_KV_HEADS
FOX_HEADS = 8
LRU_WIDTH = D_MODEL // 2
LRU_BLOCKS = 8
LRU_BLOCK_DIM = LRU_WIDTH // LRU_BLOCKS
CONV_WIDTH = 4
LRU_C = 8.0
REL_BUCKETS = 32
REL_MAX_DIST = 128
D_FF = 2816
N_BRANCH = 3
ATT_W = SWA_Q_HEADS * HEAD_DIM
KV_W = SWA_KV_HEADS * HEAD_DIM
SCALE = HEAD_DIM ** -0.5
EPS = 1e-6

_ORIG = (("qa", ATT_W), ("ka", KV_W), ("va", KV_W), ("qf", ATT_W), ("kf", ATT_W), ("vf", ATT_W),
         ("fl", FOX_HEADS), ("xc", LRU_WIDTH), ("yc", LRU_WIDTH), ("gates", N_BRANCH * D_MODEL))
IN_COLS = sum(w for _, w in _ORIG)
_WORK = (("gates", 3072), ("qa", 512), ("qf", 512), ("kf", 512), ("vf", 512), ("xc", 512), ("yc", 512),
         ("ka", 128), ("va", 128), ("fl", 128))
WORK_COLS = sum(w for _, w in _WORK)


def _offsets(table):
    off, out = 0, {}
    for n, w in table:
        out[n] = (off, w)
        off += w
    return out


_ORIG_OFF = _offsets(_ORIG)
_WORK_OFF = _offsets(_WORK)

ADAM_LR = 0.001
ADAM_B1 = 0.9
ADAM_B2 = 0.999
ADAM_EPS = 1e-08
ADAM_WD = 0.01
ADAM_STEP = 10

VMEM_LIMIT = 62 * 1024 * 1024
FOX_TILE = 1408
LANE = 128


def _tile(n, cap, mult=LANE):
    if n <= cap:
        return n
    best = None
    for d in range(mult, cap + 1, mult):
        if n % d == 0:
            best = d
    assert best is not None, (n, cap, mult)
    return best


def _params(sem):
    return pltpu.CompilerParams(dimension_semantics=sem, vmem_limit_bytes=VMEM_LIMIT)


def _sigmoid(x):
    return 1.0 / (1.0 + jnp.exp(-x))


def _log_sigmoid(x):
    return jnp.minimum(x, 0.0) - jnp.log(1.0 + jnp.exp(-jnp.abs(x)))


def _neg_expm1(x):
    series = -x * (1.0 + x * (0.5 + x * (1.0 / 6.0 + x * (1.0 / 24.0 + x * (1.0 / 120.0)))))
    return jnp.where(x > -0.1, series, 1.0 - jnp.exp(x))


_GELU_C = math.sqrt(2.0 / math.pi)


def _gelu(x):
    return 0.5 * x * (1.0 + jnp.tanh(_GELU_C * (x + 0.044715 * x * x * x)))


def _gelu_grad(x):
    t = jnp.tanh(_GELU_C * (x + 0.044715 * x * x * x))
    return 0.5 * (1.0 + t) + 0.5 * x * (1.0 - t * t) * _GELU_C * (1.0 + 3.0 * 0.044715 * x * x)


def _xch_ops(src_ref, out_ref, send_sems, recv_sems, local_sem, gather):
    x, y, c = lax.axis_index("x"), lax.axis_index("y"), lax.axis_index("c")
    me = 4 * x + 2 * y + c

    def copy(r, mine):
        px, py, pc = x ^ ((r >> 2) & 1), y ^ ((r >> 1) & 1), c ^ (r & 1)
        pid = 4 * px + 2 * py + pc
        return pltpu.make_async_remote_copy(
            src_ref=src_ref if gather else src_ref.at[pid], dst_ref=out_ref.at[me if mine else pid],
            send_sem=send_sems.at[r - 1], recv_sem=recv_sems.at[r - 1],
            device_id=(px, py, pc), device_id_type=pl.DeviceIdType.MESH)

    def local():
        return pltpu.make_async_copy(src_ref if gather else src_ref.at[me], out_ref.at[me], local_sem)

    def start():
        local().start()
        for r in range(1, N_DEV):
            copy(r, True).start()

    def wait():
        for r in range(1, N_DEV):
            copy(r, False).wait_recv()
        for r in range(1, N_DEV):
            copy(r, True).wait_send()
        local().wait()

    return start, wait


_XCH_SEMS = [pltpu.SemaphoreType.DMA((N_DEV - 1,)), pltpu.SemaphoreType.DMA((N_DEV - 1,)), pltpu.SemaphoreType.DMA]
_ANY = pl.BlockSpec(memory_space=pl.ANY)


def _xch_shape(src, gather):
    return jax.ShapeDtypeStruct((N_DEV,) + tuple(src.shape if gather else src.shape[1:]), src.dtype)


def _exchange(src, gather, name):
    def body(src_ref, out_ref, send_sems, recv_sems, local_sem):
        start, wait = _xch_ops(src_ref, out_ref, send_sems, recv_sems, local_sem, gather)
        start()
        wait()

    return pl.pallas_call(body, name=name, in_specs=[_ANY], out_specs=_ANY, out_shape=_xch_shape(src, gather),
                          scratch_shapes=list(_XCH_SEMS))(src)


def _pcall(body, name, grid, in_specs, out_specs, out_shape, scratch_shapes, sem, args, host=()):
    if not host:
        outs = pl.pallas_call(body, name=name, grid=grid, in_specs=in_specs, out_specs=out_specs, out_shape=out_shape,
                              scratch_shapes=scratch_shapes, compiler_params=_params(sem))(*args)
        return outs, []
    n_in, n_out, n_scr, n_x = len(args), len(out_shape), len(scratch_shapes), len(host)

    def wrapped(*refs):
        ins, xin = refs[:n_in], refs[n_in:n_in + n_x]
        outs, xout = refs[n_in + n_x:n_in + n_x + n_out], refs[n_in + n_x + n_out:n_in + 2 * n_x + n_out]
        scr = refs[n_in + 2 * n_x + n_out:n_in + 2 * n_x + n_out + n_scr]
        sems = refs[n_in + 2 * n_x + n_out + n_scr:]
        first = functools.reduce(jnp.logical_and, [pl.program_id(k) == 0 for k in range(len(grid))])
        last = functools.reduce(jnp.logical_and, [pl.program_id(k) == grid[k] - 1 for k in range(len(grid))])
        ops = [_xch_ops(xin[k], xout[k], *sems[3 * k:3 * k + 3], host[k][1]) for k in range(n_x)]

        @pl.when(first)
        def _():
            for start, _ in ops:
                start()

        body(*ins, *outs, *scr)

        @pl.when(last)
        def _():
            for _, wait in ops:
                wait()

    res = pl.pallas_call(
        wrapped, name=name, grid=grid, in_specs=list(in_specs) + [_ANY] * n_x,
        out_specs=list(out_specs) + [_ANY] * n_x,
        out_shape=list(out_shape) + [_xch_shape(s, g) for s, g in host],
        scratch_shapes=list(scratch_shapes) + list(_XCH_SEMS) * n_x,
        compiler_params=_params(("arbitrary",) * len(grid)))(*args, *[s for s, _ in host])
    return res[:n_out], res[n_out:]


def _mm(a, b, mode, name, res=None, exact=False, tm_cap=1408, tn_cap=512, tk_cap=1408, host=None, out_dtype=F32):
    if mode == "nn":
        (M, K), (K2, N) = a.shape, b.shape
    elif mode == "nt":
        (M, K), (N, K2) = a.shape, b.shape
    else:
        (K, M), (K2, N) = a.shape, b.shape
    assert K == K2, (a.shape, b.shape, mode)
    tm, tn, tk = _tile(M, tm_cap), _tile(N, tn_cap), _tile(K, tk_cap)
    nk = K // tk
    a_spec = {"nn": pl.BlockSpec((tm, tk), lambda i, j, k: (i, k)),
              "nt": pl.BlockSpec((tm, tk), lambda i, j, k: (i, k)),
              "tn": pl.BlockSpec((tk, tm), lambda i, j, k: (k, i))}[mode]
    b_spec = {"nn": pl.BlockSpec((tk, tn), lambda i, j, k: (k, j)),
              "nt": pl.BlockSpec((tn, tk), lambda i, j, k: (j, k)),
              "tn": pl.BlockSpec((tk, tn), lambda i, j, k: (k, j))}[mode]
    o_spec = pl.BlockSpec((tm, tn), lambda i, j, k: (i, j))
    has_res = res is not None

    def body(*refs):
        if has_res:
            a_ref, b_ref, r_ref, o_ref, acc_ref = refs
        else:
            a_ref, b_ref, o_ref, acc_ref = refs
        k = pl.program_id(2)

        @pl.when(k == 0)
        def _():
            acc_ref[...] = jnp.zeros_like(acc_ref)

        x, y = a_ref[...], b_ref[...]
        if exact:
            x, y, prec = x.astype(F32), y.astype(F32), lax.Precision.HIGHEST
        else:
            x, y, prec = x.astype(BF16), y.astype(BF16), None
        if mode == "tn":
            x = x.T
        dims = (((1,), (1,)), ((), ())) if mode == "nt" else (((1,), (0,)), ((), ()))
        acc_ref[...] += lax.dot_general(x, y, dims, precision=prec, preferred_element_type=F32)

        @pl.when(k == nk - 1)
        def _():
            if has_res:
                o_ref[...] = (acc_ref[...] + r_ref[...]).astype(out_dtype)
            else:
                o_ref[...] = acc_ref[...].astype(out_dtype)

    in_specs = [a_spec, b_spec] + ([o_spec] if has_res else [])
    args = (a, b) + ((res,) if has_res else ())
    (out,), xouts = _pcall(body, name, (M // tm, N // tn, nk), in_specs, [o_spec],
                           [jax.ShapeDtypeStruct((M, N), out_dtype)], [pltpu.VMEM((tm, tn), F32)],
                           ("parallel", "parallel", "arbitrary"), args, host or ())
    return out if host is None else (out, xouts)


def _rowwise(fn, name, n_rows, tr, row_ins, const_ins, row_outs, red_outs=()):
    n_row_in, n_const, n_row_out, n_red = len(row_ins), len(const_ins), len(row_outs), len(red_outs)
    nblk, per = n_rows // tr, tr // 8
    in_specs = []
    for arr, w, idx, *side in row_ins:
        width = arr.shape[1] if w is None else w
        if not side:
            in_specs.append(pl.BlockSpec((tr, width), functools.partial(lambda i, j: (i, j), j=idx)))
        elif side[0] == "prev":
            in_specs.append(pl.BlockSpec((8, width), functools.partial(
                lambda i, j: (jnp.maximum(i * per - 1, 0), j), j=idx)))
        else:
            in_specs.append(pl.BlockSpec((8, width), functools.partial(
                lambda i, j: (jnp.minimum((i + 1) * per, nblk * per - 1), j), j=idx)))
    for c in const_ins:
        in_specs.append(pl.BlockSpec(c.shape, lambda i: (0, 0)))
    out_shape = [jax.ShapeDtypeStruct((n_rows, c), dt) for c, dt in row_outs]
    out_shape += [jax.ShapeDtypeStruct((1, c), F32) for c in red_outs]
    out_specs = [pl.BlockSpec((tr, c), lambda i: (i, 0)) for c, _ in row_outs]
    out_specs += [pl.BlockSpec((1, c), lambda i: (0, 0)) for c in red_outs]

    def body(*refs):
        i = pl.program_id(0)
        rows = [r[...] for r in refs[:n_row_in]]
        consts = [r[...] for r in refs[n_row_in:n_row_in + n_const]]
        outs = refs[n_row_in + n_const:]
        row_vals, red_vals = fn(i, rows, consts)
        for k in range(n_row_out):
            outs[k][...] = row_vals[k].astype(outs[k].dtype)
        if n_red:
            @pl.when(i == 0)
            def _():
                for k in range(n_red):
                    outs[n_row_out + k][...] = jnp.zeros_like(outs[n_row_out + k])

            for k in range(n_red):
                outs[n_row_out + k][...] += red_vals[k]

    res = pl.pallas_call(
        body, name=name, grid=(n_rows // tr,), in_specs=in_specs, out_specs=out_specs, out_shape=out_shape,
        compiler_params=_params(("arbitrary",)))(*[r[0] for r in row_ins], *const_ins)
    return res


def _rows_from_prev(cur, prev8, s, first):
    rolled = pltpu.roll(cur, s, 0)
    top = jnp.where(first, 0.0, pltpu.roll(prev8, s, 0))
    r8 = lax.broadcasted_iota(jnp.int32, prev8.shape, 0)
    return jnp.concatenate([jnp.where(r8 < s, top, rolled[:8]), rolled[8:]], axis=0)


def _rows_from_next(cur, next8, s, last):
    tr = cur.shape[0]
    rolled = pltpu.roll(cur, tr - s, 0)
    bottom = jnp.where(last, 0.0, pltpu.roll(next8, 8 - s, 0))
    r8 = lax.broadcasted_iota(jnp.int32, next8.shape, 0)
    return jnp.concatenate([rolled[:tr - 8], jnp.where(r8 >= 8 - s, bottom, rolled[tr - 8:])], axis=0)


def _row_ids(i, tr):
    return i * tr + lax.broadcasted_iota(jnp.int32, (tr, 1), 0)


def _colsum(x):
    return jnp.sum(x, axis=0, keepdims=True)


def _scan(a, b, name, b2=None, reverse=False):
    n_rows, c = b.shape
    tr = _tile(n_rows, 384)
    nblk = n_rows // tr
    has_a = a is not None
    has_b2 = b2 is not None

    def body(*refs):
        o_ref, carry, a_edge = refs[-3:]
        ins = list(refs[:-3])
        a_ref = ins.pop(0) if has_a else None
        b_ref = ins.pop(0)

        @pl.when(pl.program_id(0) == 0)
        def _():
            carry[...] = jnp.zeros_like(carry)
            a_edge[...] = jnp.zeros_like(a_edge)

        rows = lax.broadcasted_iota(jnp.int32, (tr, c), 0)
        bv = b_ref[...] + ins[0][...] if has_b2 else b_ref[...]
        av = a_ref[...] if has_a else None
        if has_a and reverse:
            a_first = av[0:1, :]
            av = _rows_from_next(av, a_edge[...], 1, False)
            a_edge[...] = jnp.broadcast_to(a_first, a_edge.shape)
        s = 1
        while s < tr:
            keep = rows < tr - s if reverse else rows >= s
            shift = tr - s if reverse else s
            b_sh = jnp.where(keep, pltpu.roll(bv, shift, 0), 0.0)
            if has_a:
                a_sh = jnp.where(keep, pltpu.roll(av, shift, 0), 1.0)
                bv = av * b_sh + bv
                av = av * a_sh
            else:
                bv = b_sh + bv
            s *= 2
        h = av * carry[0:1, :] + bv if has_a else carry[0:1, :] + bv
        o_ref[...] = h
        edge = 0 if reverse else tr - 1
        carry[...] = jnp.broadcast_to(h[edge:edge + 1, :], carry.shape)

    spec = pl.BlockSpec((tr, c), (lambda i: (nblk - 1 - i, 0)) if reverse else (lambda i: (i, 0)))
    args = ((a,) if has_a else ()) + (b,) + ((b2,) if has_b2 else ())
    return pl.pallas_call(
        body, name=name, grid=(n_rows // tr,), in_specs=[spec] * len(args), out_specs=spec,
        out_shape=jax.ShapeDtypeStruct((n_rows, c), F32),
        scratch_shapes=[pltpu.VMEM((8, c), F32), pltpu.VMEM((8, c), F32)],
        compiler_params=_params(("arbitrary",)))(*args)


def _rev_scan(a, g, name, g2=None):
    return _scan(a, g, name, b2=g2, reverse=True)


def _swa_masks(blk, n_pad):
    qi = lax.broadcasted_iota(jnp.int32, (BLOCK, 2 * BLOCK), 0)
    ki = lax.broadcasted_iota(jnp.int32, (BLOCK, 2 * BLOCK), 1)
    dist = qi + BLOCK - ki
    key_abs = (blk - 1) * BLOCK + ki
    return (dist >= 0) & (dist < BLOCK) & (key_abs >= n_pad)


_QA_COL = _WORK_OFF["qa"][0] // ATT_W
_KA_COL, _VA_COL = (_WORK_OFF[n][0] // LANE for n in ("ka", "va"))


def _swa_strips(prev_ref, cur_ref):
    x = jnp.concatenate([prev_ref[...], cur_ref[...]], axis=0)
    return [x.astype(BF16), pltpu.roll(x, HEAD_DIM, 1).astype(BF16)]


def _swa_swapped(h):
    return 0 if h % 2 == h // SWA_GROUP else 1


def _swa_specs(tq, nsub):
    prev = lambda col: pl.BlockSpec((BLOCK, LANE), lambda i: (jnp.maximum(i * nsub - 1, 0), col))
    cur = lambda col: pl.BlockSpec((tq, LANE), lambda i: (i, col))
    return [pl.BlockSpec((tq, ATT_W), lambda i: (i, _QA_COL)), prev(_KA_COL), cur(_KA_COL), prev(_VA_COL), cur(_VA_COL),
            pl.BlockSpec((SWA_Q_HEADS, BLOCK, 2 * BLOCK), lambda i: (0, 0, 0)),
            pl.BlockSpec((SWA_Q_HEADS, 1, LANE), lambda i: (0, 0, 0))]


def _swa_fwd(proj, bias, sinks, n_pad, host=()):
    Tp = proj.shape[0]
    tq = _tile(Tp, 384)
    nsub = tq // BLOCK

    def body(q_ref, kp_ref, kc_ref, vp_ref, vc_ref, b_ref, s_ref, o_ref, lse_ref):
        i = pl.program_id(0)
        low = _low_lanes()
        ks, vs = _swa_strips(kp_ref, kc_ref), _swa_strips(vp_ref, vc_ref)
        for j in range(nsub):
            rows, keys = slice(j * BLOCK, (j + 1) * BLOCK), slice(j * BLOCK, (j + 2) * BLOCK)
            mask = _swa_masks(i * nsub + j, n_pad)
            for hb in range(SWA_Q_HEADS // 2):
                pair = slice(hb * LANE, (hb + 1) * LANE)
                qh = _split_pair(q_ref[rows, pair] * SCALE, low)
                outs = []
                for e in range(2):
                    h = 2 * hb + e
                    kk, vv = ks[_swa_swapped(h)][keys], vs[_swa_swapped(h)][keys]
                    s = lax.dot_general(qh[e], kk, (((1,), (1,)), ((), ())), preferred_element_type=F32)
                    s = jnp.where(mask, s + b_ref[h], NEG_INF)
                    sink = s_ref[h, :, 0:1]
                    m = jnp.maximum(jnp.max(s, axis=1, keepdims=True), sink)
                    p = jnp.exp(s - m)
                    denom = jnp.sum(p, axis=1, keepdims=True) + jnp.exp(sink - m)
                    outs.append(jnp.dot((p / denom).astype(BF16), vv, preferred_element_type=F32))
                    lse_ref[h, rows, :] = m + jnp.log(denom)
                o_ref[rows, pair] = jnp.where(low, outs[0], outs[1])

    return _pcall(
        body, "swa_fwd", (Tp // tq,), _swa_specs(tq, nsub),
        [pl.BlockSpec((tq, ATT_W), lambda i: (i, 0)), pl.BlockSpec((SWA_Q_HEADS, tq, 1), lambda i: (0, i, 0))],
        [jax.ShapeDtypeStruct((Tp, ATT_W), F32), jax.ShapeDtypeStruct((SWA_Q_HEADS, Tp, 1), F32)],
        [], ("arbitrary",), (proj, proj, proj, proj, proj, bias, sinks), host)


def _swa_bwd(proj, bias, sinks, o, lse, do, n_pad, host=()):
    Tp = proj.shape[0]
    tq = _tile(Tp, 384)
    nsub = tq // BLOCK
    nq = Tp // tq
    strip = BLOCK + tq

    def body(q_ref, kp_ref, kc_ref, vp_ref, vc_ref, b_ref, s_ref, o_ref, lse_ref, do_ref,
             dq_ref, dk_ref, dv_ref, db_ref, ds_ref, dk_sw, dv_sw):
        i = pl.program_id(0)

        @pl.when(i == 0)
        def _():
            db_ref[...] = jnp.zeros_like(db_ref)
            ds_ref[...] = jnp.zeros_like(ds_ref)

        for ref in (dk_ref, dv_ref, dk_sw, dv_sw):
            ref[...] = jnp.zeros_like(ref)
        low = _low_lanes()
        ks, vs = _swa_strips(kp_ref, kc_ref), _swa_strips(vp_ref, vc_ref)
        for j in range(nsub):
            rows, keys = slice(j * BLOCK, (j + 1) * BLOCK), slice(j * BLOCK, (j + 2) * BLOCK)
            mask = _swa_masks(i * nsub + j, n_pad)
            for hb in range(SWA_Q_HEADS // 2):
                pair = slice(hb * LANE, (hb + 1) * LANE)
                qh = _split_pair(q_ref[rows, pair] * SCALE, low)
                dof = do_ref[rows, pair]
                prod = dof * o_ref[rows, pair]
                doh = _split_pair(dof, low)
                dqs = []
                for e in range(2):
                    h = 2 * hb + e
                    sw = _swa_swapped(h)
                    kk, vv = ks[sw][keys], vs[sw][keys]
                    delta = jnp.sum(jnp.where(low, prod, 0.0) if e == 0 else jnp.where(low, 0.0, prod),
                                    axis=1, keepdims=True)
                    lse_h = lse_ref[h, rows, :]
                    s = lax.dot_general(qh[e], kk, (((1,), (1,)), ((), ())), preferred_element_type=F32)
                    s = jnp.where(mask, s + b_ref[h], NEG_INF)
                    p = jnp.exp(s - lse_h)
                    p_sink = jnp.exp(s_ref[h, :, 0:1] - lse_h)
                    dp = lax.dot_general(doh[e], vv, (((1,), (1,)), ((), ())), preferred_element_type=F32)
                    ds = p * (dp - delta)
                    db_ref[h] += ds
                    ds_ref[h] += jnp.broadcast_to(-jnp.sum(p_sink * delta, axis=0, keepdims=True), (1, LANE))
                    dqs.append(jnp.dot(ds.astype(BF16), kk, preferred_element_type=F32))
                    dk_h = jnp.dot(ds.T.astype(BF16), qh[e], preferred_element_type=F32)
                    dv_h = jnp.dot(p.T.astype(BF16), doh[e], preferred_element_type=F32)
                    if sw:
                        dk_sw[keys, :] += dk_h
                        dv_sw[keys, :] += dv_h
                    else:
                        dk_ref[0, keys, :] += dk_h
                        dv_ref[0, keys, :] += dv_h
                dq_ref[rows, pair] = jnp.where(low, dqs[0], dqs[1]) * SCALE
        dk_ref[0] += pltpu.roll(dk_sw[...], HEAD_DIM, 1)
        dv_ref[0] += pltpu.roll(dv_sw[...], HEAD_DIM, 1)

    wide = pl.BlockSpec((tq, ATT_W), lambda i: (i, 0))
    sspec = pl.BlockSpec((1, strip, LANE), lambda i: (i, 0, 0))
    return _pcall(
        body, "swa_bwd", (nq,),
        _swa_specs(tq, nsub) + [wide, pl.BlockSpec((SWA_Q_HEADS, tq, 1), lambda i: (0, i, 0)), wide],
        [wide, sspec, sspec, pl.BlockSpec((SWA_Q_HEADS, BLOCK, 2 * BLOCK), lambda i: (0, 0, 0)),
         pl.BlockSpec((SWA_Q_HEADS, 1, LANE), lambda i: (0, 0, 0))],
        [jax.ShapeDtypeStruct((Tp, ATT_W), F32), jax.ShapeDtypeStruct((nq, strip, LANE), F32),
         jax.ShapeDtypeStruct((nq, strip, LANE), F32),
         jax.ShapeDtypeStruct((SWA_Q_HEADS, BLOCK, 2 * BLOCK), F32), jax.ShapeDtypeStruct((SWA_Q_HEADS, 1, LANE), F32)],
        [pltpu.VMEM((strip, LANE), F32), pltpu.VMEM((strip, LANE), F32)], ("arbitrary",),
        (proj, proj, proj, proj, proj, bias, sinks, o, lse, do), host)


def _fox_scores(qb, kk, cq, ck, diag, tq):
    s = lax.dot_general(qb, kk, (((1,), (1,)), ((), ())), preferred_element_type=F32)
    s = s + cq - ck
    return _fox_causal(s, tq) if diag else s


def _fox_causal(s, tq):
    r = lax.broadcasted_iota(jnp.int32, (tq, tq), 0)
    c = lax.broadcasted_iota(jnp.int32, (tq, tq), 1)
    return jnp.where(c <= r, s, NEG_INF)


_FOX_PAIRS = FOX_HEADS // 2
_QF_COL, _KF_COL, _VF_COL = (_WORK_OFF[n][0] // LANE for n in ("qf", "kf", "vf"))


def _low_lanes():
    return lax.broadcasted_iota(jnp.int32, (1, LANE), 1) < HEAD_DIM


def _split_pair(x, low):
    return [jnp.where(low, x, 0.0).astype(BF16), jnp.where(low, 0.0, x).astype(BF16)]


def _fox_fwd(proj, ccol, crow, tq, host=(), prefetch=True):
    Tp = proj.shape[0]
    nq = Tp // tq

    def body(q_ref, k_ref, v_ref, cc_ref, cr_ref, o_ref, lse_ref):
        qi = pl.program_id(1)
        low = _low_lanes()
        qh = _split_pair(q_ref[...] * SCALE, low)
        cq = [cc_ref[0], cc_ref[1]]

        def scores(kb):
            kk = k_ref[pl.ds(pl.multiple_of(kb * tq, tq), tq), :].astype(BF16)
            return [_fox_scores(qh[e], kk, cq[e], cr_ref[e, kb], False, tq) for e in range(2)]

        def softmax_pv(kb, s, carry, diag):
            vv = v_ref[pl.ds(pl.multiple_of(kb * tq, tq), tq), :].astype(BF16)
            stats, upd = [], []
            for e in range(2):
                m, l = carry[2 * e], carry[2 * e + 1]
                se = _fox_causal(s[e], tq) if diag else s[e]
                m_new = jnp.maximum(m, jnp.max(se, axis=1, keepdims=True))
                alpha = jnp.exp(m - m_new)
                p = jnp.exp(se - m_new)
                stats += [m_new, alpha * l + jnp.sum(p, axis=1, keepdims=True)]
                upd.append(alpha * carry[4] + jnp.dot(p.astype(BF16), vv, preferred_element_type=F32))
            return (*stats, jnp.where(low, upd[0], upd[1]))

        col = lambda val: jnp.full((tq, 1), val, F32)
        init = (col(NEG_INF), col(0.0), col(NEG_INF), col(0.0), jnp.zeros((tq, LANE), F32))
        if prefetch:
            def step(kb, c):
                s_next = scores(kb + 1)
                return (*softmax_pv(kb, c[5:], c[:5], False), *s_next)

            c = lax.fori_loop(0, qi, step, (*init, *scores(0)))
            m0, l0, m1, l1, acc = softmax_pv(qi, c[5:], c[:5], True)
        else:
            c = lax.fori_loop(0, qi, lambda kb, c: softmax_pv(kb, scores(kb), c, False), init)
            m0, l0, m1, l1, acc = softmax_pv(qi, scores(qi), c, True)
        o_ref[...] = jnp.where(low, acc / l0, acc / l1)
        lse_ref[0] = m0 + jnp.log(l0)
        lse_ref[1] = m1 + jnp.log(l1)

    cspec = pl.BlockSpec((2, tq, 1), lambda h, i: (h, i, 0))
    return _pcall(
        body, "fox_fwd", (_FOX_PAIRS, nq),
        [pl.BlockSpec((tq, LANE), lambda h, i: (i, _QF_COL + h)), pl.BlockSpec((Tp, LANE), lambda h, i: (0, _KF_COL + h)),
         pl.BlockSpec((Tp, LANE), lambda h, i: (0, _VF_COL + h)), cspec,
         pl.BlockSpec((2, nq, 1, tq), lambda h, i: (h, 0, 0, 0))],
        [pl.BlockSpec((tq, LANE), lambda h, i: (i, h)), cspec],
        [jax.ShapeDtypeStruct((Tp, FOX_HEADS * HEAD_DIM), F32), jax.ShapeDtypeStruct((FOX_HEADS, Tp, 1), F32)],
        [], ("parallel", "arbitrary"), (proj, proj, proj, ccol, crow), host)


def _fox_bwd_dq(proj, ccol, crow, o, lse, do, tq, host=()):
    Tp = proj.shape[0]
    nq = Tp // tq

    def body(q_ref, k_ref, v_ref, cc_ref, cr_ref, o_ref, lse_ref, do_ref, dq_ref, dl_ref, dc_ref):
        qi = pl.program_id(1)
        low = _low_lanes()
        qh = _split_pair(q_ref[...] * SCALE, low)
        dof = do_ref[...]
        prod = dof * o_ref[...]
        delta = [jnp.sum(jnp.where(low, prod, 0.0), axis=1, keepdims=True),
                 jnp.sum(jnp.where(low, 0.0, prod), axis=1, keepdims=True)]
        doh = _split_pair(dof, low)
        cq = [cc_ref[0], cc_ref[1]]
        lse = [lse_ref[0], lse_ref[1]]

        def step(kb, carry, diag):
            rows = pl.ds(pl.multiple_of(kb * tq, tq), tq)
            kk, vv = k_ref[rows, :].astype(BF16), v_ref[rows, :].astype(BF16)
            dqs, dcs = [], []
            for e in range(2):
                s = _fox_scores(qh[e], kk, cq[e], cr_ref[e, kb], diag, tq)
                p = jnp.exp(s - lse[e])
                dp = lax.dot_general(doh[e], vv, (((1,), (1,)), ((), ())), preferred_element_type=F32)
                ds = p * (dp - delta[e])
                dqs.append(jnp.dot(ds.astype(BF16), kk, preferred_element_type=F32))
                dcs.append(carry[1 + e] + jnp.sum(ds, axis=1, keepdims=True))
            return (carry[0] + jnp.where(low, dqs[0], dqs[1]), *dcs)

        init = (jnp.zeros((tq, LANE), F32), jnp.zeros((tq, 1), F32), jnp.zeros((tq, 1), F32))
        dq, dc0, dc1 = step(qi, lax.fori_loop(0, qi, functools.partial(step, diag=False), init), True)
        dq_ref[...] = dq * SCALE
        dl_ref[0], dl_ref[1] = delta
        dc_ref[0], dc_ref[1] = dc0, dc1

    cspec = pl.BlockSpec((2, tq, 1), lambda h, i: (h, i, 0))
    pspec = pl.BlockSpec((tq, LANE), lambda h, i: (i, h))
    stat = jax.ShapeDtypeStruct((FOX_HEADS, Tp, 1), F32)
    return _pcall(
        body, "fox_bwd_dq", (_FOX_PAIRS, nq),
        [pl.BlockSpec((tq, LANE), lambda h, i: (i, _QF_COL + h)), pl.BlockSpec((Tp, LANE), lambda h, i: (0, _KF_COL + h)),
         pl.BlockSpec((Tp, LANE), lambda h, i: (0, _VF_COL + h)), cspec,
         pl.BlockSpec((2, nq, 1, tq), lambda h, i: (h, 0, 0, 0)), pspec, cspec, pspec],
        [pspec, cspec, cspec], [jax.ShapeDtypeStruct((Tp, FOX_HEADS * HEAD_DIM), F32), stat, stat],
        [], ("parallel", "arbitrary"), (proj, proj, proj, ccol, crow, o, lse, do), host)


def _fox_bwd_dkv(proj, ccol, crow, lse, delta, do, tq, host=()):
    Tp = proj.shape[0]
    nq = Tp // tq
    pair = lambda z: z.reshape(_FOX_PAIRS, 2, Tp).transpose(0, 2, 1)
    stats = jnp.concatenate([pair(ccol), pair(lse), pair(delta)], axis=-1).reshape(_FOX_PAIRS, nq, tq, 6)

    def body(q_ref, do_ref, st_ref, k_ref, v_ref, cr_ref, dk_ref, dv_ref, dc_ref):
        ki = pl.program_id(1)
        low = _low_lanes()
        kk, vv = k_ref[...].astype(BF16), v_ref[...].astype(BF16)

        def step(qb_i, carry, diag):
            rows = pl.ds(pl.multiple_of(qb_i * tq, tq), tq)
            qh = _split_pair(q_ref[rows, :] * SCALE, low)
            doh = _split_pair(do_ref[rows, :], low)
            st = st_ref[0, qb_i]
            dk, dv = carry[0], carry[1]
            dcs = []
            for e in range(2):
                s = _fox_scores(qh[e], kk, st[:, e:e + 1], cr_ref[e, 0], diag, tq)
                p = jnp.exp(s - st[:, 2 + e:3 + e])
                dp = lax.dot_general(doh[e], vv, (((1,), (1,)), ((), ())), preferred_element_type=F32)
                ds = p * (dp - st[:, 4 + e:5 + e])
                dv = dv + jnp.dot(p.T.astype(BF16), doh[e], preferred_element_type=F32)
                dk = dk + jnp.dot(ds.T.astype(BF16), qh[e], preferred_element_type=F32)
                dcs.append(carry[2 + e] - jnp.sum(ds, axis=0, keepdims=True))
            return (dk, dv, *dcs)

        init = (jnp.zeros((tq, LANE), F32), jnp.zeros((tq, LANE), F32), jnp.zeros((1, tq), F32), jnp.zeros((1, tq), F32))
        dk, dv, dc0, dc1 = lax.fori_loop(ki + 1, nq, functools.partial(step, diag=False), step(ki, init, True))
        dk_ref[...] = dk
        dv_ref[...] = dv
        dc_ref[0, 0] = dc0
        dc_ref[1, 0] = dc1

    rspec = pl.BlockSpec((2, 1, 1, tq), lambda h, i: (h, i, 0, 0))
    pspec = pl.BlockSpec((tq, LANE), lambda h, i: (i, h))
    wide = jax.ShapeDtypeStruct((Tp, FOX_HEADS * HEAD_DIM), F32)
    return _pcall(
        body, "fox_bwd_dkv", (_FOX_PAIRS, nq),
        [pl.BlockSpec((Tp, LANE), lambda h, i: (0, _QF_COL + h)), pl.BlockSpec((Tp, LANE), lambda h, i: (0, h)),
         pl.BlockSpec((1, nq, tq, 6), lambda h, i: (h, 0, 0, 0)),
         pl.BlockSpec((tq, LANE), lambda h, i: (i, _KF_COL + h)), pl.BlockSpec((tq, LANE), lambda h, i: (i, _VF_COL + h)),
         rspec],
        [pspec, pspec, rspec], [wide, wide, jax.ShapeDtypeStruct((FOX_HEADS, nq, 1, tq), F32)],
        [], ("parallel", "arbitrary"), (proj, do, stats, proj, proj, crow), host)


def _adam(gparts, w, m, v, name):
    P, R, C = gparts.shape
    tr = _tile(R, 256, 8)

    def body(g_ref, w_ref, m_ref, v_ref, go_ref, d_ref, mo_ref, vo_ref):
        g = g_ref[0].astype(F32)
        for p in range(1, P):
            g = g + g_ref[p].astype(F32)
        m2 =ADAM_B1 * m_ref[...] + (1.0 - ADAM_B1) * g
        v2 = ADAM_B2 * v_ref[...] + (1.0 - ADAM_B2) * (g * g)
        m_hat = m2 / (1.0 - ADAM_B1 ** ADAM_STEP)
        v_hat = v2 / (1.0 - ADAM_B2 ** ADAM_STEP)
        go_ref[...] = g
        d_ref[...] = -ADAM_LR * (m_hat / (jnp.sqrt(v_hat) + ADAM_EPS) + ADAM_WD * w_ref[...])
        mo_ref[...] = m2
        vo_ref[...] = v2

    spec = pl.BlockSpec((tr, C), lambda i: (i, 0))
    shp = jax.ShapeDtypeStruct((R, C), F32)
    return pl.pallas_call(
        body, name=name, grid=(R // tr,),
        in_specs=[pl.BlockSpec((P, tr, C), lambda i: (0, i, 0)), spec, spec, spec],
        out_specs=[spec] * 4, out_shape=[shp] * 4, compiler_params=_params(("parallel",)))(gparts, w, m, v)


def _sum_parts(gparts, name):
    P, R, C = gparts.shape
    tr = _tile(R, 256, 8)

    def body(g_ref, o_ref):
        g = g_ref[0]
        for p in range(1, P):
            g = g + g_ref[p]
        o_ref[...] = g

    return pl.pallas_call(
        body, name=name, grid=(R // tr,), in_specs=[pl.BlockSpec((P, tr, C), lambda i: (0, i, 0))],
        out_specs=pl.BlockSpec((tr, C), lambda i: (i, 0)), out_shape=jax.ShapeDtypeStruct((R, C), F32),
        compiler_params=_params(("parallel",)))(gparts)


def _pack(pieces, width, row_mult):
    flat = jnp.concatenate([p.reshape(-1) for p in pieces])
    n = flat.shape[0]
    rows = -(-n // width)
    rows = -(-rows // row_mult) * row_mult
    return jnp.pad(flat, (0, rows * width - n)).reshape(rows, width)


def _unpack(flat2d, shapes, lead=None):
    out, off = [], 0
    if lead is None:
        flat = flat2d.reshape(-1)
        for s in shapes:
            n = int(np.prod(s))
            out.append(flat[off:off + n].reshape(s))
            off += n
    else:
        flat = flat2d.reshape(lead, -1)
        for s in shapes:
            n = int(np.prod(s))
            out.append(flat[:, off:off + n].reshape((lead,) + tuple(s)))
            off += n
    return out


def _col_pieces(off, width, shard_w, fetch):
    out, c, end = [], off, off + width
    while c < end:
        d = c // shard_w
        hi = min(end, (d + 1) * shard_w)
        out.append(fetch(d, c - d * shard_w, hi - d * shard_w))
        c = hi
    return out


def _work_cols_of_orig(lo, hi, fetch):
    out = []
    for name, ow in _ORIG:
        o = _ORIG_OFF[name][0]
        a, b = max(lo, o), min(hi, o + ow)
        if a < b:
            w0 = _WORK_OFF[name][0]
            out.append(fetch(w0 + a - o, w0 + b - o))
    return out


def _t5_bucket_np(dist):
    max_exact = REL_BUCKETS // 2
    d = np.maximum(dist, 0)
    scaled = (np.log(np.maximum(d, 1).astype(np.float32) / np.float32(max_exact))
              / np.float32(math.log(REL_MAX_DIST / max_exact))).astype(np.float32)
    large = np.minimum(max_exact + (scaled * np.float32(REL_BUCKETS - max_exact)).astype(np.int32), REL_BUCKETS - 1)
    return np.where(d < max_exact, d, large)


def _bucket_onehot():
    q_idx = np.arange(BLOCK)[:, None]
    k_idx = np.arange(2 * BLOCK)[None, :]
    bucket = _t5_bucket_np(q_idx + BLOCK - k_idx).reshape(-1)
    oh = np.zeros((LANE, BLOCK * 2 * BLOCK), np.float32)
    oh[bucket, np.arange(bucket.shape[0])] = 1.0
    return oh


def kernel(x, meta_tokens, rel_bias_table, norm_mix, w_in, swa_sinks, fox_forget_bias, conv_w, conv_b, lru_w_r, lru_b_r, lru_w_i, lru_b_i, lru_lambda, w_branch, w_out, norm_ffn, w_ffn_in, w_ffn_out, norm_final, loss_target, m_meta_tokens, m_rel_bias_table, m_norm_mix, m_w_in, m_swa_sinks, m_fox_forget_bias, m_conv_w, m_conv_b, m_lru_w_r, m_lru_b_r, m_lru_w_i, m_lru_b_i, m_lru_lambda, m_w_branch, m_w_out, m_norm_ffn, m_w_ffn_in, m_w_ffn_out, m_norm_final, v_meta_tokens, v_rel_bias_table, v_norm_mix, v_w_in, v_swa_sinks, v_fox_forget_bias, v_conv_w, v_conv_b, v_lru_w_r, v_lru_b_r, v_lru_w_i, v_lru_b_i, v_lru_lambda, v_w_branch, v_w_out, v_norm_ffn, v_w_ffn_in, v_w_ffn_out, v_norm_final):
    S = x.shape[1]
    T = N_META + S
    n_pad = (-T) % BLOCK
    Tp = T + n_pad
    first = n_pad + N_META
    TR = _tile(Tp, 384)
    TQ = _tile(Tp, 384)
    TQF = _tile(Tp, FOX_TILE)
    me =4 * lax.axis_index("x") + 2 * lax.axis_index("y") + lax.axis_index("c")

    big_names = ("w_in", "w_ffn_in", "w_ffn_out", "w_branch", "w_out")
    big_w = dict(w_in=w_in, w_ffn_in=w_ffn_in, w_ffn_out=w_ffn_out, w_branch=w_branch, w_out=w_out)
    big_m = dict(w_in=m_w_in, w_ffn_in=m_w_ffn_in, w_ffn_out=m_w_ffn_out, w_branch=m_w_branch, w_out=m_w_out)
    big_v = dict(w_in=v_w_in, w_ffn_in=v_w_ffn_in, w_ffn_out=v_w_ffn_out, w_branch=v_w_branch, w_out=v_w_out)
    rows2d = lambda w: w.reshape(-1, w.shape[-1])
    in_shard_w = IN_COLS // N_DEV
    cat_axis = dict(w_in=-1, w_ffn_in=-1, w_ffn_out=0, w_branch=-1, w_out=0)

    def w_send(n, l):
        return rows2d(big_w[n][l]).astype(BF16)

    def w_build(n, g):
        shard = lambda d: g[d].reshape(big_w[n].shape[1:])
        if cat_axis[n] == 0:
            return g.reshape(-1, g.shape[-1])
        if n != "w_in":
            return jnp.concatenate([shard(d) for d in range(N_DEV)], axis=cat_axis[n])
        parts = []
        for name, width in _WORK:
            off, ow = _ORIG_OFF[name]
            parts += _col_pieces(off, ow, in_shard_w, lambda d, lo, hi: shard(d)[:, lo:hi])
            if width > ow:
                parts.append(jnp.zeros((D_MODEL, width - ow), BF16))
        return jnp.concatenate(parts, axis=-1)

    W_in, W_ffn_in, W_ffn_out, W_branch, W_out = ([None] * DEPTH for _ in range(5))
    W_in[0] = w_build("w_in", _exchange(w_send("w_in", 0), True, "gather_w_in"))

    small_sh = _exchange(_pack([meta_tokens, conv_w], D_MODEL, 8), True, "gather_small")
    g_meta, g_conv_w = _unpack(small_sh, [meta_tokens.shape, conv_w.shape], lead=N_DEV)
    meta_full = g_meta.transpose(1, 0, 2).reshape(N_META, D_MODEL)
    conv_w_full = g_conv_w.transpose(1, 2, 0, 3).reshape(DEPTH, CONV_WIDTH, LRU_WIDTH)

    onehot = jnp.asarray(_bucket_onehot())
    table_t = jnp.pad(rel_bias_table.T, ((0, 0), (0, LANE - REL_BUCKETS)))
    bias = _mm(table_t, onehot, "nn", "swa_bias", exact=True, tn_cap=4096).reshape(SWA_Q_HEADS, BLOCK, 2 * BLOCK)

    def dense_blocks(w):
        rows = []
        for b in range(LRU_BLOCKS):
            rows.append(jnp.pad(w[b], ((0, 0), (b * LRU_BLOCK_DIM, (LRU_BLOCKS - 1 - b) * LRU_BLOCK_DIM))))
        return jnp.concatenate(rows, axis=0)

    h = jnp.concatenate([jnp.zeros((n_pad, D_MODEL), F32), meta_full, x[0]], axis=0)
    saved = []

    for l in range(DEPTH):
        sv = {"h": h}
        g_mix = norm_mix[l][None, :]

        def norm_fwd(i, rows, consts):
            xx = rows[0]
            return [xx * lax.rsqrt(jnp.mean(xx * xx, axis=1, keepdims=True) + EPS) * consts[0]], []

        (u,) = _rowwise(norm_fwd, "norm_fwd", Tp, TR, [(h, None, 0)], [g_mix], [(D_MODEL, BF16)])
        proj, got = _mm(u, W_in[l], "nn", "proj", tn_cap=384,
                        host=[(w_send(n, l), True) for n in ("w_branch", "w_out", "w_ffn_out")])
        W_branch[l], W_out[l], W_ffn_out[l] = (w_build(n, g) for n, g in zip(("w_branch", "w_out", "w_ffn_out"), got))
        half = D_MODEL // 2
        next_w_in = [] if l + 1 == DEPTH else [rows2d(w_in[l + 1])[k * half:(k + 1) * half].astype(BF16) for k in (0, 1)]
        next_got = []
        sv["u"], sv["proj"] = u, proj

        def col(name):
            off, w = _WORK_OFF[name]
            return (proj, w, off // w)

        sinks_l = jnp.broadcast_to(swa_sinks[l][:, None, None], (SWA_Q_HEADS, 1, LANE))
        (o_a, lse_a), got = _swa_fwd(proj, bias, sinks_l, n_pad, host=[(s, True) for s in next_w_in[:1]])
        next_got += got
        sv.update(sinks=sinks_l, lse_a=lse_a, o_a=o_a)

        fbias = jnp.pad(fox_forget_bias[l], (0, LANE - FOX_HEADS))[None, :]

        def logf_fwd(i, rows, consts):
            return [_log_sigmoid(rows[0] + consts[0])], []

        (logf,) = _rowwise(logf_fwd, "logf_fwd", Tp, TR, [col("fl")], [fbias], [(LANE, F32)])
        cum = _scan(None, logf, "cumsum")
        cum_h = cum[:, :FOX_HEADS].T
        ccol = cum_h[:, :, None]
        crow = jnp.where(jnp.arange(Tp)[None, :] < n_pad, -NEG_INF, cum_h).reshape(FOX_HEADS, Tp // TQF, 1, TQF)
        (o_f, lse_f), got = _fox_fwd(proj, ccol, crow, TQF, host=[(w_send("w_ffn_in", l), True)],
                                     prefetch=TQF <= 384)
        W_ffn_in[l] = w_build("w_ffn_in", got[0])
        sv.update(fbias=fbias, ccol=ccol, crow=crow, lse_f=lse_f, o_f=o_f)

        cw = conv_w_full[l]
        cb = conv_b[l][None, :]

        def conv_taps(i, cur, prev8):
            return [_rows_from_prev(cur, prev8, CONV_WIDTH - 1 - t, i == 0) for t in range(CONV_WIDTH - 1)] + [cur]

        def conv_fwd(i, rows, consts):
            w, b = consts
            taps = conv_taps(i, rows[0], rows[1])
            acc = taps[0] * w[0:1, :]
            for t in range(1, CONV_WIDTH):
                acc = acc + taps[t] * w[t:t + 1, :]
            return [acc + b], []

        xc_blocks = [col("xc"), col("xc") + ("prev",)]
        (xconv,) = _rowwise(conv_fwd, "conv_fwd", Tp, TR, xc_blocks, [cw, cb], [(LRU_WIDTH, F32)])
        w_ri = jnp.concatenate([dense_blocks(lru_w_r[l]), dense_blocks(lru_w_i[l])], axis=1)
        pre = _mm(xconv, w_ri, "nn", "lru_gates")
        lru_consts = [lru_b_r[l][None, :], lru_b_i[l][None, :], lru_lambda[l][None, :]]

        def lru_fwd(i, rows, consts):
            pr, xv = rows
            b_r, b_i, lam = consts
            r = _sigmoid(pr[:, :LRU_WIDTH] + b_r)
            gi = _sigmoid(pr[:, LRU_WIDTH:] + b_i)
            log_a = LRU_C * r * _log_sigmoid(lam)
            valid = (_row_ids(i, TR) >= n_pad).astype(F32)
            inp = jnp.sqrt(_neg_expm1(2.0 * log_a)) * (gi * xv) * valid
            return [jnp.exp(log_a), inp], []

        a_dec, inp = _rowwise(lru_fwd, "lru_fwd", Tp, TR, [(pre, None, 0), (xconv, None, 0)], lru_consts,
                              [(LRU_WIDTH, F32), (LRU_WIDTH, F32)])
        hs = _scan(a_dec, inp, "lru_scan")

        def oc_fwd(i, rows, consts):
            return [rows[0] * _gelu(rows[1])], []

        (o_c,) = _rowwise(oc_fwd, "oc_fwd", Tp, TR, [(hs, None, 0), col("yc")], [], [(LRU_WIDTH, BF16)])
        sv.update(cw=cw, xconv=xconv, w_ri=w_ri, pre=pre, lru_consts=lru_consts, a_dec=a_dec, hs=hs, o_c=o_c)

        ba = _mm(o_a, W_branch[l][0], "nn", "branch")
        bf = _mm(o_f, W_branch[l][1], "nn", "branch")
        bc = _mm(o_c, W_branch[l][2], "nn", "branch")

        def merge_fwd(i, rows, consts):
            g, b0, b1, b2 = rows
            valid = (_row_ids(i, BLOCK) >= n_pad).astype(F32)
            mg = (_sigmoid(g[:, :D_MODEL]) * b0 + _sigmoid(g[:, D_MODEL:2 * D_MODEL]) * b1
                  + _sigmoid(g[:, 2 * D_MODEL:]) * b2)
            return [mg * valid], []

        (merged,) = _rowwise(merge_fwd, "merge_fwd", Tp, BLOCK,
                             [col("gates"), (ba, None, 0), (bf, None, 0), (bc, None, 0)], [], [(D_MODEL, BF16)])
        h1 = _mm(merged, W_out[l], "nn", "out_proj", res=h)
        sv.update(ba=ba, bf=bf, bc=bc, merged=merged, h1=h1)

        g_ffn = norm_ffn[l][None, :]
        (u2,) = _rowwise(norm_fwd, "norm_fwd", Tp, TR, [(h1, None, 0)], [g_ffn], [(D_MODEL, BF16)])
        ff, got = _mm(u2, W_ffn_in[l], "nn", "ffn_in", host=[(s, True) for s in next_w_in[1:]])
        next_got += got
        if l + 1 < DEPTH:
            W_in[l + 1] = w_build("w_in", jnp.concatenate(next_got, axis=1))

        def act_fwd(i, rows, consts):
            gate, up = rows
            return [gate * _sigmoid(gate) * up], []

        (act,) = _rowwise(act_fwd, "act_fwd", Tp, TR, [(ff, D_FF, 0), (ff, D_FF, 1)], [], [(D_FF, BF16)])
        h = _mm(act, W_ffn_out[l], "nn", "ffn_out", res=h1)
        sv.update(u2=u2, ff=ff, act=act)
        saved.append(sv)

    tgt = jnp.concatenate([jnp.zeros((first, D_MODEL), F32), loss_target[0]], axis=0)
    g_fin = norm_final[None, :]

    def head(i, rows, consts):
        xx, tg = rows
        g = consts[0]
        valid = (_row_ids(i, TR) >= first).astype(F32)
        rstd = lax.rsqrt(jnp.mean(xx * xx, axis=1, keepdims=True) + EPS)
        xhat = xx * rstd
        err = (xhat * g - tg) * valid
        loss_rows = 0.5 * jnp.mean(err * err, axis=1, keepdims=True)
        dy = err * (1.0 / D_MODEL)
        dxhat = dy * g
        dx = rstd * (dxhat - xhat * jnp.mean(dxhat * xhat, axis=1, keepdims=True))
        return [dx], [jnp.broadcast_to(_colsum(loss_rows), (1, LANE)), _colsum(dy * xhat)]

    dh, loss_part, d_norm_final = _rowwise(head, "loss_head", Tp, TR, [(h, None, 0), (tgt, None, 0)], [g_fin],
                                           [(D_MODEL, F32)], [LANE, D_MODEL])
    loss = lax.psum(loss_part[0, 0], ("x", "y", "c"))

    def norm_bwd_call(xin, g, du, dres):
        def norm_bwd(i, rows, consts):
            xx, dd, rr = rows
            gg = consts[0]
            valid = (_row_ids(i, TR) >= n_pad).astype(F32)
            rstd = lax.rsqrt(jnp.mean(xx * xx, axis=1, keepdims=True) + EPS)
            xhat = xx * rstd
            dxhat = dd * gg
            dx = rstd * (dxhat - xhat * jnp.mean(dxhat * xhat, axis=1, keepdims=True))
            return [rr + dx * valid], [_colsum(dd * xhat)]

        return _rowwise(norm_bwd, "norm_bwd", Tp, TR, [(xin, None, 0), (du, None, 0), (dres, None, 0)], [g],
                        [(D_MODEL, F32)], [D_MODEL])

    grads = {k: [None] * DEPTH for k in ("norm_mix", "w_in", "swa_sinks", "fox_forget_bias", "conv_w", "conv_b",
                                         "lru_w_r", "lru_b_r", "lru_w_i", "lru_b_i", "lru_lambda", "w_branch",
                                         "w_out", "norm_ffn", "w_ffn_in", "w_ffn_out")}
    d_bias_total = None

    def g_send(n, g):
        if n == "w_in":
            pieces = [jnp.concatenate(_work_cols_of_orig(d * in_shard_w, (d + 1) * in_shard_w,
                                                         lambda lo, hi: g[:, lo:hi]), axis=-1) for d in range(N_DEV)]
        elif cat_axis[n] == 0:
            return g.reshape(N_DEV, -1, g.shape[-1])
        else:
            ax = cat_axis[n] % g.ndim
            w = big_w[n].shape[1 + ax]
            pieces = [lax.slice_in_dim(g, d * w, (d + 1) * w, axis=ax) for d in range(N_DEV)]
        return jnp.stack([rows2d(p) for p in pieces])

    recv = {n: [None] * DEPTH for n in big_names}
    for l in reversed(range(DEPTH)):
        sv = saved[l]
        proj = sv["proj"]

        def col(name):
            off, w = _WORK_OFF[name]
            return (proj, w, off // w)

        dh2 = dh
        d_act = _mm(dh2, W_ffn_out[l], "nt", "d_act")
        grads["w_ffn_out"][l] = _mm(sv["act"], dh2, "tn", "dw_ffn_out", out_dtype=BF16)

        def act_bwd(i, rows, consts):
            gate, up, da = rows
            sg = _sigmoid(gate)
            d_gate = da * up * (sg * (1.0 + gate * (1.0 - sg)))
            d_up = da * (gate * sg)
            return [jnp.concatenate([d_gate, d_up], axis=1)], []

        (dff,) = _rowwise(act_bwd, "act_bwd", Tp, BLOCK, [(sv["ff"], D_FF, 0), (sv["ff"], D_FF, 1), (d_act, None, 0)],
                          [], [(2 * D_FF, BF16)])
        grads["w_ffn_in"][l] = _mm(sv["u2"], dff, "tn", "dw_ffn_in", out_dtype=BF16)
        du2 = _mm(dff, W_ffn_in[l], "nt", "du2")
        dh1, dg = norm_bwd_call(sv["h1"], norm_ffn[l][None, :], du2, dh2)
        grads["norm_ffn"][l] = dg[0]

        dmerged = _mm(dh1, W_out[l], "nt", "d_merged")
        grads["w_out"][l] = _mm(sv["merged"], dh1, "tn", "dw_out", out_dtype=BF16)

        def merge_bwd(i, rows, consts):
            g, b0, b1, b2, dm = rows
            dm = dm * (_row_ids(i, BLOCK) >= n_pad).astype(F32)
            outs, dgs = [], []
            for k, bk in enumerate((b0, b1, b2)):
                sg = _sigmoid(g[:, k * D_MODEL:(k + 1) * D_MODEL])
                outs.append(dm * sg)
                dgs.append(dm * bk * sg * (1.0 - sg))
            return outs + [jnp.concatenate(dgs, axis=1)], []

        d_ba, d_bf, d_bc, d_gates = _rowwise(
            merge_bwd, "merge_bwd", Tp, BLOCK,
            [col("gates"), (sv["ba"], None, 0), (sv["bf"], None, 0), (sv["bc"], None, 0), (dmerged, None, 0)], [],
            [(D_MODEL, BF16)] * 3 + [(3 * D_MODEL, BF16)])
        grads["w_branch"][l] = jnp.stack([_mm(sv["o_a"], d_ba, "tn", "dw_branch", out_dtype=BF16),
                                          _mm(sv["o_f"], d_bf, "tn", "dw_branch", out_dtype=BF16),
                                          _mm(sv["o_c"], d_bc, "tn", "dw_branch", out_dtype=BF16)])
        do_a = _mm(d_ba, W_branch[l][0], "nt", "d_branch")
        do_f = _mm(d_bf, W_branch[l][1], "nt", "d_branch")
        do_c = _mm(d_bc, W_branch[l][2], "nt", "d_branch")

        def oc_bwd(i, rows, consts):
            d, hv, yv = rows
            return [d * _gelu(yv), d * hv * _gelu_grad(yv)], []

        d_hs, d_yc = _rowwise(oc_bwd, "oc_bwd", Tp, TR, [(do_c, None, 0), (sv["hs"], None, 0), col("yc")], [],
                              [(LRU_WIDTH, F32)] * 2)
        d_state = _rev_scan(sv["a_dec"], d_hs, "lru_scan")

        def lru_bwd(i, rows, consts):
            dH, hv, hv_prev8, pr, xv = rows
            hp = _rows_from_prev(hv, hv_prev8, 1, i == 0)
            b_r, b_i, lam = consts
            valid = (_row_ids(i, TR) >= n_pad).astype(F32)
            r = _sigmoid(pr[:, :LRU_WIDTH] + b_r)
            gi = _sigmoid(pr[:, LRU_WIDTH:] + b_i)
            lsl = _log_sigmoid(lam)
            log_a = LRU_C * r * lsl
            a = jnp.exp(log_a)
            one_m_e = _neg_expm1(2.0 * log_a)
            mult = jnp.sqrt(one_m_e)
            d_inp = dH * valid
            d_mult = d_inp * gi * xv
            d_gi = d_inp * mult * xv
            d_x = d_inp * mult * gi
            d_log_a = dH * hp * a - d_mult * (1.0 - one_m_e) / mult
            d_pre_r = d_log_a * (LRU_C * lsl) * r * (1.0 - r)
            d_pre_i = d_gi * gi * (1.0 - gi)
            d_lam = _colsum(d_log_a * (LRU_C * r)) * _sigmoid(-lam)
            return [jnp.concatenate([d_pre_r, d_pre_i], axis=1), d_x], [_colsum(d_pre_r), _colsum(d_pre_i), d_lam]

        d_pre, d_xdir, d_b_r, d_b_i, d_lam = _rowwise(
            lru_bwd, "lru_bwd", Tp, TR, [(d_state, None, 0), (sv["hs"], None, 0), (sv["hs"], None, 0, "prev"),
                                         (sv["pre"], None, 0), (sv["xconv"], None, 0)], sv["lru_consts"],
            [(2 * LRU_WIDTH, BF16), (LRU_WIDTH, F32)], [LRU_WIDTH] * 3)
        grads["lru_b_r"][l], grads["lru_b_i"][l], grads["lru_lambda"][l] = d_b_r[0], d_b_i[0], d_lam[0]
        d_w_ri = _mm(sv["xconv"], d_pre, "tn", "dw_lru_gates")

        def diag_blocks(wd):
            w4 = wd.reshape(LRU_BLOCKS, LRU_BLOCK_DIM, LRU_BLOCKS, LRU_BLOCK_DIM)
            return jnp.stack([w4[b, :, b, :] for b in range(LRU_BLOCKS)])

        grads["lru_w_r"][l] = diag_blocks(d_w_ri[:, :LRU_WIDTH])
        grads["lru_w_i"][l] = diag_blocks(d_w_ri[:, LRU_WIDTH:])
        d_conv = _mm(d_pre, sv["w_ri"], "nt", "d_xconv", res=d_xdir)

        def conv_bwd(i, rows, consts):
            w = consts[0]
            dc, dc_next8, xcur, x_prev8 = rows
            d = [dc] + [_rows_from_next(dc, dc_next8, s, i == Tp // TR - 1) for s in range(1, CONV_WIDTH)]
            xsh = conv_taps(i, xcur, x_prev8)
            dxc = d[0] * w[CONV_WIDTH - 1:CONV_WIDTH, :]
            for s in range(1, CONV_WIDTH):
                dxc = dxc + d[s] * w[CONV_WIDTH - 1 - s:CONV_WIDTH - s, :]
            return [dxc], [_colsum(xsh[t] * d[0]) for t in range(CONV_WIDTH)] + [_colsum(d[0])]

        res = _rowwise(conv_bwd, "conv_bwd", Tp, TR,
                       [(d_conv, None, 0), (d_conv, None, 0, "next"), col("xc"), col("xc") + ("prev",)], [sv["cw"]],
                       [(LRU_WIDTH, F32)], [LRU_WIDTH] * (CONV_WIDTH + 1))
        d_xc = res[0]
        grads["conv_w"][l] = jnp.concatenate(res[1:1 + CONV_WIDTH], axis=0)
        grads["conv_b"][l] = res[1 + CONV_WIDTH][0]

        (dqf, delta_f, dccol), got = _fox_bwd_dq(
            proj, sv["ccol"], sv["crow"], sv["o_f"], sv["lse_f"], do_f, TQF,
            host=[(g_send("w_ffn_in", grads["w_ffn_in"][l]), False)])
        recv["w_ffn_in"][l] = got[0]
        ride = [("w_out", l), ("w_branch", l)] + ([("w_in", l + 1)] if l + 1 < DEPTH else [])
        (dkf, dvf, dcrow), got = _fox_bwd_dkv(
            proj, sv["ccol"], sv["crow"], sv["lse_f"], delta_f, do_f, TQF,
            host=[(g_send(n, grads[n][ll]), False) for n, ll in ride])
        for (n, ll), r in zip(ride, got):
            recv[n][ll] = r
        lanes = lambda z: jnp.pad(z.reshape(FOX_HEADS, Tp).T, ((0, 0), (0, LANE - FOX_HEADS)))
        dlogf = _rev_scan(None, lanes(dcrow), "cumsum_bwd", g2=lanes(dccol))

        def logf_bwd(i, rows, consts):
            dl, fl = rows
            valid = (_row_ids(i, TR) >= n_pad).astype(F32)
            lane_ok = (lax.broadcasted_iota(jnp.int32, (1, LANE), 1) < FOX_HEADS).astype(F32)
            dfl = dl * _sigmoid(-(fl + consts[0])) * valid * lane_ok
            return [dfl], [_colsum(dfl)]

        d_fl, d_fb = _rowwise(logf_bwd, "logf_bwd", Tp, TR, [(dlogf, None, 0), col("fl")], [sv["fbias"]],
                              [(LANE, F32)], [LANE])
        grads["fox_forget_bias"][l] = d_fb[0, :FOX_HEADS]

        (dqa, dk_strip, dv_strip, d_bias_l, d_sink), got = _swa_bwd(
            proj, bias, sv["sinks"], sv["o_a"], sv["lse_a"], do_a, n_pad,
            host=[(g_send("w_ffn_out", grads["w_ffn_out"][l]), False)])
        recv["w_ffn_out"][l] = got[0]
        grads["swa_sinks"][l] = d_sink[:, 0, 0]
        d_bias_total = d_bias_l if d_bias_total is None else d_bias_total + d_bias_l

        def kv_parts(strip):
            own = strip[:, BLOCK:, :].reshape(Tp, KV_W)
            nxt = jnp.pad(strip[1:, :BLOCK, :], ((0, 1), (TQ - BLOCK, 0), (0, 0))).reshape(Tp, KV_W)
            return [(own, None, 0), (nxt, None, 0)]

        def assemble(i, rows, consts):
            dg, dqa_, dqf_, dkf_, dvf_, dxc_, dyc_, dk0, dk1, dv0, dv1, dfl_ = rows
            parts = [dg, dqa_, dqf_, dkf_, dvf_, dxc_, dyc_, dk0 + dk1, dv0 + dv1, dfl_]
            return [jnp.concatenate([p.astype(BF16) for p in parts], axis=1)], []

        (dproj,) = _rowwise(
            assemble, "assemble_dproj", Tp, BLOCK,
            [(d_gates, None, 0), (dqa, None, 0), (dqf, None, 0), (dkf, None, 0), (dvf, None, 0), (d_xc, None, 0),
             (d_yc, None, 0)] + kv_parts(dk_strip) + kv_parts(dv_strip) + [(d_fl, None, 0)], [],
            [(WORK_COLS, BF16)])
        grads["w_in"][l] = _mm(sv["u"], dproj, "tn", "dw_in", tn_cap=384, out_dtype=BF16)
        if l == 0:
            du, got = _mm(dproj, W_in[l], "nt", "du", tk_cap=2176,
                          host=[(g_send("w_in", grads["w_in"][0]), False)])
            recv["w_in"][0] = got[0]
        else:
            du = _mm(dproj, W_in[l], "nt", "du", tk_cap=2176)
        dh, dg = norm_bwd_call(sv["h"], norm_mix[l][None, :], du, dh1)
        grads["norm_mix"][l] = dg[0]

    d_table = _mm(d_bias_total.reshape(SWA_Q_HEADS, BLOCK * 2 * BLOCK), onehot, "nt", "d_rel_table", exact=True,
                  tk_cap=4096)
    g_rel = d_table[:, :REL_BUCKETS].T
    g_meta_full = dh[n_pad:first]
    grad_x = dh[first:][None]

    stack = lambda k: jnp.stack(grads[k])
    big_res = {}
    for n in big_names:
        per_layer = [_adam(recv[n][l], rows2d(big_w[n][l]), rows2d(big_m[n][l]), rows2d(big_v[n][l]), "adam_" + n)
                     for l in range(DEPTH)]
        big_res[n] = [jnp.stack([per_layer[l][k] for l in range(DEPTH)]).reshape(big_w[n].shape) for k in range(4)]

    rep_names = ("rel_bias_table", "norm_mix", "swa_sinks", "fox_forget_bias", "conv_b", "lru_w_r", "lru_b_r",
                 "lru_w_i", "lru_b_i", "lru_lambda", "norm_ffn", "norm_final")
    rep_w = dict(rel_bias_table=rel_bias_table, norm_mix=norm_mix, swa_sinks=swa_sinks,
                 fox_forget_bias=fox_forget_bias, conv_b=conv_b, lru_w_r=lru_w_r, lru_b_r=lru_b_r, lru_w_i=lru_w_i,
                 lru_b_i=lru_b_i, lru_lambda=lru_lambda, norm_ffn=norm_ffn, norm_final=norm_final)
    rep_m = dict(rel_bias_table=m_rel_bias_table, norm_mix=m_norm_mix, swa_sinks=m_swa_sinks,
                 fox_forget_bias=m_fox_forget_bias, conv_b=m_conv_b, lru_w_r=m_lru_w_r, lru_b_r=m_lru_b_r,
                 lru_w_i=m_lru_w_i, lru_b_i=m_lru_b_i, lru_lambda=m_lru_lambda, norm_ffn=m_norm_ffn,
                 norm_final=m_norm_final)
    rep_v = dict(rel_bias_table=v_rel_bias_table, norm_mix=v_norm_mix, swa_sinks=v_swa_sinks,
                 fox_forget_bias=v_fox_forget_bias, conv_b=v_conv_b, lru_w_r=v_lru_w_r, lru_b_r=v_lru_b_r,
                 lru_w_i=v_lru_w_i, lru_b_i=v_lru_b_i, lru_lambda=v_lru_lambda, norm_ffn=v_norm_ffn,
                 norm_final=v_norm_final)
    rep_g = {n: (g_rel if n == "rel_bias_table" else d_norm_final[0] if n == "norm_final" else stack(n))
             for n in rep_names}
    small_g = [rep_g[n] for n in rep_names] + [g_meta_full, stack("conv_w")]
    small_shapes = [rep_w[n].shape for n in rep_names] + [(N_META, D_MODEL), (DEPTH, CONV_WIDTH, LRU_WIDTH)]
    gs_all = _exchange(_pack(small_g, D_MODEL, 8), True, "gather_small_grads")
    gs_sum = _unpack(_sum_parts(gs_all, "sum_small_grads"), small_shapes)
    gsum = dict(zip(rep_names, gs_sum[:len(rep_names)]))
    g_meta_sh = lax.dynamic_slice_in_dim(gs_sum[-2], me * (D_MODEL // N_DEV), D_MODEL // N_DEV, axis=1)
    g_convw_sh = lax.dynamic_slice_in_dim(gs_sum[-1], me * (LRU_WIDTH // N_DEV), LRU_WIDTH // N_DEV, axis=2)
    sm_names = rep_names + ("meta_tokens", "conv_w")
    sm_w = [rep_w[n] for n in rep_names] + [meta_tokens, conv_w]
    sm_m = [rep_m[n] for n in rep_names] + [m_meta_tokens, m_conv_w]
    sm_v = [rep_v[n] for n in rep_names] + [v_meta_tokens, v_conv_w]
    sm_g = [gsum[n] for n in rep_names] + [g_meta_sh, g_convw_sh]
    sm_shapes = [w.shape for w in sm_w]
    sm_out = [_unpack(o, sm_shapes) for o in _adam(_pack(sm_g, D_MODEL, 8)[None], _pack(sm_w, D_MODEL, 8),
                                                   _pack(sm_m, D_MODEL, 8), _pack(sm_v, D_MODEL, 8), "adam_small")]
    sm_res = {n: [sm_out[k][j] for k in range(4)] for j, n in enumerate(sm_names)}

    order = ("meta_tokens", "rel_bias_table", "norm_mix", "w_in", "swa_sinks", "fox_forget_bias", "conv_w", "conv_b",
             "lru_w_r", "lru_b_r", "lru_w_i", "lru_b_i", "lru_lambda", "w_branch", "w_out", "norm_ffn", "w_ffn_in",
             "w_ffn_out", "norm_final")
    allres = {**big_res, **sm_res}
    outs = [loss, grad_x]
    for k in range(4):
        outs += [allres[n][k] for n in order]
    return tuple(outs)
```

```python
import functools
import math

import numpy as np
import jax
import jax.numpy as jnp
from jax import lax
from jax.experimental import pallas as pl
from jax.experimental.pallas import tpu as pltpu

F32 = jnp.float32
BF16 = jnp.bfloat16

N_DEV = 8
D_MODEL = 1024
DEPTH = 4
HEAD_DIM = 64
N_META = 16
BLOCK = 128
NEG_INF = -1e30
SWA_Q_HEADS = 8
SWA_KV_HEADS = 2
SWA_GROUP = SWA_Q_HEADS // SWA_KV_HEADS
FOX_HEADS = 8
LRU_WIDTH = D_MODEL // 2
LRU_BLOCKS = 8
LRU_BLOCK_DIM = LRU_WIDTH // LRU_BLOCKS
CONV_WIDTH = 4
LRU_C = 8.0
REL_BUCKETS = 32
REL_MAX_DIST = 128
D_FF = 2816
N_BRANCH = 3
ATT_W = SWA_Q_HEADS * HEAD_DIM
KV_W = SWA_KV_HEADS * HEAD_DIM
SCALE = HEAD_DIM ** -0.5
EPS = 1e-6

_ORIG = (("qa", ATT_W), ("ka", KV_W), ("va", KV_W), ("qf", ATT_W), ("kf", ATT_W), ("vf", ATT_W),
         ("fl", FOX_HEADS), ("xc", LRU_WIDTH), ("yc", LRU_WIDTH), ("gates", N_BRANCH * D_MODEL))
IN_COLS = sum(w for _, w in _ORIG)
_WORK = (("gates", 3072), ("qa", 512), ("qf", 512), ("kf", 512), ("vf", 512), ("xc", 512), ("yc", 512),
         ("ka", 128), ("va", 128), ("fl", 128))
WORK_COLS = sum(w for _, w in _WORK)


def _offsets(table):
    off, out = 0, {}
    for n, w in table:
        out[n] = (off, w)
        off += w
    return out


_ORIG_OFF = _offsets(_ORIG)
_WORK_OFF = _offsets(_WORK)

ADAM_LR = 0.001
ADAM_B1 = 0.9
ADAM_B2 = 0.999
ADAM_EPS = 1e-08
ADAM_WD = 0.01
ADAM_STEP = 10

VMEM_LIMIT = 62 * 1024 * 1024
FOX_TILE = 1408
LANE = 128


def _tile(n, cap, mult=LANE):
    if n <= cap:
        return n
    best = None
    for d in range(mult, cap + 1, mult):
        if n % d == 0:
            best = d
    assert best is not None, (n, cap, mult)
    return best


def _params(sem):
    return pltpu.CompilerParams(dimension_semantics=sem, vmem_limit_bytes=VMEM_LIMIT)


def _sigmoid(x):
    return 1.0 / (1.0 + jnp.exp(-x))


def _log_sigmoid(x):
    return jnp.minimum(x, 0.0) - jnp.log(1.0 + jnp.exp(-jnp.abs(x)))


def _neg_expm1(x):
    series = -x * (1.0 + x * (0.5 + x * (1.0 / 6.0 + x * (1.0 / 24.0 + x * (1.0 / 120.0)))))
    return jnp.where(x > -0.1, series, 1.0 - jnp.exp(x))


_GELU_C = math.sqrt(2.0 / math.pi)


def _gelu(x):
    return 0.5 * x * (1.0 + jnp.tanh(_GELU_C * (x + 0.044715 * x * x * x)))


def _gelu_grad(x):
    t = jnp.tanh(_GELU_C * (x + 0.044715 * x * x * x))
    return 0.5 * (1.0 + t) + 0.5 * x * (1.0 - t * t) * _GELU_C * (1.0 + 3.0 * 0.044715 * x * x)


def _xch_ops(src_ref, out_ref, send_sems, recv_sems, local_sem, gather):
    x, y, c = lax.axis_index("x"), lax.axis_index("y"), lax.axis_index("c")
    me = 4 * x + 2 * y + c

    def copy(r, mine):
        px, py, pc = x ^ ((r >> 2) & 1), y ^ ((r >> 1) & 1), c ^ (r & 1)
        pid = 4 * px + 2 * py + pc
        return pltpu.make_async_remote_copy(
            src_ref=src_ref if gather else src_ref.at[pid], dst_ref=out_ref.at[me if mine else pid],
            send_sem=send_sems.at[r - 1], recv_sem=recv_sems.at[r - 1],
            device_id=(px, py, pc), device_id_type=pl.DeviceIdType.MESH)

    def local():
        return pltpu.make_async_copy(src_ref if gather else src_ref.at[me], out_ref.at[me], local_sem)

    def start():
        local().start()
        for r in range(1, N_DEV):
            copy(r, True).start()

    def wait():
        for r in range(1, N_DEV):
            copy(r, False).wait_recv()
        for r in range(1, N_DEV):
            copy(r, True).wait_send()
        local().wait()

    return start, wait


_XCH_SEMS = [pltpu.SemaphoreType.DMA((N_DEV - 1,)), pltpu.SemaphoreType.DMA((N_DEV - 1,)), pltpu.SemaphoreType.DMA]
_ANY = pl.BlockSpec(memory_space=pl.ANY)


def _xch_shape(src, gather):
    return jax.ShapeDtypeStruct((N_DEV,) + tuple(src.shape if gather else src.shape[1:]), src.dtype)


def _exchange(src, gather, name):
    def body(src_ref, out_ref, send_sems, recv_sems, local_sem):
        start, wait = _xch_ops(src_ref, out_ref, send_sems, recv_sems, local_sem, gather)
        start()
        wait()

    return pl.pallas_call(body, name=name, in_specs=[_ANY], out_specs=_ANY, out_shape=_xch_shape(src, gather),
                          scratch_shapes=list(_XCH_SEMS))(src)


def _pcall(body, name, grid, in_specs, out_specs, out_shape, scratch_shapes, sem, args, host=()):
    if not host:
        outs = pl.pallas_call(body, name=name, grid=grid, in_specs=in_specs, out_specs=out_specs, out_shape=out_shape,
                              scratch_shapes=scratch_shapes, compiler_params=_params(sem))(*args)
        return outs, []
    n_in, n_out, n_scr, n_x = len(args), len(out_shape), len(scratch_shapes), len(host)

    def wrapped(*refs):
        ins, xin = refs[:n_in], refs[n_in:n_in + n_x]
        outs, xout = refs[n_in + n_x:n_in + n_x + n_out], refs[n_in + n_x + n_out:n_in + 2 * n_x + n_out]
        scr = refs[n_in + 2 * n_x + n_out:n_in + 2 * n_x + n_out + n_scr]
        sems = refs[n_in + 2 * n_x + n_out + n_scr:]
        first = functools.reduce(jnp.logical_and, [pl.program_id(k) == 0 for k in range(len(grid))])
        last = functools.reduce(jnp.logical_and, [pl.program_id(k) == grid[k] - 1 for k in range(len(grid))])
        ops = [_xch_ops(xin[k], xout[k], *sems[3 * k:3 * k + 3], host[k][1]) for k in range(n_x)]

        @pl.when(first)
        def _():
            for start, _ in ops:
                start()

        body(*ins, *outs, *scr)

        @pl.when(last)
        def _():
            for _, wait in ops:
                wait()

    res = pl.pallas_call(
        wrapped, name=name, grid=grid, in_specs=list(in_specs) + [_ANY] * n_x,
        out_specs=list(out_specs) + [_ANY] * n_x,
        out_shape=list(out_shape) + [_xch_shape(s, g) for s, g in host],
        scratch_shapes=list(scratch_shapes) + list(_XCH_SEMS) * n_x,
        compiler_params=_params(("arbitrary",) * len(grid)))(*args, *[s for s, _ in host])
    return res[:n_out], res[n_out:]


def _mm(a, b, mode, name, res=None, exact=False, tm_cap=1408, tn_cap=512, tk_cap=1408, host=None, out_dtype=F32):
    if mode == "nn":
        (M, K), (K2, N) = a.shape, b.shape
    elif mode == "nt":
        (M, K), (N, K2) = a.shape, b.shape
    else:
        (K, M), (K2, N) = a.shape, b.shape
    assert K == K2, (a.shape, b.shape, mode)
    tm, tn, tk = _tile(M, tm_cap), _tile(N, tn_cap), _tile(K, tk_cap)
    nk = K // tk
    a_spec = {"nn": pl.BlockSpec((tm, tk), lambda i, j, k: (i, k)),
              "nt": pl.BlockSpec((tm, tk), lambda i, j, k: (i, k)),
              "tn": pl.BlockSpec((tk, tm), lambda i, j, k: (k, i))}[mode]
    b_spec = {"nn": pl.BlockSpec((tk, tn), lambda i, j, k: (k, j)),
              "nt": pl.BlockSpec((tn, tk), lambda i, j, k: (j, k)),
              "tn": pl.BlockSpec((tk, tn), lambda i, j, k: (k, j))}[mode]
    o_spec = pl.BlockSpec((tm, tn), lambda i, j, k: (i, j))
    has_res = res is not None

    def body(*refs):
        if has_res:
            a_ref, b_ref, r_ref, o_ref, acc_ref = refs
        else:
            a_ref, b_ref, o_ref, acc_ref = refs
        k = pl.program_id(2)

        @pl.when(k == 0)
        def _():
            acc_ref[...] = jnp.zeros_like(acc_ref)

        x, y = a_ref[...], b_ref[...]
        if exact:
            x, y, prec = x.astype(F32), y.astype(F32), lax.Precision.HIGHEST
        else:
            x, y, prec = x.astype(BF16), y.astype(BF16), None
        if mode == "tn":
            x = x.T
        dims = (((1,), (1,)), ((), ())) if mode == "nt" else (((1,), (0,)), ((), ()))
        acc_ref[...] += lax.dot_general(x, y, dims, precision=prec, preferred_element_type=F32)

        @pl.when(k == nk - 1)
        def _():
            if has_res:
                o_ref[...] = (acc_ref[...] + r_ref[...]).astype(out_dtype)
            else:
                o_ref[...] = acc_ref[...].astype(out_dtype)

    in_specs = [a_spec, b_spec] + ([o_spec] if has_res else [])
    args = (a, b) + ((res,) if has_res else ())
    (out,), xouts = _pcall(body, name, (M // tm, N // tn, nk), in_specs, [o_spec],
                           [jax.ShapeDtypeStruct((M, N), out_dtype)], [pltpu.VMEM((tm, tn), F32)],
                           ("parallel", "parallel", "arbitrary"), args, host or ())
    return out if host is None else (out, xouts)


def _rowwise(fn, name, n_rows, tr, row_ins, const_ins, row_outs, red_outs=()):
    n_row_in, n_const, n_row_out, n_red = len(row_ins), len(const_ins), len(row_outs), len(red_outs)
    nblk, per = n_rows // tr, tr // 8
    in_specs = []
    for arr, w, idx, *side in row_ins:
        width = arr.shape[1] if w is None else w
        if not side:
            in_specs.append(pl.BlockSpec((tr, width), functools.partial(lambda i, j: (i, j), j=idx)))
        elif side[0] == "prev":
            in_specs.append(pl.BlockSpec((8, width), functools.partial(
                lambda i, j: (jnp.maximum(i * per - 1, 0), j), j=idx)))
        else:
            in_specs.append(pl.BlockSpec((8, width), functools.partial(
                lambda i, j: (jnp.minimum((i + 1) * per, nblk * per - 1), j), j=idx)))
    for c in const_ins:
        in_specs.append(pl.BlockSpec(c.shape, lambda i: (0, 0)))
    out_shape = [jax.ShapeDtypeStruct((n_rows, c), dt) for c, dt in row_outs]
    out_shape += [jax.ShapeDtypeStruct((1, c), F32) for c in red_outs]
    out_specs = [pl.BlockSpec((tr, c), lambda i: (i, 0)) for c, _ in row_outs]
    out_specs += [pl.BlockSpec((1, c), lambda i: (0, 0)) for c in red_outs]

    def body(*refs):
        i = pl.program_id(0)
        rows = [r[...] for r in refs[:n_row_in]]
        consts = [r[...] for r in refs[n_row_in:n_row_in + n_const]]
        outs = refs[n_row_in + n_const:]
        row_vals, red_vals = fn(i, rows, consts)
        for k in range(n_row_out):
            outs[k][...] = row_vals[k].astype(outs[k].dtype)
        if n_red:
            @pl.when(i == 0)
            def _():
                for k in range(n_red):
                    outs[n_row_out + k][...] = jnp.zeros_like(outs[n_row_out + k])

            for k in range(n_red):
                outs[n_row_out + k][...] += red_vals[k]

    res = pl.pallas_call(
        body, name=name, grid=(n_rows // tr,), in_specs=in_specs, out_specs=out_specs, out_shape=out_shape,
        compiler_params=_params(("arbitrary",)))(*[r[0] for r in row_ins], *const_ins)
    return res


def _rows_from_prev(cur, prev8, s, first):
    rolled = pltpu.roll(cur, s, 0)
    top = jnp.where(first, 0.0, pltpu.roll(prev8, s, 0))
    r8 = lax.broadcasted_iota(jnp.int32, prev8.shape, 0)
    return jnp.concatenate([jnp.where(r8 < s, top, rolled[:8]), rolled[8:]], axis=0)


def _rows_from_next(cur, next8, s, last):
    tr = cur.shape[0]
    rolled = pltpu.roll(cur, tr - s, 0)
    bottom = jnp.where(last, 0.0, pltpu.roll(next8, 8 - s, 0))
    r8 = lax.broadcasted_iota(jnp.int32, next8.shape, 0)
    return jnp.concatenate([rolled[:tr - 8], jnp.where(r8 >= 8 - s, bottom, rolled[tr - 8:])], axis=0)


def _row_ids(i, tr):
    return i * tr + lax.broadcasted_iota(jnp.int32, (tr, 1), 0)


def _colsum(x):
    return jnp.sum(x, axis=0, keepdims=True)


def _scan(a, b, name, b2=None, reverse=False):
    n_rows, c = b.shape
    tr = _tile(n_rows, 384)
    nblk = n_rows // tr
    has_a = a is not None
    has_b2 = b2 is not None

    def body(*refs):
        o_ref, carry, a_edge = refs[-3:]
        ins = list(refs[:-3])
        a_ref = ins.pop(0) if has_a else None
        b_ref = ins.pop(0)

        @pl.when(pl.program_id(0) == 0)
        def _():
            carry[...] = jnp.zeros_like(carry)
            a_edge[...] = jnp.zeros_like(a_edge)

        rows = lax.broadcasted_iota(jnp.int32, (tr, c), 0)
        bv = b_ref[...] + ins[0][...] if has_b2 else b_ref[...]
        av = a_ref[...] if has_a else None
        if has_a and reverse:
            a_first = av[0:1, :]
            av = _rows_from_next(av, a_edge[...], 1, False)
            a_edge[...] = jnp.broadcast_to(a_first, a_edge.shape)
        s = 1
        while s < tr:
            keep = rows < tr - s if reverse else rows >= s
            shift = tr - s if reverse else s
            b_sh = jnp.where(keep, pltpu.roll(bv, shift, 0), 0.0)
            if has_a:
                a_sh = jnp.where(keep, pltpu.roll(av, shift, 0), 1.0)
                bv = av * b_sh + bv
                av = av * a_sh
            else:
                bv = b_sh + bv
            s *= 2
        h = av * carry[0:1, :] + bv if has_a else carry[0:1, :] + bv
        o_ref[...] = h
        edge = 0 if reverse else tr - 1
        carry[...] = jnp.broadcast_to(h[edge:edge + 1, :], carry.shape)

    spec = pl.BlockSpec((tr, c), (lambda i: (nblk - 1 - i, 0)) if reverse else (lambda i: (i, 0)))
    args = ((a,) if has_a else ()) + (b,) + ((b2,) if has_b2 else ())
    return pl.pallas_call(
        body, name=name, grid=(n_rows // tr,), in_specs=[spec] * len(args), out_specs=spec,
        out_shape=jax.ShapeDtypeStruct((n_rows, c), F32),
        scratch_shapes=[pltpu.VMEM((8, c), F32), pltpu.VMEM((8, c), F32)],
        compiler_params=_params(("arbitrary",)))(*args)


def _rev_scan(a, g, name, g2=None):
    return _scan(a, g, name, b2=g2, reverse=True)


def _swa_masks(blk, n_pad):
    qi = lax.broadcasted_iota(jnp.int32, (BLOCK, 2 * BLOCK), 0)
    ki = lax.broadcasted_iota(jnp.int32, (BLOCK, 2 * BLOCK), 1)
    dist = qi + BLOCK - ki
    key_abs = (blk - 1) * BLOCK + ki
    return (dist >= 0) & (dist < BLOCK) & (key_abs >= n_pad)


_QA_COL = _WORK_OFF["qa"][0] // ATT_W
_KA_COL, _VA_COL = (_WORK_OFF[n][0] // LANE for n in ("ka", "va"))


def _swa_strips(prev_ref, cur_ref):
    x = jnp.concatenate([prev_ref[...], cur_ref[...]], axis=0)
    return [x.astype(BF16), pltpu.roll(x, HEAD_DIM, 1).astype(BF16)]


def _swa_swapped(h):
    return 0 if h % 2 == h // SWA_GROUP else 1


def _swa_specs(tq, nsub):
    prev = lambda col: pl.BlockSpec((BLOCK, LANE), lambda i: (jnp.maximum(i * nsub - 1, 0), col))
    cur = lambda col: pl.BlockSpec((tq, LANE), lambda i: (i, col))
    return [pl.BlockSpec((tq, ATT_W), lambda i: (i, _QA_COL)), prev(_KA_COL), cur(_KA_COL), prev(_VA_COL), cur(_VA_COL),
            pl.BlockSpec((SWA_Q_HEADS, BLOCK, 2 * BLOCK), lambda i: (0, 0, 0)),
            pl.BlockSpec((SWA_Q_HEADS, 1, LANE), lambda i: (0, 0, 0))]


def _swa_fwd(proj, bias, sinks, n_pad, host=()):
    Tp = proj.shape[0]
    tq = _tile(Tp, 384)
    nsub = tq // BLOCK

    def body(q_ref, kp_ref, kc_ref, vp_ref, vc_ref, b_ref, s_ref, o_ref, lse_ref):
        i = pl.program_id(0)
        low = _low_lanes()
        ks, vs = _swa_strips(kp_ref, kc_ref), _swa_strips(vp_ref, vc_ref)
        for j in range(nsub):
            rows, keys = slice(j * BLOCK, (j + 1) * BLOCK), slice(j * BLOCK, (j + 2) * BLOCK)
            mask = _swa_masks(i * nsub + j, n_pad)
            for hb in range(SWA_Q_HEADS // 2):
                pair = slice(hb * LANE, (hb + 1) * LANE)
                qh = _split_pair(q_ref[rows, pair] * SCALE, low)
                outs = []
                for e in range(2):
                    h = 2 * hb + e
                    kk, vv = ks[_swa_swapped(h)][keys], vs[_swa_swapped(h)][keys]
                    s = lax.dot_general(qh[e], kk, (((1,), (1,)), ((), ())), preferred_element_type=F32)
                    s = jnp.where(mask, s + b_ref[h], NEG_INF)
                    sink = s_ref[h, :, 0:1]
                    m = jnp.maximum(jnp.max(s, axis=1, keepdims=True), sink)
                    p = jnp.exp(s - m)
                    denom = jnp.sum(p, axis=1, keepdims=True) + jnp.exp(sink - m)
                    outs.append(jnp.dot((p / denom).astype(BF16), vv, preferred_element_type=F32))
                    lse_ref[h, rows, :] = m + jnp.log(denom)
                o_ref[rows, pair] = jnp.where(low, outs[0], outs[1])

    return _pcall(
        body, "swa_fwd", (Tp // tq,), _swa_specs(tq, nsub),
        [pl.BlockSpec((tq, ATT_W), lambda i: (i, 0)), pl.BlockSpec((SWA_Q_HEADS, tq, 1), lambda i: (0, i, 0))],
        [jax.ShapeDtypeStruct((Tp, ATT_W), F32), jax.ShapeDtypeStruct((SWA_Q_HEADS, Tp, 1), F32)],
        [], ("arbitrary",), (proj, proj, proj, proj, proj, bias, sinks), host)


def _swa_bwd(proj, bias, sinks, o, lse, do, n_pad, host=()):
    Tp = proj.shape[0]
    tq = _tile(Tp, 384)
    nsub = tq // BLOCK
    nq = Tp // tq
    strip = BLOCK + tq

    def body(q_ref, kp_ref, kc_ref, vp_ref, vc_ref, b_ref, s_ref, o_ref, lse_ref, do_ref,
             dq_ref, dk_ref, dv_ref, db_ref, ds_ref, dk_sw, dv_sw):
        i = pl.program_id(0)

        @pl.when(i == 0)
        def _():
            db_ref[...] = jnp.zeros_like(db_ref)
            ds_ref[...] = jnp.zeros_like(ds_ref)

        for ref in (dk_ref, dv_ref, dk_sw, dv_sw):
            ref[...] = jnp.zeros_like(ref)
        low = _low_lanes()
        ks, vs = _swa_strips(kp_ref, kc_ref), _swa_strips(vp_ref, vc_ref)
        for j in range(nsub):
            rows, keys = slice(j * BLOCK, (j + 1) * BLOCK), slice(j * BLOCK, (j + 2) * BLOCK)
            mask = _swa_masks(i * nsub + j, n_pad)
            for hb in range(SWA_Q_HEADS // 2):
                pair = slice(hb * LANE, (hb + 1) * LANE)
                qh = _split_pair(q_ref[rows, pair] * SCALE, low)
                dof = do_ref[rows, pair]
                prod = dof * o_ref[rows, pair]
                doh = _split_pair(dof, low)
                dqs = []
                for e in range(2):
                    h = 2 * hb + e
                    sw = _swa_swapped(h)
                    kk, vv = ks[sw][keys], vs[sw][keys]
                    delta = jnp.sum(jnp.where(low, prod, 0.0) if e == 0 else jnp.where(low, 0.0, prod),
                                    axis=1, keepdims=True)
                    lse_h = lse_ref[h, rows, :]
                    s = lax.dot_general(qh[e], kk, (((1,), (1,)), ((), ())), preferred_element_type=F32)
                    s = jnp.where(mask, s + b_ref[h], NEG_INF)
                    p = jnp.exp(s - lse_h)
                    p_sink = jnp.exp(s_ref[h, :, 0:1] - lse_h)
                    dp = lax.dot_general(doh[e], vv, (((1,), (1,)), ((), ())), preferred_element_type=F32)
                    ds = p * (dp - delta)
                    db_ref[h] += ds
                    ds_ref[h] += jnp.broadcast_to(-jnp.sum(p_sink * delta, axis=0, keepdims=True), (1, LANE))
                    dqs.append(jnp.dot(ds.astype(BF16), kk, preferred_element_type=F32))
                    dk_h = jnp.dot(ds.T.astype(BF16), qh[e], preferred_element_type=F32)
                    dv_h = jnp.dot(p.T.astype(BF16), doh[e], preferred_element_type=F32)
                    if sw:
                        dk_sw[keys, :] += dk_h
                        dv_sw[keys, :] += dv_h
                    else:
                        dk_ref[0, keys, :] += dk_h
                        dv_ref[0, keys, :] += dv_h
                dq_ref[rows, pair] = jnp.where(low, dqs[0], dqs[1]) * SCALE
        dk_ref[0] += pltpu.roll(dk_sw[...], HEAD_DIM, 1)
        dv_ref[0] += pltpu.roll(dv_sw[...], HEAD_DIM, 1)

    wide = pl.BlockSpec((tq, ATT_W), lambda i: (i, 0))
    sspec = pl.BlockSpec((1, strip, LANE), lambda i: (i, 0, 0))
    return _pcall(
        body, "swa_bwd", (nq,),
        _swa_specs(tq, nsub) + [wide, pl.BlockSpec((SWA_Q_HEADS, tq, 1), lambda i: (0, i, 0)), wide],
        [wide, sspec, sspec, pl.BlockSpec((SWA_Q_HEADS, BLOCK, 2 * BLOCK), lambda i: (0, 0, 0)),
         pl.BlockSpec((SWA_Q_HEADS, 1, LANE), lambda i: (0, 0, 0))],
        [jax.ShapeDtypeStruct((Tp, ATT_W), F32), jax.ShapeDtypeStruct((nq, strip, LANE), F32),
         jax.ShapeDtypeStruct((nq, strip, LANE), F32),
         jax.ShapeDtypeStruct((SWA_Q_HEADS, BLOCK, 2 * BLOCK), F32), jax.ShapeDtypeStruct((SWA_Q_HEADS, 1, LANE), F32)],
        [pltpu.VMEM((strip, LANE), F32), pltpu.VMEM((strip, LANE), F32)], ("arbitrary",),
        (proj, proj, proj, proj, proj, bias, sinks, o, lse, do), host)


def _fox_scores(qb, kk, cq, ck, diag, tq):
    s = lax.dot_general(qb, kk, (((1,), (1,)), ((), ())), preferred_element_type=F32)
    s = s + cq - ck
    return _fox_causal(s, tq) if diag else s


def _fox_causal(s, tq):
    r = lax.broadcasted_iota(jnp.int32, (tq, tq), 0)
    c = lax.broadcasted_iota(jnp.int32, (tq, tq), 1)
    return jnp.where(c <= r, s, NEG_INF)


_FOX_PAIRS = FOX_HEADS // 2
_QF_COL, _KF_COL, _VF_COL = (_WORK_OFF[n][0] // LANE for n in ("qf", "kf", "vf"))


def _low_lanes():
    return lax.broadcasted_iota(jnp.int32, (1, LANE), 1) < HEAD_DIM


def _split_pair(x, low):
    return [jnp.where(low, x, 0.0).astype(BF16), jnp.where(low, 0.0, x).astype(BF16)]


def _fox_fwd(proj, ccol, crow, tq, host=(), prefetch=True):
    Tp = proj.shape[0]
    nq = Tp // tq

    def body(q_ref, k_ref, v_ref, cc_ref, cr_ref, o_ref, lse_ref):
        qi = pl.program_id(1)
        low = _low_lanes()
        qh = _split_pair(q_ref[...] * SCALE, low)
        cq = [cc_ref[0], cc_ref[1]]

        def scores(kb):
            kk = k_ref[pl.ds(pl.multiple_of(kb * tq, tq), tq), :].astype(BF16)
            return [_fox_scores(qh[e], kk, cq[e], cr_ref[e, kb], False, tq) for e in range(2)]

        def softmax_pv(kb, s, carry, diag):
            vv = v_ref[pl.ds(pl.multiple_of(kb * tq, tq), tq), :].astype(BF16)
            stats, upd = [], []
            for e in range(2):
                m, l = carry[2 * e], carry[2 * e + 1]
                se = _fox_causal(s[e], tq) if diag else s[e]
                m_new = jnp.maximum(m, jnp.max(se, axis=1, keepdims=True))
                alpha = jnp.exp(m - m_new)
                p = jnp.exp(se - m_new)
                stats += [m_new, alpha * l + jnp.sum(p, axis=1, keepdims=True)]
                upd.append(alpha * carry[4] + jnp.dot(p.astype(BF16), vv, preferred_element_type=F32))
            return (*stats, jnp.where(low, upd[0], upd[1]))

        col = lambda val: jnp.full((tq, 1), val, F32)
        init = (col(NEG_INF), col(0.0), col(NEG_INF), col(0.0), jnp.zeros((tq, LANE), F32))
        if prefetch:
            def step(kb, c):
                s_next = scores(kb + 1)
                return (*softmax_pv(kb, c[5:], c[:5], False), *s_next)

            c = lax.fori_loop(0, qi, step, (*init, *scores(0)))
            m0, l0, m1, l1, acc = softmax_pv(qi, c[5:], c[:5], True)
        else:
            c = lax.fori_loop(0, qi, lambda kb, c: softmax_pv(kb, scores(kb), c, False), init)
            m0, l0, m1, l1, acc = softmax_pv(qi, scores(qi), c, True)
        o_ref[...] = jnp.where(low, acc / l0, acc / l1)
        lse_ref[0] = m0 + jnp.log(l0)
        lse_ref[1] = m1 + jnp.log(l1)

    cspec = pl.BlockSpec((2, tq, 1), lambda h, i: (h, i, 0))
    return _pcall(
        body, "fox_fwd", (_FOX_PAIRS, nq),
        [pl.BlockSpec((tq, LANE), lambda h, i: (i, _QF_COL + h)), pl.BlockSpec((Tp, LANE), lambda h, i: (0, _KF_COL + h)),
         pl.BlockSpec((Tp, LANE), lambda h, i: (0, _VF_COL + h)), cspec,
         pl.BlockSpec((2, nq, 1, tq), lambda h, i: (h, 0, 0, 0))],
        [pl.BlockSpec((tq, LANE), lambda h, i: (i, h)), cspec],
        [jax.ShapeDtypeStruct((Tp, FOX_HEADS * HEAD_DIM), F32), jax.ShapeDtypeStruct((FOX_HEADS, Tp, 1), F32)],
        [], ("parallel", "arbitrary"), (proj, proj, proj, ccol, crow), host)


def _fox_bwd_dq(proj, ccol, crow, o, lse, do, tq, host=()):
    Tp = proj.shape[0]
    nq = Tp // tq

    def body(q_ref, k_ref, v_ref, cc_ref, cr_ref, o_ref, lse_ref, do_ref, dq_ref, dl_ref, dc_ref):
        qi = pl.program_id(1)
        low = _low_lanes()
        qh = _split_pair(q_ref[...] * SCALE, low)
        dof = do_ref[...]
        prod = dof * o_ref[...]
        delta = [jnp.sum(jnp.where(low, prod, 0.0), axis=1, keepdims=True),
                 jnp.sum(jnp.where(low, 0.0, prod), axis=1, keepdims=True)]
        doh = _split_pair(dof, low)
        cq = [cc_ref[0], cc_ref[1]]
        lse = [lse_ref[0], lse_ref[1]]

        def step(kb, carry, diag):
            rows = pl.ds(pl.multiple_of(kb * tq, tq), tq)
            kk, vv = k_ref[rows, :].astype(BF16), v_ref[rows, :].astype(BF16)
            dqs, dcs = [], []
            for e in range(2):
                s = _fox_scores(qh[e], kk, cq[e], cr_ref[e, kb], diag, tq)
                p = jnp.exp(s - lse[e])
                dp = lax.dot_general(doh[e], vv, (((1,), (1,)), ((), ())), preferred_element_type=F32)
                ds = p * (dp - delta[e])
                dqs.append(jnp.dot(ds.astype(BF16), kk, preferred_element_type=F32))
                dcs.append(carry[1 + e] + jnp.sum(ds, axis=1, keepdims=True))
            return (carry[0] + jnp.where(low, dqs[0], dqs[1]), *dcs)

        init = (jnp.zeros((tq, LANE), F32), jnp.zeros((tq, 1), F32), jnp.zeros((tq, 1), F32))
        dq, dc0, dc1 = step(qi, lax.fori_loop(0, qi, functools.partial(step, diag=False), init), True)
        dq_ref[...] = dq * SCALE
        dl_ref[0], dl_ref[1] = delta
        dc_ref[0], dc_ref[1] = dc0, dc1

    cspec = pl.BlockSpec((2, tq, 1), lambda h, i: (h, i, 0))
    pspec = pl.BlockSpec((tq, LANE), lambda h, i: (i, h))
    stat = jax.ShapeDtypeStruct((FOX_HEADS, Tp, 1), F32)
    return _pcall(
        body, "fox_bwd_dq", (_FOX_PAIRS, nq),
        [pl.BlockSpec((tq, LANE), lambda h, i: (i, _QF_COL + h)), pl.BlockSpec((Tp, LANE), lambda h, i: (0, _KF_COL + h)),
         pl.BlockSpec((Tp, LANE), lambda h, i: (0, _VF_COL + h)), cspec,
         pl.BlockSpec((2, nq, 1, tq), lambda h, i: (h, 0, 0, 0)), pspec, cspec, pspec],
        [pspec, cspec, cspec], [jax.ShapeDtypeStruct((Tp, FOX_HEADS * HEAD_DIM), F32), stat, stat],
        [], ("parallel", "arbitrary"), (proj, proj, proj, ccol, crow, o, lse, do), host)


def _fox_bwd_dkv(proj, ccol, crow, lse, delta, do, tq, host=()):
    Tp = proj.shape[0]
    nq = Tp // tq
    pair = lambda z: z.reshape(_FOX_PAIRS, 2, Tp).transpose(0, 2, 1)
    stats = jnp.concatenate([pair(ccol), pair(lse), pair(delta)], axis=-1).reshape(_FOX_PAIRS, nq, tq, 6)

    def body(q_ref, do_ref, st_ref, k_ref, v_ref, cr_ref, dk_ref, dv_ref, dc_ref):
        ki = pl.program_id(1)
        low = _low_lanes()
        kk, vv = k_ref[...].astype(BF16), v_ref[...].astype(BF16)

        def step(qb_i, carry, diag):
            rows = pl.ds(pl.multiple_of(qb_i * tq, tq), tq)
            qh = _split_pair(q_ref[rows, :] * SCALE, low)
            doh = _split_pair(do_ref[rows, :], low)
            st = st_ref[0, qb_i]
            dk, dv = carry[0], carry[1]
            dcs = []
            for e in range(2):
                s = _fox_scores(qh[e], kk, st[:, e:e + 1], cr_ref[e, 0], diag, tq)
                p = jnp.exp(s - st[:, 2 + e:3 + e])
                dp = lax.dot_general(doh[e], vv, (((1,), (1,)), ((), ())), preferred_element_type=F32)
                ds = p * (dp - st[:, 4 + e:5 + e])
                dv = dv + jnp.dot(p.T.astype(BF16), doh[e], preferred_element_type=F32)
                dk = dk + jnp.dot(ds.T.astype(BF16), qh[e], preferred_element_type=F32)
                dcs.append(carry[2 + e] - jnp.sum(ds, axis=0, keepdims=True))
            return (dk, dv, *dcs)

        init = (jnp.zeros((tq, LANE), F32), jnp.zeros((tq, LANE), F32), jnp.zeros((1, tq), F32), jnp.zeros((1, tq), F32))
        dk, dv, dc0, dc1 = lax.fori_loop(ki + 1, nq, functools.partial(step, diag=False), step(ki, init, True))
        dk_ref[...] = dk
        dv_ref[...] = dv
        dc_ref[0, 0] = dc0
        dc_ref[1, 0] = dc1

    rspec = pl.BlockSpec((2, 1, 1, tq), lambda h, i: (h, i, 0, 0))
    pspec = pl.BlockSpec((tq, LANE), lambda h, i: (i, h))
    wide = jax.ShapeDtypeStruct((Tp, FOX_HEADS * HEAD_DIM), F32)
    return _pcall(
        body, "fox_bwd_dkv", (_FOX_PAIRS, nq),
        [pl.BlockSpec((Tp, LANE), lambda h, i: (0, _QF_COL + h)), pl.BlockSpec((Tp, LANE), lambda h, i: (0, h)),
         pl.BlockSpec((1, nq, tq, 6), lambda h, i: (h, 0, 0, 0)),
         pl.BlockSpec((tq, LANE), lambda h, i: (i, _KF_COL + h)), pl.BlockSpec((tq, LANE), lambda h, i: (i, _VF_COL + h)),
         rspec],
        [pspec, pspec, rspec], [wide, wide, jax.ShapeDtypeStruct((FOX_HEADS, nq, 1, tq), F32)],
        [], ("parallel", "arbitrary"), (proj, do, stats, proj, proj, crow), host)


def _adam(gparts, w, m, v, name, layer=0):
    P, R, C = gparts.shape
    tr = _tile(R, 256, 8)
    first = layer * (R // tr)

    def body(g_ref, w_ref, m_ref, v_ref, go_ref, d_ref, mo_ref, vo_ref):
        g = g_ref[0].astype(F32)
        for p in range(1, P):
            g = g + g_ref[p].astype(F32)
        m2 = ADAM_B1 * m_ref[...] + (1.0 - ADAM_B1) * g
        v2 = ADAM_B2 * v_ref[...] + (1.0 - ADAM_B2) * (g * g)
        m_hat = m2 / (1.0 - ADAM_B1 ** ADAM_STEP)
        v_hat = v2 / (1.0 - ADAM_B2 ** ADAM_STEP)
        go_ref[...] = g
        d_ref[...] = -ADAM_LR * (m_hat / (jnp.sqrt(v_hat) + ADAM_EPS) + ADAM_WD * w_ref[...])
        mo_ref[...] = m2
        vo_ref[...] = v2

    spec = pl.BlockSpec((tr, C), lambda i: (i, 0))
    wspec = pl.BlockSpec((tr, C), lambda i: (first + i, 0))
    shp = jax.ShapeDtypeStruct((R, C), F32)
    return pl.pallas_call(
        body, name=name, grid=(R // tr,),
        in_specs=[pl.BlockSpec((P, tr, C), lambda i: (0, i, 0)), wspec, wspec, wspec],
        out_specs=[spec] * 4, out_shape=[shp] * 4, compiler_params=_params(("parallel",)))(gparts, w, m, v)


def _sum_parts(gparts, name):
    P, R, C = gparts.shape
    tr = _tile(R, 256, 8)

    def body(g_ref, o_ref):
        g = g_ref[0]
        for p in range(1, P):
            g = g + g_ref[p]
        o_ref[...] = g

    return pl.pallas_call(
        body, name=name, grid=(R // tr,), in_specs=[pl.BlockSpec((P, tr, C), lambda i: (0, i, 0))],
        out_specs=pl.BlockSpec((tr, C), lambda i: (i, 0)), out_shape=jax.ShapeDtypeStruct((R, C), F32),
        compiler_params=_params(("parallel",)))(gparts)


def _pack(pieces, width, row_mult):
    flat = jnp.concatenate([p.reshape(-1) for p in pieces])
    n = flat.shape[0]
    rows = -(-n // width)
    rows = -(-rows // row_mult) * row_mult
    return jnp.pad(flat, (0, rows * width - n)).reshape(rows, width)


def _unpack(flat2d, shapes, lead=None):
    out, off = [], 0
    if lead is None:
        flat = flat2d.reshape(-1)
        for s in shapes:
            n = int(np.prod(s))
            out.append(flat[off:off + n].reshape(s))
            off += n
    else:
        flat = flat2d.reshape(lead, -1)
        for s in shapes:
            n = int(np.prod(s))
            out.append(flat[:, off:off + n].reshape((lead,) + tuple(s)))
            off += n
    return out


def _col_pieces(off, width, shard_w, fetch):
    out, c, end = [], off, off + width
    while c < end:
        d = c // shard_w
        hi = min(end, (d + 1) * shard_w)
        out.append(fetch(d, c - d * shard_w, hi - d * shard_w))
        c = hi
    return out


def _work_cols_of_orig(lo, hi, fetch):
    out = []
    for name, ow in _ORIG:
        o = _ORIG_OFF[name][0]
        a, b = max(lo, o), min(hi, o + ow)
        if a < b:
            w0 = _WORK_OFF[name][0]
            out.append(fetch(w0 + a - o, w0 + b - o))
    return out


def _t5_bucket_np(dist):
    max_exact = REL_BUCKETS // 2
    d = np.maximum(dist, 0)
    scaled = (np.log(np.maximum(d, 1).astype(np.float32) / np.float32(max_exact))
              / np.float32(math.log(REL_MAX_DIST / max_exact))).astype(np.float32)
    large = np.minimum(max_exact + (scaled * np.float32(REL_BUCKETS - max_exact)).astype(np.int32), REL_BUCKETS - 1)
    return np.where(d < max_exact, d, large)


def _bucket_onehot():
    q_idx = np.arange(BLOCK)[:, None]
    k_idx = np.arange(2 * BLOCK)[None, :]
    bucket = _t5_bucket_np(q_idx + BLOCK - k_idx).reshape(-1)
    oh = np.zeros((LANE, BLOCK * 2 * BLOCK), np.float32)
    oh[bucket, np.arange(bucket.shape[0])] = 1.0
    return oh


def kernel(x, meta_tokens, rel_bias_table, norm_mix, w_in, swa_sinks, fox_forget_bias, conv_w, conv_b, lru_w_r, lru_b_r, lru_w_i, lru_b_i, lru_lambda, w_branch, w_out, norm_ffn, w_ffn_in, w_ffn_out, norm_final, loss_target, m_meta_tokens, m_rel_bias_table, m_norm_mix, m_w_in, m_swa_sinks, m_fox_forget_bias, m_conv_w, m_conv_b, m_lru_w_r, m_lru_b_r, m_lru_w_i, m_lru_b_i, m_lru_lambda, m_w_branch, m_w_out, m_norm_ffn, m_w_ffn_in, m_w_ffn_out, m_norm_final, v_meta_tokens, v_rel_bias_table, v_norm_mix, v_w_in, v_swa_sinks, v_fox_forget_bias, v_conv_w, v_conv_b, v_lru_w_r, v_lru_b_r, v_lru_w_i, v_lru_b_i, v_lru_lambda, v_w_branch, v_w_out, v_norm_ffn, v_w_ffn_in, v_w_ffn_out, v_norm_final):
    S = x.shape[1]
    T = N_META + S
    n_pad = (-T) % BLOCK
    Tp = T + n_pad
    first = n_pad + N_META
    TR = _tile(Tp, 384)
    TQ = _tile(Tp, 384)
    TQF = _tile(Tp, FOX_TILE)
    me =4 * lax.axis_index("x") + 2 * lax.axis_index("y") + lax.axis_index("c")

    big_names = ("w_in", "w_ffn_in", "w_ffn_out", "w_branch", "w_out")
    big_w = dict(w_in=w_in, w_ffn_in=w_ffn_in, w_ffn_out=w_ffn_out, w_branch=w_branch, w_out=w_out)
    big_m = dict(w_in=m_w_in, w_ffn_in=m_w_ffn_in, w_ffn_out=m_w_ffn_out, w_branch=m_w_branch, w_out=m_w_out)
    big_v = dict(w_in=v_w_in, w_ffn_in=v_w_ffn_in, w_ffn_out=v_w_ffn_out, w_branch=v_w_branch, w_out=v_w_out)
    rows2d = lambda w: w.reshape(-1, w.shape[-1])
    in_shard_w = IN_COLS // N_DEV
    cat_axis = dict(w_in=-1, w_ffn_in=-1, w_ffn_out=0, w_branch=-1, w_out=0)

    def w_send(n, l):
        return rows2d(big_w[n][l]).astype(BF16)

    def w_build(n, g):
        shard = lambda d: g[d].reshape(big_w[n].shape[1:])
        if cat_axis[n] == 0:
            return g.reshape(-1, g.shape[-1])
        if n != "w_in":
            return jnp.concatenate([shard(d) for d in range(N_DEV)], axis=cat_axis[n])
        parts = []
        for name, width in _WORK:
            off, ow = _ORIG_OFF[name]
            parts += _col_pieces(off, ow, in_shard_w, lambda d, lo, hi: shard(d)[:, lo:hi])
            if width > ow:
                parts.append(jnp.zeros((D_MODEL, width - ow), BF16))
        return jnp.concatenate(parts, axis=-1)

    W_in, W_ffn_in, W_ffn_out, W_branch, W_out = ([None] * DEPTH for _ in range(5))
    W_in[0] = w_build("w_in", _exchange(w_send("w_in", 0), True, "gather_w_in"))

    small_sh = _exchange(_pack([meta_tokens, conv_w], D_MODEL, 8), True, "gather_small")
    g_meta, g_conv_w = _unpack(small_sh, [meta_tokens.shape, conv_w.shape], lead=N_DEV)
    meta_full = g_meta.transpose(1, 0, 2).reshape(N_META, D_MODEL)
    conv_w_full = g_conv_w.transpose(1, 2, 0, 3).reshape(DEPTH, CONV_WIDTH, LRU_WIDTH)

    onehot = jnp.asarray(_bucket_onehot())
    table_t = jnp.pad(rel_bias_table.T, ((0, 0), (0, LANE - REL_BUCKETS)))
    bias = _mm(table_t, onehot, "nn", "swa_bias", exact=True, tn_cap=4096).reshape(SWA_Q_HEADS, BLOCK, 2 * BLOCK)

    def dense_blocks(w):
        rows = []
        for b in range(LRU_BLOCKS):
            rows.append(jnp.pad(w[b], ((0, 0), (b * LRU_BLOCK_DIM, (LRU_BLOCKS - 1 - b) * LRU_BLOCK_DIM))))
        return jnp.concatenate(rows, axis=0)

    h = jnp.concatenate([jnp.zeros((n_pad, D_MODEL), F32), meta_full, x[0]], axis=0)
    saved = []

    for l in range(DEPTH):
        sv = {"h": h}
        g_mix = norm_mix[l][None, :]

        def norm_fwd(i, rows, consts):
            xx = rows[0]
            return [xx * lax.rsqrt(jnp.mean(xx * xx, axis=1, keepdims=True) + EPS) * consts[0]], []

        (u,) = _rowwise(norm_fwd, "norm_fwd", Tp, TR, [(h, None, 0)], [g_mix], [(D_MODEL, BF16)])
        proj, got = _mm(u, W_in[l], "nn", "proj", tn_cap=384,
                        host=[(w_send(n, l), True) for n in ("w_branch", "w_out", "w_ffn_out")])
        W_branch[l], W_out[l], W_ffn_out[l] = (w_build(n, g) for n, g in zip(("w_branch", "w_out", "w_ffn_out"), got))
        half = D_MODEL // 2
        next_w_in = [] if l + 1 == DEPTH else [rows2d(w_in[l + 1])[k * half:(k + 1) * half].astype(BF16) for k in (0, 1)]
        next_got = []
        sv["u"], sv["proj"] = u, proj

        def col(name):
            off, w = _WORK_OFF[name]
            return (proj, w, off // w)

        sinks_l = jnp.broadcast_to(swa_sinks[l][:, None, None], (SWA_Q_HEADS, 1, LANE))
        (o_a, lse_a), got = _swa_fwd(proj, bias, sinks_l, n_pad, host=[(s, True) for s in next_w_in[:1]])
        next_got += got
        sv.update(sinks=sinks_l, lse_a=lse_a, o_a=o_a)

        fbias = jnp.pad(fox_forget_bias[l], (0, LANE - FOX_HEADS))[None, :]

        def logf_fwd(i, rows, consts):
            return [_log_sigmoid(rows[0] + consts[0])], []

        (logf,) = _rowwise(logf_fwd, "logf_fwd", Tp, TR, [col("fl")], [fbias], [(LANE, F32)])
        cum = _scan(None, logf, "cumsum")
        cum_h = cum[:, :FOX_HEADS].T
        ccol = cum_h[:, :, None]
        crow = jnp.where(jnp.arange(Tp)[None, :] < n_pad, -NEG_INF, cum_h).reshape(FOX_HEADS, Tp // TQF, 1, TQF)
        (o_f, lse_f), got = _fox_fwd(proj, ccol, crow, TQF, host=[(w_send("w_ffn_in", l), True)],
                                     prefetch=TQF <= 384)
        W_ffn_in[l] = w_build("w_ffn_in", got[0])
        sv.update(fbias=fbias, ccol=ccol, crow=crow, lse_f=lse_f, o_f=o_f)

        cw = conv_w_full[l]
        cb = conv_b[l][None, :]

        def conv_taps(i, cur, prev8):
            return [_rows_from_prev(cur, prev8, CONV_WIDTH - 1 - t, i == 0) for t in range(CONV_WIDTH - 1)] + [cur]

        def conv_fwd(i, rows, consts):
            w, b = consts
            taps = conv_taps(i, rows[0], rows[1])
            acc = taps[0] * w[0:1, :]
            for t in range(1, CONV_WIDTH):
                acc = acc + taps[t] * w[t:t + 1, :]
            return [acc + b], []

        xc_blocks = [col("xc"), col("xc") + ("prev",)]
        (xconv,) = _rowwise(conv_fwd, "conv_fwd", Tp, TR, xc_blocks, [cw, cb], [(LRU_WIDTH, F32)])
        w_ri = jnp.concatenate([dense_blocks(lru_w_r[l]), dense_blocks(lru_w_i[l])], axis=1)
        pre = _mm(xconv, w_ri, "nn", "lru_gates")
        lru_consts = [lru_b_r[l][None, :], lru_b_i[l][None, :], lru_lambda[l][None, :]]

        def lru_fwd(i, rows, consts):
            pr, xv = rows
            b_r, b_i, lam = consts
            r = _sigmoid(pr[:, :LRU_WIDTH] + b_r)
            gi = _sigmoid(pr[:, LRU_WIDTH:] + b_i)
            log_a = LRU_C * r * _log_sigmoid(lam)
            valid = (_row_ids(i, TR) >= n_pad).astype(F32)
            inp = jnp.sqrt(_neg_expm1(2.0 * log_a)) * (gi * xv) * valid
            return [jnp.exp(log_a), inp], []

        a_dec, inp = _rowwise(lru_fwd, "lru_fwd", Tp, TR, [(pre, None, 0), (xconv, None, 0)], lru_consts,
                              [(LRU_WIDTH, F32), (LRU_WIDTH, F32)])
        hs = _scan(a_dec, inp, "lru_scan")

        def oc_fwd(i, rows, consts):
            return [rows[0] * _gelu(rows[1])], []

        (o_c,) = _rowwise(oc_fwd, "oc_fwd", Tp, TR, [(hs, None, 0), col("yc")], [], [(LRU_WIDTH, BF16)])
        sv.update(cw=cw, xconv=xconv, w_ri=w_ri, pre=pre, lru_consts=lru_consts, a_dec=a_dec, hs=hs, o_c=o_c)

        ba = _mm(o_a, W_branch[l][0], "nn", "branch")
        bf = _mm(o_f, W_branch[l][1], "nn", "branch")
        bc = _mm(o_c, W_branch[l][2], "nn", "branch")

        def merge_fwd(i, rows, consts):
            g, b0, b1, b2 = rows
            valid = (_row_ids(i, BLOCK) >= n_pad).astype(F32)
            mg = (_sigmoid(g[:, :D_MODEL]) * b0 + _sigmoid(g[:, D_MODEL:2 * D_MODEL]) * b1
                  + _sigmoid(g[:, 2 * D_MODEL:]) * b2)
            return [mg * valid], []

        (merged,) = _rowwise(merge_fwd, "merge_fwd", Tp, BLOCK,
                             [col("gates"), (ba, None, 0), (bf, None, 0), (bc, None, 0)], [], [(D_MODEL, BF16)])
        h1 = _mm(merged, W_out[l], "nn", "out_proj", res=h)
        sv.update(ba=ba, bf=bf, bc=bc, merged=merged, h1=h1)

        g_ffn = norm_ffn[l][None, :]
        (u2,) = _rowwise(norm_fwd, "norm_fwd", Tp, TR, [(h1, None, 0)], [g_ffn], [(D_MODEL, BF16)])
        ff, got = _mm(u2, W_ffn_in[l], "nn", "ffn_in", host=[(s, True) for s in next_w_in[1:]])
        next_got += got
        if l + 1 < DEPTH:
            W_in[l + 1] = w_build("w_in", jnp.concatenate(next_got, axis=1))

        def act_fwd(i, rows, consts):
            gate, up = rows
            return [gate * _sigmoid(gate) * up], []

        (act,) = _rowwise(act_fwd, "act_fwd", Tp, TR, [(ff, D_FF, 0), (ff, D_FF, 1)], [], [(D_FF, BF16)])
        h = _mm(act, W_ffn_out[l], "nn", "ffn_out", res=h1)
        sv.update(u2=u2, ff=ff, act=act)
        saved.append(sv)

    tgt = jnp.concatenate([jnp.zeros((first, D_MODEL), F32), loss_target[0]], axis=0)
    g_fin = norm_final[None, :]

    def head(i, rows, consts):
        xx, tg = rows
        g = consts[0]
        valid = (_row_ids(i, TR) >= first).astype(F32)
        rstd = lax.rsqrt(jnp.mean(xx * xx, axis=1, keepdims=True) + EPS)
        xhat = xx * rstd
        err = (xhat * g - tg) * valid
        loss_rows = 0.5 * jnp.mean(err * err, axis=1, keepdims=True)
        dy = err * (1.0 / D_MODEL)
        dxhat = dy * g
        dx = rstd * (dxhat - xhat * jnp.mean(dxhat * xhat, axis=1, keepdims=True))
        return [dx], [jnp.broadcast_to(_colsum(loss_rows), (1, LANE)), _colsum(dy * xhat)]

    dh, loss_part, d_norm_final = _rowwise(head, "loss_head", Tp, TR, [(h, None, 0), (tgt, None, 0)], [g_fin],
                                           [(D_MODEL, F32)], [LANE, D_MODEL])
    loss = lax.psum(loss_part[0, 0], ("x", "y", "c"))

    def norm_bwd_call(xin, g, du, dres):
        def norm_bwd(i, rows, consts):
            xx, dd, rr = rows
            gg = consts[0]
            valid = (_row_ids(i, TR) >= n_pad).astype(F32)
            rstd = lax.rsqrt(jnp.mean(xx * xx, axis=1, keepdims=True) + EPS)
            xhat = xx * rstd
            dxhat = dd * gg
            dx = rstd * (dxhat - xhat * jnp.mean(dxhat * xhat, axis=1, keepdims=True))
            return [rr + dx * valid], [_colsum(dd * xhat)]

        return _rowwise(norm_bwd, "norm_bwd", Tp, TR, [(xin, None, 0), (du, None, 0), (dres, None, 0)], [g],
                        [(D_MODEL, F32)], [D_MODEL])

    grads = {k: [None] * DEPTH for k in ("norm_mix", "w_in", "swa_sinks", "fox_forget_bias", "conv_w", "conv_b",
                                         "lru_w_r", "lru_b_r", "lru_w_i", "lru_b_i", "lru_lambda", "w_branch",
                                         "w_out", "norm_ffn", "w_ffn_in", "w_ffn_out")}
    d_bias_total = None

    def g_send(n, g):
        if n == "w_in":
            pieces = [jnp.concatenate(_work_cols_of_orig(d * in_shard_w, (d + 1) * in_shard_w,
                                                         lambda lo, hi: g[:, lo:hi]), axis=-1) for d in range(N_DEV)]
        elif cat_axis[n] == 0:
            return g.reshape(N_DEV, -1, g.shape[-1])
        else:
            ax = cat_axis[n] % g.ndim
            w = big_w[n].shape[1 + ax]
            pieces = [lax.slice_in_dim(g, d * w, (d + 1) * w, axis=ax) for d in range(N_DEV)]
        return jnp.stack([rows2d(p) for p in pieces])

    recv = {n: [None] * DEPTH for n in big_names}
    for l in reversed(range(DEPTH)):
        sv = saved[l]
        proj = sv["proj"]

        def col(name):
            off, w = _WORK_OFF[name]
            return (proj, w, off // w)

        dh2 = dh
        d_act = _mm(dh2, W_ffn_out[l], "nt", "d_act")
        grads["w_ffn_out"][l] = _mm(sv["act"], dh2, "tn", "dw_ffn_out", out_dtype=BF16)

        def act_bwd(i, rows, consts):
            gate, up, da = rows
            sg = _sigmoid(gate)
            d_gate = da * up * (sg * (1.0 + gate * (1.0 - sg)))
            d_up = da * (gate * sg)
            return [jnp.concatenate([d_gate, d_up], axis=1)], []

        (dff,) = _rowwise(act_bwd, "act_bwd", Tp, BLOCK, [(sv["ff"], D_FF, 0), (sv["ff"], D_FF, 1), (d_act, None, 0)],
                          [], [(2 * D_FF, BF16)])
        grads["w_ffn_in"][l] = _mm(sv["u2"], dff, "tn", "dw_ffn_in", out_dtype=BF16)
        du2 = _mm(dff, W_ffn_in[l], "nt", "du2")
        dh1, dg = norm_bwd_call(sv["h1"], norm_ffn[l][None, :], du2, dh2)
        grads["norm_ffn"][l] = dg[0]

        dmerged = _mm(dh1, W_out[l], "nt", "d_merged")
        grads["w_out"][l] = _mm(sv["merged"], dh1, "tn", "dw_out", out_dtype=BF16)

        def merge_bwd(i, rows, consts):
            g, b0, b1, b2, dm = rows
            dm = dm * (_row_ids(i, BLOCK) >= n_pad).astype(F32)
            outs, dgs = [], []
            for k, bk in enumerate((b0, b1, b2)):
                sg = _sigmoid(g[:, k * D_MODEL:(k + 1) * D_MODEL])
                outs.append(dm * sg)
                dgs.append(dm * bk * sg * (1.0 - sg))
            return outs + [jnp.concatenate(dgs, axis=1)], []

        d_ba, d_bf, d_bc, d_gates = _rowwise(
            merge_bwd, "merge_bwd", Tp, BLOCK,
            [col("gates"), (sv["ba"], None, 0), (sv["bf"], None, 0), (sv["bc"], None, 0), (dmerged, None, 0)], [],
            [(D_MODEL, BF16)] * 3 + [(3 * D_MODEL, BF16)])
        grads["w_branch"][l] = jnp.stack([_mm(sv["o_a"], d_ba, "tn", "dw_branch", out_dtype=BF16),
                                          _mm(sv["o_f"], d_bf, "tn", "dw_branch", out_dtype=BF16),
                                          _mm(sv["o_c"], d_bc, "tn", "dw_branch", out_dtype=BF16)])
        do_a = _mm(d_ba, W_branch[l][0], "nt", "d_branch")
        do_f = _mm(d_bf, W_branch[l][1], "nt", "d_branch")
        do_c = _mm(d_bc, W_branch[l][2], "nt", "d_branch")

        def oc_bwd(i, rows, consts):
            d, hv, yv = rows
            return [d * _gelu(yv), d * hv * _gelu_grad(yv)], []

        d_hs, d_yc = _rowwise(oc_bwd, "oc_bwd", Tp, TR, [(do_c, None, 0), (sv["hs"], None, 0), col("yc")], [],
                              [(LRU_WIDTH, F32)] * 2)
        d_state = _rev_scan(sv["a_dec"], d_hs, "lru_scan")

        def lru_bwd(i, rows, consts):
            dH, hv, hv_prev8, pr, xv = rows
            hp = _rows_from_prev(hv, hv_prev8, 1, i == 0)
            b_r, b_i, lam = consts
            valid = (_row_ids(i, TR) >= n_pad).astype(F32)
            r = _sigmoid(pr[:, :LRU_WIDTH] + b_r)
            gi = _sigmoid(pr[:, LRU_WIDTH:] + b_i)
            lsl = _log_sigmoid(lam)
            log_a = LRU_C * r * lsl
            a = jnp.exp(log_a)
            one_m_e = _neg_expm1(2.0 * log_a)
            mult = jnp.sqrt(one_m_e)
            d_inp = dH * valid
            d_mult = d_inp * gi * xv
            d_gi = d_inp * mult * xv
            d_x = d_inp * mult * gi
            d_log_a = dH * hp * a - d_mult * (1.0 - one_m_e) / mult
            d_pre_r = d_log_a * (LRU_C * lsl) * r * (1.0 - r)
            d_pre_i = d_gi * gi * (1.0 - gi)
            d_lam = _colsum(d_log_a * (LRU_C * r)) * _sigmoid(-lam)
            return [jnp.concatenate([d_pre_r, d_pre_i], axis=1), d_x], [_colsum(d_pre_r), _colsum(d_pre_i), d_lam]

        d_pre, d_xdir, d_b_r, d_b_i, d_lam = _rowwise(
            lru_bwd, "lru_bwd", Tp, TR, [(d_state, None, 0), (sv["hs"], None, 0), (sv["hs"], None, 0, "prev"),
                                         (sv["pre"], None, 0), (sv["xconv"], None, 0)], sv["lru_consts"],
            [(2 * LRU_WIDTH, BF16), (LRU_WIDTH, F32)], [LRU_WIDTH] * 3)
        grads["lru_b_r"][l], grads["lru_b_i"][l], grads["lru_lambda"][l] = d_b_r[0], d_b_i[0], d_lam[0]
        d_w_ri = _mm(sv["xconv"], d_pre, "tn", "dw_lru_gates")

        def diag_blocks(wd):
            w4 = wd.reshape(LRU_BLOCKS, LRU_BLOCK_DIM, LRU_BLOCKS, LRU_BLOCK_DIM)
            return jnp.stack([w4[b, :, b, :] for b in range(LRU_BLOCKS)])

        grads["lru_w_r"][l] = diag_blocks(d_w_ri[:, :LRU_WIDTH])
        grads["lru_w_i"][l] = diag_blocks(d_w_ri[:, LRU_WIDTH:])
        d_conv = _mm(d_pre, sv["w_ri"], "nt", "d_xconv", res=d_xdir)

        def conv_bwd(i, rows, consts):
            w = consts[0]
            dc, dc_next8, xcur, x_prev8 = rows
            d = [dc] + [_rows_from_next(dc, dc_next8, s, i == Tp // TR - 1) for s in range(1, CONV_WIDTH)]
            xsh = conv_taps(i, xcur, x_prev8)
            dxc = d[0] * w[CONV_WIDTH - 1:CONV_WIDTH, :]
            for s in range(1, CONV_WIDTH):
                dxc = dxc + d[s] * w[CONV_WIDTH - 1 - s:CONV_WIDTH - s, :]
            return [dxc], [_colsum(xsh[t] * d[0]) for t in range(CONV_WIDTH)] + [_colsum(d[0])]

        res = _rowwise(conv_bwd, "conv_bwd", Tp, TR,
                       [(d_conv, None, 0), (d_conv, None, 0, "next"), col("xc"), col("xc") + ("prev",)], [sv["cw"]],
                       [(LRU_WIDTH, F32)], [LRU_WIDTH] * (CONV_WIDTH + 1))
        d_xc = res[0]
        grads["conv_w"][l] = jnp.concatenate(res[1:1 + CONV_WIDTH], axis=0)
        grads["conv_b"][l] = res[1 + CONV_WIDTH][0]

        (dqf, delta_f, dccol), got = _fox_bwd_dq(
            proj, sv["ccol"], sv["crow"], sv["o_f"], sv["lse_f"], do_f, TQF,
            host=[(g_send("w_ffn_in", grads["w_ffn_in"][l]), False)])
        recv["w_ffn_in"][l] = got[0]
        ride = [("w_out", l), ("w_branch", l)] + ([("w_in", l + 1)] if l + 1 < DEPTH else [])
        (dkf, dvf, dcrow), got = _fox_bwd_dkv(
            proj, sv["ccol"], sv["crow"], sv["lse_f"], delta_f, do_f, TQF,
            host=[(g_send(n, grads[n][ll]), False) for n, ll in ride])
        for (n, ll), r in zip(ride, got):
            recv[n][ll] = r
        lanes = lambda z: jnp.pad(z.reshape(FOX_HEADS, Tp).T, ((0, 0), (0, LANE - FOX_HEADS)))
        dlogf = _rev_scan(None, lanes(dcrow), "cumsum_bwd", g2=lanes(dccol))

        def logf_bwd(i, rows, consts):
            dl, fl = rows
            valid = (_row_ids(i, TR) >= n_pad).astype(F32)
            lane_ok = (lax.broadcasted_iota(jnp.int32, (1, LANE), 1) < FOX_HEADS).astype(F32)
            dfl = dl * _sigmoid(-(fl + consts[0])) * valid * lane_ok
            return [dfl], [_colsum(dfl)]

        d_fl, d_fb = _rowwise(logf_bwd, "logf_bwd", Tp, TR, [(dlogf, None, 0), col("fl")], [sv["fbias"]],
                              [(LANE, F32)], [LANE])
        grads["fox_forget_bias"][l] = d_fb[0, :FOX_HEADS]

        (dqa, dk_strip, dv_strip, d_bias_l, d_sink), got = _swa_bwd(
            proj, bias, sv["sinks"], sv["o_a"], sv["lse_a"], do_a, n_pad,
            host=[(g_send("w_ffn_out", grads["w_ffn_out"][l]), False)])
        recv["w_ffn_out"][l] = got[0]
        grads["swa_sinks"][l] = d_sink[:, 0, 0]
        d_bias_total = d_bias_l if d_bias_total is None else d_bias_total + d_bias_l

        def kv_parts(strip):
            own = strip[:, BLOCK:, :].reshape(Tp, KV_W)
            nxt = jnp.pad(strip[1:, :BLOCK, :], ((0, 1), (TQ - BLOCK, 0), (0, 0))).reshape(Tp, KV_W)
            return [(own, None, 0), (nxt, None, 0)]

        def assemble(i, rows, consts):
            dg, dqa_, dqf_, dkf_, dvf_, dxc_, dyc_, dk0, dk1, dv0, dv1, dfl_ = rows
            parts = [dg, dqa_, dqf_, dkf_, dvf_, dxc_, dyc_, dk0 + dk1, dv0 + dv1, dfl_]
            return [jnp.concatenate([p.astype(BF16) for p in parts], axis=1)], []

        (dproj,) = _rowwise(
            assemble, "assemble_dproj", Tp, BLOCK,
            [(d_gates, None, 0), (dqa, None, 0), (dqf, None, 0), (dkf, None, 0), (dvf, None, 0), (d_xc, None, 0),
             (d_yc, None, 0)] + kv_parts(dk_strip) + kv_parts(dv_strip) + [(d_fl, None, 0)], [],
            [(WORK_COLS, BF16)])
        grads["w_in"][l] = _mm(sv["u"], dproj, "tn", "dw_in", tn_cap=384, out_dtype=BF16)
        if l == 0:
            du, got = _mm(dproj, W_in[l], "nt", "du", tk_cap=2176,
                          host=[(g_send("w_in", grads["w_in"][0]), False)])
            recv["w_in"][0] = got[0]
        else:
            du = _mm(dproj, W_in[l], "nt", "du", tk_cap=2176)
        dh, dg = norm_bwd_call(sv["h"], norm_mix[l][None, :], du, dh1)
        grads["norm_mix"][l] = dg[0]

    d_table = _mm(d_bias_total.reshape(SWA_Q_HEADS, BLOCK * 2 * BLOCK), onehot, "nt", "d_rel_table", exact=True,
                  tk_cap=4096)
    g_rel = d_table[:, :REL_BUCKETS].T
    g_meta_full = dh[n_pad:first]
    grad_x = dh[first:][None]

    stack = lambda k: jnp.stack(grads[k])
    big_res = {}
    for n in big_names:
        per_layer = [_adam(recv[n][l], rows2d(big_w[n]), rows2d(big_m[n]), rows2d(big_v[n]), "adam_" + n, layer=l)
                     for l in range(DEPTH)]
        big_res[n] = [jnp.stack([per_layer[l][k] for l in range(DEPTH)]).reshape(big_w[n].shape) for k in range(4)]

    rep_names = ("rel_bias_table", "norm_mix", "swa_sinks", "fox_forget_bias", "conv_b", "lru_w_r", "lru_b_r",
                 "lru_w_i", "lru_b_i", "lru_lambda", "norm_ffn", "norm_final")
    rep_w = dict(rel_bias_table=rel_bias_table, norm_mix=norm_mix, swa_sinks=swa_sinks,
                 fox_forget_bias=fox_forget_bias, conv_b=conv_b, lru_w_r=lru_w_r, lru_b_r=lru_b_r, lru_w_i=lru_w_i,
                 lru_b_i=lru_b_i, lru_lambda=lru_lambda, norm_ffn=norm_ffn, norm_final=norm_final)
    rep_m = dict(rel_bias_table=m_rel_bias_table, norm_mix=m_norm_mix, swa_sinks=m_swa_sinks,
                 fox_forget_bias=m_fox_forget_bias, conv_b=m_conv_b, lru_w_r=m_lru_w_r, lru_b_r=m_lru_b_r,
                 lru_w_i=m_lru_w_i, lru_b_i=m_lru_b_i, lru_lambda=m_lru_lambda, norm_ffn=m_norm_ffn,
                 norm_final=m_norm_final)
    rep_v = dict(rel_bias_table=v_rel_bias_table, norm_mix=v_norm_mix, swa_sinks=v_swa_sinks,
                 fox_forget_bias=v_fox_forget_bias, conv_b=v_conv_b, lru_w_r=v_lru_w_r, lru_b_r=v_lru_b_r,
                 lru_w_i=v_lru_w_i, lru_b_i=v_lru_b_i, lru_lambda=v_lru_lambda, norm_ffn=v_norm_ffn,
                 norm_final=v_norm_final)
    rep_g = {n: (g_rel if n == "rel_bias_table" else d_norm_final[0] if n == "norm_final" else stack(n))
             for n in rep_names}
    small_g = [rep_g[n] for n in rep_names] + [g_meta_full, stack("conv_w")]
    small_shapes = [rep_w[n].shape for n in rep_names] + [(N_META, D_MODEL), (DEPTH, CONV_WIDTH, LRU_WIDTH)]
    gs_all = _exchange(_pack(small_g, D_MODEL, 8), True, "gather_small_grads")
    gs_sum = _unpack(_sum_parts(gs_all, "sum_small_grads"), small_shapes)
    gsum = dict(zip(rep_names, gs_sum[:len(rep_names)]))
    g_meta_sh = lax.dynamic_slice_in_dim(gs_sum[-2], me * (D_MODEL // N_DEV), D_MODEL // N_DEV, axis=1)
    g_convw_sh = lax.dynamic_slice_in_dim(gs_sum[-1], me * (LRU_WIDTH // N_DEV), LRU_WIDTH // N_DEV, axis=2)
    sm_names = rep_names + ("meta_tokens", "conv_w")
    sm_w = [rep_w[n] for n in rep_names] + [meta_tokens, conv_w]
    sm_m = [rep_m[n] for n in rep_names] + [m_meta_tokens, m_conv_w]
    sm_v = [rep_v[n] for n in rep_names] + [v_meta_tokens, v_conv_w]
    sm_g = [gsum[n] for n in rep_names] + [g_meta_sh, g_convw_sh]
    sm_shapes = [w.shape for w in sm_w]
    sm_out = [_unpack(o, sm_shapes) for o in _adam(_pack(sm_g, D_MODEL, 8)[None], _pack(sm_w, D_MODEL, 8),
                                                   _pack(sm_m, D_MODEL, 8), _pack(sm_v, D_MODEL, 8), "adam_small")]
    sm_res = {n: [sm_out[k][j] for k in range(4)] for j, n in enumerate(sm_names)}

    order = ("meta_tokens", "rel_bias_table", "norm_mix", "w_in", "swa_sinks", "fox_forget_bias", "conv_w", "conv_b",
             "lru_w_r", "lru_b_r", "lru_w_i", "lru_b_i", "lru_lambda", "w_branch", "w_out", "norm_ffn", "w_ffn_in",
             "w_ffn_out", "norm_final")
    allres = {**big_res, **sm_res}
    outs = [loss, grad_x]
    for k in range(4):
        outs += [allres[n][k] for n in order]
    return tuple(outs)
```

```python
import functools
import math

import numpy as np
import jax
import jax.numpy as jnp
from jax import lax
from jax.experimental import pallas as pl
from jax.experimental.pallas import tpu as pltpu

F32 = jnp.float32
BF16 = jnp.bfloat16

N_DEV = 8
D_MODEL = 1024
DEPTH = 4
HEAD_DIM = 64
N_META = 16
BLOCK = 128
NEG_INF = -1e30
SWA_Q_HEADS = 8
SWA_KV_HEADS = 2
SWA_GROUP = SWA_Q_HEADS // SWA_KV_HEADS
FOX_HEADS = 8
LRU_WIDTH = D_MODEL // 2
LRU_BLOCKS = 8
LRU_BLOCK_DIM = LRU_WIDTH // LRU_BLOCKS
CONV_WIDTH = 4
LRU_C = 8.0
REL_BUCKETS = 32
REL_MAX_DIST = 128
D_FF = 2816
N_BRANCH = 3
ATT_W = SWA_Q_HEADS * HEAD_DIM
KV_W = SWA_KV_HEADS * HEAD_DIM
SCALE = HEAD_DIM ** -0.5
EPS = 1e-6

_ORIG = (("qa", ATT_W), ("ka", KV_W), ("va", KV_W), ("qf", ATT_W), ("kf", ATT_W), ("vf", ATT_W),
         ("fl", FOX_HEADS), ("xc", LRU_WIDTH), ("yc", LRU_WIDTH), ("gates", N_BRANCH * D_MODEL))
IN_COLS = sum(w for _, w in _ORIG)
_WORK = (("gates", 3072), ("qa", 512), ("qf", 512), ("kf", 512), ("vf", 512), ("xc", 512), ("yc", 512),
         ("ka", 128), ("va", 128), ("fl", 128))
WORK_COLS = sum(w for _, w in _WORK)


def _offsets(table):
    off, out = 0, {}
    for n, w in table:
        out[n] = (off, w)
        off += w
    return out


_ORIG_OFF = _offsets(_ORIG)
_WORK_OFF = _offsets(_WORK)

ADAM_LR = 0.001
ADAM_B1 = 0.9
ADAM_B2 = 0.999
ADAM_EPS = 1e-08
ADAM_WD = 0.01
ADAM_STEP = 10

VMEM_LIMIT = 62 * 1024 * 1024
FOX_TILE = 1408
LANE = 128


def _tile(n, cap, mult=LANE):
    if n <= cap:
        return n
    best = None
    for d in range(mult, cap + 1, mult):
        if n % d == 0:
            best = d
    assert best is not None, (n, cap, mult)
    return best


def _params(sem):
    return pltpu.CompilerParams(dimension_semantics=sem, vmem_limit_bytes=VMEM_LIMIT)


def _sigmoid(x):
    return 1.0 / (1.0 + jnp.exp(-x))


def _log_sigmoid(x):
    return jnp.minimum(x, 0.0) - jnp.log(1.0 + jnp.exp(-jnp.abs(x)))


def _neg_expm1(x):
    series = -x * (1.0 + x * (0.5 + x * (1.0 / 6.0 + x * (1.0 / 24.0 + x * (1.0 / 120.0)))))
    return jnp.where(x > -0.1, series, 1.0 - jnp.exp(x))


_GELU_C = math.sqrt(2.0 / math.pi)


def _gelu(x):
    return 0.5 * x * (1.0 + jnp.tanh(_GELU_C * (x + 0.044715 * x * x * x)))


def _gelu_grad(x):
    t = jnp.tanh(_GELU_C * (x + 0.044715 * x * x * x))
    return 0.5 * (1.0 + t) + 0.5 * x * (1.0 - t * t) * _GELU_C * (1.0 + 3.0 * 0.044715 * x * x)


def _xch_ops(src_ref, out_ref, send_sems, recv_sems, local_sem, gather):
    x, y, c = lax.axis_index("x"), lax.axis_index("y"), lax.axis_index("c")
    me = 4 * x + 2 * y + c

    def copy(r, mine):
        px, py, pc = x ^ ((r >> 2) & 1), y ^ ((r >> 1) & 1), c ^ (r & 1)
        pid = 4 * px + 2 * py + pc
        return pltpu.make_async_remote_copy(
            src_ref=src_ref if gather else src_ref.at[pid], dst_ref=out_ref.at[me if mine else pid],
            send_sem=send_sems.at[r - 1], recv_sem=recv_sems.at[r - 1],
            device_id=(px, py, pc), device_id_type=pl.DeviceIdType.MESH)

    def local():
        return pltpu.make_async_copy(src_ref if gather else src_ref.at[me], out_ref.at[me], local_sem)

    def start():
        local().start()
        for r in range(1, N_DEV):
            copy(r, True).start()

    def wait():
        for r in range(1, N_DEV):
            copy(r, False).wait_recv()
        for r in range(1, N_DEV):
            copy(r, True).wait_send()
        local().wait()

    return start, wait


_XCH_SEMS = [pltpu.SemaphoreType.DMA((N_DEV - 1,)), pltpu.SemaphoreType.DMA((N_DEV - 1,)), pltpu.SemaphoreType.DMA]
_ANY = pl.BlockSpec(memory_space=pl.ANY)


def _xch_shape(src, gather):
    return jax.ShapeDtypeStruct((N_DEV,) + tuple(src.shape if gather else src.shape[1:]), src.dtype)


def _exchange(src, gather, name):
    def body(src_ref, out_ref, send_sems, recv_sems, local_sem):
        start, wait = _xch_ops(src_ref, out_ref, send_sems, recv_sems, local_sem, gather)
        start()
        wait()

    return pl.pallas_call(body, name=name, in_specs=[_ANY], out_specs=_ANY, out_shape=_xch_shape(src, gather),
                          scratch_shapes=list(_XCH_SEMS))(src)


def _pcall(body, name, grid, in_specs, out_specs, out_shape, scratch_shapes, sem, args, host=()):
    if not host:
        outs = pl.pallas_call(body, name=name, grid=grid, in_specs=in_specs, out_specs=out_specs, out_shape=out_shape,
                              scratch_shapes=scratch_shapes, compiler_params=_params(sem))(*args)
        return outs, []
    n_in, n_out, n_scr, n_x = len(args), len(out_shape), len(scratch_shapes), len(host)

    def wrapped(*refs):
        ins, xin = refs[:n_in], refs[n_in:n_in + n_x]
        outs, xout = refs[n_in + n_x:n_in + n_x + n_out], refs[n_in + n_x + n_out:n_in + 2 * n_x + n_out]
        scr = refs[n_in + 2 * n_x + n_out:n_in + 2 * n_x + n_out + n_scr]
        sems = refs[n_in + 2 * n_x + n_out + n_scr:]
        first = functools.reduce(jnp.logical_and, [pl.program_id(k) == 0 for k in range(len(grid))])
        last = functools.reduce(jnp.logical_and, [pl.program_id(k) == grid[k] - 1 for k in range(len(grid))])
        ops = [_xch_ops(xin[k], xout[k], *sems[3 * k:3 * k + 3], host[k][1]) for k in range(n_x)]

        @pl.when(first)
        def _():
            for start, _ in ops:
                start()

        body(*ins, *outs, *scr)

        @pl.when(last)
        def _():
            for _, wait in ops:
                wait()

    res = pl.pallas_call(
        wrapped, name=name, grid=grid, in_specs=list(in_specs) + [_ANY] * n_x,
        out_specs=list(out_specs) + [_ANY] * n_x,
        out_shape=list(out_shape) + [_xch_shape(s, g) for s, g in host],
        scratch_shapes=list(scratch_shapes) + list(_XCH_SEMS) * n_x,
        compiler_params=_params(("arbitrary",) * len(grid)))(*args, *[s for s, _ in host])
    return res[:n_out], res[n_out:]


def _mm(a, b, mode, name, res=None, exact=False, tm_cap=1408, tn_cap=512, tk_cap=1408, host=None, out_dtype=F32):
    if mode == "nn":
        (M, K), (K2, N) = a.shape, b.shape
    elif mode == "nt":
        (M, K), (N, K2) = a.shape, b.shape
    else:
        (K, M), (K2, N) = a.shape, b.shape
    assert K == K2, (a.shape, b.shape, mode)
    tm, tn, tk = _tile(M, tm_cap), _tile(N, tn_cap), _tile(K, tk_cap)
    nk = K // tk
    a_spec = {"nn": pl.BlockSpec((tm, tk), lambda i, j, k: (i, k)),
              "nt": pl.BlockSpec((tm, tk), lambda i, j, k: (i, k)),
              "tn": pl.BlockSpec((tk, tm), lambda i, j, k: (k, i))}[mode]
    b_spec = {"nn": pl.BlockSpec((tk, tn), lambda i, j, k: (k, j)),
              "nt": pl.BlockSpec((tn, tk), lambda i, j, k: (j, k)),
              "tn": pl.BlockSpec((tk, tn), lambda i, j, k: (k, j))}[mode]
    o_spec = pl.BlockSpec((tm, tn), lambda i, j, k: (i, j))
    has_res = res is not None

    def body(*refs):
        if has_res:
            a_ref, b_ref, r_ref, o_ref, acc_ref = refs
        else:
            a_ref, b_ref, o_ref, acc_ref = refs
        k = pl.program_id(2)

        @pl.when(k == 0)
        def _():
            acc_ref[...] = jnp.zeros_like(acc_ref)

        x, y = a_ref[...], b_ref[...]
        if exact:
            x, y, prec = x.astype(F32), y.astype(F32), lax.Precision.HIGHEST
        else:
            x, y, prec = x.astype(BF16), y.astype(BF16), None
        if mode == "tn":
            x = x.T
        dims = (((1,), (1,)), ((), ())) if mode == "nt" else (((1,), (0,)), ((), ()))
        acc_ref[...] += lax.dot_general(x, y, dims, precision=prec, preferred_element_type=F32)

        @pl.when(k == nk - 1)
        def _():
            if has_res:
                o_ref[...] = (acc_ref[...] + r_ref[...]).astype(out_dtype)
            else:
                o_ref[...] = acc_ref[...].astype(out_dtype)

    in_specs = [a_spec, b_spec] + ([o_spec] if has_res else [])
    args = (a, b) + ((res,) if has_res else ())
    (out,), xouts = _pcall(body, name, (M // tm, N // tn, nk), in_specs, [o_spec],
                           [jax.ShapeDtypeStruct((M, N), out_dtype)], [pltpu.VMEM((tm, tn), F32)],
                           ("parallel", "parallel", "arbitrary"), args, host or ())
    return out if host is None else (out, xouts)


def _rowwise(fn, name, n_rows, tr, row_ins, const_ins, row_outs, red_outs=()):
    n_row_in, n_const, n_row_out, n_red = len(row_ins), len(const_ins), len(row_outs), len(red_outs)
    nblk, per = n_rows // tr, tr // 8
    in_specs = []
    for arr, w, idx, *side in row_ins:
        width = arr.shape[1] if w is None else w
        if not side:
            in_specs.append(pl.BlockSpec((tr, width), functools.partial(lambda i, j: (i, j), j=idx)))
        elif side[0] == "prev":
            in_specs.append(pl.BlockSpec((8, width), functools.partial(
                lambda i, j: (jnp.maximum(i * per - 1, 0), j), j=idx)))
        else:
            in_specs.append(pl.BlockSpec((8, width), functools.partial(
                lambda i, j: (jnp.minimum((i + 1) * per, nblk * per - 1), j), j=idx)))
    for c in const_ins:
        in_specs.append(pl.BlockSpec(c.shape, lambda i: (0, 0)))
    out_shape = [jax.ShapeDtypeStruct((n_rows, c), dt) for c, dt in row_outs]
    out_shape += [jax.ShapeDtypeStruct((1, c), F32) for c in red_outs]
    out_specs = [pl.BlockSpec((tr, c), lambda i: (i, 0)) for c, _ in row_outs]
    out_specs += [pl.BlockSpec((1, c), lambda i: (0, 0)) for c in red_outs]

    def body(*refs):
        i = pl.program_id(0)
        rows = [r[...] for r in refs[:n_row_in]]
        consts = [r[...] for r in refs[n_row_in:n_row_in + n_const]]
        outs = refs[n_row_in + n_const:]
        row_vals, red_vals = fn(i, rows, consts)
        for k in range(n_row_out):
            outs[k][...] = row_vals[k].astype(outs[k].dtype)
        if n_red:
            @pl.when(i == 0)
            def _():
                for k in range(n_red):
                    outs[n_row_out + k][...] = jnp.zeros_like(outs[n_row_out + k])

            for k in range(n_red):
                outs[n_row_out + k][...] += red_vals[k]

    res = pl.pallas_call(
        body, name=name, grid=(n_rows // tr,), in_specs=in_specs, out_specs=out_specs, out_shape=out_shape,
        compiler_params=_params(("arbitrary",)))(*[r[0] for r in row_ins], *const_ins)
    return res


def _rows_from_prev(cur, prev8, s, first):
    rolled = pltpu.roll(cur, s, 0)
    top = jnp.where(first, 0.0, pltpu.roll(prev8, s, 0))
    r8 = lax.broadcasted_iota(jnp.int32, prev8.shape, 0)
    return jnp.concatenate([jnp.where(r8 < s, top, rolled[:8]), rolled[8:]], axis=0)


def _rows_from_next(cur, next8, s, last):
    tr = cur.shape[0]
    rolled = pltpu.roll(cur, tr - s, 0)
    bottom = jnp.where(last, 0.0, pltpu.roll(next8, 8 - s, 0))
    r8 = lax.broadcasted_iota(jnp.int32, next8.shape, 0)
    return jnp.concatenate([rolled[:tr - 8], jnp.where(r8 >= 8 - s, bottom, rolled[tr - 8:])], axis=0)


def _row_ids(i, tr):
    return i * tr + lax.broadcasted_iota(jnp.int32, (tr, 1), 0)


def _colsum(x):
    return jnp.sum(x, axis=0, keepdims=True)


def _scan(a, b, name, b2=None, reverse=False):
    n_rows, c = b.shape
    tr = _tile(n_rows, 384)
    nblk = n_rows // tr
    has_a = a is not None
    has_b2 = b2 is not None

    def body(*refs):
        o_ref, carry, a_edge = refs[-3:]
        ins = list(refs[:-3])
        a_ref = ins.pop(0) if has_a else None
        b_ref = ins.pop(0)

        @pl.when(pl.program_id(0) == 0)
        def _():
            carry[...] = jnp.zeros_like(carry)
            a_edge[...] = jnp.zeros_like(a_edge)

        rows = lax.broadcasted_iota(jnp.int32, (tr, c), 0)
        bv = b_ref[...] + ins[0][...] if has_b2 else b_ref[...]
        av = a_ref[...] if has_a else None
        if has_a and reverse:
            a_first = av[0:1, :]
            av = _rows_from_next(av, a_edge[...], 1, False)
            a_edge[...] = jnp.broadcast_to(a_first, a_edge.shape)
        s = 1
        while s < tr:
            keep = rows < tr - s if reverse else rows >= s
            shift = tr - s if reverse else s
            b_sh = jnp.where(keep, pltpu.roll(bv, shift, 0), 0.0)
            if has_a:
                a_sh = jnp.where(keep, pltpu.roll(av, shift, 0), 1.0)
                bv = av * b_sh + bv
                av = av * a_sh
            else:
                bv = b_sh + bv
            s *= 2
        h = av * carry[0:1, :] + bv if has_a else carry[0:1, :] + bv
        o_ref[...] = h
        edge = 0 if reverse else tr - 1
        carry[...] = jnp.broadcast_to(h[edge:edge + 1, :], carry.shape)

    spec = pl.BlockSpec((tr, c), (lambda i: (nblk - 1 - i, 0)) if reverse else (lambda i: (i, 0)))
    args = ((a,) if has_a else ()) + (b,) + ((b2,) if has_b2 else ())
    return pl.pallas_call(
        body, name=name, grid=(n_rows // tr,), in_specs=[spec] * len(args), out_specs=spec,
        out_shape=jax.ShapeDtypeStruct((n_rows, c), F32),
        scratch_shapes=[pltpu.VMEM((8, c), F32), pltpu.VMEM((8, c), F32)],
        compiler_params=_params(("arbitrary",)))(*args)


def _rev_scan(a, g, name, g2=None):
    return _scan(a, g, name, b2=g2, reverse=True)


def _swa_masks(blk, n_pad):
    qi = lax.broadcasted_iota(jnp.int32, (BLOCK, 2 * BLOCK), 0)
    ki = lax.broadcasted_iota(jnp.int32, (BLOCK, 2 * BLOCK), 1)
    dist = qi + BLOCK - ki
    key_abs = (blk - 1) * BLOCK + ki
    return (dist >= 0) & (dist < BLOCK) & (key_abs >= n_pad)


_QA_COL = _WORK_OFF["qa"][0] // ATT_W
_KA_COL, _VA_COL = (_WORK_OFF[n][0] // LANE for n in ("ka", "va"))


def _swa_strips(prev_ref, cur_ref):
    x = jnp.concatenate([prev_ref[...], cur_ref[...]], axis=0)
    return [x.astype(BF16), pltpu.roll(x, HEAD_DIM, 1).astype(BF16)]


def _swa_swapped(h):
    return 0 if h % 2 == h // SWA_GROUP else 1


def _swa_specs(tq, nsub):
    prev = lambda col: pl.BlockSpec((BLOCK, LANE), lambda i: (jnp.maximum(i * nsub - 1, 0), col))
    cur = lambda col: pl.BlockSpec((tq, LANE), lambda i: (i, col))
    return [pl.BlockSpec((tq, ATT_W), lambda i: (i, _QA_COL)), prev(_KA_COL), cur(_KA_COL), prev(_VA_COL), cur(_VA_COL),
            pl.BlockSpec((SWA_Q_HEADS, BLOCK, 2 * BLOCK), lambda i: (0, 0, 0)),
            pl.BlockSpec((SWA_Q_HEADS, 1, LANE), lambda i: (0, 0, 0))]


def _swa_fwd(proj, bias, sinks, n_pad, host=()):
    Tp = proj.shape[0]
    tq = _tile(Tp, 384)
    nsub = tq // BLOCK

    def body(q_ref, kp_ref, kc_ref, vp_ref, vc_ref, b_ref, s_ref, o_ref, lse_ref):
        i = pl.program_id(0)
        low = _low_lanes()
        ks, vs = _swa_strips(kp_ref, kc_ref), _swa_strips(vp_ref, vc_ref)
        for j in range(nsub):
            rows, keys = slice(j * BLOCK, (j + 1) * BLOCK), slice(j * BLOCK, (j + 2) * BLOCK)
            mask = _swa_masks(i * nsub + j, n_pad)
            for hb in range(SWA_Q_HEADS // 2):
                pair = slice(hb * LANE, (hb + 1) * LANE)
                qh = _split_pair(q_ref[rows, pair] * SCALE, low)
                outs = []
                for e in range(2):
                    h = 2 * hb + e
                    kk, vv = ks[_swa_swapped(h)][keys], vs[_swa_swapped(h)][keys]
                    s = lax.dot_general(qh[e], kk, (((1,), (1,)), ((), ())), preferred_element_type=F32)
                    s = jnp.where(mask, s + b_ref[h], NEG_INF)
                    sink = s_ref[h, :, 0:1]
                    m = jnp.maximum(jnp.max(s, axis=1, keepdims=True), sink)
                    p = jnp.exp(s - m)
                    denom = jnp.sum(p, axis=1, keepdims=True) + jnp.exp(sink - m)
                    outs.append(jnp.dot((p / denom).astype(BF16), vv, preferred_element_type=F32))
                    lse_ref[h, rows, :] = m + jnp.log(denom)
                o_ref[rows, pair] = jnp.where(low, outs[0], outs[1])

    return _pcall(
        body, "swa_fwd", (Tp // tq,), _swa_specs(tq, nsub),
        [pl.BlockSpec((tq, ATT_W), lambda i: (i, 0)), pl.BlockSpec((SWA_Q_HEADS, tq, 1), lambda i: (0, i, 0))],
        [jax.ShapeDtypeStruct((Tp, ATT_W), F32), jax.ShapeDtypeStruct((SWA_Q_HEADS, Tp, 1), F32)],
        [], ("arbitrary",), (proj, proj, proj, proj, proj, bias, sinks), host)


def _swa_bwd(proj, bias, sinks, o, lse, do, n_pad, host=()):
    Tp = proj.shape[0]
    tq = _tile(Tp, 384)
    nsub = tq // BLOCK
    nq = Tp // tq
    strip = BLOCK + tq

    def body(q_ref, kp_ref, kc_ref, vp_ref, vc_ref, b_ref, s_ref, o_ref, lse_ref, do_ref,
             dq_ref, dk_ref, dv_ref, db_ref, ds_ref, dk_sw, dv_sw):
        i = pl.program_id(0)

        @pl.when(i == 0)
        def _():
            db_ref[...] = jnp.zeros_like(db_ref)
            ds_ref[...] = jnp.zeros_like(ds_ref)

        for ref in (dk_ref, dv_ref, dk_sw, dv_sw):
            ref[...] = jnp.zeros_like(ref)
        low = _low_lanes()
        ks, vs = _swa_strips(kp_ref, kc_ref), _swa_strips(vp_ref, vc_ref)
        for j in range(nsub):
            rows, keys = slice(j * BLOCK, (j + 1) * BLOCK), slice(j * BLOCK, (j + 2) * BLOCK)
            mask = _swa_masks(i * nsub + j, n_pad)
            for hb in range(SWA_Q_HEADS // 2):
                pair = slice(hb * LANE, (hb + 1) * LANE)
                qh = _split_pair(q_ref[rows, pair] * SCALE, low)
                dof = do_ref[rows, pair]
                prod = dof * o_ref[rows, pair]
                doh = _split_pair(dof, low)
                dqs = []
                for e in range(2):
                    h = 2 * hb + e
                    sw = _swa_swapped(h)
                    kk, vv = ks[sw][keys], vs[sw][keys]
                    delta = jnp.sum(jnp.where(low, prod, 0.0) if e == 0 else jnp.where(low, 0.0, prod),
                                    axis=1, keepdims=True)
                    lse_h = lse_ref[h, rows, :]
                    s = lax.dot_general(qh[e], kk, (((1,), (1,)), ((), ())), preferred_element_type=F32)
                    s = jnp.where(mask, s + b_ref[h], NEG_INF)
                    p = jnp.exp(s - lse_h)
                    p_sink = jnp.exp(s_ref[h, :, 0:1] - lse_h)
                    dp = lax.dot_general(doh[e], vv, (((1,), (1,)), ((), ())), preferred_element_type=F32)
                    ds = p * (dp - delta)
                    db_ref[h] += ds
                    ds_ref[h] += jnp.broadcast_to(-jnp.sum(p_sink * delta, axis=0, keepdims=True), (1, LANE))
                    dqs.append(jnp.dot(ds.astype(BF16), kk, preferred_element_type=F32))
                    dk_h = jnp.dot(ds.T.astype(BF16), qh[e], preferred_element_type=F32)
                    dv_h = jnp.dot(p.T.astype(BF16), doh[e], preferred_element_type=F32)
                    if sw:
                        dk_sw[keys, :] += dk_h
                        dv_sw[keys, :] += dv_h
                    else:
                        dk_ref[0, keys, :] += dk_h
                        dv_ref[0, keys, :] += dv_h
                dq_ref[rows, pair] = jnp.where(low, dqs[0], dqs[1]) * SCALE
        dk_ref[0] += pltpu.roll(dk_sw[...], HEAD_DIM, 1)
        dv_ref[0] += pltpu.roll(dv_sw[...], HEAD_DIM, 1)

    wide = pl.BlockSpec((tq, ATT_W), lambda i: (i, 0))
    sspec = pl.BlockSpec((1, strip, LANE), lambda i: (i, 0, 0))
    return _pcall(
        body, "swa_bwd", (nq,),
        _swa_specs(tq, nsub) + [wide, pl.BlockSpec((SWA_Q_HEADS, tq, 1), lambda i: (0, i, 0)), wide],
        [wide, sspec, sspec, pl.BlockSpec((SWA_Q_HEADS, BLOCK, 2 * BLOCK), lambda i: (0, 0, 0)),
         pl.BlockSpec((SWA_Q_HEADS, 1, LANE), lambda i: (0, 0, 0))],
        [jax.ShapeDtypeStruct((Tp, ATT_W), F32), jax.ShapeDtypeStruct((nq, strip, LANE), F32),
         jax.ShapeDtypeStruct((nq, strip, LANE), F32),
         jax.ShapeDtypeStruct((SWA_Q_HEADS, BLOCK, 2 * BLOCK), F32), jax.ShapeDtypeStruct((SWA_Q_HEADS, 1, LANE), F32)],
        [pltpu.VMEM((strip, LANE), F32), pltpu.VMEM((strip, LANE), F32)], ("arbitrary",),
        (proj, proj, proj, proj, proj, bias, sinks, o, lse, do), host)


def _fox_scores(qb, kk, cq, ck, diag, tq):
    s = lax.dot_general(qb, kk, (((1,), (1,)), ((), ())), preferred_element_type=F32)
    s = s + cq - ck
    return _fox_causal(s, tq) if diag else s


def _fox_causal(s, tq):
    r = lax.broadcasted_iota(jnp.int32, (tq, tq), 0)
    c = lax.broadcasted_iota(jnp.int32, (tq, tq), 1)
    return jnp.where(c <= r, s, NEG_INF)


_FOX_PAIRS = FOX_HEADS // 2
_QF_COL, _KF_COL, _VF_COL = (_WORK_OFF[n][0] // LANE for n in ("qf", "kf", "vf"))


def _low_lanes():
    return lax.broadcasted_iota(jnp.int32, (1, LANE), 1) < HEAD_DIM


def _split_pair(x, low):
    return [jnp.where(low, x, 0.0).astype(BF16), jnp.where(low, 0.0, x).astype(BF16)]


def _fox_fwd(proj, ccol, crow, tq, host=(), prefetch=True):
    Tp = proj.shape[0]
    nq = Tp // tq

    def body(q_ref, k_ref, v_ref, cc_ref, cr_ref, o_ref, lse_ref):
        qi = pl.program_id(1)
        low = _low_lanes()
        qh = _split_pair(q_ref[...] * SCALE, low)
        cq = [cc_ref[0], cc_ref[1]]

        def scores(kb):
            kk = k_ref[pl.ds(pl.multiple_of(kb * tq, tq), tq), :].astype(BF16)
            return [_fox_scores(qh[e], kk, cq[e], cr_ref[e, kb], False, tq) for e in range(2)]

        def softmax_pv(kb, s, carry, diag):
            vv = v_ref[pl.ds(pl.multiple_of(kb * tq, tq), tq), :].astype(BF16)
            stats, upd = [], []
            for e in range(2):
                m, l = carry[2 * e], carry[2 * e + 1]
                se = _fox_causal(s[e], tq) if diag else s[e]
                m_new = jnp.maximum(m, jnp.max(se, axis=1, keepdims=True))
                alpha = jnp.exp(m - m_new)
                p = jnp.exp(se - m_new)
                stats += [m_new, alpha * l + jnp.sum(p, axis=1, keepdims=True)]
                upd.append(alpha * carry[4] + jnp.dot(p.astype(BF16), vv, preferred_element_type=F32))
            return (*stats, jnp.where(low, upd[0], upd[1]))

        col = lambda val: jnp.full((tq, 1), val, F32)
        init = (col(NEG_INF), col(0.0), col(NEG_INF), col(0.0), jnp.zeros((tq, LANE), F32))
        if prefetch:
            def step(kb, c):
                s_next = scores(kb + 1)
                return (*softmax_pv(kb, c[5:], c[:5], False), *s_next)

            c = lax.fori_loop(0, qi, step, (*init, *scores(0)))
            m0, l0, m1, l1, acc = softmax_pv(qi, c[5:], c[:5], True)
        else:
            c = lax.fori_loop(0, qi, lambda kb, c: softmax_pv(kb, scores(kb), c, False), init)
            m0, l0, m1, l1, acc = softmax_pv(qi, scores(qi), c, True)
        o_ref[...] = jnp.where(low, acc / l0, acc / l1)
        lse_ref[0] = m0 + jnp.log(l0)
        lse_ref[1] = m1 + jnp.log(l1)

    cspec = pl.BlockSpec((2, tq, 1), lambda h, i: (h, i, 0))
    return _pcall(
        body, "fox_fwd", (_FOX_PAIRS, nq),
        [pl.BlockSpec((tq, LANE), lambda h, i: (i, _QF_COL + h)), pl.BlockSpec((Tp, LANE), lambda h, i: (0, _KF_COL + h)),
         pl.BlockSpec((Tp, LANE), lambda h, i: (0, _VF_COL + h)), cspec,
         pl.BlockSpec((2, nq, 1, tq), lambda h, i: (h, 0, 0, 0))],
        [pl.BlockSpec((tq, LANE), lambda h, i: (i, h)), cspec],
        [jax.ShapeDtypeStruct((Tp, FOX_HEADS * HEAD_DIM), F32), jax.ShapeDtypeStruct((FOX_HEADS, Tp, 1), F32)],
        [], ("parallel", "arbitrary"), (proj, proj, proj, ccol, crow), host)


def _fox_bwd_dq(proj, ccol, crow, o, lse, do, tq, host=()):
    Tp = proj.shape[0]
    nq = Tp // tq

    def body(q_ref, k_ref, v_ref, cc_ref, cr_ref, o_ref, lse_ref, do_ref, dq_ref, dl_ref, dc_ref):
        qi = pl.program_id(1)
        low = _low_lanes()
        qh = _split_pair(q_ref[...] * SCALE, low)
        dof = do_ref[...]
        prod = dof * o_ref[...]
        delta = [jnp.sum(jnp.where(low, prod, 0.0), axis=1, keepdims=True),
                 jnp.sum(jnp.where(low, 0.0, prod), axis=1, keepdims=True)]
        doh = _split_pair(dof, low)
        cq = [cc_ref[0], cc_ref[1]]
        lse = [lse_ref[0], lse_ref[1]]

        def step(kb, carry, diag):
            rows = pl.ds(pl.multiple_of(kb * tq, tq), tq)
            kk, vv = k_ref[rows, :].astype(BF16), v_ref[rows, :].astype(BF16)
            dqs, dcs = [], []
            for e in range(2):
                s = _fox_scores(qh[e], kk, cq[e], cr_ref[e, kb], diag, tq)
                p = jnp.exp(s - lse[e])
                dp = lax.dot_general(doh[e], vv, (((1,), (1,)), ((), ())), preferred_element_type=F32)
                ds = p * (dp - delta[e])
                dqs.append(jnp.dot(ds.astype(BF16), kk, preferred_element_type=F32))
                dcs.append(carry[1 + e] + jnp.sum(ds, axis=1, keepdims=True))
            return (carry[0] + jnp.where(low, dqs[0], dqs[1]), *dcs)

        init = (jnp.zeros((tq, LANE), F32), jnp.zeros((tq, 1), F32), jnp.zeros((tq, 1), F32))
        dq, dc0, dc1 = step(qi, lax.fori_loop(0, qi, functools.partial(step, diag=False), init), True)
        dq_ref[...] = dq * SCALE
        dl_ref[0], dl_ref[1] = delta
        dc_ref[0], dc_ref[1] = dc0, dc1

    cspec = pl.BlockSpec((2, tq, 1), lambda h, i: (h, i, 0))
    pspec = pl.BlockSpec((tq, LANE), lambda h, i: (i, h))
    stat = jax.ShapeDtypeStruct((FOX_HEADS, Tp, 1), F32)
    return _pcall(
        body, "fox_bwd_dq", (_FOX_PAIRS, nq),
        [pl.BlockSpec((tq, LANE), lambda h, i: (i, _QF_COL + h)), pl.BlockSpec((Tp, LANE), lambda h, i: (0, _KF_COL + h)),
         pl.BlockSpec((Tp, LANE), lambda h, i: (0, _VF_COL + h)), cspec,
         pl.BlockSpec((2, nq, 1, tq), lambda h, i: (h, 0, 0, 0)), pspec, cspec, pspec],
        [pspec, cspec, cspec], [jax.ShapeDtypeStruct((Tp, FOX_HEADS * HEAD_DIM), F32), stat, stat],
        [], ("parallel", "arbitrary"), (proj, proj, proj, ccol, crow, o, lse, do), host)


def _fox_bwd_dkv(proj, ccol, crow, lse, delta, do, tq, host=()):
    Tp = proj.shape[0]
    nq = Tp // tq
    pair = lambda z: z.reshape(_FOX_PAIRS, 2, Tp).transpose(0, 2, 1)
    stats = jnp.concatenate([pair(ccol), pair(lse), pair(delta)], axis=-1).reshape(_FOX_PAIRS, nq, tq, 6)

    def body(q_ref, do_ref, st_ref, k_ref, v_ref, cr_ref, dk_ref, dv_ref, dc_ref):
        ki = pl.program_id(1)
        low = _low_lanes()
        kk, vv = k_ref[...].astype(BF16), v_ref[...].astype(BF16)

        def step(qb_i, carry, diag):
            rows = pl.ds(pl.multiple_of(qb_i * tq, tq), tq)
            qh = _split_pair(q_ref[rows, :] * SCALE, low)
            doh = _split_pair(do_ref[rows, :], low)
            st = st_ref[0, qb_i]
            dk, dv = carry[0], carry[1]
            dcs = []
            for e in range(2):
                s = _fox_scores(qh[e], kk, st[:, e:e + 1], cr_ref[e, 0], diag, tq)
                p = jnp.exp(s - st[:, 2 + e:3 + e])
                dp = lax.dot_general(doh[e], vv, (((1,), (1,)), ((), ())), preferred_element_type=F32)
                ds = p * (dp - st[:, 4 + e:5 + e])
                dv = dv + jnp.dot(p.T.astype(BF16), doh[e], preferred_element_type=F32)
                dk = dk + jnp.dot(ds.T.astype(BF16), qh[e], preferred_element_type=F32)
                dcs.append(carry[2 + e] - jnp.sum(ds, axis=0, keepdims=True))
            return (dk, dv, *dcs)

        init = (jnp.zeros((tq, LANE), F32), jnp.zeros((tq, LANE), F32), jnp.zeros((1, tq), F32), jnp.zeros((1, tq), F32))
        dk, dv, dc0, dc1 = lax.fori_loop(ki + 1, nq, functools.partial(step, diag=False), step(ki, init, True))
        dk_ref[...] = dk
        dv_ref[...] = dv
        dc_ref[0, 0] = dc0
        dc_ref[1, 0] = dc1

    rspec = pl.BlockSpec((2, 1, 1, tq), lambda h, i: (h, i, 0, 0))
    pspec = pl.BlockSpec((tq, LANE), lambda h, i: (i, h))
    wide = jax.ShapeDtypeStruct((Tp, FOX_HEADS * HEAD_DIM), F32)
    return _pcall(
        body, "fox_bwd_dkv", (_FOX_PAIRS, nq),
        [pl.BlockSpec((Tp, LANE), lambda h, i: (0, _QF_COL + h)), pl.BlockSpec((Tp, LANE), lambda h, i: (0, h)),
         pl.BlockSpec((1, nq, tq, 6), lambda h, i: (h, 0, 0, 0)),
         pl.BlockSpec((tq, LANE), lambda h, i: (i, _KF_COL + h)), pl.BlockSpec((tq, LANE), lambda h, i: (i, _VF_COL + h)),
         rspec],
        [pspec, pspec, rspec], [wide, wide, jax.ShapeDtypeStruct((FOX_HEADS, nq, 1, tq), F32)],
        [], ("parallel", "arbitrary"), (proj, do, stats, proj, proj, crow), host)


def _adam(gparts, w, m, v, name, layer=0):
    P, R, C = gparts.shape
    tr = _tile(R, 256, 8)
    first = layer * (R // tr)

    def body(g_ref, w_ref, m_ref, v_ref, go_ref, d_ref, mo_ref, vo_ref):
        g = g_ref[0].astype(F32)
        for p in range(1, P):
            g = g + g_ref[p].astype(F32)
        m2 = ADAM_B1 * m_ref[...] + (1.0 - ADAM_B1) * g
        v2 = ADAM_B2 * v_ref[...] + (1.0 - ADAM_B2) * (g * g)
        m_hat = m2 / (1.0 - ADAM_B1 ** ADAM_STEP)
        v_hat = v2 / (1.0 - ADAM_B2 ** ADAM_STEP)
        go_ref[...] = g
        d_ref[...] = -ADAM_LR * (m_hat / (jnp.sqrt(v_hat) + ADAM_EPS) + ADAM_WD * w_ref[...])
        mo_ref[...] = m2
        vo_ref[...] = v2

    spec = pl.BlockSpec((tr, C), lambda i: (i, 0))
    wspec = pl.BlockSpec((tr, C), lambda i: (first + i, 0))
    shp = jax.ShapeDtypeStruct((R, C), F32)
    return pl.pallas_call(
        body, name=name, grid=(R // tr,),
        in_specs=[pl.BlockSpec((P, tr, C), lambda i: (0, i, 0)), wspec, wspec, wspec],
        out_specs=[spec] * 4, out_shape=[shp] * 4, compiler_params=_params(("parallel",)))(gparts, w, m, v)


def _sum_parts(gparts, name):
    P, R, C = gparts.shape
    tr = _tile(R, 256, 8)

    def body(g_ref, o_ref):
        g = g_ref[0]
        for p in range(1, P):
            g = g + g_ref[p]
        o_ref[...] = g

    return pl.pallas_call(
        body, name=name, grid=(R // tr,), in_specs=[pl.BlockSpec((P, tr, C), lambda i: (0, i, 0))],
        out_specs=pl.BlockSpec((tr, C), lambda i: (i, 0)), out_shape=jax.ShapeDtypeStruct((R, C), F32),
        compiler_params=_params(("parallel",)))(gparts)


def _pack(pieces, width, row_mult):
    flat = jnp.concatenate([p.reshape(-1) for p in pieces])
    n = flat.shape[0]
    rows = -(-n // width)
    rows = -(-rows // row_mult) * row_mult
    return jnp.pad(flat, (0, rows * width - n)).reshape(rows, width)


def _unpack(flat2d, shapes, lead=None):
    out, off = [], 0
    if lead is None:
        flat = flat2d.reshape(-1)
        for s in shapes:
            n = int(np.prod(s))
            out.append(flat[off:off + n].reshape(s))
            off += n
    else:
        flat = flat2d.reshape(lead, -1)
        for s in shapes:
            n = int(np.prod(s))
            out.append(flat[:, off:off + n].reshape((lead,) + tuple(s)))
            off += n
    return out


def _col_pieces(off, width, shard_w, fetch):
    out, c, end = [], off, off + width
    while c < end:
        d = c // shard_w
        hi = min(end, (d + 1) * shard_w)
        out.append(fetch(d, c - d * shard_w, hi - d * shard_w))
        c = hi
    return out


def _work_cols_of_orig(lo, hi, fetch):
    out = []
    for name, ow in _ORIG:
        o = _ORIG_OFF[name][0]
        a, b = max(lo, o), min(hi, o + ow)
        if a < b:
            w0 = _WORK_OFF[name][0]
            out.append(fetch(w0 + a - o, w0 + b - o))
    return out


def _t5_bucket_np(dist):
    max_exact = REL_BUCKETS // 2
    d = np.maximum(dist, 0)
    scaled = (np.log(np.maximum(d, 1).astype(np.float32) / np.float32(max_exact))
              / np.float32(math.log(REL_MAX_DIST / max_exact))).astype(np.float32)
    large = np.minimum(max_exact + (scaled * np.float32(REL_BUCKETS - max_exact)).astype(np.int32), REL_BUCKETS - 1)
    return np.where(d < max_exact, d, large)


def _bucket_onehot():
    q_idx = np.arange(BLOCK)[:, None]
    k_idx = np.arange(2 * BLOCK)[None, :]
    bucket = _t5_bucket_np(q_idx + BLOCK - k_idx).reshape(-1)
    oh = np.zeros((LANE, BLOCK * 2 * BLOCK), np.float32)
    oh[bucket, np.arange(bucket.shape[0])] = 1.0
    return oh


def kernel(x, meta_tokens, rel_bias_table, norm_mix, w_in, swa_sinks, fox_forget_bias, conv_w, conv_b, lru_w_r, lru_b_r, lru_w_i, lru_b_i, lru_lambda, w_branch, w_out, norm_ffn, w_ffn_in, w_ffn_out, norm_final, loss_target, m_meta_tokens, m_rel_bias_table, m_norm_mix, m_w_in, m_swa_sinks, m_fox_forget_bias, m_conv_w, m_conv_b, m_lru_w_r, m_lru_b_r, m_lru_w_i, m_lru_b_i, m_lru_lambda, m_w_branch, m_w_out, m_norm_ffn, m_w_ffn_in, m_w_ffn_out, m_norm_final, v_meta_tokens, v_rel_bias_table, v_norm_mix, v_w_in, v_swa_sinks, v_fox_forget_bias, v_conv_w, v_conv_b, v_lru_w_r, v_lru_b_r, v_lru_w_i, v_lru_b_i, v_lru_lambda, v_w_branch, v_w_out, v_norm_ffn, v_w_ffn_in, v_w_ffn_out, v_norm_final):
    S = x.shape[1]
    T = N_META + S
    n_pad = (-T) % BLOCK
    Tp = T + n_pad
    first = n_pad + N_META
    TR = _tile(Tp, 384)
    TQ = _tile(Tp, 384)
    TQF = _tile(Tp, FOX_TILE)
    me =4 * lax.axis_index("x") + 2 * lax.axis_index("y") + lax.axis_index("c")

    big_names = ("w_in", "w_ffn_in", "w_ffn_out", "w_branch", "w_out")
    big_w = dict(w_in=w_in, w_ffn_in=w_ffn_in, w_ffn_out=w_ffn_out, w_branch=w_branch, w_out=w_out)
    big_m = dict(w_in=m_w_in, w_ffn_in=m_w_ffn_in, w_ffn_out=m_w_ffn_out, w_branch=m_w_branch, w_out=m_w_out)
    big_v = dict(w_in=v_w_in, w_ffn_in=v_w_ffn_in, w_ffn_out=v_w_ffn_out, w_branch=v_w_branch, w_out=v_w_out)
    rows2d = lambda w: w.reshape(-1, w.shape[-1])
    in_shard_w = IN_COLS // N_DEV
    cat_axis = dict(w_in=-1, w_ffn_in=-1, w_ffn_out=0, w_branch=-1, w_out=0)

    def w_send(n, l):
        return rows2d(big_w[n][l]).astype(BF16)

    def w_build(n, g):
        shard = lambda d: g[d].reshape(big_w[n].shape[1:])
        if cat_axis[n] == 0:
            return g.reshape(-1, g.shape[-1])
        if n != "w_in":
            return jnp.concatenate([shard(d) for d in range(N_DEV)], axis=cat_axis[n])
        parts = []
        for name, width in _WORK:
            off, ow = _ORIG_OFF[name]
            parts += _col_pieces(off, ow, in_shard_w, lambda d, lo, hi: shard(d)[:, lo:hi])
            if width > ow:
                parts.append(jnp.zeros((D_MODEL, width - ow), BF16))
        return jnp.concatenate(parts, axis=-1)

    W_in, W_ffn_in, W_ffn_out, W_branch, W_out = ([None] * DEPTH for _ in range(5))
    W_in[0] = w_build("w_in", _exchange(w_send("w_in", 0), True, "gather_w_in"))

    small_sh = _exchange(_pack([meta_tokens, conv_w], D_MODEL, 8), True, "gather_small")
    g_meta, g_conv_w = _unpack(small_sh, [meta_tokens.shape, conv_w.shape], lead=N_DEV)
    meta_full = g_meta.transpose(1, 0, 2).reshape(N_META, D_MODEL)
    conv_w_full = g_conv_w.transpose(1, 2, 0, 3).reshape(DEPTH, CONV_WIDTH, LRU_WIDTH)

    onehot = jnp.asarray(_bucket_onehot())
    table_t = jnp.pad(rel_bias_table.T, ((0, 0), (0, LANE - REL_BUCKETS)))
    bias = _mm(table_t, onehot, "nn", "swa_bias", exact=True, tn_cap=4096).reshape(SWA_Q_HEADS, BLOCK, 2 * BLOCK)

    def dense_blocks(w):
        rows = []
        for b in range(LRU_BLOCKS):
            rows.append(jnp.pad(w[b], ((0, 0), (b * LRU_BLOCK_DIM, (LRU_BLOCKS - 1 - b) * LRU_BLOCK_DIM))))
        return jnp.concatenate(rows, axis=0)

    h = jnp.concatenate([jnp.zeros((n_pad, D_MODEL), F32), meta_full, x[0]], axis=0)
    saved = []

    for l in range(DEPTH):
        sv = {"h": h}
        g_mix = norm_mix[l][None, :]

        def norm_fwd(i, rows, consts):
            xx = rows[0]
            return [xx * lax.rsqrt(jnp.mean(xx * xx, axis=1, keepdims=True) + EPS) * consts[0]], []

        (u,) = _rowwise(norm_fwd, "norm_fwd", Tp, TR, [(h, None, 0)], [g_mix], [(D_MODEL, BF16)])
        proj, got = _mm(u, W_in[l], "nn", "proj", tn_cap=384,
                        host=[(w_send(n, l), True) for n in ("w_branch", "w_out", "w_ffn_out")])
        W_branch[l], W_out[l], W_ffn_out[l] = (w_build(n, g) for n, g in zip(("w_branch", "w_out", "w_ffn_out"), got))
        half = D_MODEL // 2
        next_w_in = [] if l + 1 == DEPTH else [rows2d(w_in[l + 1])[k * half:(k + 1) * half].astype(BF16) for k in (0, 1)]
        next_got = []
        sv["u"], sv["proj"] = u, proj

        def col(name):
            off, w = _WORK_OFF[name]
            return (proj, w, off // w)

        sinks_l = jnp.broadcast_to(swa_sinks[l][:, None, None], (SWA_Q_HEADS, 1, LANE))
        (o_a, lse_a), got = _swa_fwd(proj, bias, sinks_l, n_pad, host=[(s, True) for s in next_w_in[:1]])
        next_got += got
        sv.update(sinks=sinks_l, lse_a=lse_a, o_a=o_a)

        fbias = jnp.pad(fox_forget_bias[l], (0, LANE - FOX_HEADS))[None, :]

        def logf_fwd(i, rows, consts):
            return [_log_sigmoid(rows[0] + consts[0])], []

        (logf,) = _rowwise(logf_fwd, "logf_fwd", Tp, TR, [col("fl")], [fbias], [(LANE, F32)])
        cum = _scan(None, logf, "cumsum")
        cum_h = cum[:, :FOX_HEADS].T
        ccol = cum_h[:, :, None]
        crow = jnp.where(jnp.arange(Tp)[None, :] < n_pad, -NEG_INF, cum_h).reshape(FOX_HEADS, Tp // TQF, 1, TQF)
        (o_f, lse_f), got = _fox_fwd(proj, ccol, crow, TQF, host=[(w_send("w_ffn_in", l), True)],
                                     prefetch=TQF <= 384)
        W_ffn_in[l] = w_build("w_ffn_in", got[0])
        sv.update(fbias=fbias, ccol=ccol, crow=crow, lse_f=lse_f, o_f=o_f)

        cw = conv_w_full[l]
        cb = conv_b[l][None, :]

        def conv_taps(i, cur, prev8):
            return [_rows_from_prev(cur, prev8, CONV_WIDTH - 1 - t, i == 0) for t in range(CONV_WIDTH - 1)] + [cur]

        def conv_fwd(i, rows, consts):
            w, b = consts
            taps = conv_taps(i, rows[0], rows[1])
            acc = taps[0] * w[0:1, :]
            for t in range(1, CONV_WIDTH):
                acc = acc + taps[t] * w[t:t + 1, :]
            return [acc + b], []

        xc_blocks = [col("xc"), col("xc") + ("prev",)]
        (xconv,) = _rowwise(conv_fwd, "conv_fwd", Tp, TR, xc_blocks, [cw, cb], [(LRU_WIDTH, F32)])
        w_ri = jnp.concatenate([dense_blocks(lru_w_r[l]), dense_blocks(lru_w_i[l])], axis=1)
        pre = _mm(xconv, w_ri, "nn", "lru_gates")
        lru_consts = [lru_b_r[l][None, :], lru_b_i[l][None, :], lru_lambda[l][None, :]]

        def lru_fwd(i, rows, consts):
            pr, xv = rows
            b_r, b_i, lam = consts
            r = _sigmoid(pr[:, :LRU_WIDTH] + b_r)
            gi = _sigmoid(pr[:, LRU_WIDTH:] + b_i)
            log_a = LRU_C * r * _log_sigmoid(lam)
            valid = (_row_ids(i, TR) >= n_pad).astype(F32)
            inp = jnp.sqrt(_neg_expm1(2.0 * log_a)) * (gi * xv) * valid
            return [jnp.exp(log_a), inp], []

        a_dec, inp = _rowwise(lru_fwd, "lru_fwd", Tp, TR, [(pre, None, 0), (xconv, None, 0)], lru_consts,
                              [(LRU_WIDTH, F32), (LRU_WIDTH, F32)])
        hs = _scan(a_dec, inp, "lru_scan")

        def oc_fwd(i, rows, consts):
            return [rows[0] * _gelu(rows[1])], []

        (o_c,) = _rowwise(oc_fwd, "oc_fwd", Tp, TR, [(hs, None, 0), col("yc")], [], [(LRU_WIDTH, BF16)])
        sv.update(cw=cw, xconv=xconv, w_ri=w_ri, pre=pre, lru_consts=lru_consts, a_dec=a_dec, hs=hs, o_c=o_c)

        ba = _mm(o_a, W_branch[l][0], "nn", "branch")
        bf = _mm(o_f, W_branch[l][1], "nn", "branch")
        bc = _mm(o_c, W_branch[l][2], "nn", "branch")

        def merge_fwd(i, rows, consts):
            g, b0, b1, b2 = rows
            valid = (_row_ids(i, BLOCK) >= n_pad).astype(F32)
            mg = (_sigmoid(g[:, :D_MODEL]) * b0 + _sigmoid(g[:, D_MODEL:2 * D_MODEL]) * b1
                  + _sigmoid(g[:, 2 * D_MODEL:]) * b2)
            return [mg * valid], []

        (merged,) = _rowwise(merge_fwd, "merge_fwd", Tp, BLOCK,
                             [col("gates"), (ba, None, 0), (bf, None, 0), (bc, None, 0)], [], [(D_MODEL, BF16)])
        h1 = _mm(merged, W_out[l], "nn", "out_proj", res=h)
        sv.update(ba=ba, bf=bf, bc=bc, merged=merged, h1=h1)

        g_ffn = norm_ffn[l][None, :]
        (u2,) = _rowwise(norm_fwd, "norm_fwd", Tp, TR, [(h1, None, 0)], [g_ffn], [(D_MODEL, BF16)])
        ff, got = _mm(u2, W_ffn_in[l], "nn", "ffn_in", host=[(s, True) for s in next_w_in[1:]])
        next_got += got
        if l + 1 < DEPTH:
            W_in[l + 1] = w_build("w_in", jnp.concatenate(next_got, axis=1))

        def act_fwd(i, rows, consts):
            gate, up = rows
            return [gate * _sigmoid(gate) * up], []

        (act,) = _rowwise(act_fwd, "act_fwd", Tp, TR, [(ff, D_FF, 0), (ff, D_FF, 1)], [], [(D_FF, BF16)])
        h = _mm(act, W_ffn_out[l], "nn", "ffn_out", res=h1)
        sv.update(u2=u2, ff=ff, act=act)
        saved.append(sv)

    tgt = jnp.concatenate([jnp.zeros((first, D_MODEL), F32), loss_target[0]], axis=0)
    g_fin = norm_final[None, :]

    def head(i, rows, consts):
        xx, tg = rows
        g = consts[0]
        valid = (_row_ids(i, TR) >= first).astype(F32)
        rstd = lax.rsqrt(jnp.mean(xx * xx, axis=1, keepdims=True) + EPS)
        xhat = xx * rstd
        err = (xhat * g - tg) * valid
        loss_rows = 0.5 * jnp.mean(err * err, axis=1, keepdims=True)
        dy = err * (1.0 / D_MODEL)
        dxhat = dy * g
        dx = rstd * (dxhat - xhat * jnp.mean(dxhat * xhat, axis=1, keepdims=True))
        return [dx], [jnp.broadcast_to(_colsum(loss_rows), (1, LANE)), _colsum(dy * xhat)]

    dh, loss_part, d_norm_final = _rowwise(head, "loss_head", Tp, TR, [(h, None, 0), (tgt, None, 0)], [g_fin],
                                           [(D_MODEL, F32)], [LANE, D_MODEL])
    loss = lax.psum(loss_part[0, 0], ("x", "y", "c"))

    def norm_bwd_call(xin, g, du, dres):
        def norm_bwd(i, rows, consts):
            xx, dd, rr = rows
            gg = consts[0]
            valid = (_row_ids(i, TR) >= n_pad).astype(F32)
            rstd = lax.rsqrt(jnp.mean(xx * xx, axis=1, keepdims=True) + EPS)
            xhat = xx * rstd
            dxhat = dd * gg
            dx = rstd * (dxhat - xhat * jnp.mean(dxhat * xhat, axis=1, keepdims=True))
            return [rr + dx * valid], [_colsum(dd * xhat)]

        return _rowwise(norm_bwd, "norm_bwd", Tp, TR, [(xin, None, 0), (du, None, 0), (dres, None, 0)], [g],
                        [(D_MODEL, F32)], [D_MODEL])

    grads = {k: [None] * DEPTH for k in ("norm_mix", "w_in", "swa_sinks", "fox_forget_bias", "conv_w", "conv_b",
                                         "lru_w_r", "lru_b_r", "lru_w_i", "lru_b_i", "lru_lambda", "w_branch",
                                         "w_out", "norm_ffn", "w_ffn_in", "w_ffn_out")}
    d_bias_total = None

    def g_send(n, g):
        if n == "w_in":
            pieces = [jnp.concatenate(_work_cols_of_orig(d * in_shard_w, (d + 1) * in_shard_w,
                                                         lambda lo, hi: g[:, lo:hi]), axis=-1) for d in range(N_DEV)]
        elif cat_axis[n] == 0:
            return g.reshape(N_DEV, -1, g.shape[-1])
        else:
            ax = cat_axis[n] % g.ndim
            w = big_w[n].shape[1 + ax]
            pieces = [lax.slice_in_dim(g, d * w, (d + 1) * w, axis=ax) for d in range(N_DEV)]
        return jnp.stack([rows2d(p) for p in pieces])

    recv = {n: [None] * DEPTH for n in big_names}
    for l in reversed(range(DEPTH)):
        sv = saved[l]
        proj = sv["proj"]

        def col(name):
            off, w = _WORK_OFF[name]
            return (proj, w, off // w)

        dh2 = dh
        d_act = _mm(dh2, W_ffn_out[l], "nt", "d_act")
        grads["w_ffn_out"][l] = _mm(sv["act"], dh2, "tn", "dw_ffn_out", out_dtype=BF16)

        def act_bwd(i, rows, consts):
            gate, up, da = rows
            sg = _sigmoid(gate)
            d_gate = da * up * (sg * (1.0 + gate * (1.0 - sg)))
            d_up = da * (gate * sg)
            return [jnp.concatenate([d_gate, d_up], axis=1)], []

        (dff,) = _rowwise(act_bwd, "act_bwd", Tp, BLOCK, [(sv["ff"], D_FF, 0), (sv["ff"], D_FF, 1), (d_act, None, 0)],
                          [], [(2 * D_FF, BF16)])
        grads["w_ffn_in"][l] = _mm(sv["u2"], dff, "tn", "dw_ffn_in", tn_cap=1408, out_dtype=BF16)
        du2 = _mm(dff, W_ffn_in[l], "nt", "du2")
        dh1, dg = norm_bwd_call(sv["h1"], norm_ffn[l][None, :], du2, dh2)
        grads["norm_ffn"][l] = dg[0]

        dmerged = _mm(dh1, W_out[l], "nt", "d_merged")
        grads["w_out"][l] = _mm(sv["merged"], dh1, "tn", "dw_out", out_dtype=BF16)

        def merge_bwd(i, rows, consts):
            g, b0, b1, b2, dm = rows
            dm = dm * (_row_ids(i, BLOCK) >= n_pad).astype(F32)
            outs, dgs = [], []
            for k, bk in enumerate((b0, b1, b2)):
                sg = _sigmoid(g[:, k * D_MODEL:(k + 1) * D_MODEL])
                outs.append(dm * sg)
                dgs.append(dm * bk * sg * (1.0 - sg))
            return outs + [jnp.concatenate(dgs, axis=1)], []

        d_ba, d_bf, d_bc, d_gates = _rowwise(
            merge_bwd, "merge_bwd", Tp, BLOCK,
            [col("gates"), (sv["ba"], None, 0), (sv["bf"], None, 0), (sv["bc"], None, 0), (dmerged, None, 0)], [],
            [(D_MODEL, BF16)] * 3 + [(3 * D_MODEL, BF16)])
        grads["w_branch"][l] = jnp.stack([_mm(sv["o_a"], d_ba, "tn", "dw_branch", out_dtype=BF16),
                                          _mm(sv["o_f"], d_bf, "tn", "dw_branch", out_dtype=BF16),
                                          _mm(sv["o_c"], d_bc, "tn", "dw_branch", out_dtype=BF16)])
        do_a = _mm(d_ba, W_branch[l][0], "nt", "d_branch")
        do_f = _mm(d_bf, W_branch[l][1], "nt", "d_branch")
        do_c = _mm(d_bc, W_branch[l][2], "nt", "d_branch")

        def oc_bwd(i, rows, consts):
            d, hv, yv = rows
            return [d * _gelu(yv), d * hv * _gelu_grad(yv)], []

        d_hs, d_yc = _rowwise(oc_bwd, "oc_bwd", Tp, TR, [(do_c, None, 0), (sv["hs"], None, 0), col("yc")], [],
                              [(LRU_WIDTH, F32)] * 2)
        d_state = _rev_scan(sv["a_dec"], d_hs, "lru_scan")

        def lru_bwd(i, rows, consts):
            dH, hv, hv_prev8, pr, xv = rows
            hp = _rows_from_prev(hv, hv_prev8, 1, i == 0)
            b_r, b_i, lam = consts
            valid = (_row_ids(i, TR) >= n_pad).astype(F32)
            r = _sigmoid(pr[:, :LRU_WIDTH] + b_r)
            gi = _sigmoid(pr[:, LRU_WIDTH:] + b_i)
            lsl = _log_sigmoid(lam)
            log_a = LRU_C * r * lsl
            a = jnp.exp(log_a)
            one_m_e = _neg_expm1(2.0 * log_a)
            mult = jnp.sqrt(one_m_e)
            d_inp = dH * valid
            d_mult = d_inp * gi * xv
            d_gi = d_inp * mult * xv
            d_x = d_inp * mult * gi
            d_log_a = dH * hp * a - d_mult * (1.0 - one_m_e) / mult
            d_pre_r = d_log_a * (LRU_C * lsl) * r * (1.0 - r)
            d_pre_i = d_gi * gi * (1.0 - gi)
            d_lam = _colsum(d_log_a * (LRU_C * r)) * _sigmoid(-lam)
            return [jnp.concatenate([d_pre_r, d_pre_i], axis=1), d_x], [_colsum(d_pre_r), _colsum(d_pre_i), d_lam]

        d_pre, d_xdir, d_b_r, d_b_i, d_lam = _rowwise(
            lru_bwd, "lru_bwd", Tp, TR, [(d_state, None, 0), (sv["hs"], None, 0), (sv["hs"], None, 0, "prev"),
                                         (sv["pre"], None, 0), (sv["xconv"], None, 0)], sv["lru_consts"],
            [(2 * LRU_WIDTH, BF16), (LRU_WIDTH, F32)], [LRU_WIDTH] * 3)
        grads["lru_b_r"][l], grads["lru_b_i"][l], grads["lru_lambda"][l] = d_b_r[0], d_b_i[0], d_lam[0]
        d_w_ri = _mm(sv["xconv"], d_pre, "tn", "dw_lru_gates")

        def diag_blocks(wd):
            w4 = wd.reshape(LRU_BLOCKS, LRU_BLOCK_DIM, LRU_BLOCKS, LRU_BLOCK_DIM)
            return jnp.stack([w4[b, :, b, :] for b in range(LRU_BLOCKS)])

        grads["lru_w_r"][l] = diag_blocks(d_w_ri[:, :LRU_WIDTH])
        grads["lru_w_i"][l] = diag_blocks(d_w_ri[:, LRU_WIDTH:])
        d_conv = _mm(d_pre, sv["w_ri"], "nt", "d_xconv", res=d_xdir)

        def conv_bwd(i, rows, consts):
            w = consts[0]
            dc, dc_next8, xcur, x_prev8 = rows
            d = [dc] + [_rows_from_next(dc, dc_next8, s, i == Tp // TR - 1) for s in range(1, CONV_WIDTH)]
            xsh = conv_taps(i, xcur, x_prev8)
            dxc = d[0] * w[CONV_WIDTH - 1:CONV_WIDTH, :]
            for s in range(1, CONV_WIDTH):
                dxc = dxc + d[s] * w[CONV_WIDTH - 1 - s:CONV_WIDTH - s, :]
            return [dxc], [_colsum(xsh[t] * d[0]) for t in range(CONV_WIDTH)] + [_colsum(d[0])]

        res = _rowwise(conv_bwd, "conv_bwd", Tp, TR,
                       [(d_conv, None, 0), (d_conv, None, 0, "next"), col("xc"), col("xc") + ("prev",)], [sv["cw"]],
                       [(LRU_WIDTH, F32)], [LRU_WIDTH] * (CONV_WIDTH + 1))
        d_xc = res[0]
        grads["conv_w"][l] = jnp.concatenate(res[1:1 + CONV_WIDTH], axis=0)
        grads["conv_b"][l] = res[1 + CONV_WIDTH][0]

        (dqf, delta_f, dccol), got = _fox_bwd_dq(
            proj, sv["ccol"], sv["crow"], sv["o_f"], sv["lse_f"], do_f, TQF,
            host=[(g_send("w_ffn_in", grads["w_ffn_in"][l]), False)])
        recv["w_ffn_in"][l] = got[0]
        ride = [("w_out", l), ("w_branch", l)] + ([("w_in", l + 1)] if l + 1 < DEPTH else [])
        (dkf, dvf, dcrow), got = _fox_bwd_dkv(
            proj, sv["ccol"], sv["crow"], sv["lse_f"], delta_f, do_f, TQF,
            host=[(g_send(n, grads[n][ll]), False) for n, ll in ride])
        for (n, ll), r in zip(ride, got):
            recv[n][ll] = r
        lanes = lambda z: jnp.pad(z.reshape(FOX_HEADS, Tp).T, ((0, 0), (0, LANE - FOX_HEADS)))
        dlogf = _rev_scan(None, lanes(dcrow), "cumsum_bwd", g2=lanes(dccol))

        def logf_bwd(i, rows, consts):
            dl, fl = rows
            valid = (_row_ids(i, TR) >= n_pad).astype(F32)
            lane_ok = (lax.broadcasted_iota(jnp.int32, (1, LANE), 1) < FOX_HEADS).astype(F32)
            dfl = dl * _sigmoid(-(fl + consts[0])) * valid * lane_ok
            return [dfl], [_colsum(dfl)]

        d_fl, d_fb = _rowwise(logf_bwd, "logf_bwd", Tp, TR, [(dlogf, None, 0), col("fl")], [sv["fbias"]],
                              [(LANE, F32)], [LANE])
        grads["fox_forget_bias"][l] = d_fb[0, :FOX_HEADS]

        (dqa, dk_strip, dv_strip, d_bias_l, d_sink), got = _swa_bwd(
            proj, bias, sv["sinks"], sv["o_a"], sv["lse_a"], do_a, n_pad,
            host=[(g_send("w_ffn_out", grads["w_ffn_out"][l]), False)])
        recv["w_ffn_out"][l] = got[0]
        grads["swa_sinks"][l] = d_sink[:, 0, 0]
        d_bias_total = d_bias_l if d_bias_total is None else d_bias_total + d_bias_l

        def kv_parts(strip):
            own = strip[:, BLOCK:, :].reshape(Tp, KV_W)
            nxt = jnp.pad(strip[1:, :BLOCK, :], ((0, 1), (TQ - BLOCK, 0), (0, 0))).reshape(Tp, KV_W)
            return [(own, None, 0), (nxt, None, 0)]

        def assemble(i, rows, consts):
            dg, dqa_, dqf_, dkf_, dvf_, dxc_, dyc_, dk0, dk1, dv0, dv1, dfl_ = rows
            parts = [dg, dqa_, dqf_, dkf_, dvf_, dxc_, dyc_, dk0 + dk1, dv0 + dv1, dfl_]
            return [jnp.concatenate([p.astype(BF16) for p in parts], axis=1)], []

        (dproj,) = _rowwise(
            assemble, "assemble_dproj", Tp, BLOCK,
            [(d_gates, None, 0), (dqa, None, 0), (dqf, None, 0), (dkf, None, 0), (dvf, None, 0), (d_xc, None, 0),
             (d_yc, None, 0)] + kv_parts(dk_strip) + kv_parts(dv_strip) + [(d_fl, None, 0)], [],
            [(WORK_COLS, BF16)])
        grads["w_in"][l] = _mm(sv["u"], dproj, "tn", "dw_in", tn_cap=2176, out_dtype=BF16)
        if l == 0:
            du, got = _mm(dproj, W_in[l], "nt", "du", tk_cap=2176,
                          host=[(g_send("w_in", grads["w_in"][0]), False)])
            recv["w_in"][0] = got[0]
        else:
            du = _mm(dproj, W_in[l], "nt", "du", tk_cap=2176)
        dh, dg = norm_bwd_call(sv["h"], norm_mix[l][None, :], du, dh1)
        grads["norm_mix"][l] = dg[0]

    d_table = _mm(d_bias_total.reshape(SWA_Q_HEADS, BLOCK * 2 * BLOCK), onehot, "nt", "d_rel_table", exact=True,
                  tk_cap=4096)
    g_rel = d_table[:, :REL_BUCKETS].T
    g_meta_full = dh[n_pad:first]
    grad_x = dh[first:][None]

    stack = lambda k: jnp.stack(grads[k])
    big_res = {}
    for n in big_names:
        per_layer = [_adam(recv[n][l], rows2d(big_w[n]), rows2d(big_m[n]), rows2d(big_v[n]), "adam_" + n, layer=l)
                     for l in range(DEPTH)]
        big_res[n] = [jnp.stack([per_layer[l][k] for l in range(DEPTH)]).reshape(big_w[n].shape) for k in range(4)]

    rep_names = ("rel_bias_table", "norm_mix", "swa_sinks", "fox_forget_bias", "conv_b", "lru_w_r", "lru_b_r",
                 "lru_w_i", "lru_b_i", "lru_lambda", "norm_ffn", "norm_final")
    rep_w = dict(rel_bias_table=rel_bias_table, norm_mix=norm_mix, swa_sinks=swa_sinks,
                 fox_forget_bias=fox_forget_bias, conv_b=conv_b, lru_w_r=lru_w_r, lru_b_r=lru_b_r, lru_w_i=lru_w_i,
                 lru_b_i=lru_b_i, lru_lambda=lru_lambda, norm_ffn=norm_ffn, norm_final=norm_final)
    rep_m = dict(rel_bias_table=m_rel_bias_table, norm_mix=m_norm_mix, swa_sinks=m_swa_sinks,
                 fox_forget_bias=m_fox_forget_bias, conv_b=m_conv_b, lru_w_r=m_lru_w_r, lru_b_r=m_lru_b_r,
                 lru_w_i=m_lru_w_i, lru_b_i=m_lru_b_i, lru_lambda=m_lru_lambda, norm_ffn=m_norm_ffn,
                 norm_final=m_norm_final)
    rep_v = dict(rel_bias_table=v_rel_bias_table, norm_mix=v_norm_mix, swa_sinks=v_swa_sinks,
                 fox_forget_bias=v_fox_forget_bias, conv_b=v_conv_b, lru_w_r=v_lru_w_r, lru_b_r=v_lru_b_r,
                 lru_w_i=v_lru_w_i, lru_b_i=v_lru_b_i, lru_lambda=v_lru_lambda, norm_ffn=v_norm_ffn,
                 norm_final=v_norm_final)
    rep_g = {n: (g_rel if n == "rel_bias_table" else d_norm_final[0] if n == "norm_final" else stack(n))
             for n in rep_names}
    small_g = [rep_g[n] for n in rep_names] + [g_meta_full, stack("conv_w")]
    small_shapes = [rep_w[n].shape for n in rep_names] + [(N_META, D_MODEL), (DEPTH, CONV_WIDTH, LRU_WIDTH)]
    gs_all = _exchange(_pack(small_g, D_MODEL, 8), True, "gather_small_grads")
    gs_sum = _unpack(_sum_parts(gs_all, "sum_small_grads"), small_shapes)
    gsum = dict(zip(rep_names, gs_sum[:len(rep_names)]))
    g_meta_sh = lax.dynamic_slice_in_dim(gs_sum[-2], me * (D_MODEL // N_DEV), D_MODEL // N_DEV, axis=1)
    g_convw_sh = lax.dynamic_slice_in_dim(gs_sum[-1], me * (LRU_WIDTH // N_DEV), LRU_WIDTH // N_DEV, axis=2)
    sm_names = rep_names + ("meta_tokens", "conv_w")
    sm_w = [rep_w[n] for n in rep_names] + [meta_tokens, conv_w]
    sm_m = [rep_m[n] for n in rep_names] + [m_meta_tokens, m_conv_w]
    sm_v = [rep_v[n] for n in rep_names] + [v_meta_tokens, v_conv_w]
    sm_g = [gsum[n] for n in rep_names] + [g_meta_sh, g_convw_sh]
    sm_shapes = [w.shape for w in sm_w]
    sm_out = [_unpack(o, sm_shapes) for o in _adam(_pack(sm_g, D_MODEL, 8)[None], _pack(sm_w, D_MODEL, 8),
                                                   _pack(sm_m, D_MODEL, 8), _pack(sm_v, D_MODEL, 8), "adam_small")]
    sm_res = {n: [sm_out[k][j] for k in range(4)] for j, n in enumerate(sm_names)}

    order = ("meta_tokens", "rel_bias_table", "norm_mix", "w_in", "swa_sinks", "fox_forget_bias", "conv_w", "conv_b",
             "lru_w_r", "lru_b_r", "lru_w_i", "lru_b_i", "lru_lambda", "w_branch", "w_out", "norm_ffn", "w_ffn_in",
             "w_ffn_out", "norm_final")
    allres = {**big_res, **sm_res}
    outs = [loss, grad_x]
    for k in range(4):
        outs += [allres[n][k] for n in order]
    return tuple(outs)
```

```python
import functools
import math

import numpy as np
import jax
import jax.numpy as jnp
from jax import lax
from jax.experimental import pallas as pl
from jax.experimental.pallas import tpu as pltpu

F32 = jnp.float32
BF16 = jnp.bfloat16

N_DEV = 8
D_MODEL = 1024
DEPTH = 4
HEAD_DIM = 64
N_META = 16
BLOCK = 128
NEG_INF = -1e30
SWA_Q_HEADS = 8
SWA_KV_HEADS = 2
SWA_GROUP = SWA_Q_HEADS // SWA_KV_HEADS
FOX_HEADS = 8
LRU_WIDTH = D_MODEL // 2
LRU_BLOCKS = 8
LRU_BLOCK_DIM = LRU_WIDTH // LRU_BLOCKS
CONV_WIDTH = 4
LRU_C = 8.0
REL_BUCKETS = 32
REL_MAX_DIST = 128
D_FF = 2816
N_BRANCH = 3
ATT_W = SWA_Q_HEADS * HEAD_DIM
KV_W = SWA_KV_HEADS * HEAD_DIM
SCALE = HEAD_DIM ** -0.5
EPS = 1e-6

_ORIG = (("qa", ATT_W), ("ka", KV_W), ("va", KV_W), ("qf", ATT_W), ("kf", ATT_W), ("vf", ATT_W),
         ("fl", FOX_HEADS), ("xc", LRU_WIDTH), ("yc", LRU_WIDTH), ("gates", N_BRANCH * D_MODEL))
IN_COLS = sum(w for _, w in _ORIG)
_WORK = (("gates", 3072), ("qa", 512), ("qf", 512), ("kf", 512), ("vf", 512), ("xc", 512), ("yc", 512),
         ("ka", 128), ("va", 128), ("fl", 128))
WORK_COLS = sum(w for _, w in _WORK)


def _offsets(table):
    off, out = 0, {}
    for n, w in table:
        out[n] = (off, w)
        off += w
    return out


_ORIG_OFF = _offsets(_ORIG)
_WORK_OFF = _offsets(_WORK)

ADAM_LR = 0.001
ADAM_B1 = 0.9
ADAM_B2 = 0.999
ADAM_EPS = 1e-08
ADAM_WD = 0.01
ADAM_STEP = 10

VMEM_LIMIT = 62 * 1024 * 1024
FOX_TILE = 1408
LANE = 128


def _tile(n, cap, mult=LANE):
    if n <= cap:
        return n
    best = None
    for d in range(mult, cap + 1, mult):
        if n % d == 0:
            best = d
    assert best is not None, (n, cap, mult)
    return best


def _params(sem):
    return pltpu.CompilerParams(dimension_semantics=sem, vmem_limit_bytes=VMEM_LIMIT)


def _sigmoid(x):
    return 1.0 / (1.0 + jnp.exp(-x))


def _log_sigmoid(x):
    return jnp.minimum(x, 0.0) - jnp.log(1.0 + jnp.exp(-jnp.abs(x)))


def _neg_expm1(x):
    series = -x * (1.0 + x * (0.5 + x * (1.0 / 6.0 + x * (1.0 / 24.0 + x * (1.0 / 120.0)))))
    return jnp.where(x > -0.1, series, 1.0 - jnp.exp(x))


_GELU_C = math.sqrt(2.0 / math.pi)


def _gelu(x):
    return 0.5 * x * (1.0 + jnp.tanh(_GELU_C * (x + 0.044715 * x * x * x)))


def _gelu_grad(x):
    t = jnp.tanh(_GELU_C * (x + 0.044715 * x * x * x))
    return 0.5 * (1.0 + t) + 0.5 * x * (1.0 - t * t) * _GELU_C * (1.0 + 3.0 * 0.044715 * x * x)


def _xch_ops(src_ref, out_ref, send_sems, recv_sems, local_sem, gather):
    x, y, c = lax.axis_index("x"), lax.axis_index("y"), lax.axis_index("c")
    me = 4 * x + 2 * y + c

    def copy(r, mine):
        px, py, pc = x ^ ((r >> 2) & 1), y ^ ((r >> 1) & 1), c ^ (r & 1)
        pid = 4 * px + 2 * py + pc
        return pltpu.make_async_remote_copy(
            src_ref=src_ref if gather else src_ref.at[pid], dst_ref=out_ref.at[me if mine else pid],
            send_sem=send_sems.at[r - 1], recv_sem=recv_sems.at[r - 1],
            device_id=(px, py, pc), device_id_type=pl.DeviceIdType.MESH)

    def local():
        return pltpu.make_async_copy(src_ref if gather else src_ref.at[me], out_ref.at[me], local_sem)

    def start():
        local().start()
        for r in range(1, N_DEV):
            copy(r, True).start()

    def wait():
        for r in range(1, N_DEV):
            copy(r, False).wait_recv()
        for r in range(1, N_DEV):
            copy(r, True).wait_send()
        local().wait()

    return start, wait


_XCH_SEMS = [pltpu.SemaphoreType.DMA((N_DEV - 1,)), pltpu.SemaphoreType.DMA((N_DEV - 1,)), pltpu.SemaphoreType.DMA]
_ANY = pl.BlockSpec(memory_space=pl.ANY)


def _xch_shape(src, gather):
    return jax.ShapeDtypeStruct((N_DEV,) + tuple(src.shape if gather else src.shape[1:]), src.dtype)


def _exchange(src, gather, name):
    def body(src_ref, out_ref, send_sems, recv_sems, local_sem):
        start, wait = _xch_ops(src_ref, out_ref, send_sems, recv_sems, local_sem, gather)
        start()
        wait()

    return pl.pallas_call(body, name=name, in_specs=[_ANY], out_specs=_ANY, out_shape=_xch_shape(src, gather),
                          scratch_shapes=list(_XCH_SEMS))(src)


def _exchange_many(items, name):
    n = len(items)

    def body(*refs):
        ops = [_xch_ops(refs[k], refs[n + k], *refs[2 * n + 3 * k:2 * n + 3 * k + 3], items[k][1]) for k in range(n)]
        for start, _ in ops:
            start()
        for _, wait in ops:
            wait()

    return pl.pallas_call(body, name=name, in_specs=[_ANY] * n, out_specs=[_ANY] * n,
                          out_shape=[_xch_shape(s, g) for s, g in items],
                          scratch_shapes=list(_XCH_SEMS) * n)(*[s for s, _ in items])


def _pcall(body, name, grid, in_specs, out_specs, out_shape, scratch_shapes, sem, args, host=()):
    if not host:
        outs = pl.pallas_call(body, name=name, grid=grid, in_specs=in_specs, out_specs=out_specs, out_shape=out_shape,
                              scratch_shapes=scratch_shapes, compiler_params=_params(sem))(*args)
        return outs, []
    n_in, n_out, n_scr, n_x = len(args), len(out_shape), len(scratch_shapes), len(host)

    def wrapped(*refs):
        ins, xin = refs[:n_in], refs[n_in:n_in + n_x]
        outs, xout = refs[n_in + n_x:n_in + n_x + n_out], refs[n_in + n_x + n_out:n_in + 2 * n_x + n_out]
        scr = refs[n_in + 2 * n_x + n_out:n_in + 2 * n_x + n_out + n_scr]
        sems = refs[n_in + 2 * n_x + n_out + n_scr:]
        first = functools.reduce(jnp.logical_and, [pl.program_id(k) == 0 for k in range(len(grid))])
        last = functools.reduce(jnp.logical_and, [pl.program_id(k) == grid[k] - 1 for k in range(len(grid))])
        ops = [_xch_ops(xin[k], xout[k], *sems[3 * k:3 * k + 3], host[k][1]) for k in range(n_x)]

        @pl.when(first)
        def _():
            for start, _ in ops:
                start()

        body(*ins, *outs, *scr)

        @pl.when(last)
        def _():
            for _, wait in ops:
                wait()

    res = pl.pallas_call(
        wrapped, name=name, grid=grid, in_specs=list(in_specs) + [_ANY] * n_x,
        out_specs=list(out_specs) + [_ANY] * n_x,
        out_shape=list(out_shape) + [_xch_shape(s, g) for s, g in host],
        scratch_shapes=list(scratch_shapes) + list(_XCH_SEMS) * n_x,
        compiler_params=_params(("arbitrary",) * len(grid)))(*args, *[s for s, _ in host])
    return res[:n_out], res[n_out:]


def _mm(a, b, mode, name, res=None, exact=False, tm_cap=1408, tn_cap=512, tk_cap=1408, host=None, out_dtype=F32):
    if mode == "nn":
        (M, K), (K2, N) = a.shape, b.shape
    elif mode == "nt":
        (M, K), (N, K2) = a.shape, b.shape
    else:
        (K, M), (K2, N) = a.shape, b.shape
    assert K == K2, (a.shape, b.shape, mode)
    tm, tn, tk = _tile(M, tm_cap), _tile(N, tn_cap), _tile(K, tk_cap)
    nk = K // tk
    a_spec = {"nn": pl.BlockSpec((tm, tk), lambda i, j, k: (i, k)),
              "nt": pl.BlockSpec((tm, tk), lambda i, j, k: (i, k)),
              "tn": pl.BlockSpec((tk, tm), lambda i, j, k: (k, i))}[mode]
    b_spec = {"nn": pl.BlockSpec((tk, tn), lambda i, j, k: (k, j)),
              "nt": pl.BlockSpec((tn, tk), lambda i, j, k: (j, k)),
              "tn": pl.BlockSpec((tk, tn), lambda i, j, k: (k, j))}[mode]
    o_spec = pl.BlockSpec((tm, tn), lambda i, j, k: (i, j))
    has_res = res is not None

    def body(*refs):
        if has_res:
            a_ref, b_ref, r_ref, o_ref, acc_ref = refs
        else:
            a_ref, b_ref, o_ref, acc_ref = refs
        k = pl.program_id(2)

        @pl.when(k == 0)
        def _():
            acc_ref[...] = jnp.zeros_like(acc_ref)

        x, y = a_ref[...], b_ref[...]
        if exact:
            x, y, prec = x.astype(F32), y.astype(F32), lax.Precision.HIGHEST
        else:
            x, y, prec = x.astype(BF16), y.astype(BF16), None
        if mode == "tn":
            x = x.T
        dims = (((1,), (1,)), ((), ())) if mode == "nt" else (((1,), (0,)), ((), ()))
        acc_ref[...] += lax.dot_general(x, y, dims, precision=prec, preferred_element_type=F32)

        @pl.when(k == nk - 1)
        def _():
            if has_res:
                o_ref[...] = (acc_ref[...] + r_ref[...]).astype(out_dtype)
            else:
                o_ref[...] = acc_ref[...].astype(out_dtype)

    in_specs = [a_spec, b_spec] + ([o_spec] if has_res else [])
    args = (a, b) + ((res,) if has_res else ())
    (out,), xouts = _pcall(body, name, (M // tm, N // tn, nk), in_specs, [o_spec],
                           [jax.ShapeDtypeStruct((M, N), out_dtype)], [pltpu.VMEM((tm, tn), F32)],
                           ("parallel", "parallel", "arbitrary"), args, host or ())
    return out if host is None else (out, xouts)


def _rowwise(fn, name, n_rows, tr, row_ins, const_ins, row_outs, red_outs=()):
    n_row_in, n_const, n_row_out, n_red = len(row_ins), len(const_ins), len(row_outs), len(red_outs)
    nblk, per = n_rows // tr, tr // 8
    in_specs = []
    for arr, w, idx, *side in row_ins:
        width = arr.shape[1] if w is None else w
        if not side:
            in_specs.append(pl.BlockSpec((tr, width), functools.partial(lambda i, j: (i, j), j=idx)))
        elif side[0] == "prev":
            in_specs.append(pl.BlockSpec((8, width), functools.partial(
                lambda i, j: (jnp.maximum(i * per - 1, 0), j), j=idx)))
        else:
            in_specs.append(pl.BlockSpec((8, width), functools.partial(
                lambda i, j: (jnp.minimum((i + 1) * per, nblk * per - 1), j), j=idx)))
    for c in const_ins:
        in_specs.append(pl.BlockSpec(c.shape, lambda i: (0, 0)))
    out_shape = [jax.ShapeDtypeStruct((n_rows, c), dt) for c, dt in row_outs]
    out_shape += [jax.ShapeDtypeStruct((1, c), F32) for c in red_outs]
    out_specs = [pl.BlockSpec((tr, c), lambda i: (i, 0)) for c, _ in row_outs]
    out_specs += [pl.BlockSpec((1, c), lambda i: (0, 0)) for c in red_outs]

    def body(*refs):
        i = pl.program_id(0)
        rows = [r[...] for r in refs[:n_row_in]]
        consts = [r[...] for r in refs[n_row_in:n_row_in + n_const]]
        outs = refs[n_row_in + n_const:]
        row_vals, red_vals = fn(i, rows, consts)
        for k in range(n_row_out):
            outs[k][...] = row_vals[k].astype(outs[k].dtype)
        if n_red:
            @pl.when(i == 0)
            def _():
                for k in range(n_red):
                    outs[n_row_out + k][...] = jnp.zeros_like(outs[n_row_out + k])

            for k in range(n_red):
                outs[n_row_out + k][...] += red_vals[k]

    res = pl.pallas_call(
        body, name=name, grid=(n_rows // tr,), in_specs=in_specs, out_specs=out_specs, out_shape=out_shape,
        compiler_params=_params(("arbitrary",)))(*[r[0] for r in row_ins], *const_ins)
    return res


def _rows_from_prev(cur, prev8, s, first):
    rolled = pltpu.roll(cur, s, 0)
    top = jnp.where(first, 0.0, pltpu.roll(prev8, s, 0))
    r8 = lax.broadcasted_iota(jnp.int32, prev8.shape, 0)
    return jnp.concatenate([jnp.where(r8 < s, top, rolled[:8]), rolled[8:]], axis=0)


def _rows_from_next(cur, next8, s, last):
    tr = cur.shape[0]
    rolled = pltpu.roll(cur, tr - s, 0)
    bottom = jnp.where(last, 0.0, pltpu.roll(next8, 8 - s, 0))
    r8 = lax.broadcasted_iota(jnp.int32, next8.shape, 0)
    return jnp.concatenate([rolled[:tr - 8], jnp.where(r8 >= 8 - s, bottom, rolled[tr - 8:])], axis=0)


def _row_ids(i, tr):
    return i * tr + lax.broadcasted_iota(jnp.int32, (tr, 1), 0)


def _colsum(x):
    return jnp.sum(x, axis=0, keepdims=True)


def _scan(a, b, name, b2=None, reverse=False):
    n_rows, c = b.shape
    tr = _tile(n_rows, 384)
    nblk = n_rows // tr
    has_a = a is not None
    has_b2 = b2 is not None

    def body(*refs):
        o_ref, carry, a_edge = refs[-3:]
        ins = list(refs[:-3])
        a_ref = ins.pop(0) if has_a else None
        b_ref = ins.pop(0)

        @pl.when(pl.program_id(0) == 0)
        def _():
            carry[...] = jnp.zeros_like(carry)
            a_edge[...] = jnp.zeros_like(a_edge)

        rows = lax.broadcasted_iota(jnp.int32, (tr, c), 0)
        bv = b_ref[...] + ins[0][...] if has_b2 else b_ref[...]
        av = a_ref[...] if has_a else None
        if has_a and reverse:
            a_first = av[0:1, :]
            av = _rows_from_next(av, a_edge[...], 1, False)
            a_edge[...] = jnp.broadcast_to(a_first, a_edge.shape)
        s = 1
        while s < tr:
            keep = rows < tr - s if reverse else rows >= s
            shift = tr - s if reverse else s
            b_sh = jnp.where(keep, pltpu.roll(bv, shift, 0), 0.0)
            if has_a:
                a_sh = jnp.where(keep, pltpu.roll(av, shift, 0), 1.0)
                bv = av * b_sh + bv
                av = av * a_sh
            else:
                bv = b_sh + bv
            s *= 2
        h = av * carry[0:1, :] + bv if has_a else carry[0:1, :] + bv
        o_ref[...] = h
        edge = 0 if reverse else tr - 1
        carry[...] = jnp.broadcast_to(h[edge:edge + 1, :], carry.shape)

    spec = pl.BlockSpec((tr, c), (lambda i: (nblk - 1 - i, 0)) if reverse else (lambda i: (i, 0)))
    args = ((a,) if has_a else ()) + (b,) + ((b2,) if has_b2 else ())
    return pl.pallas_call(
        body, name=name, grid=(n_rows // tr,), in_specs=[spec] * len(args), out_specs=spec,
        out_shape=jax.ShapeDtypeStruct((n_rows, c), F32),
        scratch_shapes=[pltpu.VMEM((8, c), F32), pltpu.VMEM((8, c), F32)],
        compiler_params=_params(("arbitrary",)))(*args)


def _rev_scan(a, g, name, g2=None):
    return _scan(a, g, name, b2=g2, reverse=True)


def _swa_masks(blk, n_pad):
    qi = lax.broadcasted_iota(jnp.int32, (BLOCK, 2 * BLOCK), 0)
    ki = lax.broadcasted_iota(jnp.int32, (BLOCK, 2 * BLOCK), 1)
    dist = qi + BLOCK - ki
    key_abs = (blk - 1) * BLOCK + ki
    return (dist >= 0) & (dist < BLOCK) & (key_abs >= n_pad)


_QA_COL = _WORK_OFF["qa"][0] // ATT_W
_KA_COL, _VA_COL = (_WORK_OFF[n][0] // LANE for n in ("ka", "va"))


def _swa_strips(prev_ref, cur_ref):
    x = jnp.concatenate([prev_ref[...], cur_ref[...]], axis=0)
    return [x.astype(BF16), pltpu.roll(x, HEAD_DIM, 1).astype(BF16)]


def _swa_swapped(h):
    return 0 if h % 2 == h // SWA_GROUP else 1


def _swa_specs(tq, nsub):
    prev = lambda col: pl.BlockSpec((BLOCK, LANE), lambda i: (jnp.maximum(i * nsub - 1, 0), col))
    cur = lambda col: pl.BlockSpec((tq, LANE), lambda i: (i, col))
    return [pl.BlockSpec((tq, ATT_W), lambda i: (i, _QA_COL)), prev(_KA_COL), cur(_KA_COL), prev(_VA_COL), cur(_VA_COL),
            pl.BlockSpec((SWA_Q_HEADS, BLOCK, 2 * BLOCK), lambda i: (0, 0, 0)),
            pl.BlockSpec((SWA_Q_HEADS, 1, LANE), lambda i: (0, 0, 0))]


def _swa_fwd(proj, bias, sinks, n_pad, host=()):
    Tp = proj.shape[0]
    tq = _tile(Tp, 384)
    nsub = tq // BLOCK

    def body(q_ref, kp_ref, kc_ref, vp_ref, vc_ref, b_ref, s_ref, o_ref, lse_ref):
        i = pl.program_id(0)
        low = _low_lanes()
        ks, vs = _swa_strips(kp_ref, kc_ref), _swa_strips(vp_ref, vc_ref)
        for j in range(nsub):
            rows, keys = slice(j * BLOCK, (j + 1) * BLOCK), slice(j * BLOCK, (j + 2) * BLOCK)
            mask = _swa_masks(i * nsub + j, n_pad)
            for hb in range(SWA_Q_HEADS // 2):
                pair = slice(hb * LANE, (hb + 1) * LANE)
                qh = _split_pair(q_ref[rows, pair] * SCALE, low)
                outs = []
                for e in range(2):
                    h = 2 * hb + e
                    kk, vv = ks[_swa_swapped(h)][keys], vs[_swa_swapped(h)][keys]
                    s = lax.dot_general(qh[e], kk, (((1,), (1,)), ((), ())), preferred_element_type=F32)
                    s = jnp.where(mask, s + b_ref[h], NEG_INF)
                    sink = s_ref[h, :, 0:1]
                    m = jnp.maximum(jnp.max(s, axis=1, keepdims=True), sink)
                    p = jnp.exp(s - m)
                    denom = jnp.sum(p, axis=1, keepdims=True) + jnp.exp(sink - m)
                    outs.append(jnp.dot((p / denom).astype(BF16), vv, preferred_element_type=F32))
                    lse_ref[h, rows, :] = m + jnp.log(denom)
                o_ref[rows, pair] = jnp.where(low, outs[0], outs[1])

    return _pcall(
        body, "swa_fwd", (Tp // tq,), _swa_specs(tq, nsub),
        [pl.BlockSpec((tq, ATT_W), lambda i: (i, 0)), pl.BlockSpec((SWA_Q_HEADS, tq, 1), lambda i: (0, i, 0))],
        [jax.ShapeDtypeStruct((Tp, ATT_W), F32), jax.ShapeDtypeStruct((SWA_Q_HEADS, Tp, 1), F32)],
        [], ("arbitrary",), (proj, proj, proj, proj, proj, bias, sinks), host)


def _swa_bwd(proj, bias, sinks, o, lse, do, n_pad, host=()):
    Tp = proj.shape[0]
    tq = _tile(Tp, 384)
    nsub = tq // BLOCK
    nq = Tp // tq
    strip = BLOCK + tq

    def body(q_ref, kp_ref, kc_ref, vp_ref, vc_ref, b_ref, s_ref, o_ref, lse_ref, do_ref,
             dq_ref, dk_ref, dv_ref, db_ref, ds_ref, dk_sw, dv_sw):
        i = pl.program_id(0)

        @pl.when(i == 0)
        def _():
            db_ref[...] = jnp.zeros_like(db_ref)
            ds_ref[...] = jnp.zeros_like(ds_ref)

        for ref in (dk_ref, dv_ref, dk_sw, dv_sw):
            ref[...] = jnp.zeros_like(ref)
        low = _low_lanes()
        ks, vs = _swa_strips(kp_ref, kc_ref), _swa_strips(vp_ref, vc_ref)
        for j in range(nsub):
            rows, keys = slice(j * BLOCK, (j + 1) * BLOCK), slice(j * BLOCK, (j + 2) * BLOCK)
            mask = _swa_masks(i * nsub + j, n_pad)
            for hb in range(SWA_Q_HEADS // 2):
                pair = slice(hb * LANE, (hb + 1) * LANE)
                qh = _split_pair(q_ref[rows, pair] * SCALE, low)
                dof = do_ref[rows, pair]
                prod = dof * o_ref[rows, pair]
                doh = _split_pair(dof, low)
                dqs = []
                for e in range(2):
                    h = 2 * hb + e
                    sw = _swa_swapped(h)
                    kk, vv = ks[sw][keys], vs[sw][keys]
                    delta = jnp.sum(jnp.where(low, prod, 0.0) if e == 0 else jnp.where(low, 0.0, prod),
                                    axis=1, keepdims=True)
                    lse_h = lse_ref[h, rows, :]
                    s = lax.dot_general(qh[e], kk, (((1,), (1,)), ((), ())), preferred_element_type=F32)
                    s = jnp.where(mask, s + b_ref[h], NEG_INF)
                    p = jnp.exp(s - lse_h)
                    p_sink = jnp.exp(s_ref[h, :, 0:1] - lse_h)
                    dp = lax.dot_general(doh[e], vv, (((1,), (1,)), ((), ())), preferred_element_type=F32)
                    ds = p * (dp - delta)
                    db_ref[h] += ds
                    ds_ref[h] += jnp.broadcast_to(-jnp.sum(p_sink * delta, axis=0, keepdims=True), (1, LANE))
                    dqs.append(jnp.dot(ds.astype(BF16), kk, preferred_element_type=F32))
                    dk_h = jnp.dot(ds.T.astype(BF16), qh[e], preferred_element_type=F32)
                    dv_h = jnp.dot(p.T.astype(BF16), doh[e], preferred_element_type=F32)
                    if sw:
                        dk_sw[keys, :] += dk_h
                        dv_sw[keys, :] += dv_h
                    else:
                        dk_ref[0, keys, :] += dk_h
                        dv_ref[0, keys, :] += dv_h
                dq_ref[rows, pair] = jnp.where(low, dqs[0], dqs[1]) * SCALE
        dk_ref[0] += pltpu.roll(dk_sw[...], HEAD_DIM, 1)
        dv_ref[0] += pltpu.roll(dv_sw[...], HEAD_DIM, 1)

    wide = pl.BlockSpec((tq, ATT_W), lambda i: (i, 0))
    sspec = pl.BlockSpec((1, strip, LANE), lambda i: (i, 0, 0))
    return _pcall(
        body, "swa_bwd", (nq,),
        _swa_specs(tq, nsub) + [wide, pl.BlockSpec((SWA_Q_HEADS, tq, 1), lambda i: (0, i, 0)), wide],
        [wide, sspec, sspec, pl.BlockSpec((SWA_Q_HEADS, BLOCK, 2 * BLOCK), lambda i: (0, 0, 0)),
         pl.BlockSpec((SWA_Q_HEADS, 1, LANE), lambda i: (0, 0, 0))],
        [jax.ShapeDtypeStruct((Tp, ATT_W), F32), jax.ShapeDtypeStruct((nq, strip, LANE), F32),
         jax.ShapeDtypeStruct((nq, strip, LANE), F32),
         jax.ShapeDtypeStruct((SWA_Q_HEADS, BLOCK, 2 * BLOCK), F32), jax.ShapeDtypeStruct((SWA_Q_HEADS, 1, LANE), F32)],
        [pltpu.VMEM((strip, LANE), F32), pltpu.VMEM((strip, LANE), F32)], ("arbitrary",),
        (proj, proj, proj, proj, proj, bias, sinks, o, lse, do), host)


def _fox_scores(qb, kk, cq, ck, diag, tq):
    s = lax.dot_general(qb, kk, (((1,), (1,)), ((), ())), preferred_element_type=F32)
    s = s + cq - ck
    return _fox_causal(s, tq) if diag else s


def _fox_causal(s, tq):
    r = lax.broadcasted_iota(jnp.int32, (tq, tq), 0)
    c = lax.broadcasted_iota(jnp.int32, (tq, tq), 1)
    return jnp.where(c <= r, s, NEG_INF)


_FOX_PAIRS = FOX_HEADS // 2
_QF_COL, _KF_COL, _VF_COL = (_WORK_OFF[n][0] // LANE for n in ("qf", "kf", "vf"))


def _low_lanes():
    return lax.broadcasted_iota(jnp.int32, (1, LANE), 1) < HEAD_DIM


def _split_pair(x, low):
    return [jnp.where(low, x, 0.0).astype(BF16), jnp.where(low, 0.0, x).astype(BF16)]


def _fox_fwd(proj, ccol, crow, tq, host=(), prefetch=True):
    Tp = proj.shape[0]
    nq = Tp // tq

    def body(q_ref, k_ref, v_ref, cc_ref, cr_ref, o_ref, lse_ref):
        qi = pl.program_id(1)
        low = _low_lanes()
        qh = _split_pair(q_ref[...] * SCALE, low)
        cq = [cc_ref[0], cc_ref[1]]

        def scores(kb):
            kk = k_ref[pl.ds(pl.multiple_of(kb * tq, tq), tq), :].astype(BF16)
            return [_fox_scores(qh[e], kk, cq[e], cr_ref[e, kb], False, tq) for e in range(2)]

        def softmax_pv(kb, s, carry, diag):
            vv = v_ref[pl.ds(pl.multiple_of(kb * tq, tq), tq), :].astype(BF16)
            stats, upd = [], []
            for e in range(2):
                m, l = carry[2 * e], carry[2 * e + 1]
                se = _fox_causal(s[e], tq) if diag else s[e]
                m_new = jnp.maximum(m, jnp.max(se, axis=1, keepdims=True))
                alpha = jnp.exp(m - m_new)
                p = jnp.exp(se - m_new)
                stats += [m_new, alpha * l + jnp.sum(p, axis=1, keepdims=True)]
                upd.append(alpha * carry[4] + jnp.dot(p.astype(BF16), vv, preferred_element_type=F32))
            return (*stats, jnp.where(low, upd[0], upd[1]))

        col = lambda val: jnp.full((tq, 1), val, F32)
        init = (col(NEG_INF), col(0.0), col(NEG_INF), col(0.0), jnp.zeros((tq, LANE), F32))
        if prefetch:
            def step(kb, c):
                s_next = scores(kb + 1)
                return (*softmax_pv(kb, c[5:], c[:5], False), *s_next)

            c = lax.fori_loop(0, qi, step, (*init, *scores(0)))
            m0, l0, m1, l1, acc = softmax_pv(qi, c[5:], c[:5], True)
        else:
            c = lax.fori_loop(0, qi, lambda kb, c: softmax_pv(kb, scores(kb), c, False), init)
            m0, l0, m1, l1, acc = softmax_pv(qi, scores(qi), c, True)
        o_ref[...] = jnp.where(low, acc / l0, acc / l1)
        lse_ref[0] = m0 + jnp.log(l0)
        lse_ref[1] = m1 + jnp.log(l1)

    cspec = pl.BlockSpec((2, tq, 1), lambda h, i: (h, i, 0))
    return _pcall(
        body, "fox_fwd", (_FOX_PAIRS, nq),
        [pl.BlockSpec((tq, LANE), lambda h, i: (i, _QF_COL + h)), pl.BlockSpec((Tp, LANE), lambda h, i: (0, _KF_COL + h)),
         pl.BlockSpec((Tp, LANE), lambda h, i: (0, _VF_COL + h)), cspec,
         pl.BlockSpec((2, nq, 1, tq), lambda h, i: (h, 0, 0, 0))],
        [pl.BlockSpec((tq, LANE), lambda h, i: (i, h)), cspec],
        [jax.ShapeDtypeStruct((Tp, FOX_HEADS * HEAD_DIM), F32), jax.ShapeDtypeStruct((FOX_HEADS, Tp, 1), F32)],
        [], ("parallel", "arbitrary"), (proj, proj, proj, ccol, crow), host)


def _fox_bwd_dq(proj, ccol, crow, o, lse, do, tq, host=()):
    Tp = proj.shape[0]
    nq = Tp // tq

    def body(q_ref, k_ref, v_ref, cc_ref, cr_ref, o_ref, lse_ref, do_ref, dq_ref, dl_ref, dc_ref):
        qi = pl.program_id(1)
        low = _low_lanes()
        qh = _split_pair(q_ref[...] * SCALE, low)
        dof = do_ref[...]
        prod = dof * o_ref[...]
        delta = [jnp.sum(jnp.where(low, prod, 0.0), axis=1, keepdims=True),
                 jnp.sum(jnp.where(low, 0.0, prod), axis=1, keepdims=True)]
        doh = _split_pair(dof, low)
        cq = [cc_ref[0], cc_ref[1]]
        lse = [lse_ref[0], lse_ref[1]]

        def step(kb, carry, diag):
            rows = pl.ds(pl.multiple_of(kb * tq, tq), tq)
            kk, vv = k_ref[rows, :].astype(BF16), v_ref[rows, :].astype(BF16)
            dqs, dcs = [], []
            for e in range(2):
                s = _fox_scores(qh[e], kk, cq[e], cr_ref[e, kb], diag, tq)
                p = jnp.exp(s - lse[e])
                dp = lax.dot_general(doh[e], vv, (((1,), (1,)), ((), ())), preferred_element_type=F32)
                ds = p * (dp - delta[e])
                dqs.append(jnp.dot(ds.astype(BF16), kk, preferred_element_type=F32))
                dcs.append(carry[1 + e] + jnp.sum(ds, axis=1, keepdims=True))
            return (carry[0] + jnp.where(low, dqs[0], dqs[1]), *dcs)

        init = (jnp.zeros((tq, LANE), F32), jnp.zeros((tq, 1), F32), jnp.zeros((tq, 1), F32))
        dq, dc0, dc1 = step(qi, lax.fori_loop(0, qi, functools.partial(step, diag=False), init), True)
        dq_ref[...] = dq * SCALE
        dl_ref[0], dl_ref[1] = delta
        dc_ref[0], dc_ref[1] = dc0, dc1

    cspec = pl.BlockSpec((2, tq, 1), lambda h, i: (h, i, 0))
    pspec = pl.BlockSpec((tq, LANE), lambda h, i: (i, h))
    stat = jax.ShapeDtypeStruct((FOX_HEADS, Tp, 1), F32)
    return _pcall(
        body, "fox_bwd_dq", (_FOX_PAIRS, nq),
        [pl.BlockSpec((tq, LANE), lambda h, i: (i, _QF_COL + h)), pl.BlockSpec((Tp, LANE), lambda h, i: (0, _KF_COL + h)),
         pl.BlockSpec((Tp, LANE), lambda h, i: (0, _VF_COL + h)), cspec,
         pl.BlockSpec((2, nq, 1, tq), lambda h, i: (h, 0, 0, 0)), pspec, cspec, pspec],
        [pspec, cspec, cspec], [jax.ShapeDtypeStruct((Tp, FOX_HEADS * HEAD_DIM), F32), stat, stat],
        [], ("parallel", "arbitrary"), (proj, proj, proj, ccol, crow, o, lse, do), host)


def _fox_bwd_dkv(proj, ccol, crow, lse, delta, do, tq, host=()):
    Tp = proj.shape[0]
    nq = Tp // tq
    pair = lambda z: z.reshape(_FOX_PAIRS, 2, Tp).transpose(0, 2, 1)
    stats = jnp.concatenate([pair(ccol), pair(lse), pair(delta)], axis=-1).reshape(_FOX_PAIRS, nq, tq, 6)

    def body(q_ref, do_ref, st_ref, k_ref, v_ref, cr_ref, dk_ref, dv_ref, dc_ref):
        ki = pl.program_id(1)
        low = _low_lanes()
        kk, vv = k_ref[...].astype(BF16), v_ref[...].astype(BF16)

        def step(qb_i, carry, diag):
            rows = pl.ds(pl.multiple_of(qb_i * tq, tq), tq)
            qh = _split_pair(q_ref[rows, :] * SCALE, low)
            doh = _split_pair(do_ref[rows, :], low)
            st = st_ref[0, qb_i]
            dk, dv = carry[0], carry[1]
            dcs = []
            for e in range(2):
                s = _fox_scores(qh[e], kk, st[:, e:e + 1], cr_ref[e, 0], diag, tq)
                p = jnp.exp(s - st[:, 2 + e:3 + e])
                dp = lax.dot_general(doh[e], vv, (((1,), (1,)), ((), ())), preferred_element_type=F32)
                ds = p * (dp - st[:, 4 + e:5 + e])
                dv = dv + jnp.dot(p.T.astype(BF16), doh[e], preferred_element_type=F32)
                dk = dk + jnp.dot(ds.T.astype(BF16), qh[e], preferred_element_type=F32)
                dcs.append(carry[2 + e] - jnp.sum(ds, axis=0, keepdims=True))
            return (dk, dv, *dcs)

        init = (jnp.zeros((tq, LANE), F32), jnp.zeros((tq, LANE), F32), jnp.zeros((1, tq), F32), jnp.zeros((1, tq), F32))
        dk, dv, dc0, dc1 = lax.fori_loop(ki + 1, nq, functools.partial(step, diag=False), step(ki, init, True))
        dk_ref[...] = dk
        dv_ref[...] = dv
        dc_ref[0, 0] = dc0
        dc_ref[1, 0] = dc1

    rspec = pl.BlockSpec((2, 1, 1, tq), lambda h, i: (h, i, 0, 0))
    pspec = pl.BlockSpec((tq, LANE), lambda h, i: (i, h))
    wide = jax.ShapeDtypeStruct((Tp, FOX_HEADS * HEAD_DIM), F32)
    return _pcall(
        body, "fox_bwd_dkv", (_FOX_PAIRS, nq),
        [pl.BlockSpec((Tp, LANE), lambda h, i: (0, _QF_COL + h)), pl.BlockSpec((Tp, LANE), lambda h, i: (0, h)),
         pl.BlockSpec((1, nq, tq, 6), lambda h, i: (h, 0, 0, 0)),
         pl.BlockSpec((tq, LANE), lambda h, i: (i, _KF_COL + h)), pl.BlockSpec((tq, LANE), lambda h, i: (i, _VF_COL + h)),
         rspec],
        [pspec, pspec, rspec], [wide, wide, jax.ShapeDtypeStruct((FOX_HEADS, nq, 1, tq), F32)],
        [], ("parallel", "arbitrary"), (proj, do, stats, proj, proj, crow), host)


def _adam(gparts, w, m, v, name, layer=0):
    P, R, C = gparts.shape
    tr = _tile(R, 256, 8)
    first = layer * (R // tr)

    def body(g_ref, w_ref, m_ref, v_ref, go_ref, d_ref, mo_ref, vo_ref):
        g = g_ref[0].astype(F32)
        for p in range(1, P):
            g = g + g_ref[p].astype(F32)
        m2 = ADAM_B1 * m_ref[...] + (1.0 - ADAM_B1) * g
        v2 = ADAM_B2 * v_ref[...] + (1.0 - ADAM_B2) * (g * g)
        m_hat = m2 / (1.0 - ADAM_B1 ** ADAM_STEP)
        v_hat = v2 / (1.0 - ADAM_B2 ** ADAM_STEP)
        go_ref[...] = g
        d_ref[...] = -ADAM_LR * (m_hat / (jnp.sqrt(v_hat) + ADAM_EPS) + ADAM_WD * w_ref[...])
        mo_ref[...] = m2
        vo_ref[...] = v2

    spec = pl.BlockSpec((tr, C), lambda i: (i, 0))
    wspec = pl.BlockSpec((tr, C), lambda i: (first + i, 0))
    shp = jax.ShapeDtypeStruct((R, C), F32)
    return pl.pallas_call(
        body, name=name, grid=(R // tr,),
        in_specs=[pl.BlockSpec((P, tr, C), lambda i: (0, i, 0)), wspec, wspec, wspec],
        out_specs=[spec] * 4, out_shape=[shp] * 4, compiler_params=_params(("parallel",)))(gparts, w, m, v)


def _sum_parts(gparts, name):
    P, R, C = gparts.shape
    tr = _tile(R, 256, 8)

    def body(g_ref, o_ref):
        g = g_ref[0]
        for p in range(1, P):
            g = g + g_ref[p]
        o_ref[...] = g

    return pl.pallas_call(
        body, name=name, grid=(R // tr,), in_specs=[pl.BlockSpec((P, tr, C), lambda i: (0, i, 0))],
        out_specs=pl.BlockSpec((tr, C), lambda i: (i, 0)), out_shape=jax.ShapeDtypeStruct((R, C), F32),
        compiler_params=_params(("parallel",)))(gparts)


def _pack(pieces, width, row_mult):
    flat = jnp.concatenate([p.reshape(-1) for p in pieces])
    n = flat.shape[0]
    rows = -(-n // width)
    rows = -(-rows // row_mult) * row_mult
    return jnp.pad(flat, (0, rows * width - n)).reshape(rows, width)


def _unpack(flat2d, shapes, lead=None):
    out, off = [], 0
    if lead is None:
        flat = flat2d.reshape(-1)
        for s in shapes:
            n = int(np.prod(s))
            out.append(flat[off:off + n].reshape(s))
            off += n
    else:
        flat = flat2d.reshape(lead, -1)
        for s in shapes:
            n = int(np.prod(s))
            out.append(flat[:, off:off + n].reshape((lead,) + tuple(s)))
            off += n
    return out


def _col_pieces(off, width, shard_w, fetch):
    out, c, end = [], off, off + width
    while c < end:
        d = c // shard_w
        hi = min(end, (d + 1) * shard_w)
        out.append(fetch(d, c - d * shard_w, hi - d * shard_w))
        c = hi
    return out


def _work_cols_of_orig(lo, hi, fetch):
    out = []
    for name, ow in _ORIG:
        o = _ORIG_OFF[name][0]
        a, b = max(lo, o), min(hi, o + ow)
        if a < b:
            w0 = _WORK_OFF[name][0]
            out.append(fetch(w0 + a - o, w0 + b - o))
    return out


def _t5_bucket_np(dist):
    max_exact = REL_BUCKETS // 2
    d = np.maximum(dist, 0)
    scaled = (np.log(np.maximum(d, 1).astype(np.float32) / np.float32(max_exact))
              / np.float32(math.log(REL_MAX_DIST / max_exact))).astype(np.float32)
    large = np.minimum(max_exact + (scaled * np.float32(REL_BUCKETS - max_exact)).astype(np.int32), REL_BUCKETS - 1)
    return np.where(d < max_exact, d, large)


def _bucket_onehot():
    q_idx = np.arange(BLOCK)[:, None]
    k_idx = np.arange(2 * BLOCK)[None, :]
    bucket = _t5_bucket_np(q_idx + BLOCK - k_idx).reshape(-1)
    oh = np.zeros((LANE, BLOCK * 2 * BLOCK), np.float32)
    oh[bucket, np.arange(bucket.shape[0])] = 1.0
    return oh


def kernel(x, meta_tokens, rel_bias_table, norm_mix, w_in, swa_sinks, fox_forget_bias, conv_w, conv_b, lru_w_r, lru_b_r, lru_w_i, lru_b_i, lru_lambda, w_branch, w_out, norm_ffn, w_ffn_in, w_ffn_out, norm_final, loss_target, m_meta_tokens, m_rel_bias_table, m_norm_mix, m_w_in, m_swa_sinks, m_fox_forget_bias, m_conv_w, m_conv_b, m_lru_w_r, m_lru_b_r, m_lru_w_i, m_lru_b_i, m_lru_lambda, m_w_branch, m_w_out, m_norm_ffn, m_w_ffn_in, m_w_ffn_out, m_norm_final, v_meta_tokens, v_rel_bias_table, v_norm_mix, v_w_in, v_swa_sinks, v_fox_forget_bias, v_conv_w, v_conv_b, v_lru_w_r, v_lru_b_r, v_lru_w_i, v_lru_b_i, v_lru_lambda, v_w_branch, v_w_out, v_norm_ffn, v_w_ffn_in, v_w_ffn_out, v_norm_final):
    S = x.shape[1]
    T = N_META + S
    n_pad = (-T) % BLOCK
    Tp = T + n_pad
    first = n_pad + N_META
    TR = _tile(Tp, 384)
    TQ = _tile(Tp, 384)
    TQF = _tile(Tp, FOX_TILE)
    me =4 * lax.axis_index("x") + 2 * lax.axis_index("y") + lax.axis_index("c")

    big_names = ("w_in", "w_ffn_in", "w_ffn_out", "w_branch", "w_out")
    big_w = dict(w_in=w_in, w_ffn_in=w_ffn_in, w_ffn_out=w_ffn_out, w_branch=w_branch, w_out=w_out)
    big_m = dict(w_in=m_w_in, w_ffn_in=m_w_ffn_in, w_ffn_out=m_w_ffn_out, w_branch=m_w_branch, w_out=m_w_out)
    big_v = dict(w_in=v_w_in, w_ffn_in=v_w_ffn_in, w_ffn_out=v_w_ffn_out, w_branch=v_w_branch, w_out=v_w_out)
    rows2d = lambda w: w.reshape(-1, w.shape[-1])
    in_shard_w = IN_COLS // N_DEV
    cat_axis = dict(w_in=-1, w_ffn_in=-1, w_ffn_out=0, w_branch=-1, w_out=0)

    def w_send(n, l):
        return rows2d(big_w[n][l]).astype(BF16)

    def w_build(n, g):
        shard = lambda d: g[d].reshape(big_w[n].shape[1:])
        if cat_axis[n] == 0:
            return g.reshape(-1, g.shape[-1])
        if n != "w_in":
            return jnp.concatenate([shard(d) for d in range(N_DEV)], axis=cat_axis[n])
        parts = []
        for name, width in _WORK:
            off, ow = _ORIG_OFF[name]
            parts += _col_pieces(off, ow, in_shard_w, lambda d, lo, hi: shard(d)[:, lo:hi])
            if width > ow:
                parts.append(jnp.zeros((D_MODEL, width - ow), BF16))
        return jnp.concatenate(parts, axis=-1)

    W_in, W_ffn_in, W_ffn_out, W_branch, W_out = ([None] * DEPTH for _ in range(5))
    first_w_in, small_sh = _exchange_many(
        [(w_send("w_in", 0), True), (_pack([meta_tokens, conv_w], D_MODEL, 8), True)], "gather_first")
    W_in[0] = w_build("w_in", first_w_in)
    g_meta, g_conv_w = _unpack(small_sh, [meta_tokens.shape, conv_w.shape], lead=N_DEV)
    meta_full = g_meta.transpose(1, 0, 2).reshape(N_META, D_MODEL)
    conv_w_full = g_conv_w.transpose(1, 2, 0, 3).reshape(DEPTH, CONV_WIDTH, LRU_WIDTH)

    onehot = jnp.asarray(_bucket_onehot())
    table_t = jnp.pad(rel_bias_table.T, ((0, 0), (0, LANE - REL_BUCKETS)))
    bias = _mm(table_t, onehot, "nn", "swa_bias", exact=True, tn_cap=4096).reshape(SWA_Q_HEADS, BLOCK, 2 * BLOCK)

    def dense_blocks(w):
        rows = []
        for b in range(LRU_BLOCKS):
            rows.append(jnp.pad(w[b], ((0, 0), (b * LRU_BLOCK_DIM, (LRU_BLOCKS - 1 - b) * LRU_BLOCK_DIM))))
        return jnp.concatenate(rows, axis=0)

    h = jnp.concatenate([jnp.zeros((n_pad, D_MODEL), F32), meta_full, x[0]], axis=0)
    saved = []

    for l in range(DEPTH):
        sv = {"h": h}
        g_mix = norm_mix[l][None, :]

        def norm_fwd(i, rows, consts):
            xx = rows[0]
            return [xx * lax.rsqrt(jnp.mean(xx * xx, axis=1, keepdims=True) + EPS) * consts[0]], []

        (u,) = _rowwise(norm_fwd, "norm_fwd", Tp, TR, [(h, None, 0)], [g_mix], [(D_MODEL, BF16)])
        proj, got = _mm(u, W_in[l], "nn", "proj", tn_cap=384,
                        host=[(w_send(n, l), True) for n in ("w_branch", "w_out", "w_ffn_out")])
        W_branch[l], W_out[l], W_ffn_out[l] = (w_build(n, g) for n, g in zip(("w_branch", "w_out", "w_ffn_out"), got))
        half = D_MODEL // 2
        next_w_in = [] if l + 1 == DEPTH else [rows2d(w_in[l + 1])[k * half:(k + 1) * half].astype(BF16) for k in (0, 1)]
        next_got = []
        sv["u"], sv["proj"] = u, proj

        def col(name):
            off, w = _WORK_OFF[name]
            return (proj, w, off // w)

        sinks_l = jnp.broadcast_to(swa_sinks[l][:, None, None], (SWA_Q_HEADS, 1, LANE))
        (o_a, lse_a), got = _swa_fwd(proj, bias, sinks_l, n_pad, host=[(s, True) for s in next_w_in[:1]])
        next_got += got
        sv.update(sinks=sinks_l, lse_a=lse_a, o_a=o_a)

        fbias = jnp.pad(fox_forget_bias[l], (0, LANE - FOX_HEADS))[None, :]

        def logf_fwd(i, rows, consts):
            return [_log_sigmoid(rows[0] + consts[0])], []

        (logf,) = _rowwise(logf_fwd, "logf_fwd", Tp, TR, [col("fl")], [fbias], [(LANE, F32)])
        cum = _scan(None, logf, "cumsum")
        cum_h = cum[:, :FOX_HEADS].T
        ccol = cum_h[:, :, None]
        crow = jnp.where(jnp.arange(Tp)[None, :] < n_pad, -NEG_INF, cum_h).reshape(FOX_HEADS, Tp // TQF, 1, TQF)
        (o_f, lse_f), got = _fox_fwd(proj, ccol, crow, TQF, host=[(w_send("w_ffn_in", l), True)],
                                     prefetch=TQF <= 384)
        W_ffn_in[l] = w_build("w_ffn_in", got[0])
        sv.update(fbias=fbias, ccol=ccol, crow=crow, lse_f=lse_f, o_f=o_f)

        cw = conv_w_full[l]
        cb = conv_b[l][None, :]

        def conv_taps(i, cur, prev8):
            return [_rows_from_prev(cur, prev8, CONV_WIDTH - 1 - t, i == 0) for t in range(CONV_WIDTH - 1)] + [cur]

        def conv_fwd(i, rows, consts):
            w, b = consts
            taps = conv_taps(i, rows[0], rows[1])
            acc = taps[0] * w[0:1, :]
            for t in range(1, CONV_WIDTH):
                acc = acc + taps[t] * w[t:t + 1, :]
            return [acc + b], []

        xc_blocks = [col("xc"), col("xc") + ("prev",)]
        (xconv,) = _rowwise(conv_fwd, "conv_fwd", Tp, TR, xc_blocks, [cw, cb], [(LRU_WIDTH, F32)])
        w_ri = jnp.concatenate([dense_blocks(lru_w_r[l]), dense_blocks(lru_w_i[l])], axis=1)
        pre = _mm(xconv, w_ri, "nn", "lru_gates")
        lru_consts = [lru_b_r[l][None, :], lru_b_i[l][None, :], lru_lambda[l][None, :]]

        def lru_fwd(i, rows, consts):
            pr, xv = rows
            b_r, b_i, lam = consts
            r = _sigmoid(pr[:, :LRU_WIDTH] + b_r)
            gi = _sigmoid(pr[:, LRU_WIDTH:] + b_i)
            log_a = LRU_C * r * _log_sigmoid(lam)
            valid = (_row_ids(i, TR) >= n_pad).astype(F32)
            inp = jnp.sqrt(_neg_expm1(2.0 * log_a)) * (gi * xv) * valid
            return [jnp.exp(log_a), inp], []

        a_dec, inp = _rowwise(lru_fwd, "lru_fwd", Tp, TR, [(pre, None, 0), (xconv, None, 0)], lru_consts,
                              [(LRU_WIDTH, F32), (LRU_WIDTH, F32)])
        hs = _scan(a_dec, inp, "lru_scan")

        def oc_fwd(i, rows, consts):
            return [rows[0] * _gelu(rows[1])], []

        (o_c,) = _rowwise(oc_fwd, "oc_fwd", Tp, TR, [(hs, None, 0), col("yc")], [], [(LRU_WIDTH, BF16)])
        sv.update(cw=cw, xconv=xconv, w_ri=w_ri, pre=pre, lru_consts=lru_consts, a_dec=a_dec, hs=hs, o_c=o_c)

        ba = _mm(o_a, W_branch[l][0], "nn", "branch")
        bf = _mm(o_f, W_branch[l][1], "nn", "branch")
        bc = _mm(o_c, W_branch[l][2], "nn", "branch")

        def merge_fwd(i, rows, consts):
            g, b0, b1, b2 = rows
            valid = (_row_ids(i, BLOCK) >= n_pad).astype(F32)
            mg = (_sigmoid(g[:, :D_MODEL]) * b0 + _sigmoid(g[:, D_MODEL:2 * D_MODEL]) * b1
                  + _sigmoid(g[:, 2 * D_MODEL:]) * b2)
            return [mg * valid], []

        (merged,) = _rowwise(merge_fwd, "merge_fwd", Tp, BLOCK,
                             [col("gates"), (ba, None, 0), (bf, None, 0), (bc, None, 0)], [], [(D_MODEL, BF16)])
        h1 = _mm(merged, W_out[l], "nn", "out_proj", res=h)
        sv.update(ba=ba, bf=bf, bc=bc, merged=merged, h1=h1)

        g_ffn = norm_ffn[l][None, :]
        (u2,) = _rowwise(norm_fwd, "norm_fwd", Tp, TR, [(h1, None, 0)], [g_ffn], [(D_MODEL, BF16)])
        ff, got = _mm(u2, W_ffn_in[l], "nn", "ffn_in", host=[(s, True) for s in next_w_in[1:]])
        next_got += got
        if l + 1 < DEPTH:
            W_in[l + 1] = w_build("w_in", jnp.concatenate(next_got, axis=1))

        def act_fwd(i, rows, consts):
            gate, up = rows
            return [gate * _sigmoid(gate) * up], []

        (act,) = _rowwise(act_fwd, "act_fwd", Tp, TR, [(ff, D_FF, 0), (ff, D_FF, 1)], [], [(D_FF, BF16)])
        h = _mm(act, W_ffn_out[l], "nn", "ffn_out", res=h1)
        sv.update(u2=u2, ff=ff, act=act)
        saved.append(sv)

    tgt = jnp.concatenate([jnp.zeros((first, D_MODEL), F32), loss_target[0]], axis=0)
    g_fin = norm_final[None, :]

    def head(i, rows, consts):
        xx, tg = rows
        g = consts[0]
        valid = (_row_ids(i, TR) >= first).astype(F32)
        rstd = lax.rsqrt(jnp.mean(xx * xx, axis=1, keepdims=True) + EPS)
        xhat = xx * rstd
        err = (xhat * g - tg) * valid
        loss_rows = 0.5 * jnp.mean(err * err, axis=1, keepdims=True)
        dy = err * (1.0 / D_MODEL)
        dxhat = dy * g
        dx = rstd * (dxhat - xhat * jnp.mean(dxhat * xhat, axis=1, keepdims=True))
        return [dx], [jnp.broadcast_to(_colsum(loss_rows), (1, LANE)), _colsum(dy * xhat)]

    dh, loss_part, d_norm_final = _rowwise(head, "loss_head", Tp, TR, [(h, None, 0), (tgt, None, 0)], [g_fin],
                                           [(D_MODEL, F32)], [LANE, D_MODEL])
    loss = lax.psum(loss_part[0, 0], ("x", "y", "c"))

    def norm_bwd_call(xin, g, du, dres):
        def norm_bwd(i, rows, consts):
            xx, dd, rr = rows
            gg = consts[0]
            valid = (_row_ids(i, TR) >= n_pad).astype(F32)
            rstd = lax.rsqrt(jnp.mean(xx * xx, axis=1, keepdims=True) + EPS)
            xhat = xx * rstd
            dxhat = dd * gg
            dx = rstd * (dxhat - xhat * jnp.mean(dxhat * xhat, axis=1, keepdims=True))
            return [rr + dx * valid], [_colsum(dd * xhat)]

        return _rowwise(norm_bwd, "norm_bwd", Tp, TR, [(xin, None, 0), (du, None, 0), (dres, None, 0)], [g],
                        [(D_MODEL, F32)], [D_MODEL])

    grads = {k: [None] * DEPTH for k in ("norm_mix", "w_in", "swa_sinks", "fox_forget_bias", "conv_w", "conv_b",
                                         "lru_w_r", "lru_b_r", "lru_w_i", "lru_b_i", "lru_lambda", "w_branch",
                                         "w_out", "norm_ffn", "w_ffn_in", "w_ffn_out")}
    d_bias_total = None

    def g_send(n, g):
        if n == "w_in":
            pieces = [jnp.concatenate(_work_cols_of_orig(d * in_shard_w, (d + 1) * in_shard_w,
                                                         lambda lo, hi: g[:, lo:hi]), axis=-1) for d in range(N_DEV)]
        elif cat_axis[n] == 0:
            return g.reshape(N_DEV, -1, g.shape[-1])
        else:
            ax = cat_axis[n] % g.ndim
            w = big_w[n].shape[1 + ax]
            pieces = [lax.slice_in_dim(g, d * w, (d + 1) * w, axis=ax) for d in range(N_DEV)]
        return jnp.stack([rows2d(p) for p in pieces])

    recv = {n: [None] * DEPTH for n in big_names}
    for l in reversed(range(DEPTH)):
        sv = saved[l]
        proj = sv["proj"]

        def col(name):
            off, w = _WORK_OFF[name]
            return (proj, w, off // w)

        dh2 = dh
        d_act = _mm(dh2, W_ffn_out[l], "nt", "d_act")
        grads["w_ffn_out"][l] = _mm(sv["act"], dh2, "tn", "dw_ffn_out", out_dtype=BF16)

        def act_bwd(i, rows, consts):
            gate, up, da = rows
            sg = _sigmoid(gate)
            d_gate = da * up * (sg * (1.0 + gate * (1.0 - sg)))
            d_up = da * (gate * sg)
            return [jnp.concatenate([d_gate, d_up], axis=1)], []

        (dff,) = _rowwise(act_bwd, "act_bwd", Tp, BLOCK, [(sv["ff"], D_FF, 0), (sv["ff"], D_FF, 1), (d_act, None, 0)],
                          [], [(2 * D_FF, BF16)])
        grads["w_ffn_in"][l] = _mm(sv["u2"], dff, "tn", "dw_ffn_in", tn_cap=1408, out_dtype=BF16)
        du2 = _mm(dff, W_ffn_in[l], "nt", "du2")
        dh1, dg = norm_bwd_call(sv["h1"], norm_ffn[l][None, :], du2, dh2)
        grads["norm_ffn"][l] = dg[0]

        dmerged = _mm(dh1, W_out[l], "nt", "d_merged")
        grads["w_out"][l] = _mm(sv["merged"], dh1, "tn", "dw_out", out_dtype=BF16)

        def merge_bwd(i, rows, consts):
            g, b0, b1, b2, dm = rows
            dm = dm * (_row_ids(i, BLOCK) >= n_pad).astype(F32)
            outs, dgs = [], []
            for k, bk in enumerate((b0, b1, b2)):
                sg = _sigmoid(g[:, k * D_MODEL:(k + 1) * D_MODEL])
                outs.append(dm * sg)
                dgs.append(dm * bk * sg * (1.0 - sg))
            return outs + [jnp.concatenate(dgs, axis=1)], []

        d_ba, d_bf, d_bc, d_gates = _rowwise(
            merge_bwd, "merge_bwd", Tp, BLOCK,
            [col("gates"), (sv["ba"], None, 0), (sv["bf"], None, 0), (sv["bc"], None, 0), (dmerged, None, 0)], [],
            [(D_MODEL, BF16)] * 3 + [(3 * D_MODEL, BF16)])
        grads["w_branch"][l] = jnp.stack([_mm(sv["o_a"], d_ba, "tn", "dw_branch", out_dtype=BF16),
                                          _mm(sv["o_f"], d_bf, "tn", "dw_branch", out_dtype=BF16),
                                          _mm(sv["o_c"], d_bc, "tn", "dw_branch", out_dtype=BF16)])
        do_a = _mm(d_ba, W_branch[l][0], "nt", "d_branch")
        do_f = _mm(d_bf, W_branch[l][1], "nt", "d_branch")
        do_c = _mm(d_bc, W_branch[l][2], "nt", "d_branch")

        def oc_bwd(i, rows, consts):
            d, hv, yv = rows
            return [d * _gelu(yv), d * hv * _gelu_grad(yv)], []

        d_hs, d_yc = _rowwise(oc_bwd, "oc_bwd", Tp, TR, [(do_c, None, 0), (sv["hs"], None, 0), col("yc")], [],
                              [(LRU_WIDTH, F32)] * 2)
        d_state = _rev_scan(sv["a_dec"], d_hs, "lru_scan")

        def lru_bwd(i, rows, consts):
            dH, hv, hv_prev8, pr, xv = rows
            hp = _rows_from_prev(hv, hv_prev8, 1, i == 0)
            b_r, b_i, lam = consts
            valid = (_row_ids(i, TR) >= n_pad).astype(F32)
            r = _sigmoid(pr[:, :LRU_WIDTH] + b_r)
            gi = _sigmoid(pr[:, LRU_WIDTH:] + b_i)
            lsl = _log_sigmoid(lam)
            log_a = LRU_C * r * lsl
            a = jnp.exp(log_a)
            one_m_e = _neg_expm1(2.0 * log_a)
            mult = jnp.sqrt(one_m_e)
            d_inp = dH * valid
            d_mult = d_inp * gi * xv
            d_gi = d_inp * mult * xv
            d_x = d_inp * mult * gi
            d_log_a = dH * hp * a - d_mult * (1.0 - one_m_e) / mult
            d_pre_r = d_log_a * (LRU_C * lsl) * r * (1.0 - r)
            d_pre_i = d_gi * gi * (1.0 - gi)
            d_lam = _colsum(d_log_a * (LRU_C * r)) * _sigmoid(-lam)
            return [jnp.concatenate([d_pre_r, d_pre_i], axis=1), d_x], [_colsum(d_pre_r), _colsum(d_pre_i), d_lam]

        d_pre, d_xdir, d_b_r, d_b_i, d_lam = _rowwise(
            lru_bwd, "lru_bwd", Tp, TR, [(d_state, None, 0), (sv["hs"], None, 0), (sv["hs"], None, 0, "prev"),
                                         (sv["pre"], None, 0), (sv["xconv"], None, 0)], sv["lru_consts"],
            [(2 * LRU_WIDTH, BF16), (LRU_WIDTH, F32)], [LRU_WIDTH] * 3)
        grads["lru_b_r"][l], grads["lru_b_i"][l], grads["lru_lambda"][l] = d_b_r[0], d_b_i[0], d_lam[0]
        d_w_ri = _mm(sv["xconv"], d_pre, "tn", "dw_lru_gates")

        def diag_blocks(wd):
            w4 = wd.reshape(LRU_BLOCKS, LRU_BLOCK_DIM, LRU_BLOCKS, LRU_BLOCK_DIM)
            return jnp.stack([w4[b, :, b, :] for b in range(LRU_BLOCKS)])

        grads["lru_w_r"][l] = diag_blocks(d_w_ri[:, :LRU_WIDTH])
        grads["lru_w_i"][l] = diag_blocks(d_w_ri[:, LRU_WIDTH:])
        d_conv = _mm(d_pre, sv["w_ri"], "nt", "d_xconv", res=d_xdir)

        def conv_bwd(i, rows, consts):
            w = consts[0]
            dc, dc_next8, xcur, x_prev8 = rows
            d = [dc] + [_rows_from_next(dc, dc_next8, s, i == Tp // TR - 1) for s in range(1, CONV_WIDTH)]
            xsh = conv_taps(i, xcur, x_prev8)
            dxc = d[0] * w[CONV_WIDTH - 1:CONV_WIDTH, :]
            for s in range(1, CONV_WIDTH):
                dxc = dxc + d[s] * w[CONV_WIDTH - 1 - s:CONV_WIDTH - s, :]
            return [dxc], [_colsum(xsh[t] * d[0]) for t in range(CONV_WIDTH)] + [_colsum(d[0])]

        res = _rowwise(conv_bwd, "conv_bwd", Tp, TR,
                       [(d_conv, None, 0), (d_conv, None, 0, "next"), col("xc"), col("xc") + ("prev",)], [sv["cw"]],
                       [(LRU_WIDTH, F32)], [LRU_WIDTH] * (CONV_WIDTH + 1))
        d_xc = res[0]
        grads["conv_w"][l] = jnp.concatenate(res[1:1 + CONV_WIDTH], axis=0)
        grads["conv_b"][l] = res[1 + CONV_WIDTH][0]

        (dqf, delta_f, dccol), got = _fox_bwd_dq(
            proj, sv["ccol"], sv["crow"], sv["o_f"], sv["lse_f"], do_f, TQF,
            host=[(g_send("w_ffn_in", grads["w_ffn_in"][l]), False)])
        recv["w_ffn_in"][l] = got[0]
        ride = [("w_out", l), ("w_branch", l)] + ([("w_in", l + 1)] if l + 1 < DEPTH else [])
        (dkf, dvf, dcrow), got = _fox_bwd_dkv(
            proj, sv["ccol"], sv["crow"], sv["lse_f"], delta_f, do_f, TQF,
            host=[(g_send(n, grads[n][ll]), False) for n, ll in ride])
        for (n, ll), r in zip(ride, got):
            recv[n][ll] = r
        lanes = lambda z: jnp.pad(z.reshape(FOX_HEADS, Tp).T, ((0, 0), (0, LANE - FOX_HEADS)))
        dlogf = _rev_scan(None, lanes(dcrow), "cumsum_bwd", g2=lanes(dccol))

        def logf_bwd(i, rows, consts):
            dl, fl = rows
            valid = (_row_ids(i, TR) >= n_pad).astype(F32)
            lane_ok = (lax.broadcasted_iota(jnp.int32, (1, LANE), 1) < FOX_HEADS).astype(F32)
            dfl = dl * _sigmoid(-(fl + consts[0])) * valid * lane_ok
            return [dfl], [_colsum(dfl)]

        d_fl, d_fb = _rowwise(logf_bwd, "logf_bwd", Tp, TR, [(dlogf, None, 0), col("fl")], [sv["fbias"]],
                              [(LANE, F32)], [LANE])
        grads["fox_forget_bias"][l] = d_fb[0, :FOX_HEADS]

        (dqa, dk_strip, dv_strip, d_bias_l, d_sink), got = _swa_bwd(
            proj, bias, sv["sinks"], sv["o_a"], sv["lse_a"], do_a, n_pad,
            host=[(g_send("w_ffn_out", grads["w_ffn_out"][l]), False)])
        recv["w_ffn_out"][l] = got[0]
        grads["swa_sinks"][l] = d_sink[:, 0, 0]
        d_bias_total = d_bias_l if d_bias_total is None else d_bias_total + d_bias_l

        def kv_parts(strip):
            own = strip[:, BLOCK:, :].reshape(Tp, KV_W)
            nxt = jnp.pad(strip[1:, :BLOCK, :], ((0, 1), (TQ - BLOCK, 0), (0, 0))).reshape(Tp, KV_W)
            return [(own, None, 0), (nxt, None, 0)]

        def assemble(i, rows, consts):
            dg, dqa_, dqf_, dkf_, dvf_, dxc_, dyc_, dk0, dk1, dv0, dv1, dfl_ = rows
            parts = [dg, dqa_, dqf_, dkf_, dvf_, dxc_, dyc_, dk0 + dk1, dv0 + dv1, dfl_]
            return [jnp.concatenate([p.astype(BF16) for p in parts], axis=1)], []

        (dproj,) = _rowwise(
            assemble, "assemble_dproj", Tp, BLOCK,
            [(d_gates, None, 0), (dqa, None, 0), (dqf, None, 0), (dkf, None, 0), (dvf, None, 0), (d_xc, None, 0),
             (d_yc, None, 0)] + kv_parts(dk_strip) + kv_parts(dv_strip) + [(d_fl, None, 0)], [],
            [(WORK_COLS, BF16)])
        grads["w_in"][l] = _mm(sv["u"], dproj, "tn", "dw_in", tn_cap=2176, out_dtype=BF16)
        if l == 0:
            du, got = _mm(dproj, W_in[l], "nt", "du", tk_cap=2176,
                          host=[(g_send("w_in", grads["w_in"][0]), False)])
            recv["w_in"][0] = got[0]
        else:
            du = _mm(dproj, W_in[l], "nt", "du", tk_cap=2176)
        dh, dg = norm_bwd_call(sv["h"], norm_mix[l][None, :], du, dh1)
        grads["norm_mix"][l] = dg[0]

    d_table = _mm(d_bias_total.reshape(SWA_Q_HEADS, BLOCK * 2 * BLOCK), onehot, "nt", "d_rel_table", exact=True,
                  tk_cap=4096)
    g_rel = d_table[:, :REL_BUCKETS].T
    g_meta_full = dh[n_pad:first]
    grad_x = dh[first:][None]

    stack = lambda k: jnp.stack(grads[k])
    big_res = {}
    for n in big_names:
        per_layer = [_adam(recv[n][l], rows2d(big_w[n]), rows2d(big_m[n]), rows2d(big_v[n]), "adam_" + n, layer=l)
                     for l in range(DEPTH)]
        big_res[n] = [jnp.stack([per_layer[l][k] for l in range(DEPTH)]).reshape(big_w[n].shape) for k in range(4)]

    rep_names = ("rel_bias_table", "norm_mix", "swa_sinks", "fox_forget_bias", "conv_b", "lru_w_r", "lru_b_r",
                 "lru_w_i", "lru_b_i", "lru_lambda", "norm_ffn", "norm_final")
    rep_w = dict(rel_bias_table=rel_bias_table, norm_mix=norm_mix, swa_sinks=swa_sinks,
                 fox_forget_bias=fox_forget_bias, conv_b=conv_b, lru_w_r=lru_w_r, lru_b_r=lru_b_r, lru_w_i=lru_w_i,
                 lru_b_i=lru_b_i, lru_lambda=lru_lambda, norm_ffn=norm_ffn, norm_final=norm_final)
    rep_m = dict(rel_bias_table=m_rel_bias_table, norm_mix=m_norm_mix, swa_sinks=m_swa_sinks,
                 fox_forget_bias=m_fox_forget_bias, conv_b=m_conv_b, lru_w_r=m_lru_w_r, lru_b_r=m_lru_b_r,
                 lru_w_i=m_lru_w_i, lru_b_i=m_lru_b_i, lru_lambda=m_lru_lambda, norm_ffn=m_norm_ffn,
                 norm_final=m_norm_final)
    rep_v = dict(rel_bias_table=v_rel_bias_table, norm_mix=v_norm_mix, swa_sinks=v_swa_sinks,
                 fox_forget_bias=v_fox_forget_bias, conv_b=v_conv_b, lru_w_r=v_lru_w_r, lru_b_r=v_lru_b_r,
                 lru_w_i=v_lru_w_i, lru_b_i=v_lru_b_i, lru_lambda=v_lru_lambda, norm_ffn=v_norm_ffn,
                 norm_final=v_norm_final)
    rep_g = {n: (g_rel if n == "rel_bias_table" else d_norm_final[0] if n == "norm_final" else stack(n))
             for n in rep_names}
    small_g = [rep_g[n] for n in rep_names] + [g_meta_full, stack("conv_w")]
    small_shapes = [rep_w[n].shape for n in rep_names] + [(N_META, D_MODEL), (DEPTH, CONV_WIDTH, LRU_WIDTH)]
    gs_all = _exchange(_pack(small_g, D_MODEL, 8), True, "gather_small_grads")
    gs_sum = _unpack(_sum_parts(gs_all, "sum_small_grads"), small_shapes)
    gsum = dict(zip(rep_names, gs_sum[:len(rep_names)]))
    g_meta_sh = lax.dynamic_slice_in_dim(gs_sum[-2], me * (D_MODEL // N_DEV), D_MODEL // N_DEV, axis=1)
    g_convw_sh = lax.dynamic_slice_in_dim(gs_sum[-1], me * (LRU_WIDTH // N_DEV), LRU_WIDTH // N_DEV, axis=2)
    sm_names = rep_names + ("meta_tokens", "conv_w")
    sm_w = [rep_w[n] for n in rep_names] + [meta_tokens, conv_w]
    sm_m = [rep_m[n] for n in rep_names] + [m_meta_tokens, m_conv_w]
    sm_v = [rep_v[n] for n in rep_names] + [v_meta_tokens, v_conv_w]
    sm_g = [gsum[n] for n in rep_names] + [g_meta_sh, g_convw_sh]
    sm_shapes = [w.shape for w in sm_w]
    sm_out = [_unpack(o, sm_shapes) for o in _adam(_pack(sm_g, D_MODEL, 8)[None], _pack(sm_w, D_MODEL, 8),
                                                   _pack(sm_m, D_MODEL, 8), _pack(sm_v, D_MODEL, 8), "adam_small")]
    sm_res = {n: [sm_out[k][j] for k in range(4)] for j, n in enumerate(sm_names)}

    order = ("meta_tokens", "rel_bias_table", "norm_mix", "w_in", "swa_sinks", "fox_forget_bias", "conv_w", "conv_b",
             "lru_w_r", "lru_b_r", "lru_w_i", "lru_b_i", "lru_lambda", "w_branch", "w_out", "norm_ffn", "w_ffn_in",
             "w_ffn_out", "norm_final")
    allres = {**big_res, **sm_res}
    outs = [loss, grad_x]
    for k in range(4):
        outs += [allres[n][k] for n in order]
    return tuple(outs)
```
